```python
import math
import jax, jax.numpy as jnp
from jax import lax
import numpy as np

D_MODEL = 1024
BATCH = 4
SEQ = 8192
DEPTH = 2

GRID_W = 64
CTX_LEN = 256
HY_CH = 512
HY_ORDER = 2
HY_EMB = 33
HY_BANDS = (HY_EMB - 1) // 2
HY_FFN = 64
HY_SHORT = 3
HY_DECAY_MIN = math.log(100.0) / 1.5
HY_DECAY_MAX = math.log(100.0) / 0.3
NA_HEADS = 8
NA_HD = 64
NA_WIN_R = 8
NA_WIN_C = 16
NA_QCB = 16
NA_KCB = 32
MLA_HEADS = 8
MLA_Q_RANK = 384
MLA_KV_RANK = 256
MLA_NOPE = 64
MLA_ROPE = 32
MLA_V = 96
ROPE_THETA = 10000.0
Q_BLOCK = 128
FN_CH = 256
FN_GROUPS = 4
FN_GD = FN_CH // FN_GROUPS
D_FF = 4 * D_MODEL
N_EVEN = (DEPTH + 1) // 2
N_ODD = DEPTH // 2
ALPHA = (2.0 * DEPTH) ** 0.25
OUT_SCALE = (8.0 * DEPTH) ** -0.25
LN_EPS = 1e-5

kernel_name = 'hybrid_hyena_natten_mla_fnet_dit'


def _norm_stats(x):
    xf = x.astype(jnp.float32)
    mu = jnp.mean(xf, -1, keepdims=True)
    var = jnp.mean(jnp.square(xf - mu), -1, keepdims=True)
    return (xf - mu) * lax.rsqrt(var + LN_EPS)


def _layer_norm(x, g, b):
    return (_norm_stats(x) * g + b).astype(x.dtype)


def _modulate(x, shift, scale):
    return (_norm_stats(x) * (1.0 + scale) + shift).astype(x.dtype)


def _rms_norm(x, g):
    xf = x.astype(jnp.float32)
    y = xf * lax.rsqrt(jnp.mean(jnp.square(xf), -1, keepdims=True) + LN_EPS)
    return (y * g).astype(x.dtype)


def _mlp(h, w1, w2):
    return jnp.square(jax.nn.relu(h @ w1)) @ w2


def _dense_attention(q, k, v, scale):
    s = jnp.einsum('bhqd,bhkd->bhqk', q, k).astype(jnp.float32) * scale
    p = jax.nn.softmax(s, axis=-1).astype(v.dtype)
    return jnp.einsum('bhqk,bhkd->bhqd', p, v)


def _short_conv(u, w):
    L = u.shape[1]
    up = jnp.pad(u, ((0, 0), (1, 1), (0, 0)))
    return up[:, :L] * w[0] + up[:, 1:L + 1] * w[1] + up[:, 2:] * w[2]


def _hyena_filters(L, w1, b1, freq, w2, b2, w3, log_decay):
    pos = jnp.arange(L, dtype=jnp.float32)
    t = pos / max(L - 1, 1)
    w = 2.0 * math.pi * pos / L
    f = jnp.linspace(1e-4, HY_BANDS - 1, HY_BANDS, dtype=jnp.float32)
    ang = w[:, None] * f[None, :]
    z = jnp.concatenate([t[:, None], jnp.cos(ang), -jnp.sin(ang)], -1).astype(w1.dtype)
    hid = jnp.sin(freq * (z @ w1 + b1))
    hid = jnp.sin(freq * (hid @ w2 + b2))
    h = hid @ w3
    decay = jnp.exp(-t[:, None] * jnp.exp(log_decay.astype(jnp.float32)))
    return h * decay.astype(h.dtype)


def _bidir_long_conv(z, h_fwd, h_bwd, skip):
    L = z.shape[1]
    filt = jnp.concatenate([h_fwd, jnp.zeros_like(h_fwd[:1]), h_bwd[:0:-1]], 0)
    zf = jnp.fft.rfft(z.astype(jnp.float32), n=2 * L, axis=1)
    hf = jnp.fft.rfft(filt.astype(jnp.float32), n=2 * L, axis=0)
    y = jnp.fft.irfft(zf * hf[None], n=2 * L, axis=1)[:, :L]
    return (y + z.astype(jnp.float32) * skip.astype(jnp.float32)).astype(z.dtype)


def _hyena(u, conv_w, w1, b1, freq, w2, b2, w3, log_decay, skip):
    L = u.shape[1]
    x1, x2, v = jnp.split(_short_conv(u, conv_w), 3, axis=-1)
    h = _hyena_filters(L, w1, b1, freq, w2, b2, w3, log_decay).reshape(L, HY_ORDER, 2, HY_CH)
    z = x1 * _bidir_long_conv(v, h[:, 0, 0], h[:, 0, 1], skip[0])
    return x2 * _bidir_long_conv(z, h[:, 1, 0], h[:, 1, 1], skip[1])


def _natten(q, k, v, kc, vc, rpb):
    B, L, _ = q.shape
    rows = L // GRID_W
    kr = min(NA_WIN_R, rows)
    shp = (B, rows, GRID_W, NA_HEADS, NA_HD)
    qg, kg, vg = q.reshape(shp), k.reshape(shp), v.reshape(shp)
    ncb = GRID_W // NA_QCB
    qcol = np.arange(GRID_W).reshape(ncb, NA_QCB)
    cs = np.clip(qcol - NA_WIN_C // 2, 0, GRID_W - NA_WIN_C)
    kb = np.clip(np.arange(ncb) * NA_QCB - NA_WIN_C // 2, 0, GRID_W - NA_KCB)
    kcol = kb[:, None] + np.arange(NA_KCB)[None, :]
    col_mask = (kcol[:, None, :] >= cs[..., None]) & (kcol[:, None, :] < cs[..., None] + NA_WIN_C)
    dc_idx = np.clip(kcol[:, None, :] - qcol[..., None] + NA_WIN_C - 1, 0, 2 * NA_WIN_C - 2)
    rpb_c = rpb[:, :, dc_idx]
    scale = NA_HD ** -0.5
    nlat = kr * NA_KCB

    def row_fn(r):
        rs = jnp.clip(r - kr // 2, 0, rows - kr)
        k_blk = lax.dynamic_slice_in_dim(kg, rs, kr, axis=1)[:, :, kcol]
        v_blk = lax.dynamic_slice_in_dim(vg, rs, kr, axis=1)[:, :, kcol]
        q_row = qg[:, r].reshape(B, ncb, NA_QCB, NA_HEADS, NA_HD)
        s_lat = jnp.einsum('bjqhd,brjkhd->bhjqrk', q_row, k_blk).astype(jnp.float32) * scale
        dr = rs + jnp.arange(kr) - r + NA_WIN_R - 1
        bias = jnp.take(rpb_c, dr, axis=1).transpose(0, 2, 3, 1, 4).astype(jnp.float32)
        s_lat = jnp.where(col_mask[:, :, None, :], s_lat + bias, -jnp.inf)
        s_lat = s_lat.reshape(B, NA_HEADS, ncb, NA_QCB, nlat)
        s_ctx = jnp.einsum('bjqhd,bkhd->bhjqk', q_row, kc).astype(jnp.float32) * scale
        p = jax.nn.softmax(jnp.concatenate([s_lat, s_ctx], -1), axis=-1).astype(v.dtype)
        p_lat = p[..., :nlat].reshape(B, NA_HEADS, ncb, NA_QCB, kr, NA_KCB)
        o = jnp.einsum('bhjqrk,brjkhd->bjqhd', p_lat, v_blk) + jnp.einsum('bhjqk,bkhd->bjqhd', p[..., nlat:], vc)
        return o.reshape(B, GRID_W, NA_HEADS * NA_HD)

    out = lax.map(row_fn, jnp.arange(rows))
    return out.transpose(1, 0, 2, 3).reshape(B, L, NA_HEADS * NA_HD)


def _mixer_ab(h, hc, need_ctx, w_in, w_out, conv_w, w1, b1, freq, w2, b2, w3, log_decay, skip, rpb):
    B, L, _ = h.shape
    Lc = hc.shape[1]
    n_hy = 3 * HY_CH
    na_w = NA_HEADS * NA_HD
    hy_args = (conv_w, w1, b1, freq, w2, b2, w3, log_decay, skip)
    u = h @ w_in
    q, k, v = jnp.split(u[..., n_hy:], 3, axis=-1)
    if need_ctx:
        uc = hc @ w_in
        kvc = uc[..., n_hy + na_w:]
    else:
        kvc = hc @ w_in[:, n_hy + na_w:]
    kc, vc = jnp.split(kvc, 2, axis=-1)
    kc = kc.reshape(B, Lc, NA_HEADS, NA_HD)
    vc = vc.reshape(B, Lc, NA_HEADS, NA_HD)
    y_hy = _hyena(u[..., :n_hy], *hy_args)
    y_na = _natten(q, k, v, kc, vc, rpb)
    yl = jnp.concatenate([y_hy, y_na], -1) @ w_out
    if not need_ctx:
        return yl, None
    qc = uc[..., n_hy:n_hy + na_w].reshape(B, Lc, NA_HEADS, NA_HD).transpose(0, 2, 1, 3)
    yc_na = _dense_attention(qc, kc.transpose(0, 2, 1, 3), vc.transpose(0, 2, 1, 3), NA_HD ** -0.5)
    yc_na = yc_na.transpose(0, 2, 1, 3).reshape(B, Lc, na_w)
    yc_hy = _hyena(uc[..., :n_hy], *hy_args)
    yc = jnp.concatenate([yc_hy, yc_na], -1) @ w_out
    return yl, yc


def _axial_rope_tables(L):
    t = jnp.arange(L, dtype=jnp.int32)
    rows = (t // GRID_W).astype(jnp.float32)
    cols = (t % GRID_W).astype(jnp.float32)
    half = MLA_ROPE // 2
    inv = ROPE_THETA ** (-jnp.arange(0, half, 2, dtype=jnp.float32) / half)
    ar = rows[:, None] * inv[None, :]
    ac = cols[:, None] * inv[None, :]
    ang = jnp.concatenate([ar, ar, ac, ac], -1)
    return jnp.cos(ang), jnp.sin(ang)


def _rotate(x, cos, sin):
    qd = MLA_ROPE // 4
    xf = x.astype(jnp.float32)
    a, b, c2, d = xf[..., :qd], xf[..., qd:2 * qd], xf[..., 2 * qd:3 * qd], xf[..., 3 * qd:]
    rot = jnp.concatenate([-b, a, -d, c2], -1)
    return (xf * cos + rot * sin).astype(x.dtype)


def _mla_q(cq, g, w_uq, rope):
    B, L, _ = cq.shape
    q = (_rms_norm(cq, g) @ w_uq).reshape(B, L, MLA_HEADS, MLA_NOPE + MLA_ROPE)
    q_nope, q_pe = q[..., :MLA_NOPE], q[..., MLA_NOPE:]
    if rope is not None:
        q_pe = _rotate(q_pe, *rope)
    return jnp.concatenate([q_nope, q_pe], -1).transpose(0, 2, 1, 3)


def _mla_kv(ckv, kpe, g, w_ukv, rope):
    B, L, _ = ckv.shape
    kv = (_rms_norm(ckv, g) @ w_ukv).reshape(B, L, MLA_HEADS, MLA_NOPE + MLA_V)
    k_nope, v = kv[..., :MLA_NOPE], kv[..., MLA_NOPE:]
    if rope is not None:
        kpe = _rotate(kpe, *rope)
    k_pe = jnp.broadcast_to(kpe[:, :, None, :], (B, L, MLA_HEADS, MLA_ROPE))
    k = jnp.concatenate([k_nope, k_pe], -1)
    return k.transpose(0, 2, 1, 3), v.transpose(0, 2, 1, 3)


def _mla_blocked_attention(q, k, v, kc, vc):
    B, H, L, dq = q.shape
    nb = L // Q_BLOCK
    scale = dq ** -0.5
    qb = q.reshape(B, H, nb, Q_BLOCK, dq).transpose(2, 0, 1, 3, 4)

    def block(qi):
        s = jnp.concatenate([jnp.einsum('bhqd,bhkd->bhqk', qi, k), jnp.einsum('bhqd,bhkd->bhqk', qi, kc)], -1)
        p = jax.nn.softmax(s.astype(jnp.float32) * scale, axis=-1).astype(v.dtype)
        return jnp.einsum('bhqk,bhkd->bhqd', p[..., :L], v) + jnp.einsum('bhqk,bhkd->bhqd', p[..., L:], vc)

    o = lax.map(block, qb)
    return o.transpose(1, 0, 3, 2, 4).reshape(B, L, H * MLA_V)


def _fnet(u, g, b):
    B, L, _ = u.shape
    ug = _layer_norm(u.reshape(B, L, FN_GROUPS, FN_GD), g.reshape(FN_GROUPS, FN_GD), b.reshape(FN_GROUPS, FN_GD))
    y = jnp.fft.fft2(ug.astype(jnp.float32), axes=(1, 3), norm='ortho').real
    return y.astype(u.dtype).reshape(B, L, FN_CH)


def _mixer_cd(h, hc, need_ctx, w_in, w_out, q_norm, w_uq, kv_norm, w_ukv, fn_g, fn_b):
    B, L, _ = h.shape
    Lc = hc.shape[1]
    o_kv = MLA_Q_RANK
    o_pe = o_kv + MLA_KV_RANK
    o_fn = o_pe + MLA_ROPE
    cos, sin = _axial_rope_tables(L)
    u = h @ w_in
    q = _mla_q(u[..., :o_kv], q_norm, w_uq, (cos[:, None, :], sin[:, None, :]))
    k, v = _mla_kv(u[..., o_kv:o_pe], u[..., o_pe:o_fn], kv_norm, w_ukv, (cos, sin))
    kvc = hc @ w_in[:, o_kv:o_fn]
    kc, vc = _mla_kv(kvc[..., :MLA_KV_RANK], kvc[..., MLA_KV_RANK:], kv_norm, w_ukv, None)
    y_mla = _mla_blocked_attention(q, k, v, kc, vc)
    y_fn = _fnet(u[..., o_fn:], fn_g, fn_b)
    yl = jnp.concatenate([y_mla, y_fn], -1) @ w_out
    if not need_ctx:
        return yl, None
    qc = _mla_q(hc @ w_in[:, :o_kv], q_norm, w_uq, None)
    yc_mla = _dense_attention(qc, kc, vc, (MLA_NOPE + MLA_ROPE) ** -0.5)
    yc_mla = yc_mla.transpose(0, 2, 1, 3).reshape(B, Lc, MLA_HEADS * MLA_V)
    yc_fn = _fnet(hc @ w_in[:, o_fn:], fn_g, fn_b)
    yc = jnp.concatenate([yc_mla, yc_fn], -1) @ w_out
    return yl, yc


def setup_inputs(seed: int = 0) -> dict:
    key = jax.random.key(seed)
    ks = iter(jax.random.split(key, 40))
    f32 = jnp.float32

    def nrm(shape, s):
        return jax.random.normal(next(ks), shape, f32) * s

    D = D_MODEL
    ab_in = 3 * HY_CH + 3 * NA_HEADS * NA_HD
    ab_out = HY_CH + NA_HEADS * NA_HD
    cd_in = MLA_Q_RANK + MLA_KV_RANK + MLA_ROPE + FN_CH
    cd_out = MLA_HEADS * MLA_V + FN_CH
    return {
        'x': nrm((BATCH, SEQ, D), 1.0),
        'c': nrm((BATCH, D), 1.0),
        'ctx': nrm((BATCH, CTX_LEN, D), 1.0),
        'c_ctx': nrm((D,), 1.0),
        'mod_w': nrm((DEPTH, D, 6 * D), 0.5 * D ** -0.5),
        'mod_b': nrm((DEPTH, 6 * D), 0.02),
        'ln_g': 1.0 + nrm((DEPTH, 2, D), 0.02),
        'ln_b': nrm((DEPTH, 2, D), 0.02),
        'mlp_w1': nrm((DEPTH, D, D_FF), D ** -0.5),
        'mlp_w2': nrm((DEPTH, D_FF, D), OUT_SCALE * D_FF ** -0.5),
        'ab_w_in': nrm((N_EVEN, D, ab_in), D ** -0.5),
        'ab_w_out': nrm((N_EVEN, ab_out, D), OUT_SCALE * ab_out ** -0.5),
        'hy_conv_w': nrm((N_EVEN, HY_SHORT, 3 * HY_CH), HY_SHORT ** -0.5),
        'hy_w1': nrm((N_EVEN, HY_EMB, HY_FFN), HY_EMB ** -0.5),
        'hy_b1': nrm((N_EVEN, HY_FFN), 0.1),
        'hy_freq': 1.0 + nrm((N_EVEN, HY_FFN), 0.1),
        'hy_w2': nrm((N_EVEN, HY_FFN, HY_FFN), HY_FFN ** -0.5),
        'hy_b2': nrm((N_EVEN, HY_FFN), 0.1),
        'hy_w3': nrm((N_EVEN, HY_FFN, HY_ORDER * 2 * HY_CH), 0.01),
        'hy_log_decay': jnp.log(jax.random.uniform(next(ks), (N_EVEN, HY_ORDER * 2 * HY_CH), f32, HY_DECAY_MIN, HY_DECAY_MAX)),
        'hy_skip': nrm((N_EVEN, HY_ORDER, HY_CH), 0.5),
        'na_rpb': nrm((N_EVEN, NA_HEADS, 2 * NA_WIN_R - 1, 2 * NA_WIN_C - 1), 0.02),
        'cd_w_in': nrm((N_ODD, D, cd_in), D ** -0.5),
        'cd_w_out': nrm((N_ODD, cd_out, D), OUT_SCALE * cd_out ** -0.5),
        'mla_q_norm': 1.0 + nrm((N_ODD, MLA_Q_RANK), 0.02),
        'mla_w_uq': nrm((N_ODD, MLA_Q_RANK, MLA_HEADS * (MLA_NOPE + MLA_ROPE)), MLA_Q_RANK ** -0.5),
        'mla_kv_norm': 1.0 + nrm((N_ODD, MLA_KV_RANK), 0.02),
        'mla_w_ukv': nrm((N_ODD, MLA_KV_RANK, MLA_HEADS * (MLA_NOPE + MLA_V)), MLA_KV_RANK ** -0.5),
        'fn_norm_g': 1.0 + nrm((N_ODD, FN_CH), 0.02),
        'fn_norm_b': nrm((N_ODD, FN_CH), 0.02),
    }


def reference(x, c, ctx, c_ctx, mod_w, mod_b, ln_g, ln_b, mlp_w1, mlp_w2,
              ab_w_in, ab_w_out, hy_conv_w, hy_w1, hy_b1, hy_freq, hy_w2, hy_b2, hy_w3, hy_log_decay, hy_skip, na_rpb,
              cd_w_in, cd_w_out, mla_q_norm, mla_w_uq, mla_kv_norm, mla_w_ukv, fn_norm_g, fn_norm_b):
    xl, xc = x, ctx
    for l in range(DEPTH):
        need_ctx = l < DEPTH - 1
        i = l // 2
        m_lat = jax.nn.silu(c) @ mod_w[l] + mod_b[l]
        m_ctx = jax.nn.silu(c_ctx) @ mod_w[l] + mod_b[l]
        sh1, sc1, g1, sh2, sc2, g2 = jnp.split(m_lat[:, None, :], 6, axis=-1)
        sh1c, sc1c, g1c, sh2c, sc2c, g2c = jnp.split(m_ctx, 6, axis=-1)
        h = _modulate(xl, sh1, sc1)
        hc = _modulate(xc, sh1c, sc1c)
        if l % 2 == 0:
            yl, yc = _mixer_ab(h, hc, need_ctx, ab_w_in[i], ab_w_out[i], hy_conv_w[i], hy_w1[i], hy_b1[i],
                               hy_freq[i], hy_w2[i], hy_b2[i], hy_w3[i], hy_log_decay[i], hy_skip[i], na_rpb[i])
        else:
            yl, yc = _mixer_cd(h, hc, need_ctx, cd_w_in[i], cd_w_out[i], mla_q_norm[i], mla_w_uq[i],
                               mla_kv_norm[i], mla_w_ukv[i], fn_norm_g[i], fn_norm_b[i])
        xl = _layer_norm(ALPHA * xl + g1 * yl, ln_g[l, 0], ln_b[l, 0])
        xl = _layer_norm(ALPHA * xl + g2 * _mlp(_modulate(xl, sh2, sc2), mlp_w1[l], mlp_w2[l]), ln_g[l, 1], ln_b[l, 1])
        if need_ctx:
            xc = _layer_norm(ALPHA * xc + g1c * yc, ln_g[l, 0], ln_b[l, 0])
            xc = _layer_norm(ALPHA * xc + g2c * _mlp(_modulate(xc, sh2c, sc2c), mlp_w1[l], mlp_w2[l]), ln_g[l, 1], ln_b[l, 1])
    return xl
```

```python
import functools
import math

import numpy as np
import jax
import jax.numpy as jnp
from jax import lax
from jax.experimental import pallas as pl
from jax.experimental.pallas import tpu as pltpu

F32 = jnp.float32
BF16 = jnp.bfloat16

D_MODEL = 1024
DEPTH = 2
GRID_W = 64
HY_CH = 512
HY_EMB = 33
HY_BANDS = (HY_EMB - 1) // 2
NA_HEADS = 8
NA_HD = 64
NA_WIN_R = 8
NA_WIN_C = 16
MLA_HEADS = 8
MLA_Q_RANK = 384
MLA_KV_RANK = 256
MLA_NOPE = 64
MLA_ROPE = 32
MLA_V = 96
ROPE_THETA = 10000.0
FN_CH = 256
FN_GROUPS = 4
FN_GD = FN_CH // FN_GROUPS
D_FF = 4 * D_MODEL
ALPHA = (2.0 * DEPTH) ** 0.25
LN_EPS = 1e-5

LANES = 128
ROW_BLK = 256
HEAD_PAD = 128
FFT_N2 = 128
VMEM_LIMIT = 56 * 1024 * 1024
NEG_BIG = -1e30


def _cparams(sem, vmem=VMEM_LIMIT):
    return pltpu.CompilerParams(dimension_semantics=sem, vmem_limit_bytes=vmem)


def _resident(shape):
    nd = len(shape)
    return pl.BlockSpec(shape, lambda *_: (0,) * nd)


def _norm_rows(x):
    mu = jnp.mean(x, axis=-1, keepdims=True)
    xc = x - mu
    var = jnp.mean(xc * xc, axis=-1, keepdims=True)
    return xc * lax.rsqrt(var + LN_EPS)


def _dot(a, b):
    return jnp.dot(a, b, preferred_element_type=F32)


def _dot_nt(a, b):
    return lax.dot_general(a, b, (((1,), (1,)), ((), ())), preferred_element_type=F32)


def _mod_kernel(c_ref, w_ref, b_ref, o_ref):
    c = c_ref[...]
    s = c * (1.0 / (1.0 + jnp.exp(-c)))
    o_ref[0] = jnp.dot(s, w_ref[0], preferred_element_type=F32,
                       precision=lax.Precision.HIGHEST) + b_ref[0]


def _mod_vectors(cc, mod_w, mod_b):
    depth, d, n = mod_w.shape
    nb = 1024
    return pl.pallas_call(
        _mod_kernel,
        grid=(depth, n // nb),
        in_specs=[pl.BlockSpec((8, d), lambda l, j: (0, 0)),
                  pl.BlockSpec((1, d, nb), lambda l, j: (l, 0, j)),
                  pl.BlockSpec((1, 1, nb), lambda l, j: (l, 0, j))],
        out_specs=pl.BlockSpec((1, 8, nb), lambda l, j: (l, 0, j)),
        out_shape=jax.ShapeDtypeStruct((depth, 8, n), F32),
        compiler_params=_cparams(("parallel", "parallel")),
        name="mod_vectors",
    )(cc, mod_w, mod_b.reshape(depth, 1, n))


def _front_ab_kernel(x_ref, mod_ref, w_ref, u_ref):
    m = mod_ref[0, 0]
    h = _norm_rows(x_ref[0]) * (1.0 + m[1:2]) + m[0:1]
    u_ref[0] = _dot(h.astype(BF16), w_ref[...]).astype(BF16)


def _front_ab(xt, modtab, w_in):
    b, lt, d = xt.shape
    n = w_in.shape[1]
    nlat = (lt // ROW_BLK) - 1
    return pl.pallas_call(
        _front_ab_kernel,
        grid=(b, lt // ROW_BLK),
        in_specs=[pl.BlockSpec((1, ROW_BLK, d), lambda i, j: (i, j, 0)),
                  pl.BlockSpec((1, 1, 6, d), lambda i, j: (i, j // nlat, 0, 0)),
                  _resident((d, n))],
        out_specs=pl.BlockSpec((1, ROW_BLK, n), lambda i, j: (i, j, 0)),
        out_shape=jax.ShapeDtypeStruct((b, lt, n), BF16),
        compiler_params=_cparams(("parallel", "parallel")),
        name="front_ab",
    )(xt, modtab, w_in)


def _hy_prep_kernel(cur_ref, prev_ref, next_ref, w_ref, x1_ref, x2_ref, v_ref, *, nblk):
    j = pl.program_id(1)
    cur = cur_ref[0].astype(F32)
    rows = cur.shape[0]
    has_prev = (j > 0).astype(F32)
    has_next = (j < nblk - 1).astype(F32)
    prev_row = prev_ref[0][7:8].astype(F32) * has_prev
    next_row = next_ref[0][0:1].astype(F32) * has_next
    rid = lax.broadcasted_iota(jnp.int32, (rows, 1), 0)
    up = jnp.where(rid == 0, prev_row, pltpu.roll(cur, 1, axis=0))
    dn = jnp.where(rid == rows - 1, next_row, pltpu.roll(cur, rows - 1, axis=0))
    w = w_ref[...]
    y = up * w[0:1] + cur * w[1:2] + dn * w[2:3]
    c = HY_CH
    x1_ref[0] = y[:, :c].astype(BF16)
    x2_ref[0] = y[:, c:2 * c].astype(BF16)
    v_ref[0] = y[:, 2 * c:].astype(BF16)


def _hy_prep(u, conv_w, blk0, nblk):
    b, lt, _ = u.shape
    n = 3 * HY_CH
    sub = ROW_BLK // 8
    last8 = lt // 8 - 1
    out = jax.ShapeDtypeStruct((b, nblk * ROW_BLK, HY_CH), BF16)
    ospec = pl.BlockSpec((1, ROW_BLK, HY_CH), lambda i, j: (i, j, 0))
    return pl.pallas_call(
        functools.partial(_hy_prep_kernel, nblk=nblk),
        grid=(b, nblk),
        in_specs=[pl.BlockSpec((1, ROW_BLK, n), lambda i, j: (i, blk0 + j, 0)),
                  pl.BlockSpec((1, 8, n), lambda i, j: (i, jnp.maximum((blk0 + j) * sub - 1, 0), 0)),
                  pl.BlockSpec((1, 8, n), lambda i, j: (i, jnp.minimum((blk0 + j + 1) * sub, last8), 0)),
                  _resident((3, n))],
        out_specs=[ospec, ospec, ospec],
        out_shape=[out, out, out],
        compiler_params=_cparams(("parallel", "parallel")),
        name="hy_prep",
    )(u, u, u, conv_w)


def _hy_filt_kernel(z_ref, w1_ref, b1_ref, fr_ref, w2_ref, b2_ref, w3_ref, ld_ref, o_ref):
    hi = lax.Precision.HIGHEST
    z = z_ref[...]
    fr = fr_ref[...]
    hid = jnp.sin(fr * (jnp.dot(z, w1_ref[...], preferred_element_type=F32, precision=hi) + b1_ref[...]))
    hid = jnp.sin(fr * (jnp.dot(hid, w2_ref[...], preferred_element_type=F32, precision=hi) + b2_ref[...]))
    h = jnp.dot(hid, w3_ref[...], preferred_element_type=F32, precision=hi)
    t = z[:, 0:1]
    o_ref[...] = h * jnp.exp(-t * jnp.exp(ld_ref[...]))


def _pad2(a, rows, cols):
    return jnp.pad(a, ((0, rows - a.shape[0]), (0, cols - a.shape[1])))


def _hy_filters(length, w1, b1, freq, w2, b2, w3, log_decay):
    pos = jnp.arange(length, dtype=F32)
    t = pos / max(length - 1, 1)
    w = 2.0 * math.pi * pos / length
    f = jnp.linspace(1e-4, HY_BANDS - 1, HY_BANDS, dtype=F32)
    ang = w[:, None] * f[None, :]
    z = jnp.concatenate([t[:, None], jnp.cos(ang), -jnp.sin(ang)], -1)
    z = _pad2(z, length, LANES)
    n = w3.shape[1]
    rb = min(length, 512)
    vec = lambda a: _pad2(a.reshape(1, -1), 1, LANES)
    return pl.pallas_call(
        _hy_filt_kernel,
        grid=(length // rb,),
        in_specs=[pl.BlockSpec((rb, LANES), lambda i: (i, 0)),
                  _resident((LANES, LANES)), _resident((1, LANES)), _resident((1, LANES)),
                  _resident((LANES, LANES)), _resident((1, LANES)),
                  _resident((LANES, n)), _resident((1, n))],
        out_specs=pl.BlockSpec((rb, n), lambda i: (i, 0)),
        out_shape=jax.ShapeDtypeStruct((length, n), F32),
        compiler_params=_cparams(("parallel",)),
        name="hy_filters",
    )(z, _pad2(w1, LANES, LANES), vec(b1), vec(freq), _pad2(w2, LANES, LANES), vec(b2),
      _pad2(w3, LANES, n), log_decay.reshape(1, n))


def _bidir_taps(h, skip, length):
    h4 = h.reshape(length, 2, 2, HY_CH)
    cols = []
    for o in range(2):
        hf = h4[:, o, 0].at[0].add(skip[o])
        hb = h4[:, o, 1]
        cols.append(jnp.concatenate([hf, jnp.zeros_like(hf[:1]), hb[:0:-1]], 0))
    return jnp.concatenate(cols, -1)


def _left_mm_kernel(m_ref, x_ref, o_ref):
    o_ref[0] = _dot(m_ref[...], x_ref[0]).astype(o_ref.dtype)


def _left_mm_gate_kernel(m_ref, x_ref, g_ref, o_ref):
    o_ref[0] = (_dot(m_ref[...], x_ref[0]) * g_ref[0].astype(F32)).astype(o_ref.dtype)


def _left_mm(mat, x, out_dtype, lane_blk, gate=None):
    g, k, n = x.shape
    m = mat.shape[0]
    lane_blk = min(lane_blk, n)
    xs = pl.BlockSpec((1, k, lane_blk), lambda i, j: (i, 0, j))
    os_ = pl.BlockSpec((1, m, lane_blk), lambda i, j: (i, 0, j))
    if gate is None:
        kern, ins, specs = _left_mm_kernel, (mat, x), [_resident((m, k)), xs]
    else:
        kern, ins, specs = _left_mm_gate_kernel, (mat, x, gate), [_resident((m, k)), xs, os_]
    return pl.pallas_call(
        kern, grid=(g, n // lane_blk), in_specs=specs, out_specs=os_,
        out_shape=jax.ShapeDtypeStruct((g, m, n), out_dtype),
        compiler_params=_cparams(("parallel", "parallel")),
        name="left_mm",
    )(*ins)


def _cs(num, den, rows, cols):
    ang = 2.0 * np.pi * np.outer(np.arange(rows), np.arange(cols)) * (num / den)
    return np.cos(ang), np.sin(ang)


def _twiddle(n1, n2):
    k1 = lax.broadcasted_iota(jnp.int32, (n1, n2, 1), 0)
    m2 = lax.broadcasted_iota(jnp.int32, (n1, n2, 1), 1)
    ang = (k1 * m2).astype(F32) * (2.0 * math.pi / (n1 * n2))
    return jnp.cos(ang), jnp.sin(ang)


def _slab_kernel(a_ref, c_ref, s_ref, mf_ref, mi_ref, h_ref, o_ref):
    half = FFT_N2
    ar = a_ref[0, 0, 0].astype(F32)
    ai = a_ref[0, 1, 0].astype(F32)
    c = c_ref[0]
    s = s_ref[0]
    t = jnp.concatenate([ar * c + ai * s, ai * c - ar * s], axis=0).astype(BF16)
    x = _dot(mf_ref[...], t)
    xr, xi = x[:half], x[half:]
    hr = h_ref[0, 0]
    hi = h_ref[1, 0]
    y = jnp.concatenate([xr * hr - xi * hi, xr * hi + xi * hr], axis=0).astype(BF16)
    bm = _dot(mi_ref[...], y)
    br, bi = bm[:half], bm[half:]
    o_ref[0, 0, 0] = (br * c - bi * s).astype(o_ref.dtype)
    o_ref[0, 1, 0] = (bi * c + br * s).astype(o_ref.dtype)


def _slab_filter_kernel(a_ref, c_ref, s_ref, mf_ref, o_ref):
    half = FFT_N2
    ar = a_ref[0, 0, 0].astype(F32)
    ai = a_ref[0, 1, 0].astype(F32)
    c = c_ref[0]
    s = s_ref[0]
    t = jnp.concatenate([ar * c + ai * s, ai * c - ar * s], axis=0).astype(BF16)
    x = _dot(mf_ref[...], t)
    o_ref[0, 0] = x[:half]
    o_ref[1, 0] = x[half:]


def _dft_mats():
    n = FFT_N2
    c, s = _cs(1, n, n, n)
    f1_half = np.concatenate([c[:, :n // 2], -s[:, :n // 2]], 0)
    f1_full = np.concatenate([c, -s], 0)
    m_fwd = np.block([[c, s], [-s, c]])
    m_inv = np.block([[c, -s], [s, c]])
    f1_inv = np.concatenate([c[:n // 2, :], -s[:n // 2, :]], 1) / (n * n)
    cast = lambda a: jnp.asarray(a, dtype=BF16)
    return cast(f1_half), cast(f1_full), cast(m_fwd), cast(m_inv), cast(f1_inv)


def _filter_spectrum(taps, f1_full, m_fwd, twc, tws):
    n = FFT_N2
    ch = taps.shape[1]
    a = _left_mm(f1_full, taps.astype(BF16).reshape(1, n, n * ch), BF16, 8192)
    a = a.reshape(1, 2, n, n, ch)
    cb = 512
    return pl.pallas_call(
        _slab_filter_kernel,
        grid=(n, ch // cb),
        in_specs=[pl.BlockSpec((1, 2, 1, n, cb), lambda k, j: (0, 0, k, 0, j)),
                  pl.BlockSpec((1, n, 1), lambda k, j: (k, 0, 0)),
                  pl.BlockSpec((1, n, 1), lambda k, j: (k, 0, 0)),
                  _resident((2 * n, 2 * n))],
        out_specs=pl.BlockSpec((2, 1, n, cb), lambda k, j: (0, k, 0, j)),
        out_shape=jax.ShapeDtypeStruct((2, n, n, ch), F32),
        compiler_params=_cparams(("parallel", "parallel")),
        name="filter_spectrum",
    )(a, twc, tws, m_fwd)


def _long_conv_gate(z, gate, spec, order, mats, twc, tws):
    f1_half, _, m_fwd, m_inv, f1_inv = mats
    b, length, c = z.shape
    n = FFT_N2
    n1 = length // n
    a = _left_mm(f1_half, z.reshape(b, n1, n * c), BF16, 4096)
    a = a.reshape(b, 2, n, n, c)
    slab = pl.BlockSpec((1, 2, 1, n, c), lambda k, i: (i, 0, k, 0, 0))
    bm = pl.pallas_call(
        _slab_kernel,
        grid=(n, b),
        in_specs=[slab,
                  pl.BlockSpec((1, n, 1), lambda k, i: (k, 0, 0)),
                  pl.BlockSpec((1, n, 1), lambda k, i: (k, 0, 0)),
                  _resident((2 * n, 2 * n)), _resident((2 * n, 2 * n)),
                  pl.BlockSpec((2, 1, n, c), lambda k, i: (0, k, 0, order))],
        out_specs=slab,
        out_shape=jax.ShapeDtypeStruct((b, 2, n, n, c), BF16),
        compiler_params=_cparams(("parallel", "parallel")),
        name="conv_slab",
    )(a, twc, tws, m_fwd, m_inv, spec)
    y = _left_mm(f1_inv, bm.reshape(b, 2 * n, n * c), BF16, 4096, gate=gate.reshape(b, n1, n * c))
    return y.reshape(b, length, c)


def _hy_ctx_kernel(v_ref, x1_ref, x2_ref, f_ref, fi_ref, h_ref, o_ref):
    nf = f_ref.shape[0] // 2
    zin = v_ref[0]
    gates = (x1_ref, x2_ref)
    for o in range(2):
        x = _dot(f_ref[...], zin)
        xr, xi = x[:nf], x[nf:]
        hr = h_ref[o, :nf]
        hi = h_ref[o, nf:]
        y = jnp.concatenate([xr * hr - xi * hi, xr * hi + xi * hr], axis=0).astype(BF16)
        zin = (_dot(fi_ref[...], y) * gates[o][0].astype(F32)).astype(BF16)
    o_ref[0] = zin


def _hy_ctx(v, x1, x2, taps):
    b, lc, c = v.shape
    nf = 2 * lc
    cm, sm = _cs(1, nf, nf, nf)
    fwd = jnp.asarray(np.concatenate([cm[:, :lc], -sm[:, :lc]], 0), dtype=BF16)
    fwd_full = jnp.asarray(np.concatenate([cm, -sm], 0), dtype=BF16)
    inv = jnp.asarray(np.concatenate([cm[:lc, :], -sm[:lc, :]], 1) / nf, dtype=BF16)
    spec = _left_mm(fwd_full, taps.astype(BF16).reshape(1, nf, 2 * c), F32, 2 * c)
    spec = spec.reshape(2 * nf, 2, c).transpose(1, 0, 2)
    blk = pl.BlockSpec((1, lc, c), lambda i: (i, 0, 0))
    return pl.pallas_call(
        _hy_ctx_kernel,
        grid=(b,),
        in_specs=[blk, blk, blk, _resident((2 * nf, lc)), _resident((lc, 2 * nf)),
                  _resident((2, 2 * nf, c))],
        out_specs=blk,
        out_shape=jax.ShapeDtypeStruct((b, lc, c), BF16),
        compiler_params=_cparams(("parallel",)),
        name="hy_ctx",
    )(v, x1, x2, fwd, inv, spec)


def _pair_rows(q2):
    lane = lax.broadcasted_iota(jnp.int32, q2.shape, 1)
    zero = jnp.zeros_like(q2)
    return jnp.concatenate([jnp.where(lane < NA_HD, q2, zero), jnp.where(lane >= NA_HD, q2, zero)], axis=0)


def _unpair_rows(o):
    r = o.shape[0] // 2
    lane = lax.broadcasted_iota(jnp.int32, (r, o.shape[1]), 1)
    return jnp.where(lane < NA_HD, o[:r], o[r:])


def _pair_softmax_pv(scores, values):
    m = scores[0].max(axis=-1, keepdims=True)
    for s in scores[1:]:
        m = jnp.maximum(m, s.max(axis=-1, keepdims=True))
    den = None
    acc = None
    for s, v in zip(scores, values):
        p = jnp.exp(s - m)
        d = p.sum(axis=-1, keepdims=True)
        a = _dot(p.astype(BF16), v)
        den = d if den is None else den + d
        acc = a if acc is None else acc + a
    return acc / den


def _natten_kernel(q_ref, k0, k1, k2, k3, v0, v1, v2, v3, kc_ref, vc_ref, bias_ref, o_ref,
                   kwin, vwin, *, rows):
    g = pl.program_id(1)
    rb = 4 * GRID_W
    for i, (kr, vr) in enumerate(((k0, v0), (k1, v1), (k2, v2), (k3, v3))):
        kwin[i * rb:(i + 1) * rb, :] = kr[0]
        vwin[i * rb:(i + 1) * rb, :] = vr[0]
    base = 4 * jnp.clip(2 * g - 1, 0, rows // 4 - 4)
    nwin = NA_WIN_R * GRID_W
    qscale = jnp.asarray(NA_HD ** -0.5, BF16)

    def row_body(rr, carry):
        r = 8 * g + rr
        rs = jnp.clip(r - NA_WIN_R // 2, 0, rows - NA_WIN_R)
        st = pl.multiple_of((rs - base) * GRID_W, GRID_W)
        d0 = rs - r + NA_WIN_R - 1
        qo = pl.multiple_of(rr * GRID_W, GRID_W)
        for p in range(NA_HEADS // 2):
            ls = slice(p * LANES, (p + 1) * LANES)
            qp = _pair_rows(q_ref[0, pl.ds(qo, GRID_W), ls] * qscale)
            s_lat = _dot_nt(qp, kwin[pl.ds(st, nwin), ls]) + bias_ref[d0, p].astype(F32)
            s_ctx = _dot_nt(qp, kc_ref[0, :, ls])
            o = _pair_softmax_pv([s_lat, s_ctx], [vwin[pl.ds(st, nwin), ls], vc_ref[0, :, ls]])
            o_ref[0, pl.ds(qo, GRID_W), ls] = _unpair_rows(o).astype(o_ref.dtype)
        return carry

    lax.fori_loop(0, 8, row_body, 0)


def _natten_bias(rpb):
    c = np.arange(GRID_W)[:, None]
    kc = np.arange(GRID_W)[None, :]
    cs = np.clip(c - NA_WIN_C // 2, 0, GRID_W - NA_WIN_C)
    valid = (kc >= cs) & (kc < cs + NA_WIN_C)
    dc = np.clip(kc - c + NA_WIN_C - 1, 0, 2 * NA_WIN_C - 2)
    tb = jnp.where(valid[None, None], rpb[:, :, dc], NEG_BIG)
    slabs = []
    for d0 in range(NA_WIN_R):
        s = tb[:, d0:d0 + NA_WIN_R]
        s = s.transpose(0, 2, 1, 3).reshape(NA_HEADS, GRID_W, NA_WIN_R * GRID_W)
        slabs.append(s.reshape(NA_HEADS // 2, 2 * GRID_W, NA_WIN_R * GRID_W))
    return jnp.stack(slabs).astype(BF16)


def _natten(u, bias, length):
    b, lt, _ = u.shape
    c = NA_HEADS * NA_HD
    rows = length // GRID_W
    qb, kb, vb = 3, 4, 5
    rb = 4 * GRID_W
    nkb = length // rb
    qrows = 8 * GRID_W
    ctx_blk = length // ROW_BLK
    lc = lt - length

    def kv_spec(col, off):
        return pl.BlockSpec((1, rb, c), lambda i, g: (i, jnp.clip(2 * g - 1, 0, nkb - 4) + off, col))

    return pl.pallas_call(
        functools.partial(_natten_kernel, rows=rows),
        grid=(b, rows // 8),
        in_specs=[pl.BlockSpec((1, qrows, c), lambda i, g: (i, g, qb))]
                 + [kv_spec(kb, o) for o in range(4)] + [kv_spec(vb, o) for o in range(4)]
                 + [pl.BlockSpec((1, lc, c), lambda i, g: (i, ctx_blk, kb)),
                    pl.BlockSpec((1, lc, c), lambda i, g: (i, ctx_blk, vb)),
                    _resident(bias.shape)],
        out_specs=pl.BlockSpec((1, qrows, c), lambda i, g: (i, g, 0)),
        out_shape=jax.ShapeDtypeStruct((b, length, c), BF16),
        scratch_shapes=[pltpu.VMEM((4 * rb, c), BF16), pltpu.VMEM((4 * rb, c), BF16)],
        compiler_params=_cparams(("parallel", "parallel")),
        name="natten",
    )(u, *([u] * 8), u, u, bias)


def _ctx_attn_kernel(q_ref, k_ref, v_ref, o_ref):
    qscale = jnp.asarray(NA_HD ** -0.5, BF16)
    for p in range(NA_HEADS // 2):
        ls = slice(p * LANES, (p + 1) * LANES)
        qp = _pair_rows(q_ref[0, :, ls] * qscale)
        o = _pair_softmax_pv([_dot_nt(qp, k_ref[0, :, ls])], [v_ref[0, :, ls]])
        o_ref[0, :, ls] = _unpair_rows(o).astype(o_ref.dtype)


def _ctx_attn(u, length):
    b, lt, _ = u.shape
    c = NA_HEADS * NA_HD
    lc = lt - length
    blk = length // lc
    spec = lambda col: pl.BlockSpec((1, lc, c), lambda i: (i, blk, col))
    return pl.pallas_call(
        _ctx_attn_kernel,
        grid=(b,),
        in_specs=[spec(3), spec(4), spec(5)],
        out_specs=pl.BlockSpec((1, lc, c), lambda i: (i, 0, 0)),
        out_shape=jax.ShapeDtypeStruct((b, lc, c), BF16),
        compiler_params=_cparams(("parallel",)),
        name="ctx_attn",
    )(u, u, u)


def _post_kernel(*refs, has_ctx, ctx_blk):
    if has_ctx:
        x_ref, mod_ref, ya_ref, yb_ref, yac_ref, ybc_ref, wa_ref, wb_ref, lng_ref, lnb_ref, w1_ref, w2_ref, o_ref = refs
    else:
        x_ref, mod_ref, ya_ref, yb_ref, wa_ref, wb_ref, lng_ref, lnb_ref, w1_ref, w2_ref, o_ref = refs
    x = x_ref[0]
    ya = ya_ref[0]
    yb = yb_ref[0]
    if has_ctx:
        is_ctx = (jnp.zeros((ROW_BLK, 1), jnp.int32) + pl.program_id(1)) == ctx_blk
        ya = jnp.where(is_ctx, yac_ref[0], ya)
        yb = jnp.where(is_ctx, ybc_ref[0], yb)
    y = _dot(ya, wa_ref[...]) + _dot(yb, wb_ref[...])
    m = mod_ref[0, 0]
    lng = lng_ref[...]
    lnb = lnb_ref[...]
    x1 = _norm_rows(ALPHA * x + m[2:3] * y) * lng[0:1] + lnb[0:1]
    h = (_norm_rows(x1) * (1.0 + m[4:5]) + m[3:4]).astype(BF16)
    ff = w1_ref.shape[1]
    step = 1024
    acc = jnp.zeros_like(x)
    for c in range(ff // step):
        a = _dot(h, w1_ref[:, c * step:(c + 1) * step])
        a = jnp.maximum(a, 0.0)
        acc = acc + _dot((a * a).astype(BF16), w2_ref[c * step:(c + 1) * step, :])
    o_ref[0] = _norm_rows(ALPHA * x1 + m[5:6] * acc) * lng[1:2] + lnb[1:2]


def _post(xt, modtab, ya, yb, yac, ybc, wa, wb, lng, lnb, w1, w2, length):
    b, lt, d = xt.shape
    has_ctx = yac is not None
    nlat = length // ROW_BLK
    nblk = nlat + (1 if has_ctx else 0)
    ka, kb = ya.shape[-1], yb.shape[-1]
    lat = lambda k: pl.BlockSpec((1, ROW_BLK, k), lambda i, j: (i, jnp.minimum(j, nlat - 1), 0))
    ctx = lambda k: pl.BlockSpec((1, ROW_BLK, k), lambda i, j: (i, 0, 0))
    in_specs = [pl.BlockSpec((1, ROW_BLK, d), lambda i, j: (i, j, 0)),
                pl.BlockSpec((1, 1, 6, d), lambda i, j: (i, j // nlat, 0, 0)),
                lat(ka), lat(kb)]
    args = [xt, modtab, ya, yb]
    if has_ctx:
        in_specs += [ctx(ka), ctx(kb)]
        args += [yac, ybc]
    in_specs += [_resident(wa.shape), _resident(wb.shape), _resident(lng.shape), _resident(lnb.shape),
                 _resident(w1.shape), _resident(w2.shape)]
    args += [wa, wb, lng, lnb, w1, w2]
    return pl.pallas_call(
        functools.partial(_post_kernel, has_ctx=has_ctx, ctx_blk=nlat),
        grid=(b, nblk),
        in_specs=in_specs,
        out_specs=pl.BlockSpec((1, ROW_BLK, d), lambda i, j: (i, j, 0)),
        out_shape=jax.ShapeDtypeStruct((b, nblk * ROW_BLK, d), F32),
        compiler_params=_cparams(("parallel", "parallel")),
        name="post_mixer",
    )(*args)


def _rope(x, cos, sinl, sinr):
    reps = x.shape[1] // LANES
    tile = lambda t: jnp.concatenate([t] * reps, axis=1)
    n = x.shape[1]
    quarter = MLA_ROPE // 4
    return (x * tile(cos) + pltpu.roll(x, n - quarter, axis=1) * tile(sinl)
            + pltpu.roll(x, quarter, axis=1) * tile(sinr))


def _front_cd_kernel(x_ref, mod_ref, w_ref, qn_ref, kvn_ref, wuq_ref, wuk_ref, wuv_ref, epe_ref,
                     fng_ref, fnb_ref, cbd_ref, sbd_ref,
                     cq_ref, slq_ref, srq_ref, ck_ref, slk_ref, srk_ref,
                     q_ref, k_ref, v_ref, p_ref, qf_ref):
    m = mod_ref[0, 0]
    h = _norm_rows(x_ref[0]) * (1.0 + m[1:2]) + m[0:1]
    u = _dot(h.astype(BF16), w_ref[...])
    o_kv = MLA_Q_RANK
    o_fn = o_kv + MLA_KV_RANK
    o_pe = o_fn + FN_CH

    def rms(x, g):
        return x * lax.rsqrt(jnp.mean(x * x, axis=-1, keepdims=True) + LN_EPS) * g

    cq = rms(u[:, :o_kv], qn_ref[...]).astype(BF16)
    q = _dot(cq, wuq_ref[...])
    q_ref[0] = _rope(q, cq_ref[...], slq_ref[...], srq_ref[...]).astype(BF16)

    ckv = rms(u[:, o_kv:o_fn], kvn_ref[...]).astype(BF16)
    kpe = _dot(u[:, o_pe:].astype(BF16), epe_ref[...])
    k = _dot(ckv, wuk_ref[...]) + _rope(kpe, ck_ref[...], slk_ref[...], srk_ref[...])
    k_ref[0] = k.astype(BF16)
    v_ref[0] = _dot(ckv, wuv_ref[...]).astype(BF16)

    uf = u[:, o_fn:o_pe]
    lane = lax.broadcasted_iota(jnp.int32, uf.shape, 1)
    mean = jnp.zeros_like(uf)
    for g in range(FN_GROUPS):
        sel = (lane >= g * FN_GD) & (lane < (g + 1) * FN_GD)
        mg = jnp.sum(jnp.where(sel, uf, 0.0), axis=-1, keepdims=True) * (1.0 / FN_GD)
        mean = jnp.where(sel, mg, mean)
    uc = uf - mean
    var = jnp.zeros_like(uf)
    for g in range(FN_GROUPS):
        sel = (lane >= g * FN_GD) & (lane < (g + 1) * FN_GD)
        vg = jnp.sum(jnp.where(sel, uc * uc, 0.0), axis=-1, keepdims=True) * (1.0 / FN_GD)
        var = jnp.where(sel, vg, var)
    ug = (uc * lax.rsqrt(var + LN_EPS) * fng_ref[...] + fnb_ref[...]).astype(BF16)
    p_ref[0] = _dot(ug, cbd_ref[...]).astype(BF16)
    qf_ref[0] = _dot(ug, sbd_ref[...]).astype(BF16)


def _rope_tables(length, lt, scale):
    t = jnp.arange(lt, dtype=jnp.int32)
    rows = (t // GRID_W).astype(F32)
    cols = (t % GRID_W).astype(F32)
    half = MLA_ROPE // 2
    inv = ROPE_THETA ** (-jnp.arange(0, half, 2, dtype=F32) / half)
    ar = rows[:, None] * inv[None, :]
    ac = cols[:, None] * inv[None, :]
    ang = jnp.concatenate([ar, ar, ac, ac], -1)
    is_lat = (t < length)[:, None]
    cos = jnp.where(is_lat, jnp.cos(ang), 1.0)
    sin = jnp.where(is_lat, jnp.sin(ang), 0.0)
    qd = MLA_ROPE // 4
    ones = jnp.ones((lt, MLA_NOPE), F32)
    zeros = jnp.zeros((lt, MLA_NOPE), F32)
    tail1 = jnp.ones((lt, HEAD_PAD - MLA_NOPE - MLA_ROPE), F32)
    tail0 = jnp.zeros((lt, HEAD_PAD - MLA_NOPE - MLA_ROPE), F32)
    z8 = jnp.zeros((lt, qd), F32)
    c = jnp.concatenate([ones, cos, tail1], -1)
    sl = jnp.concatenate([zeros, -sin[:, :qd], z8, -sin[:, 2 * qd:3 * qd], z8, tail0], -1)
    sr = jnp.concatenate([zeros, z8, sin[:, qd:2 * qd], z8, sin[:, 3 * qd:], tail0], -1)
    return c * scale, sl * scale, sr * scale


def _head_slots(w, per_head, take_from, take_n):
    k = w.shape[0]
    w3 = w.reshape(k, MLA_HEADS, per_head)[:, :, take_from:take_from + take_n]
    w3 = jnp.pad(w3, ((0, 0), (0, 0), (0, HEAD_PAD - take_n)))
    return w3.reshape(k, MLA_HEADS * HEAD_PAD)


def _front_cd(xt, modtab, w_in, q_norm, w_uq, kv_norm, w_ukv, fn_g, fn_b, length):
    b, lt, d = xt.shape
    nlat = length // ROW_BLK
    o_kv = MLA_Q_RANK
    o_pe = o_kv + MLA_KV_RANK
    o_fn = o_pe + MLA_ROPE
    hw = MLA_HEADS * HEAD_PAD
    w_perm = jnp.concatenate([w_in[:, :o_pe], w_in[:, o_fn:], w_in[:, o_pe:o_fn],
                              jnp.zeros((d, LANES - MLA_ROPE), w_in.dtype)], -1).astype(BF16)
    wuq = _head_slots(w_uq, MLA_NOPE + MLA_ROPE, 0, MLA_NOPE + MLA_ROPE).astype(BF16)
    wuk = _head_slots(w_ukv, MLA_NOPE + MLA_V, 0, MLA_NOPE).astype(BF16)
    wuv = _head_slots(w_ukv, MLA_NOPE + MLA_V, MLA_NOPE, MLA_V).astype(BF16)
    epe = np.zeros((LANES, hw), np.float32)
    for hd in range(MLA_HEADS):
        for i in range(MLA_ROPE):
            epe[i, hd * HEAD_PAD + MLA_NOPE + i] = 1.0
    epe = jnp.asarray(epe, dtype=BF16)
    cm, sm = _cs(1, FN_GD, FN_GD, FN_GD)
    eye = np.eye(FN_GROUPS)
    cbd = jnp.asarray(np.kron(eye, cm), dtype=BF16)
    sbd = jnp.asarray(np.kron(eye, -sm), dtype=BF16)
    qtab = _rope_tables(length, lt, (MLA_NOPE + MLA_ROPE) ** -0.5)
    ktab = _rope_tables(length, lt, 1.0)
    row = lambda n: pl.BlockSpec((1, ROW_BLK, n), lambda i, j: (i, j, 0))
    tab = pl.BlockSpec((ROW_BLK, HEAD_PAD), lambda i, j: (j, 0))
    out = lambda n: jax.ShapeDtypeStruct((b, lt, n), BF16)
    return pl.pallas_call(
        _front_cd_kernel,
        grid=(b, lt // ROW_BLK),
        in_specs=[row(d), pl.BlockSpec((1, 1, 6, d), lambda i, j: (i, j // nlat, 0, 0)),
                  _resident(w_perm.shape), _resident((1, MLA_Q_RANK)), _resident((1, MLA_KV_RANK)),
                  _resident(wuq.shape), _resident(wuk.shape), _resident(wuv.shape), _resident(epe.shape),
                  _resident((1, FN_CH)), _resident((1, FN_CH)), _resident(cbd.shape), _resident(sbd.shape),
                  tab, tab, tab, tab, tab, tab],
        out_specs=[row(hw), row(hw), row(hw), row(FN_CH), row(FN_CH)],
        out_shape=[out(hw), out(hw), out(hw), out(FN_CH), out(FN_CH)],
        compiler_params=_cparams(("parallel", "parallel")),
        name="front_cd",
    )(xt, modtab, w_perm, q_norm.reshape(1, -1), kv_norm.reshape(1, -1), wuq, wuk, wuv, epe,
      fn_g.reshape(1, -1), fn_b.reshape(1, -1), cbd, sbd, *qtab, *ktab)


def _flash_kernel(q_ref, k_ref, v_ref, o_ref, *, chunk):
    q = q_ref[0]
    tq = q.shape[0]
    nchunk = k_ref.shape[1] // chunk

    def body(c, carry):
        m, l, acc = carry
        ks = pl.multiple_of(c * chunk, chunk)
        s = _dot_nt(q, k_ref[0, pl.ds(ks, chunk), :])
        m_new = jnp.maximum(m, s.max(axis=-1, keepdims=True))
        a = jnp.exp(m - m_new)
        p = jnp.exp(s - m_new)
        l = l * a + p.sum(axis=-1, keepdims=True)
        acc = acc * a + _dot(p.astype(BF16), v_ref[0, pl.ds(ks, chunk), :])
        return m_new, l, acc

    init = (jnp.full((tq, 1), -jnp.inf, F32), jnp.zeros((tq, 1), F32), jnp.zeros((tq, HEAD_PAD), F32))
    _, l, acc = lax.fori_loop(0, nchunk, body, init)
    o_ref[0] = (acc / l).astype(o_ref.dtype)


def _mla_attention(q, k, v, length):
    b, lt, hw = q.shape
    heads = hw // HEAD_PAD
    tq = 256
    chunk = 1408
    assert lt % chunk == 0
    kv = pl.BlockSpec((1, lt, HEAD_PAD), lambda i, h, j: (i, 0, h))
    qs = pl.BlockSpec((1, tq, HEAD_PAD), lambda i, h, j: (i, j, h))
    return pl.pallas_call(
        functools.partial(_flash_kernel, chunk=chunk),
        grid=(b, heads, length // tq),
        in_specs=[qs, kv, kv],
        out_specs=qs,
        out_shape=jax.ShapeDtypeStruct((b, length, hw), BF16),
        compiler_params=_cparams(("parallel", "parallel", "parallel")),
        name="mla_attention",
    )(q, k, v)


def _fn1_kernel(m_ref, zr_ref, zi_ref, o_ref):
    n = zr_ref.shape[1]
    a = _dot(m_ref[...], jnp.concatenate([zr_ref[0], zi_ref[0]], axis=0))
    o_ref[0, 0] = a[:n].astype(o_ref.dtype)
    o_ref[0, 1] = a[n:].astype(o_ref.dtype)


def _fn2_kernel(a_ref, c_ref, s_ref, m_ref, o_ref):
    for i in range(a_ref.shape[2]):
        ar = a_ref[0, 0, i].astype(F32)
        ai = a_ref[0, 1, i].astype(F32)
        c = c_ref[i]
        s = s_ref[i]
        t = jnp.concatenate([ar * c + ai * s, ai * c - ar * s], axis=0).astype(BF16)
        o_ref[0, i] = _dot(m_ref[...], t).astype(o_ref.dtype)


def _fnet(p, qn):
    b, length, c = p.shape
    n1 = 128
    n2 = length // n1
    cm, sm = _cs(1, n1, n1, n1)
    m1 = jnp.asarray(np.block([[cm, sm], [-sm, cm]]), dtype=BF16)
    c2, s2 = _cs(1, n2, n2, n2)
    m2 = jnp.asarray(np.concatenate([c2, s2], 1) / math.sqrt(length * FN_GD), dtype=BF16)
    twc, tws = _twiddle(n1, n2)
    lane_blk = 2048
    zs = pl.BlockSpec((1, n1, lane_blk), lambda i, j: (i, 0, j))
    a = pl.pallas_call(
        _fn1_kernel,
        grid=(b, n2 * c // lane_blk),
        in_specs=[_resident(m1.shape), zs, zs],
        out_specs=pl.BlockSpec((1, 2, n1, lane_blk), lambda i, j: (i, 0, 0, j)),
        out_shape=jax.ShapeDtypeStruct((b, 2, n1, n2 * c), BF16),
        compiler_params=_cparams(("parallel", "parallel")),
        name="fnet_stage1",
    )(m1, p.reshape(b, n1, n2 * c), qn.reshape(b, n1, n2 * c))
    kb = 8
    y = pl.pallas_call(
        _fn2_kernel,
        grid=(b, n1 // kb),
        in_specs=[pl.BlockSpec((1, 2, kb, n2, c), lambda i, j: (i, 0, j, 0, 0)),
                  pl.BlockSpec((kb, n2, 1), lambda i, j: (j, 0, 0)),
                  pl.BlockSpec((kb, n2, 1), lambda i, j: (j, 0, 0)),
                  _resident(m2.shape)],
        out_specs=pl.BlockSpec((1, kb, n2, c), lambda i, j: (i, j, 0, 0)),
        out_shape=jax.ShapeDtypeStruct((b, n1, n2, c), BF16),
        compiler_params=_cparams(("parallel", "parallel")),
        name="fnet_stage2",
    )(a.reshape(b, 2, n1, n2, c), twc, tws, m2)
    return y.transpose(0, 2, 1, 3).reshape(b, length, c)


def kernel(x, c, ctx, c_ctx, mod_w, mod_b, ln_g, ln_b, mlp_w1, mlp_w2,
           ab_w_in, ab_w_out, hy_conv_w, hy_w1, hy_b1, hy_freq, hy_w2, hy_b2, hy_w3, hy_log_decay, hy_skip, na_rpb,
           cd_w_in, cd_w_out, mla_q_norm, mla_w_uq, mla_kv_norm, mla_w_ukv, fn_norm_g, fn_norm_b):
    b, length, d = x.shape
    lc = ctx.shape[1]

    xt = jnp.concatenate([x, ctx], axis=1)
    cc = jnp.concatenate([c, c_ctx[None], jnp.zeros((8 - b - 1, d), F32)], 0)
    mods = _mod_vectors(cc, mod_w, mod_b).reshape(DEPTH, 8, 6, d)
    modtab = [jnp.stack([mods[l, :b], jnp.broadcast_to(mods[l, b], (b, 6, d))], axis=1) for l in range(DEPTH)]

    u = _front_ab(xt, modtab[0], ab_w_in[0].astype(BF16))
    nlat = length // ROW_BLK
    x1, x2, v = _hy_prep(u, hy_conv_w[0], 0, nlat)
    x1c, x2c, vc = _hy_prep(u, hy_conv_w[0], nlat, lc // ROW_BLK)
    fargs = (hy_w1[0], hy_b1[0], hy_freq[0], hy_w2[0], hy_b2[0], hy_w3[0], hy_log_decay[0])
    taps = _bidir_taps(_hy_filters(length, *fargs), hy_skip[0], length)
    taps_c = _bidir_taps(_hy_filters(lc, *fargs), hy_skip[0], lc)
    mats = _dft_mats()
    twc, tws = _twiddle(FFT_N2, FFT_N2)
    spec = _filter_spectrum(taps, mats[1], mats[2], twc, tws)
    z = _long_conv_gate(v, x1, spec, 0, mats, twc, tws)
    y_hy = _long_conv_gate(z, x2, spec, 1, mats, twc, tws)
    y_hy_c = _hy_ctx(vc, x1c, x2c, taps_c)
    y_na = _natten(u, _natten_bias(na_rpb[0]), length)
    y_na_c = _ctx_attn(u, length)
    w_out = ab_w_out[0].astype(BF16)
    xt = _post(xt, modtab[0], y_hy, y_na, y_hy_c, y_na_c, w_out[:HY_CH], w_out[HY_CH:],
               ln_g[0], ln_b[0], mlp_w1[0].astype(BF16), mlp_w2[0].astype(BF16), length)

    q, k, vv, p, qn = _front_cd(xt, modtab[1], cd_w_in[0], mla_q_norm[0], mla_w_uq[0], mla_kv_norm[0],
                                mla_w_ukv[0], fn_norm_g[0], fn_norm_b[0], length)
    o = _mla_attention(q, k, vv, length)
    y_fn = _fnet(p[:, :length], qn[:, :length])
    w_out = cd_w_out[0]
    n_mla = MLA_HEADS * MLA_V
    wa = jnp.pad(w_out[:n_mla].reshape(MLA_HEADS, MLA_V, d), ((0, 0), (0, HEAD_PAD - MLA_V), (0, 0)))
    wa = wa.reshape(MLA_HEADS * HEAD_PAD, d).astype(BF16)
    return _post(xt, modtab[1], o, y_fn, None, None, wa, w_out[n_mla:].astype(BF16),
                 ln_g[1], ln_b[1], mlp_w1[1].astype(BF16), mlp_w2[1].astype(BF16), length)
```

```python
import functools
import math

import numpy as np
import jax
import jax.numpy as jnp
from jax import lax
from jax.experimental import pallas as pl
from jax.experimental.pallas import tpu as pltpu

F32 = jnp.float32
BF16 = jnp.bfloat16

D_MODEL = 1024
DEPTH = 2
GRID_W = 64
HY_CH = 512
HY_EMB = 33
HY_BANDS = (HY_EMB - 1) // 2
NA_HEADS = 8
NA_HD = 64
NA_WIN_R = 8
NA_WIN_C = 16
MLA_HEADS = 8
MLA_Q_RANK = 384
MLA_KV_RANK = 256
MLA_NOPE = 64
MLA_ROPE = 32
MLA_V = 96
ROPE_THETA = 10000.0
FN_CH = 256
FN_GROUPS = 4
FN_GD = FN_CH // FN_GROUPS
D_FF = 4 * D_MODEL
ALPHA = (2.0 * DEPTH) ** 0.25
LN_EPS = 1e-5

LANES = 128
ROW_BLK = 256
HEAD_PAD = 128
FFT_N2 = 128
VMEM_LIMIT = 56 * 1024 * 1024
NEG_BIG = -1e30


def _cparams(sem, vmem=VMEM_LIMIT):
    return pltpu.CompilerParams(dimension_semantics=sem, vmem_limit_bytes=vmem)


def _resident(shape):
    nd = len(shape)
    return pl.BlockSpec(shape, lambda *_: (0,) * nd)


def _norm_rows(x):
    mu = jnp.mean(x, axis=-1, keepdims=True)
    xc = x - mu
    var = jnp.mean(xc * xc, axis=-1, keepdims=True)
    return xc * lax.rsqrt(var + LN_EPS)


def _dot(a, b):
    return jnp.dot(a, b, preferred_element_type=F32)


def _dot_nt(a, b):
    return lax.dot_general(a, b, (((1,), (1,)), ((), ())), preferred_element_type=F32)


def _mod_kernel(c_ref, w_ref, b_ref, o_ref):
    c = c_ref[...]
    s = c * (1.0 / (1.0 + jnp.exp(-c)))
    o_ref[0] = jnp.dot(s, w_ref[0], preferred_element_type=F32,
                       precision=lax.Precision.HIGHEST) + b_ref[0]


def _mod_vectors(cc, mod_w, mod_b):
    depth, d, n = mod_w.shape
    nb = 1024
    return pl.pallas_call(
        _mod_kernel,
        grid=(depth, n // nb),
        in_specs=[pl.BlockSpec((8, d), lambda l, j: (0, 0)),
                  pl.BlockSpec((1, d, nb), lambda l, j: (l, 0, j)),
                  pl.BlockSpec((1, 1, nb), lambda l, j: (l, 0, j))],
        out_specs=pl.BlockSpec((1, 8, nb), lambda l, j: (l, 0, j)),
        out_shape=jax.ShapeDtypeStruct((depth, 8, n), F32),
        compiler_params=_cparams(("parallel", "parallel")),
        name="mod_vectors",
    )(cc, mod_w, mod_b.reshape(depth, 1, n))


def _front_ab_kernel(x_ref, mod_ref, w_ref, u_ref):
    m = mod_ref[0, 0]
    h = _norm_rows(x_ref[0]) * (1.0 + m[1:2]) + m[0:1]
    u_ref[0] = _dot(h.astype(BF16), w_ref[...]).astype(BF16)


def _front_ab(xt, modtab, w_in):
    b, lt, d = xt.shape
    n = w_in.shape[1]
    nlat = (lt // ROW_BLK) - 1
    return pl.pallas_call(
        _front_ab_kernel,
        grid=(b, lt // ROW_BLK),
        in_specs=[pl.BlockSpec((1, ROW_BLK, d), lambda i, j: (i, j, 0)),
                  pl.BlockSpec((1, 1, 6, d), lambda i, j: (i, j // nlat, 0, 0)),
                  _resident((d, n))],
        out_specs=pl.BlockSpec((1, ROW_BLK, n), lambda i, j: (i, j, 0)),
        out_shape=jax.ShapeDtypeStruct((b, lt, n), BF16),
        compiler_params=_cparams(("parallel", "parallel")),
        name="front_ab",
    )(xt, modtab, w_in)


def _hy_prep_kernel(cur_ref, prev_ref, next_ref, w_ref, x1_ref, x2_ref, v_ref, *, nblk):
    j = pl.program_id(1)
    cur = cur_ref[0].astype(F32)
    rows = cur.shape[0]
    has_prev = (j > 0).astype(F32)
    has_next = (j < nblk - 1).astype(F32)
    prev_row = prev_ref[0][7:8].astype(F32) * has_prev
    next_row = next_ref[0][0:1].astype(F32) * has_next
    rid = lax.broadcasted_iota(jnp.int32, (rows, 1), 0)
    up = jnp.where(rid == 0, prev_row, pltpu.roll(cur, 1, axis=0))
    dn = jnp.where(rid == rows - 1, next_row, pltpu.roll(cur, rows - 1, axis=0))
    w = w_ref[...]
    y = up * w[0:1] + cur * w[1:2] + dn * w[2:3]
    c = HY_CH
    x1_ref[0] = y[:, :c].astype(BF16)
    x2_ref[0] = y[:, c:2 * c].astype(BF16)
    v_ref[0] = y[:, 2 * c:].astype(BF16)


def _hy_prep(u, conv_w, blk0, nblk):
    b, lt, _ = u.shape
    n = 3 * HY_CH
    sub = ROW_BLK // 8
    last8 = lt // 8 - 1
    out = jax.ShapeDtypeStruct((b, nblk * ROW_BLK, HY_CH), BF16)
    ospec = pl.BlockSpec((1, ROW_BLK, HY_CH), lambda i, j: (i, j, 0))
    return pl.pallas_call(
        functools.partial(_hy_prep_kernel, nblk=nblk),
        grid=(b, nblk),
        in_specs=[pl.BlockSpec((1, ROW_BLK, n), lambda i, j: (i, blk0 + j, 0)),
                  pl.BlockSpec((1, 8, n), lambda i, j: (i, jnp.maximum((blk0 + j) * sub - 1, 0), 0)),
                  pl.BlockSpec((1, 8, n), lambda i, j: (i, jnp.minimum((blk0 + j + 1) * sub, last8), 0)),
                  _resident((3, n))],
        out_specs=[ospec, ospec, ospec],
        out_shape=[out, out, out],
        compiler_params=_cparams(("parallel", "parallel")),
        name="hy_prep",
    )(u, u, u, conv_w)


def _hy_filt_kernel(z_ref, w1_ref, b1_ref, fr_ref, w2_ref, b2_ref, w3_ref, ld_ref, o_ref):
    hi = lax.Precision.HIGHEST
    z = z_ref[...]
    fr = fr_ref[...]
    hid = jnp.sin(fr * (jnp.dot(z, w1_ref[...], preferred_element_type=F32, precision=hi) + b1_ref[...]))
    hid = jnp.sin(fr * (jnp.dot(hid, w2_ref[...], preferred_element_type=F32, precision=hi) + b2_ref[...]))
    h = jnp.dot(hid, w3_ref[...], preferred_element_type=F32, precision=hi)
    t = z[:, 0:1]
    o_ref[...] = h * jnp.exp(-t * jnp.exp(ld_ref[...]))


def _pad2(a, rows, cols):
    return jnp.pad(a, ((0, rows - a.shape[0]), (0, cols - a.shape[1])))


def _hy_filters(length, w1, b1, freq, w2, b2, w3, log_decay):
    pos = jnp.arange(length, dtype=F32)
    t = pos / max(length - 1, 1)
    w = 2.0 * math.pi * pos / length
    f = jnp.linspace(1e-4, HY_BANDS - 1, HY_BANDS, dtype=F32)
    ang = w[:, None] * f[None, :]
    z = jnp.concatenate([t[:, None], jnp.cos(ang), -jnp.sin(ang)], -1)
    z = _pad2(z, length, LANES)
    n = w3.shape[1]
    rb = min(length, 512)
    vec = lambda a: _pad2(a.reshape(1, -1), 1, LANES)
    return pl.pallas_call(
        _hy_filt_kernel,
        grid=(length // rb,),
        in_specs=[pl.BlockSpec((rb, LANES), lambda i: (i, 0)),
                  _resident((LANES, LANES)), _resident((1, LANES)), _resident((1, LANES)),
                  _resident((LANES, LANES)), _resident((1, LANES)),
                  _resident((LANES, n)), _resident((1, n))],
        out_specs=pl.BlockSpec((rb, n), lambda i: (i, 0)),
        out_shape=jax.ShapeDtypeStruct((length, n), F32),
        compiler_params=_cparams(("parallel",)),
        name="hy_filters",
    )(z, _pad2(w1, LANES, LANES), vec(b1), vec(freq), _pad2(w2, LANES, LANES), vec(b2),
      _pad2(w3, LANES, n), log_decay.reshape(1, n))


def _bidir_taps(h, skip, length):
    h4 = h.reshape(length, 2, 2, HY_CH)
    cols = []
    for o in range(2):
        hf = h4[:, o, 0].at[0].add(skip[o])
        hb = h4[:, o, 1]
        cols.append(jnp.concatenate([hf, jnp.zeros_like(hf[:1]), hb[:0:-1]], 0))
    return jnp.concatenate(cols, -1)


def _left_mm_kernel(m_ref, x_ref, o_ref):
    o_ref[0] = _dot(m_ref[...], x_ref[0]).astype(o_ref.dtype)


def _left_mm_gate_kernel(m_ref, x_ref, g_ref, o_ref):
    o_ref[0] = (_dot(m_ref[...], x_ref[0]) * g_ref[0].astype(F32)).astype(o_ref.dtype)


def _left_mm(mat, x, out_dtype, lane_blk, gate=None):
    g, k, n = x.shape
    m = mat.shape[0]
    lane_blk = min(lane_blk, n)
    xs = pl.BlockSpec((1, k, lane_blk), lambda i, j: (i, 0, j))
    os_ = pl.BlockSpec((1, m, lane_blk), lambda i, j: (i, 0, j))
    if gate is None:
        kern, ins, specs = _left_mm_kernel, (mat, x), [_resident((m, k)), xs]
    else:
        kern, ins, specs = _left_mm_gate_kernel, (mat, x, gate), [_resident((m, k)), xs, os_]
    return pl.pallas_call(
        kern, grid=(g, n // lane_blk), in_specs=specs, out_specs=os_,
        out_shape=jax.ShapeDtypeStruct((g, m, n), out_dtype),
        compiler_params=_cparams(("parallel", "parallel")),
        name="left_mm",
    )(*ins)


def _cs(num, den, rows, cols):
    ang = 2.0 * np.pi * np.outer(np.arange(rows), np.arange(cols)) * (num / den)
    return np.cos(ang), np.sin(ang)


def _twiddle(n1, n2):
    k1 = lax.broadcasted_iota(jnp.int32, (n1, n2, 1), 0)
    m2 = lax.broadcasted_iota(jnp.int32, (n1, n2, 1), 1)
    ang = (k1 * m2).astype(F32) * (2.0 * math.pi / (n1 * n2))
    return jnp.cos(ang), jnp.sin(ang)


def _slab_kernel(a_ref, c_ref, s_ref, mf_ref, mi_ref, h_ref, o_ref):
    half = FFT_N2
    ar = a_ref[0, 0, 0].astype(F32)
    ai = a_ref[0, 1, 0].astype(F32)
    c = c_ref[0]
    s = s_ref[0]
    t = jnp.concatenate([ar * c + ai * s, ai * c - ar * s], axis=0).astype(BF16)
    x = _dot(mf_ref[...], t)
    xr, xi = x[:half], x[half:]
    hr = h_ref[0, 0]
    hi = h_ref[1, 0]
    y = jnp.concatenate([xr * hr - xi * hi, xr * hi + xi * hr], axis=0).astype(BF16)
    bm = _dot(mi_ref[...], y)
    br, bi = bm[:half], bm[half:]
    o_ref[0, 0, 0] = (br * c - bi * s).astype(o_ref.dtype)
    o_ref[0, 1, 0] = (bi * c + br * s).astype(o_ref.dtype)


def _slab_filter_kernel(a_ref, c_ref, s_ref, mf_ref, o_ref):
    half = FFT_N2
    ar = a_ref[0, 0, 0].astype(F32)
    ai = a_ref[0, 1, 0].astype(F32)
    c = c_ref[0]
    s = s_ref[0]
    t = jnp.concatenate([ar * c + ai * s, ai * c - ar * s], axis=0).astype(BF16)
    x = _dot(mf_ref[...], t)
    o_ref[0, 0] = x[:half]
    o_ref[1, 0] = x[half:]


def _dft_mats():
    n = FFT_N2
    c, s = _cs(1, n, n, n)
    f1_half = np.concatenate([c[:, :n // 2], -s[:, :n // 2]], 0)
    f1_full = np.concatenate([c, -s], 0)
    m_fwd = np.block([[c, s], [-s, c]])
    m_inv = np.block([[c, -s], [s, c]])
    f1_inv = np.concatenate([c[:n // 2, :], -s[:n // 2, :]], 1) / (n * n)
    cast = lambda a: jnp.asarray(a, dtype=BF16)
    return cast(f1_half), cast(f1_full), cast(m_fwd), cast(m_inv), cast(f1_inv)


def _filter_spectrum(taps, f1_full, m_fwd, twc, tws):
    n = FFT_N2
    ch = taps.shape[1]
    a = _left_mm(f1_full, taps.astype(BF16).reshape(1, n, n * ch), BF16, 8192)
    a = a.reshape(1, 2, n, n, ch)
    cb = 512
    return pl.pallas_call(
        _slab_filter_kernel,
        grid=(n, ch // cb),
        in_specs=[pl.BlockSpec((1, 2, 1, n, cb), lambda k, j: (0, 0, k, 0, j)),
                  pl.BlockSpec((1, n, 1), lambda k, j: (k, 0, 0)),
                  pl.BlockSpec((1, n, 1), lambda k, j: (k, 0, 0)),
                  _resident((2 * n, 2 * n))],
        out_specs=pl.BlockSpec((2, 1, n, cb), lambda k, j: (0, k, 0, j)),
        out_shape=jax.ShapeDtypeStruct((2, n, n, ch), F32),
        compiler_params=_cparams(("parallel", "parallel")),
        name="filter_spectrum",
    )(a, twc, tws, m_fwd)


def _long_conv_gate(z, gate, spec, order, mats, twc, tws):
    f1_half, _, m_fwd, m_inv, f1_inv = mats
    b, length, c = z.shape
    n = FFT_N2
    n1 = length // n
    a = _left_mm(f1_half, z.reshape(b, n1, n * c), BF16, 4096)
    a = a.reshape(b, 2, n, n, c)
    slab = pl.BlockSpec((1, 2, 1, n, c), lambda k, i: (i, 0, k, 0, 0))
    bm = pl.pallas_call(
        _slab_kernel,
        grid=(n, b),
        in_specs=[slab,
                  pl.BlockSpec((1, n, 1), lambda k, i: (k, 0, 0)),
                  pl.BlockSpec((1, n, 1), lambda k, i: (k, 0, 0)),
                  _resident((2 * n, 2 * n)), _resident((2 * n, 2 * n)),
                  pl.BlockSpec((2, 1, n, c), lambda k, i: (0, k, 0, order))],
        out_specs=slab,
        out_shape=jax.ShapeDtypeStruct((b, 2, n, n, c), BF16),
        compiler_params=_cparams(("parallel", "parallel")),
        name="conv_slab",
    )(a, twc, tws, m_fwd, m_inv, spec)
    y = _left_mm(f1_inv, bm.reshape(b, 2 * n, n * c), BF16, 4096, gate=gate.reshape(b, n1, n * c))
    return y.reshape(b, length, c)


def _hy_ctx_kernel(v_ref, x1_ref, x2_ref, f_ref, fi_ref, h_ref, o_ref):
    nf = f_ref.shape[0] // 2
    zin = v_ref[0]
    gates = (x1_ref, x2_ref)
    for o in range(2):
        x = _dot(f_ref[...], zin)
        xr, xi = x[:nf], x[nf:]
        hr = h_ref[o, :nf]
        hi = h_ref[o, nf:]
        y = jnp.concatenate([xr * hr - xi * hi, xr * hi + xi * hr], axis=0).astype(BF16)
        zin = (_dot(fi_ref[...], y) * gates[o][0].astype(F32)).astype(BF16)
    o_ref[0] = zin


def _hy_ctx(v, x1, x2, taps):
    b, lc, c = v.shape
    nf = 2 * lc
    cm, sm = _cs(1, nf, nf, nf)
    fwd = jnp.asarray(np.concatenate([cm[:, :lc], -sm[:, :lc]], 0), dtype=BF16)
    fwd_full = jnp.asarray(np.concatenate([cm, -sm], 0), dtype=BF16)
    inv = jnp.asarray(np.concatenate([cm[:lc, :], -sm[:lc, :]], 1) / nf, dtype=BF16)
    spec = _left_mm(fwd_full, taps.astype(BF16).reshape(1, nf, 2 * c), F32, 2 * c)
    spec = spec.reshape(2 * nf, 2, c).transpose(1, 0, 2)
    blk = pl.BlockSpec((1, lc, c), lambda i: (i, 0, 0))
    return pl.pallas_call(
        _hy_ctx_kernel,
        grid=(b,),
        in_specs=[blk, blk, blk, _resident((2 * nf, lc)), _resident((lc, 2 * nf)),
                  _resident((2, 2 * nf, c))],
        out_specs=blk,
        out_shape=jax.ShapeDtypeStruct((b, lc, c), BF16),
        compiler_params=_cparams(("parallel",)),
        name="hy_ctx",
    )(v, x1, x2, fwd, inv, spec)


def _pair_rows(q2):
    lane = lax.broadcasted_iota(jnp.int32, q2.shape, 1)
    zero = jnp.zeros_like(q2)
    return jnp.concatenate([jnp.where(lane < NA_HD, q2, zero), jnp.where(lane >= NA_HD, q2, zero)], axis=0)


def _unpair_rows(o):
    r = o.shape[0] // 2
    lane = lax.broadcasted_iota(jnp.int32, (r, o.shape[1]), 1)
    return jnp.where(lane < NA_HD, o[:r], o[r:])


def _pair_softmax_pv(scores, values):
    m = scores[0].max(axis=-1, keepdims=True)
    for s in scores[1:]:
        m = jnp.maximum(m, s.max(axis=-1, keepdims=True))
    den = None
    acc = None
    for s, v in zip(scores, values):
        p = jnp.exp(s - m)
        d = p.sum(axis=-1, keepdims=True)
        a = _dot(p.astype(BF16), v)
        den = d if den is None else den + d
        acc = a if acc is None else acc + a
    return acc / den


def _natten_kernel(q_ref, k0, k1, k2, k3, v0, v1, v2, v3, kc_ref, vc_ref, bias_ref, o_ref,
                   kwin, vwin, *, rows):
    g = pl.program_id(1)
    rb = 4 * GRID_W
    for i, (kr, vr) in enumerate(((k0, v0), (k1, v1), (k2, v2), (k3, v3))):
        kwin[i * rb:(i + 1) * rb, :] = kr[0]
        vwin[i * rb:(i + 1) * rb, :] = vr[0]
    base = 4 * jnp.clip(2 * g - 1, 0, rows // 4 - 4)
    nwin = NA_WIN_R * GRID_W
    qscale = jnp.asarray(NA_HD ** -0.5, BF16)

    def row_body(rr, carry):
        r = 8 * g + rr
        rs = jnp.clip(r - NA_WIN_R // 2, 0, rows - NA_WIN_R)
        st = pl.multiple_of((rs - base) * GRID_W, GRID_W)
        d0 = rs - r + NA_WIN_R - 1
        qo = pl.multiple_of(rr * GRID_W, GRID_W)
        for p in range(NA_HEADS // 2):
            ls = slice(p * LANES, (p + 1) * LANES)
            qp = _pair_rows(q_ref[0, pl.ds(qo, GRID_W), ls] * qscale)
            s_lat = _dot_nt(qp, kwin[pl.ds(st, nwin), ls]) + bias_ref[d0, p].astype(F32)
            s_ctx = _dot_nt(qp, kc_ref[0, :, ls])
            o = _pair_softmax_pv([s_lat, s_ctx], [vwin[pl.ds(st, nwin), ls], vc_ref[0, :, ls]])
            o_ref[0, pl.ds(qo, GRID_W), ls] = _unpair_rows(o).astype(o_ref.dtype)
        return carry

    lax.fori_loop(0, 8, row_body, 0)


def _natten_bias(rpb):
    c = np.arange(GRID_W)[:, None]
    kc = np.arange(GRID_W)[None, :]
    cs = np.clip(c - NA_WIN_C // 2, 0, GRID_W - NA_WIN_C)
    valid = (kc >= cs) & (kc < cs + NA_WIN_C)
    dc = np.clip(kc - c + NA_WIN_C - 1, 0, 2 * NA_WIN_C - 2)
    tb = jnp.where(valid[None, None], rpb[:, :, dc], NEG_BIG)
    slabs = []
    for d0 in range(NA_WIN_R):
        s = tb[:, d0:d0 + NA_WIN_R]
        s = s.transpose(0, 2, 1, 3).reshape(NA_HEADS, GRID_W, NA_WIN_R * GRID_W)
        slabs.append(s.reshape(NA_HEADS // 2, 2 * GRID_W, NA_WIN_R * GRID_W))
    return jnp.stack(slabs).astype(BF16)


def _natten(u, bias, length):
    b, lt, _ = u.shape
    c = NA_HEADS * NA_HD
    rows = length // GRID_W
    qb, kb, vb = 3, 4, 5
    rb = 4 * GRID_W
    nkb = length // rb
    qrows = 8 * GRID_W
    ctx_blk = length // ROW_BLK
    lc = lt - length

    def kv_spec(col, off):
        return pl.BlockSpec((1, rb, c), lambda i, g: (i, jnp.clip(2 * g - 1, 0, nkb - 4) + off, col))

    return pl.pallas_call(
        functools.partial(_natten_kernel, rows=rows),
        grid=(b, rows // 8),
        in_specs=[pl.BlockSpec((1, qrows, c), lambda i, g: (i, g, qb))]
                 + [kv_spec(kb, o) for o in range(4)] + [kv_spec(vb, o) for o in range(4)]
                 + [pl.BlockSpec((1, lc, c), lambda i, g: (i, ctx_blk, kb)),
                    pl.BlockSpec((1, lc, c), lambda i, g: (i, ctx_blk, vb)),
                    _resident(bias.shape)],
        out_specs=pl.BlockSpec((1, qrows, c), lambda i, g: (i, g, 0)),
        out_shape=jax.ShapeDtypeStruct((b, length, c), BF16),
        scratch_shapes=[pltpu.VMEM((4 * rb, c), BF16), pltpu.VMEM((4 * rb, c), BF16)],
        compiler_params=_cparams(("parallel", "parallel")),
        name="natten",
    )(u, *([u] * 8), u, u, bias)


def _ctx_attn_kernel(q_ref, k_ref, v_ref, o_ref):
    qscale = jnp.asarray(NA_HD ** -0.5, BF16)
    for p in range(NA_HEADS // 2):
        ls = slice(p * LANES, (p + 1) * LANES)
        qp = _pair_rows(q_ref[0, :, ls] * qscale)
        o = _pair_softmax_pv([_dot_nt(qp, k_ref[0, :, ls])], [v_ref[0, :, ls]])
        o_ref[0, :, ls] = _unpair_rows(o).astype(o_ref.dtype)


def _ctx_attn(u, length):
    b, lt, _ = u.shape
    c = NA_HEADS * NA_HD
    lc = lt - length
    blk = length // lc
    spec = lambda col: pl.BlockSpec((1, lc, c), lambda i: (i, blk, col))
    return pl.pallas_call(
        _ctx_attn_kernel,
        grid=(b,),
        in_specs=[spec(3), spec(4), spec(5)],
        out_specs=pl.BlockSpec((1, lc, c), lambda i: (i, 0, 0)),
        out_shape=jax.ShapeDtypeStruct((b, lc, c), BF16),
        compiler_params=_cparams(("parallel",)),
        name="ctx_attn",
    )(u, u, u)


def _post_kernel(*refs, has_ctx, ctx_blk):
    if has_ctx:
        x_ref, mod_ref, ya_ref, yb_ref, yac_ref, ybc_ref, wa_ref, wb_ref, lng_ref, lnb_ref, w1_ref, w2_ref, o_ref = refs
    else:
        x_ref, mod_ref, ya_ref, yb_ref, wa_ref, wb_ref, lng_ref, lnb_ref, w1_ref, w2_ref, o_ref = refs
    x = x_ref[0]
    ya = ya_ref[0]
    yb = yb_ref[0]
    if has_ctx:
        is_ctx = (jnp.zeros((ROW_BLK, 1), jnp.int32) + pl.program_id(1)) == ctx_blk
        ya = jnp.where(is_ctx, yac_ref[0], ya)
        yb = jnp.where(is_ctx, ybc_ref[0], yb)
    y = _dot(ya, wa_ref[...]) + _dot(yb, wb_ref[...])
    m = mod_ref[0, 0]
    lng = lng_ref[...]
    lnb = lnb_ref[...]
    x1 = _norm_rows(ALPHA * x + m[2:3] * y) * lng[0:1] + lnb[0:1]
    h = (_norm_rows(x1) * (1.0 + m[4:5]) + m[3:4]).astype(BF16)
    ff = w1_ref.shape[1]
    step = 1024
    acc = jnp.zeros_like(x)
    for c in range(ff // step):
        a = _dot(h, w1_ref[:, c * step:(c + 1) * step])
        a = jnp.maximum(a, 0.0)
        acc = acc + _dot((a * a).astype(BF16), w2_ref[c * step:(c + 1) * step, :])
    o_ref[0] = _norm_rows(ALPHA * x1 + m[5:6] * acc) * lng[1:2] + lnb[1:2]


def _post(xt, modtab, ya, yb, yac, ybc, wa, wb, lng, lnb, w1, w2, length):
    b, lt, d = xt.shape
    has_ctx = yac is not None
    nlat = length // ROW_BLK
    nblk = nlat + (1 if has_ctx else 0)
    ka, kb = ya.shape[-1], yb.shape[-1]
    lat = lambda k: pl.BlockSpec((1, ROW_BLK, k), lambda i, j: (i, jnp.minimum(j, nlat - 1), 0))
    ctx = lambda k: pl.BlockSpec((1, ROW_BLK, k), lambda i, j: (i, 0, 0))
    in_specs = [pl.BlockSpec((1, ROW_BLK, d), lambda i, j: (i, j, 0)),
                pl.BlockSpec((1, 1, 6, d), lambda i, j: (i, j // nlat, 0, 0)),
                lat(ka), lat(kb)]
    args = [xt, modtab, ya, yb]
    if has_ctx:
        in_specs += [ctx(ka), ctx(kb)]
        args += [yac, ybc]
    in_specs += [_resident(wa.shape), _resident(wb.shape), _resident(lng.shape), _resident(lnb.shape),
                 _resident(w1.shape), _resident(w2.shape)]
    args += [wa, wb, lng, lnb, w1, w2]
    return pl.pallas_call(
        functools.partial(_post_kernel, has_ctx=has_ctx, ctx_blk=nlat),
        grid=(b, nblk),
        in_specs=in_specs,
        out_specs=pl.BlockSpec((1, ROW_BLK, d), lambda i, j: (i, j, 0)),
        out_shape=jax.ShapeDtypeStruct((b, nblk * ROW_BLK, d), F32),
        compiler_params=_cparams(("parallel", "parallel")),
        name="post_mixer",
    )(*args)


def _rope(x, cos, sinl, sinr):
    reps = x.shape[1] // LANES
    tile = lambda t: jnp.concatenate([t] * reps, axis=1)
    n = x.shape[1]
    quarter = MLA_ROPE // 4
    return (x * tile(cos) + pltpu.roll(x, n - quarter, axis=1) * tile(sinl)
            + pltpu.roll(x, quarter, axis=1) * tile(sinr))


def _front_cd_kernel(x_ref, mod_ref, w_ref, qn_ref, kvn_ref, wuq_ref, wuk_ref, wuv_ref, epe_ref, one_ref,
                     fng_ref, fnb_ref, cbd_ref, sbd_ref,
                     cq_ref, slq_ref, srq_ref, ck_ref, slk_ref, srk_ref,
                     q_ref, k_ref, v_ref, p_ref, qf_ref):
    m = mod_ref[0, 0]
    h = _norm_rows(x_ref[0]) * (1.0 + m[1:2]) + m[0:1]
    u = _dot(h.astype(BF16), w_ref[...])
    o_kv = MLA_Q_RANK
    o_fn = o_kv + MLA_KV_RANK
    o_pe = o_fn + FN_CH

    def rms(x, g):
        return x * lax.rsqrt(jnp.mean(x * x, axis=-1, keepdims=True) + LN_EPS) * g

    cq = rms(u[:, :o_kv], qn_ref[...]).astype(BF16)
    q = _dot(cq, wuq_ref[...])
    q_ref[0] = _rope(q, cq_ref[...], slq_ref[...], srq_ref[...]).astype(BF16)

    ckv = rms(u[:, o_kv:o_fn], kvn_ref[...]).astype(BF16)
    kpe = _dot(u[:, o_pe:].astype(BF16), epe_ref[...])
    k = _dot(ckv, wuk_ref[...]) + _rope(kpe, ck_ref[...], slk_ref[...], srk_ref[...])
    k_ref[0] = k.astype(BF16)
    v_ref[0] = (_dot(ckv, wuv_ref[...]) + one_ref[...]).astype(BF16)

    uf = u[:, o_fn:o_pe]
    lane = lax.broadcasted_iota(jnp.int32, uf.shape, 1)
    mean = jnp.zeros_like(uf)
    for g in range(FN_GROUPS):
        sel = (lane >= g * FN_GD) & (lane < (g + 1) * FN_GD)
        mg = jnp.sum(jnp.where(sel, uf, 0.0), axis=-1, keepdims=True) * (1.0 / FN_GD)
        mean = jnp.where(sel, mg, mean)
    uc = uf - mean
    var = jnp.zeros_like(uf)
    for g in range(FN_GROUPS):
        sel = (lane >= g * FN_GD) & (lane < (g + 1) * FN_GD)
        vg = jnp.sum(jnp.where(sel, uc * uc, 0.0), axis=-1, keepdims=True) * (1.0 / FN_GD)
        var = jnp.where(sel, vg, var)
    ug = (uc * lax.rsqrt(var + LN_EPS) * fng_ref[...] + fnb_ref[...]).astype(BF16)
    p_ref[0] = _dot(ug, cbd_ref[...]).astype(BF16)
    qf_ref[0] = _dot(ug, sbd_ref[...]).astype(BF16)


def _rope_tables(length, lt, scale):
    t = jnp.arange(lt, dtype=jnp.int32)
    rows = (t // GRID_W).astype(F32)
    cols = (t % GRID_W).astype(F32)
    half = MLA_ROPE // 2
    inv = ROPE_THETA ** (-jnp.arange(0, half, 2, dtype=F32) / half)
    ar = rows[:, None] * inv[None, :]
    ac = cols[:, None] * inv[None, :]
    ang = jnp.concatenate([ar, ar, ac, ac], -1)
    is_lat = (t < length)[:, None]
    cos = jnp.where(is_lat, jnp.cos(ang), 1.0)
    sin = jnp.where(is_lat, jnp.sin(ang), 0.0)
    qd = MLA_ROPE // 4
    ones = jnp.ones((lt, MLA_NOPE), F32)
    zeros = jnp.zeros((lt, MLA_NOPE), F32)
    tail1 = jnp.ones((lt, HEAD_PAD - MLA_NOPE - MLA_ROPE), F32)
    tail0 = jnp.zeros((lt, HEAD_PAD - MLA_NOPE - MLA_ROPE), F32)
    z8 = jnp.zeros((lt, qd), F32)
    c = jnp.concatenate([ones, cos, tail1], -1)
    sl = jnp.concatenate([zeros, -sin[:, :qd], z8, -sin[:, 2 * qd:3 * qd], z8, tail0], -1)
    sr = jnp.concatenate([zeros, z8, sin[:, qd:2 * qd], z8, sin[:, 3 * qd:], tail0], -1)
    return c * scale, sl * scale, sr * scale


def _head_slots(w, per_head, take_from, take_n):
    k = w.shape[0]
    w3 = w.reshape(k, MLA_HEADS, per_head)[:, :, take_from:take_from + take_n]
    w3 = jnp.pad(w3, ((0, 0), (0, 0), (0, HEAD_PAD - take_n)))
    return w3.reshape(k, MLA_HEADS * HEAD_PAD)


def _front_cd(xt, modtab, w_in, q_norm, w_uq, kv_norm, w_ukv, fn_g, fn_b, length):
    b, lt, d = xt.shape
    nlat = length // ROW_BLK
    o_kv = MLA_Q_RANK
    o_pe = o_kv + MLA_KV_RANK
    o_fn = o_pe + MLA_ROPE
    hw = MLA_HEADS * HEAD_PAD
    w_perm = jnp.concatenate([w_in[:, :o_pe], w_in[:, o_fn:], w_in[:, o_pe:o_fn],
                              jnp.zeros((d, LANES - MLA_ROPE), w_in.dtype)], -1).astype(BF16)
    wuq = _head_slots(w_uq, MLA_NOPE + MLA_ROPE, 0, MLA_NOPE + MLA_ROPE).astype(BF16)
    wuk = _head_slots(w_ukv, MLA_NOPE + MLA_V, 0, MLA_NOPE).astype(BF16)
    wuv = _head_slots(w_ukv, MLA_NOPE + MLA_V, MLA_NOPE, MLA_V).astype(BF16)
    epe = np.zeros((LANES, hw), np.float32)
    for hd in range(MLA_HEADS):
        for i in range(MLA_ROPE):
            epe[i, hd * HEAD_PAD + MLA_NOPE + i] = 1.0
    epe = jnp.asarray(epe, dtype=BF16)
    ones_col = np.zeros((1, hw), np.float32)
    ones_col[0, MLA_V::HEAD_PAD] = 1.0
    ones_col = jnp.asarray(ones_col)
    cm, sm = _cs(1, FN_GD, FN_GD, FN_GD)
    eye = np.eye(FN_GROUPS)
    cbd = jnp.asarray(np.kron(eye, cm), dtype=BF16)
    sbd = jnp.asarray(np.kron(eye, -sm), dtype=BF16)
    qtab = _rope_tables(length, lt, (MLA_NOPE + MLA_ROPE) ** -0.5 * math.log2(math.e))
    ktab = _rope_tables(length, lt, 1.0)
    row = lambda n: pl.BlockSpec((1, ROW_BLK, n), lambda i, j: (i, j, 0))
    tab = pl.BlockSpec((ROW_BLK, HEAD_PAD), lambda i, j: (j, 0))
    out = lambda n: jax.ShapeDtypeStruct((b, lt, n), BF16)
    return pl.pallas_call(
        _front_cd_kernel,
        grid=(b, lt // ROW_BLK),
        in_specs=[row(d), pl.BlockSpec((1, 1, 6, d), lambda i, j: (i, j // nlat, 0, 0)),
                  _resident(w_perm.shape), _resident((1, MLA_Q_RANK)), _resident((1, MLA_KV_RANK)),
                  _resident(wuq.shape), _resident(wuk.shape), _resident(wuv.shape), _resident(epe.shape),
                  _resident(ones_col.shape),
                  _resident((1, FN_CH)), _resident((1, FN_CH)), _resident(cbd.shape), _resident(sbd.shape),
                  tab, tab, tab, tab, tab, tab],
        out_specs=[row(hw), row(hw), row(hw), row(FN_CH), row(FN_CH)],
        out_shape=[out(hw), out(hw), out(hw), out(FN_CH), out(FN_CH)],
        compiler_params=_cparams(("parallel", "parallel")),
        name="front_cd",
    )(xt, modtab, w_perm, q_norm.reshape(1, -1), kv_norm.reshape(1, -1), wuq, wuk, wuv, epe, ones_col,
      fn_g.reshape(1, -1), fn_b.reshape(1, -1), cbd, sbd, *qtab, *ktab)


def _flash_kernel(q_ref, k_ref, v_ref, o_ref, *, chunk):
    q = q_ref[0]
    nchunk = k_ref.shape[1] // chunk
    m = None
    acc = None
    for c in range(nchunk):
        ks = c * chunk
        s = _dot_nt(q, k_ref[0, ks:ks + chunk, :])
        cm = s.max(axis=-1, keepdims=True)
        m_new = cm if c == 0 else jnp.maximum(m, cm)
        pv = _dot(jnp.exp2(s - m_new).astype(BF16), v_ref[0, ks:ks + chunk, :])
        acc = pv if c == 0 else acc * jnp.exp2(m - m_new) + pv
        m = m_new
    o_ref[0] = (acc / acc[:, MLA_V:MLA_V + 1]).astype(o_ref.dtype)


def _mla_attention(q, k, v, length):
    b, lt, hw = q.shape
    heads = hw // HEAD_PAD
    tq = 256
    chunk = 8448
    assert lt % chunk == 0
    kv = pl.BlockSpec((1, lt, HEAD_PAD), lambda i, h, j: (i, 0, h))
    qs = pl.BlockSpec((1, tq, HEAD_PAD), lambda i, h, j: (i, j, h))
    return pl.pallas_call(
        functools.partial(_flash_kernel, chunk=chunk),
        grid=(b, heads, length // tq),
        in_specs=[qs, kv, kv],
        out_specs=qs,
        out_shape=jax.ShapeDtypeStruct((b, length, hw), BF16),
        compiler_params=_cparams(("parallel", "parallel", "parallel")),
        name="mla_attention",
    )(q, k, v)


def _fn1_kernel(m_ref, zr_ref, zi_ref, o_ref):
    n = zr_ref.shape[1]
    a = _dot(m_ref[...], jnp.concatenate([zr_ref[0], zi_ref[0]], axis=0))
    o_ref[0, 0] = a[:n].astype(o_ref.dtype)
    o_ref[0, 1] = a[n:].astype(o_ref.dtype)


def _fn2_kernel(a_ref, c_ref, s_ref, m_ref, o_ref):
    for i in range(a_ref.shape[2]):
        ar = a_ref[0, 0, i].astype(F32)
        ai = a_ref[0, 1, i].astype(F32)
        c = c_ref[i]
        s = s_ref[i]
        t = jnp.concatenate([ar * c + ai * s, ai * c - ar * s], axis=0).astype(BF16)
        o_ref[0, i] = _dot(m_ref[...], t).astype(o_ref.dtype)


def _fnet(p, qn):
    b, length, c = p.shape
    n1 = 128
    n2 = length // n1
    cm, sm = _cs(1, n1, n1, n1)
    m1 = jnp.asarray(np.block([[cm, sm], [-sm, cm]]), dtype=BF16)
    c2, s2 = _cs(1, n2, n2, n2)
    m2 = jnp.asarray(np.concatenate([c2, s2], 1) / math.sqrt(length * FN_GD), dtype=BF16)
    twc, tws = _twiddle(n1, n2)
    lane_blk = 2048
    zs = pl.BlockSpec((1, n1, lane_blk), lambda i, j: (i, 0, j))
    a = pl.pallas_call(
        _fn1_kernel,
        grid=(b, n2 * c // lane_blk),
        in_specs=[_resident(m1.shape), zs, zs],
        out_specs=pl.BlockSpec((1, 2, n1, lane_blk), lambda i, j: (i, 0, 0, j)),
        out_shape=jax.ShapeDtypeStruct((b, 2, n1, n2 * c), BF16),
        compiler_params=_cparams(("parallel", "parallel")),
        name="fnet_stage1",
    )(m1, p.reshape(b, n1, n2 * c), qn.reshape(b, n1, n2 * c))
    kb = 8
    y = pl.pallas_call(
        _fn2_kernel,
        grid=(b, n1 // kb),
        in_specs=[pl.BlockSpec((1, 2, kb, n2, c), lambda i, j: (i, 0, j, 0, 0)),
                  pl.BlockSpec((kb, n2, 1), lambda i, j: (j, 0, 0)),
                  pl.BlockSpec((kb, n2, 1), lambda i, j: (j, 0, 0)),
                  _resident(m2.shape)],
        out_specs=pl.BlockSpec((1, kb, n2, c), lambda i, j: (i, j, 0, 0)),
        out_shape=jax.ShapeDtypeStruct((b, n1, n2, c), BF16),
        compiler_params=_cparams(("parallel", "parallel")),
        name="fnet_stage2",
    )(a.reshape(b, 2, n1, n2, c), twc, tws, m2)
    return y.transpose(0, 2, 1, 3).reshape(b, length, c)


def kernel(x, c, ctx, c_ctx, mod_w, mod_b, ln_g, ln_b, mlp_w1, mlp_w2,
           ab_w_in, ab_w_out, hy_conv_w, hy_w1, hy_b1, hy_freq, hy_w2, hy_b2, hy_w3, hy_log_decay, hy_skip, na_rpb,
           cd_w_in, cd_w_out, mla_q_norm, mla_w_uq, mla_kv_norm, mla_w_ukv, fn_norm_g, fn_norm_b):
    b, length, d = x.shape
    lc = ctx.shape[1]

    xt = jnp.concatenate([x, ctx], axis=1)
    cc = jnp.concatenate([c, c_ctx[None], jnp.zeros((8 - b - 1, d), F32)], 0)
    mods = _mod_vectors(cc, mod_w, mod_b).reshape(DEPTH, 8, 6, d)
    modtab = [jnp.stack([mods[l, :b], jnp.broadcast_to(mods[l, b], (b, 6, d))], axis=1) for l in range(DEPTH)]

    u = _front_ab(xt, modtab[0], ab_w_in[0].astype(BF16))
    nlat = length // ROW_BLK
    x1, x2, v = _hy_prep(u, hy_conv_w[0], 0, nlat)
    x1c, x2c, vc = _hy_prep(u, hy_conv_w[0], nlat, lc // ROW_BLK)
    fargs = (hy_w1[0], hy_b1[0], hy_freq[0], hy_w2[0], hy_b2[0], hy_w3[0], hy_log_decay[0])
    taps = _bidir_taps(_hy_filters(length, *fargs), hy_skip[0], length)
    taps_c = _bidir_taps(_hy_filters(lc, *fargs), hy_skip[0], lc)
    mats = _dft_mats()
    twc, tws = _twiddle(FFT_N2, FFT_N2)
    spec = _filter_spectrum(taps, mats[1], mats[2], twc, tws)
    z = _long_conv_gate(v, x1, spec, 0, mats, twc, tws)
    y_hy = _long_conv_gate(z, x2, spec, 1, mats, twc, tws)
    y_hy_c = _hy_ctx(vc, x1c, x2c, taps_c)
    y_na = _natten(u, _natten_bias(na_rpb[0]), length)
    y_na_c = _ctx_attn(u, length)
    w_out = ab_w_out[0].astype(BF16)
    xt = _post(xt, modtab[0], y_hy, y_na, y_hy_c, y_na_c, w_out[:HY_CH], w_out[HY_CH:],
               ln_g[0], ln_b[0], mlp_w1[0].astype(BF16), mlp_w2[0].astype(BF16), length)

    q, k, vv, p, qn = _front_cd(xt, modtab[1], cd_w_in[0], mla_q_norm[0], mla_w_uq[0], mla_kv_norm[0],
                                mla_w_ukv[0], fn_norm_g[0], fn_norm_b[0], length)
    o = _mla_attention(q, k, vv, length)
    y_fn = _fnet(p[:, :length], qn[:, :length])
    w_out = cd_w_out[0]
    n_mla = MLA_HEADS * MLA_V
    wa = jnp.pad(w_out[:n_mla].reshape(MLA_HEADS, MLA_V, d), ((0, 0), (0, HEAD_PAD - MLA_V), (0, 0)))
    wa = wa.reshape(MLA_HEADS * HEAD_PAD, d).astype(BF16)
    return _post(xt, modtab[1], o, y_fn, None, None, wa, w_out[n_mla:].astype(BF16),
                 ln_g[1], ln_b[1], mlp_w1[1].astype(BF16), mlp_w2[1].astype(BF16), length)
```

```python
import functools
import math

import numpy as np
import jax
import jax.numpy as jnp
from jax import lax
from jax.experimental import pallas as pl
from jax.experimental.pallas import tpu as pltpu

F32 = jnp.float32
BF16 = jnp.bfloat16

D_MODEL = 1024
DEPTH = 2
GRID_W = 64
HY_CH = 512
HY_EMB = 33
HY_BANDS = (HY_EMB - 1) // 2
NA_HEADS = 8
NA_HD = 64
NA_WIN_R = 8
NA_WIN_C = 16
MLA_HEADS = 8
MLA_Q_RANK = 384
MLA_KV_RANK = 256
MLA_NOPE = 64
MLA_ROPE = 32
MLA_V = 96
ROPE_THETA = 10000.0
FN_CH = 256
FN_GROUPS = 4
FN_GD = FN_CH // FN_GROUPS
D_FF = 4 * D_MODEL
ALPHA = (2.0 * DEPTH) ** 0.25
LN_EPS = 1e-5

LANES = 128
ROW_BLK = 256
HEAD_PAD = 128
FFT_N2 = 128
HY_GROUP = 4
HY_CHAINS = 2
HY_CBLK = 16
VMEM_LIMIT = 56 * 1024 * 1024
NEG_BIG = -1e30


def _cparams(sem, vmem=VMEM_LIMIT):
    return pltpu.CompilerParams(dimension_semantics=sem, vmem_limit_bytes=vmem)


def _resident(shape):
    nd = len(shape)
    return pl.BlockSpec(shape, lambda *_: (0,) * nd)


def _norm_rows(x):
    mu = jnp.mean(x, axis=-1, keepdims=True)
    xc = x - mu
    var = jnp.mean(xc * xc, axis=-1, keepdims=True)
    return xc * lax.rsqrt(var + LN_EPS)


def _dot(a, b):
    return jnp.dot(a, b, preferred_element_type=F32)


def _dot_nt(a, b):
    return lax.dot_general(a, b, (((1,), (1,)), ((), ())), preferred_element_type=F32)


def _mod_kernel(c_ref, w_ref, b_ref, o_ref):
    c = c_ref[...]
    s = c * (1.0 / (1.0 + jnp.exp(-c)))
    o_ref[0] = jnp.dot(s, w_ref[0], preferred_element_type=F32,
                       precision=lax.Precision.HIGHEST) + b_ref[0]


def _mod_vectors(cc, mod_w, mod_b):
    depth, d, n = mod_w.shape
    nb = 1024
    return pl.pallas_call(
        _mod_kernel,
        grid=(depth, n // nb),
        in_specs=[pl.BlockSpec((8, d), lambda l, j: (0, 0)),
                  pl.BlockSpec((1, d, nb), lambda l, j: (l, 0, j)),
                  pl.BlockSpec((1, 1, nb), lambda l, j: (l, 0, j))],
        out_specs=pl.BlockSpec((1, 8, nb), lambda l, j: (l, 0, j)),
        out_shape=jax.ShapeDtypeStruct((depth, 8, n), F32),
        compiler_params=_cparams(("parallel", "parallel")),
        name="mod_vectors",
    )(cc, mod_w, mod_b.reshape(depth, 1, n))


def _front_ab_ctx_kernel(x_ref, mod_ref, w_ref, u_ref):
    m = mod_ref[0, 0]
    h = _norm_rows(x_ref[0]) * (1.0 + m[1:2]) + m[0:1]
    u_ref[0] = _dot(h.astype(BF16), w_ref[...]).astype(BF16)


def _front_ab_ctx(xc, modtab, w_in):
    b, lc, d = xc.shape
    n = w_in.shape[1]
    return pl.pallas_call(
        _front_ab_ctx_kernel,
        grid=(b, lc // ROW_BLK),
        in_specs=[pl.BlockSpec((1, ROW_BLK, d), lambda i, j: (i, j, 0)),
                  pl.BlockSpec((1, 1, 6, d), lambda i, j: (i, 1, 0, 0)),
                  _resident((d, n))],
        out_specs=pl.BlockSpec((1, ROW_BLK, n), lambda i, j: (i, j, 0)),
        out_shape=jax.ShapeDtypeStruct((b, lc, n), BF16),
        compiler_params=_cparams(("parallel", "parallel")),
        name="front_ab_ctx",
    )(xc, modtab, w_in)


def _front_ab_lat_kernel(x_ref, mod_ref, wq_ref, wht_ref, u_ref, ut_ref):
    m = mod_ref[0, 0]
    h = (_norm_rows(x_ref[0]) * (1.0 + m[1:2]) + m[0:1]).astype(BF16)
    u_ref[0] = _dot(h, wq_ref[...]).astype(BF16)
    ut_ref[0] = _dot_nt(wht_ref[...], h).astype(BF16)


def _front_ab_lat(x, modtab, w_qkv, w_hy_t):
    b, length, d = x.shape
    nq = w_qkv.shape[1]
    nh = w_hy_t.shape[0]
    return pl.pallas_call(
        _front_ab_lat_kernel,
        grid=(b, length // ROW_BLK),
        in_specs=[pl.BlockSpec((1, ROW_BLK, d), lambda i, j: (i, j, 0)),
                  pl.BlockSpec((1, 1, 6, d), lambda i, j: (i, 0, 0, 0)),
                  _resident((d, nq)), _resident((nh, d))],
        out_specs=[pl.BlockSpec((1, ROW_BLK, nq), lambda i, j: (i, j, 0)),
                   pl.BlockSpec((1, nh, ROW_BLK), lambda i, j: (i, 0, j))],
        out_shape=[jax.ShapeDtypeStruct((b, length, nq), BF16),
                   jax.ShapeDtypeStruct((b, nh, length), BF16)],
        compiler_params=_cparams(("parallel", "parallel")),
        name="front_ab_lat",
    )(x, modtab, w_qkv, w_hy_t)


def _hy_prep_kernel(cur_ref, prev_ref, next_ref, w_ref, x1_ref, x2_ref, v_ref, *, nblk):
    j = pl.program_id(1)
    cur = cur_ref[0].astype(F32)
    rows = cur.shape[0]
    has_prev = (j > 0).astype(F32)
    has_next = (j < nblk - 1).astype(F32)
    prev_row = prev_ref[0][7:8].astype(F32) * has_prev
    next_row = next_ref[0][0:1].astype(F32) * has_next
    rid = lax.broadcasted_iota(jnp.int32, (rows, 1), 0)
    up = jnp.where(rid == 0, prev_row, pltpu.roll(cur, 1, axis=0))
    dn = jnp.where(rid == rows - 1, next_row, pltpu.roll(cur, rows - 1, axis=0))
    w = w_ref[...]
    y = up * w[0:1] + cur * w[1:2] + dn * w[2:3]
    c = HY_CH
    x1_ref[0] = y[:, :c].astype(BF16)
    x2_ref[0] = y[:, c:2 * c].astype(BF16)
    v_ref[0] = y[:, 2 * c:].astype(BF16)


def _hy_prep(u, conv_w, blk0, nblk):
    b, lt, _ = u.shape
    n = 3 * HY_CH
    sub = ROW_BLK // 8
    last8 = lt // 8 - 1
    out = jax.ShapeDtypeStruct((b, nblk * ROW_BLK, HY_CH), BF16)
    ospec = pl.BlockSpec((1, ROW_BLK, HY_CH), lambda i, j: (i, j, 0))
    return pl.pallas_call(
        functools.partial(_hy_prep_kernel, nblk=nblk),
        grid=(b, nblk),
        in_specs=[pl.BlockSpec((1, ROW_BLK, n), lambda i, j: (i, blk0 + j, 0)),
                  pl.BlockSpec((1, 8, n), lambda i, j: (i, jnp.maximum((blk0 + j) * sub - 1, 0), 0)),
                  pl.BlockSpec((1, 8, n), lambda i, j: (i, jnp.minimum((blk0 + j + 1) * sub, last8), 0)),
                  _resident((3, n))],
        out_specs=[ospec, ospec, ospec],
        out_shape=[out, out, out],
        compiler_params=_cparams(("parallel", "parallel")),
        name="hy_prep",
    )(u, u, u, conv_w)


def _hy_filt_kernel(z_ref, w1_ref, b1_ref, fr_ref, w2_ref, b2_ref, w3_ref, ld_ref, o_ref):
    hi = lax.Precision.HIGHEST
    z = z_ref[...]
    fr = fr_ref[...]
    hid = jnp.sin(fr * (jnp.dot(z, w1_ref[...], preferred_element_type=F32, precision=hi) + b1_ref[...]))
    hid = jnp.sin(fr * (jnp.dot(hid, w2_ref[...], preferred_element_type=F32, precision=hi) + b2_ref[...]))
    h = jnp.dot(hid, w3_ref[...], preferred_element_type=F32, precision=hi)
    t = z[:, 0:1]
    o_ref[...] = h * jnp.exp(-t * jnp.exp(ld_ref[...]))


def _pad2(a, rows, cols):
    return jnp.pad(a, ((0, rows - a.shape[0]), (0, cols - a.shape[1])))


def _hy_filters(length, w1, b1, freq, w2, b2, w3, log_decay):
    pos = jnp.arange(length, dtype=F32)
    t = pos / max(length - 1, 1)
    w = 2.0 * math.pi * pos / length
    f = jnp.linspace(1e-4, HY_BANDS - 1, HY_BANDS, dtype=F32)
    ang = w[:, None] * f[None, :]
    z = jnp.concatenate([t[:, None], jnp.cos(ang), -jnp.sin(ang)], -1)
    z = _pad2(z, length, LANES)
    n = w3.shape[1]
    rb = min(length, 512)
    vec = lambda a: _pad2(a.reshape(1, -1), 1, LANES)
    return pl.pallas_call(
        _hy_filt_kernel,
        grid=(length // rb,),
        in_specs=[pl.BlockSpec((rb, LANES), lambda i: (i, 0)),
                  _resident((LANES, LANES)), _resident((1, LANES)), _resident((1, LANES)),
                  _resident((LANES, LANES)), _resident((1, LANES)),
                  _resident((LANES, n)), _resident((1, n))],
        out_specs=pl.BlockSpec((rb, n), lambda i: (i, 0)),
        out_shape=jax.ShapeDtypeStruct((length, n), F32),
        compiler_params=_cparams(("parallel",)),
        name="hy_filters",
    )(z, _pad2(w1, LANES, LANES), vec(b1), vec(freq), _pad2(w2, LANES, LANES), vec(b2),
      _pad2(w3, LANES, n), log_decay.reshape(1, n))


def _bidir_taps(h, skip, length):
    h4 = h.reshape(length, 2, 2, HY_CH)
    cols = []
    for o in range(2):
        hf = h4[:, o, 0].at[0].add(skip[o])
        hb = h4[:, o, 1]
        cols.append(jnp.concatenate([hf, jnp.zeros_like(hf[:1]), hb[:0:-1]], 0))
    return jnp.concatenate(cols, -1)


def _left_mm_kernel(m_ref, x_ref, o_ref):
    o_ref[0] = _dot(m_ref[...], x_ref[0]).astype(o_ref.dtype)


def _left_mm(mat, x, out_dtype, lane_blk):
    g, k, n = x.shape
    m = mat.shape[0]
    lane_blk = min(lane_blk, n)
    return pl.pallas_call(
        _left_mm_kernel, grid=(g, n // lane_blk),
        in_specs=[_resident((m, k)), pl.BlockSpec((1, k, lane_blk), lambda i, j: (i, 0, j))],
        out_specs=pl.BlockSpec((1, m, lane_blk), lambda i, j: (i, 0, j)),
        out_shape=jax.ShapeDtypeStruct((g, m, n), out_dtype),
        compiler_params=_cparams(("parallel", "parallel")),
        name="left_mm",
    )(mat, x)


def _cs(num, den, rows, cols):
    ang = 2.0 * np.pi * np.outer(np.arange(rows), np.arange(cols)) * (num / den)
    return np.cos(ang), np.sin(ang)


def _twiddle(n1, n2):
    k1 = lax.broadcasted_iota(jnp.int32, (n1, n2, 1), 0)
    m2 = lax.broadcasted_iota(jnp.int32, (n1, n2, 1), 1)
    ang = (k1 * m2).astype(F32) * (2.0 * math.pi / (n1 * n2))
    return jnp.cos(ang), jnp.sin(ang)


def _hy_prep_t_kernel(u_ref, w_ref, o_ref):
    u = u_ref[0].astype(F32)
    length = u.shape[1]
    lane = lax.broadcasted_iota(jnp.int32, u.shape, 1)
    up = jnp.where(lane == 0, 0.0, pltpu.roll(u, 1, axis=1))
    dn = jnp.where(lane == length - 1, 0.0, pltpu.roll(u, length - 1, axis=1))
    w = w_ref[...]
    y = up * w[:, 0:1] + u * w[:, 1:2] + dn * w[:, 2:3]
    for n1 in range(length // FFT_N2):
        o_ref[0, :, n1, :] = y[:, n1 * FFT_N2:(n1 + 1) * FFT_N2]


def _hy_prep_t(ut, conv_w_t):
    b, nch, length = ut.shape
    cb = 64
    n1 = length // FFT_N2
    return pl.pallas_call(
        _hy_prep_t_kernel,
        grid=(b, nch // cb),
        in_specs=[pl.BlockSpec((1, cb, length), lambda i, j: (i, j, 0)),
                  pl.BlockSpec((cb, 3), lambda i, j: (j, 0))],
        out_specs=pl.BlockSpec((1, cb, n1, FFT_N2), lambda i, j: (i, j, 0, 0)),
        out_shape=jax.ShapeDtypeStruct((b, nch, n1, FFT_N2), F32),
        compiler_params=_cparams(("parallel", "parallel")),
        name="hy_prep_t",
    )(ut, conv_w_t)


def _hy_filt_t_kernel(z_ref, msk_ref, w1_ref, b1_ref, fr_ref, w2_ref, b2_ref, w3_ref, ld_ref, sk_ref, o_ref):
    hi = lax.Precision.HIGHEST
    z = z_ref[...]
    fr = fr_ref[...]
    hid = jnp.sin(fr * (jnp.dot(w1_ref[...], z, preferred_element_type=F32, precision=hi) + b1_ref[...]))
    hid = jnp.sin(fr * (jnp.dot(w2_ref[...], hid, preferred_element_type=F32, precision=hi) + b2_ref[...]))
    h = jnp.dot(w3_ref[0], hid, preferred_element_type=F32, precision=hi)
    h = h * jnp.exp(-jnp.exp(ld_ref[0]) * z[0:1, :])
    msk = msk_ref[...]
    h = h * msk[0:1, :] + sk_ref[...] * msk[1:2, :]
    for s in range(o_ref.shape[1]):
        o_ref[:, s, :] = h[:, s * FFT_N2:(s + 1) * FFT_N2]


def _hy_filters_t(length, w1, b1, freq, w2, b2, w3, log_decay, skip):
    n = 2 * length
    tt = jnp.arange(n, dtype=jnp.int32)
    pos = jnp.where(tt < length, tt, n - tt).astype(F32)
    t = pos / max(length - 1, 1)
    w = 2.0 * math.pi * pos / length
    f = jnp.linspace(1e-4, HY_BANDS - 1, HY_BANDS, dtype=F32)
    ang = f[:, None] * w[None, :]
    z = jnp.concatenate([t[None, :], jnp.cos(ang), -jnp.sin(ang)], 0)
    z = jnp.pad(z, ((0, LANES - z.shape[0]), (0, 0)))
    msk = jnp.stack([(tt != length).astype(F32), (tt == 0).astype(F32)])
    msk = jnp.pad(msk, ((0, 6), (0, 0)))
    col = lambda a: _pad2(a.reshape(-1, 1), LANES, 1)
    c2 = 2 * HY_CH
    nf = w3.shape[0]
    w3d = w3.reshape(nf, 2, 2, HY_CH).transpose(2, 1, 3, 0).reshape(2, c2, nf)
    w3d = jnp.pad(w3d, ((0, 0), (0, 0), (0, LANES - nf)))
    ldd = log_decay.reshape(2, 2, HY_CH).transpose(1, 0, 2).reshape(2, c2, 1)
    rows = 8
    pb = rows * FFT_N2
    half = length // pb
    return pl.pallas_call(
        _hy_filt_t_kernel,
        grid=(n // pb,),
        in_specs=[pl.BlockSpec((LANES, pb), lambda i: (0, i)),
                  pl.BlockSpec((8, pb), lambda i: (0, i)),
                  _resident((LANES, LANES)), _resident((LANES, 1)), _resident((LANES, 1)),
                  _resident((LANES, LANES)), _resident((LANES, 1)),
                  pl.BlockSpec((1, c2, LANES), lambda i: (i // half, 0, 0)),
                  pl.BlockSpec((1, c2, 1), lambda i: (i // half, 0, 0)),
                  _resident((c2, 1))],
        out_specs=pl.BlockSpec((c2, rows, FFT_N2), lambda i: (0, i, 0)),
        out_shape=jax.ShapeDtypeStruct((c2, n // FFT_N2, FFT_N2), F32),
        compiler_params=_cparams(("parallel",)),
        name="hy_filters_t",
    )(z, msk, _pad2(w1.T, LANES, LANES), col(b1), col(freq), _pad2(w2.T, LANES, LANES), col(b2),
      w3d, ldd, skip.reshape(c2, 1))


def _dft_mats():
    n = FFT_N2
    c, s = _cs(1, n, n, n)
    f1_half = np.concatenate([c[:, :n // 2], -s[:, :n // 2]], 0)
    f1_full = np.concatenate([c, -s], 0)
    m2r = np.block([[c, -s], [s, c]])
    m2i = np.block([[c, s], [-s, c]])
    f1_inv = np.concatenate([c[:n // 2, :], -s[:n // 2, :]], 1) / (n * n)
    cast = lambda a: jnp.asarray(a, dtype=BF16)
    return cast(f1_half), cast(f1_full), cast(m2r), cast(m2i), cast(f1_inv)


def _twiddle2d(n):
    k1 = lax.broadcasted_iota(jnp.int32, (n, n), 0)
    m2 = lax.broadcasted_iota(jnp.int32, (n, n), 1)
    ang = (k1 * m2).astype(F32) * (2.0 * math.pi / (n * n))
    return jnp.cos(ang), jnp.sin(ang)


def _fwd_spectrum(xs, f1, m2r, c, s):
    n = FFT_N2
    a = _dot(f1, jnp.concatenate(xs, axis=1))
    ts = []
    for g in range(len(xs)):
        ar = a[:n, g * n:(g + 1) * n]
        ai = a[n:, g * n:(g + 1) * n]
        ts.append(jnp.concatenate([ar * c + ai * s, ai * c - ar * s], axis=1))
    return _dot(jnp.concatenate(ts, axis=0).astype(BF16), m2r)


def _filter_spec_t_kernel(t_ref, f1_ref, m2r_ref, c_ref, s_ref, o_ref):
    g = t_ref.shape[0]
    xs = [t_ref[i].astype(BF16) for i in range(g)]
    spec = _fwd_spectrum(xs, f1_ref[...], m2r_ref[...], c_ref[...], s_ref[...])
    for i in range(g):
        o_ref[i] = spec[i * FFT_N2:(i + 1) * FFT_N2]


def _filter_spec_t(taps, f1_full, m2r, twc, tws):
    nch, n1, n = taps.shape
    g = HY_GROUP
    return pl.pallas_call(
        _filter_spec_t_kernel,
        grid=(nch // g,),
        in_specs=[pl.BlockSpec((g, n1, n), lambda i: (i, 0, 0)),
                  _resident(f1_full.shape), _resident(m2r.shape), _resident((n, n)), _resident((n, n))],
        out_specs=pl.BlockSpec((g, n, 2 * n), lambda i: (i, 0, 0)),
        out_shape=jax.ShapeDtypeStruct((nch, n, 2 * n), F32),
        compiler_params=_cparams(("parallel",)),
        name="filter_spec_t",
    )(taps, f1_full, m2r, twc, tws)


def _hyena_core_kernel(x1_ref, x2_ref, v_ref, h0_ref, h1_ref, f1_ref, m2r_ref, m2i_ref, f1i_ref, c_ref, s_ref,
                       o_ref, ybuf):
    n = FFT_N2
    cb = v_ref.shape[1]
    c = c_ref[...]
    s = s_ref[...]

    def conv(xs, h_ref, g0):
        spec = _fwd_spectrum(xs, f1_ref[...], m2r_ref[...], c, s)
        ys = []
        for g in range(HY_GROUP):
            xr = spec[g * n:(g + 1) * n, :n]
            xi = spec[g * n:(g + 1) * n, n:]
            hh = h_ref[g0 + g]
            hr, hi = hh[:, :n], hh[:, n:]
            ys.append(jnp.concatenate([xr * hr - xi * hi, xr * hi + xi * hr], axis=1))
        bm = _dot(jnp.concatenate(ys, axis=0).astype(BF16), m2i_ref[...])
        bs = []
        for g in range(HY_GROUP):
            br = bm[g * n:(g + 1) * n, :n]
            bi = bm[g * n:(g + 1) * n, n:]
            bs.append(jnp.concatenate([br * c - bi * s, bi * c + br * s], axis=0))
        y = _dot(f1i_ref[...], jnp.concatenate(bs, axis=1).astype(BF16))
        return [y[:, g * n:(g + 1) * n] for g in range(HY_GROUP)]

    def body(i, carry):
        for chain in range(HY_CHAINS):
            g0 = pl.multiple_of(i * (HY_CHAINS * HY_GROUP), HY_GROUP) + chain * HY_GROUP
            y1 = conv([v_ref[0, g0 + g].astype(BF16) for g in range(HY_GROUP)], h0_ref, g0)
            z = [(y1[g] * x1_ref[0, g0 + g]).astype(BF16) for g in range(HY_GROUP)]
            y2 = conv(z, h1_ref, g0)
            for g in range(HY_GROUP):
                ybuf[g0 + g] = y2[g] * x2_ref[0, g0 + g]
        return carry

    lax.fori_loop(0, cb // (HY_CHAINS * HY_GROUP), body, 0)
    for n1 in range(ybuf.shape[1]):
        o_ref[0, :, n1 * n:(n1 + 1) * n] = ybuf[:, n1, :].astype(o_ref.dtype)


def _hyena_core(xs, spec, mats, twc, tws):
    f1_half, _, m2r, m2i, f1_inv = mats
    b, _, n1, n = xs.shape
    cb = HY_CBLK
    nblk = HY_CH // cb
    xspec = lambda off: pl.BlockSpec((1, cb, n1, n), lambda j, i: (i, off * nblk + j, 0, 0))
    hspec = lambda off: pl.BlockSpec((cb, n, 2 * n), lambda j, i: (off * nblk + j, 0, 0))
    return pl.pallas_call(
        _hyena_core_kernel,
        grid=(nblk, b),
        in_specs=[xspec(0), xspec(1), xspec(2), hspec(0), hspec(1),
                  _resident(f1_half.shape), _resident(m2r.shape), _resident(m2i.shape), _resident(f1_inv.shape),
                  _resident((n, n)), _resident((n, n))],
        out_specs=pl.BlockSpec((1, cb, n1 * n), lambda j, i: (i, j, 0)),
        out_shape=jax.ShapeDtypeStruct((b, HY_CH, n1 * n), BF16),
        scratch_shapes=[pltpu.VMEM((cb, n1, n), F32)],
        compiler_params=_cparams(("parallel", "parallel")),
        name="hyena_core",
    )(xs, xs, xs, spec, spec, f1_half, m2r, m2i, f1_inv, twc, tws)


def _hy_ctx_kernel(v_ref, x1_ref, x2_ref, f_ref, fi_ref, h_ref, o_ref):
    nf = f_ref.shape[0] // 2
    zin = v_ref[0]
    gates = (x1_ref, x2_ref)
    for o in range(2):
        x = _dot(f_ref[...], zin)
        xr, xi = x[:nf], x[nf:]
        hr = h_ref[o, :nf]
        hi = h_ref[o, nf:]
        y = jnp.concatenate([xr * hr - xi * hi, xr * hi + xi * hr], axis=0).astype(BF16)
        zin = (_dot(fi_ref[...], y) * gates[o][0].astype(F32)).astype(BF16)
    o_ref[0] = zin


def _hy_ctx(v, x1, x2, taps):
    b, lc, c = v.shape
    nf = 2 * lc
    cm, sm = _cs(1, nf, nf, nf)
    fwd = jnp.asarray(np.concatenate([cm[:, :lc], -sm[:, :lc]], 0), dtype=BF16)
    fwd_full = jnp.asarray(np.concatenate([cm, -sm], 0), dtype=BF16)
    inv = jnp.asarray(np.concatenate([cm[:lc, :], -sm[:lc, :]], 1) / nf, dtype=BF16)
    spec = _left_mm(fwd_full, taps.astype(BF16).reshape(1, nf, 2 * c), F32, 2 * c)
    spec = spec.reshape(2 * nf, 2, c).transpose(1, 0, 2)
    blk = pl.BlockSpec((1, lc, c), lambda i: (i, 0, 0))
    return pl.pallas_call(
        _hy_ctx_kernel,
        grid=(b,),
        in_specs=[blk, blk, blk, _resident((2 * nf, lc)), _resident((lc, 2 * nf)),
                  _resident((2, 2 * nf, c))],
        out_specs=blk,
        out_shape=jax.ShapeDtypeStruct((b, lc, c), BF16),
        compiler_params=_cparams(("parallel",)),
        name="hy_ctx",
    )(v, x1, x2, fwd, inv, spec)


def _pair_rows(q2):
    lane = lax.broadcasted_iota(jnp.int32, q2.shape, 1)
    zero = jnp.zeros_like(q2)
    return jnp.concatenate([jnp.where(lane < NA_HD, q2, zero), jnp.where(lane >= NA_HD, q2, zero)], axis=0)


def _unpair_rows(o):
    r = o.shape[0] // 2
    lane = lax.broadcasted_iota(jnp.int32, (r, o.shape[1]), 1)
    return jnp.where(lane < NA_HD, o[:r], o[r:])


def _pair_softmax_pv(scores, values):
    m = scores[0].max(axis=-1, keepdims=True)
    for s in scores[1:]:
        m = jnp.maximum(m, s.max(axis=-1, keepdims=True))
    den = None
    acc = None
    for s, v in zip(scores, values):
        p = jnp.exp(s - m)
        d = p.sum(axis=-1, keepdims=True)
        a = _dot(p.astype(BF16), v)
        den = d if den is None else den + d
        acc = a if acc is None else acc + a
    return acc / den


def _natten_kernel(q_ref, k0, k1, k2, k3, v0, v1, v2, v3, kc_ref, vc_ref, bias_ref, o_ref,
                   kwin, vwin, *, rows):
    g = pl.program_id(1)
    rb = 4 * GRID_W
    for i, (kr, vr) in enumerate(((k0, v0), (k1, v1), (k2, v2), (k3, v3))):
        kwin[i * rb:(i + 1) * rb, :] = kr[0]
        vwin[i * rb:(i + 1) * rb, :] = vr[0]
    base = 4 * jnp.clip(2 * g - 1, 0, rows // 4 - 4)
    nwin = NA_WIN_R * GRID_W
    qscale = jnp.asarray(NA_HD ** -0.5, BF16)

    def row_body(rr, carry):
        r = 8 * g + rr
        rs = jnp.clip(r - NA_WIN_R // 2, 0, rows - NA_WIN_R)
        st = pl.multiple_of((rs - base) * GRID_W, GRID_W)
        d0 = rs - r + NA_WIN_R - 1
        qo = pl.multiple_of(rr * GRID_W, GRID_W)
        for p in range(NA_HEADS // 2):
            ls = slice(p * LANES, (p + 1) * LANES)
            qp = _pair_rows(q_ref[0, pl.ds(qo, GRID_W), ls] * qscale)
            s_lat = _dot_nt(qp, kwin[pl.ds(st, nwin), ls]) + bias_ref[d0, p].astype(F32)
            s_ctx = _dot_nt(qp, kc_ref[0, :, ls])
            o = _pair_softmax_pv([s_lat, s_ctx], [vwin[pl.ds(st, nwin), ls], vc_ref[0, :, ls]])
            o_ref[0, pl.ds(qo, GRID_W), ls] = _unpair_rows(o).astype(o_ref.dtype)
        return carry

    lax.fori_loop(0, 8, row_body, 0)


def _natten_bias(rpb):
    c = np.arange(GRID_W)[:, None]
    kc = np.arange(GRID_W)[None, :]
    cs = np.clip(c - NA_WIN_C // 2, 0, GRID_W - NA_WIN_C)
    valid = (kc >= cs) & (kc < cs + NA_WIN_C)
    dc = np.clip(kc - c + NA_WIN_C - 1, 0, 2 * NA_WIN_C - 2)
    tb = jnp.where(valid[None, None], rpb[:, :, dc], NEG_BIG)
    slabs = []
    for d0 in range(NA_WIN_R):
        s = tb[:, d0:d0 + NA_WIN_R]
        s = s.transpose(0, 2, 1, 3).reshape(NA_HEADS, GRID_W, NA_WIN_R * GRID_W)
        slabs.append(s.reshape(NA_HEADS // 2, 2 * GRID_W, NA_WIN_R * GRID_W))
    return jnp.stack(slabs).astype(BF16)


def _natten(uq, uc, bias):
    b, length, _ = uq.shape
    c = NA_HEADS * NA_HD
    rows = length // GRID_W
    rb = 4 * GRID_W
    nkb = length // rb
    qrows = 8 * GRID_W
    lc = uc.shape[1]

    def kv_spec(col, off):
        return pl.BlockSpec((1, rb, c), lambda i, g: (i, jnp.clip(2 * g - 1, 0, nkb - 4) + off, col))

    return pl.pallas_call(
        functools.partial(_natten_kernel, rows=rows),
        grid=(b, rows // 8),
        in_specs=[pl.BlockSpec((1, qrows, c), lambda i, g: (i, g, 0))]
                 + [kv_spec(1, o) for o in range(4)] + [kv_spec(2, o) for o in range(4)]
                 + [pl.BlockSpec((1, lc, c), lambda i, g: (i, 0, 4)),
                    pl.BlockSpec((1, lc, c), lambda i, g: (i, 0, 5)),
                    _resident(bias.shape)],
        out_specs=pl.BlockSpec((1, qrows, c), lambda i, g: (i, g, 0)),
        out_shape=jax.ShapeDtypeStruct((b, length, c), BF16),
        scratch_shapes=[pltpu.VMEM((4 * rb, c), BF16), pltpu.VMEM((4 * rb, c), BF16)],
        compiler_params=_cparams(("parallel", "parallel")),
        name="natten",
    )(uq, *([uq] * 8), uc, uc, bias)


def _ctx_attn_kernel(q_ref, k_ref, v_ref, o_ref):
    qscale = jnp.asarray(NA_HD ** -0.5, BF16)
    for p in range(NA_HEADS // 2):
        ls = slice(p * LANES, (p + 1) * LANES)
        qp = _pair_rows(q_ref[0, :, ls] * qscale)
        o = _pair_softmax_pv([_dot_nt(qp, k_ref[0, :, ls])], [v_ref[0, :, ls]])
        o_ref[0, :, ls] = _unpair_rows(o).astype(o_ref.dtype)


def _ctx_attn(u):
    b, lc, _ = u.shape
    c = NA_HEADS * NA_HD
    spec = lambda col: pl.BlockSpec((1, lc, c), lambda i: (i, 0, col))
    return pl.pallas_call(
        _ctx_attn_kernel,
        grid=(b,),
        in_specs=[spec(3), spec(4), spec(5)],
        out_specs=pl.BlockSpec((1, lc, c), lambda i: (i, 0, 0)),
        out_shape=jax.ShapeDtypeStruct((b, lc, c), BF16),
        compiler_params=_cparams(("parallel",)),
        name="ctx_attn",
    )(u, u, u)


def _post_kernel(x_ref, mod_ref, ya_ref, yb_ref, wa_ref, wb_ref, lng_ref, lnb_ref, w1_ref, w2_ref, o_ref, *, ya_t):
    x = x_ref[0]
    ya = ya_ref[0]
    if ya_t:
        ya = ya.astype(F32).T.astype(BF16)
    y = _dot(ya, wa_ref[...]) + _dot(yb_ref[0], wb_ref[...])
    m = mod_ref[0, 0]
    lng = lng_ref[...]
    lnb = lnb_ref[...]
    x1 = _norm_rows(ALPHA * x + m[2:3] * y) * lng[0:1] + lnb[0:1]
    h = (_norm_rows(x1) * (1.0 + m[4:5]) + m[3:4]).astype(BF16)
    ff = w1_ref.shape[1]
    step = 1024
    acc = jnp.zeros_like(x)
    for c in range(ff // step):
        a = _dot(h, w1_ref[:, c * step:(c + 1) * step])
        a = jnp.maximum(a, 0.0)
        acc = acc + _dot((a * a).astype(BF16), w2_ref[c * step:(c + 1) * step, :])
    o_ref[0] = _norm_rows(ALPHA * x1 + m[5:6] * acc) * lng[1:2] + lnb[1:2]


def _post(x, modtab, mod_row, ya, yb, wa, wb, lng, lnb, w1, w2, ya_t=False):
    b, r, d = x.shape
    ka, kb = wa.shape[0], wb.shape[0]
    row = lambda k: pl.BlockSpec((1, ROW_BLK, k), lambda i, j: (i, j, 0))
    ya_spec = pl.BlockSpec((1, ka, ROW_BLK), lambda i, j: (i, 0, j)) if ya_t else row(ka)
    return pl.pallas_call(
        functools.partial(_post_kernel, ya_t=ya_t),
        grid=(b, r // ROW_BLK),
        in_specs=[row(d), pl.BlockSpec((1, 1, 6, d), lambda i, j: (i, mod_row, 0, 0)), ya_spec, row(kb),
                  _resident(wa.shape), _resident(wb.shape), _resident(lng.shape), _resident(lnb.shape),
                  _resident(w1.shape), _resident(w2.shape)],
        out_specs=row(d),
        out_shape=jax.ShapeDtypeStruct((b, r, d), F32),
        compiler_params=_cparams(("parallel", "parallel")),
        name="post_mixer",
    )(x, modtab, ya, yb, wa, wb, lng, lnb, w1, w2)


def _rope(x, cos, sinl, sinr):
    reps = x.shape[1] // LANES
    tile = lambda t: jnp.concatenate([t] * reps, axis=1)
    n = x.shape[1]
    quarter = MLA_ROPE // 4
    return (x * tile(cos) + pltpu.roll(x, n - quarter, axis=1) * tile(sinl)
            + pltpu.roll(x, quarter, axis=1) * tile(sinr))


def _front_cd_kernel(x_ref, xc_ref, mod_ref, w_ref, qn_ref, kvn_ref, wuq_ref, wuk_ref, wuv_ref, epe_ref, one_ref,
                     fng_ref, fnb_ref, cbd_ref, sbd_ref,
                     cq_ref, slq_ref, srq_ref, ck_ref, slk_ref, srk_ref,
                     q_ref, k_ref, v_ref, p_ref, qf_ref, *, ctx_blk):
    m = mod_ref[0, 0]
    is_ctx = (jnp.zeros((ROW_BLK, 1), jnp.int32) + pl.program_id(1)) == ctx_blk
    x = jnp.where(is_ctx, xc_ref[0], x_ref[0])
    h = _norm_rows(x) * (1.0 + m[1:2]) + m[0:1]
    u = _dot(h.astype(BF16), w_ref[...])
    o_kv = MLA_Q_RANK
    o_fn = o_kv + MLA_KV_RANK
    o_pe = o_fn + FN_CH

    def rms(x, g):
        return x * lax.rsqrt(jnp.mean(x * x, axis=-1, keepdims=True) + LN_EPS) * g

    cq = rms(u[:, :o_kv], qn_ref[...]).astype(BF16)
    q = _dot(cq, wuq_ref[...])
    q_ref[0] = _rope(q, cq_ref[...], slq_ref[...], srq_ref[...]).astype(BF16)

    ckv = rms(u[:, o_kv:o_fn], kvn_ref[...]).astype(BF16)
    kpe = _dot(u[:, o_pe:].astype(BF16), epe_ref[...])
    k = _dot(ckv, wuk_ref[...]) + _rope(kpe, ck_ref[...], slk_ref[...], srk_ref[...])
    k_ref[0] = k.astype(BF16)
    v_ref[0] = (_dot(ckv, wuv_ref[...]) + one_ref[...]).astype(BF16)

    uf = u[:, o_fn:o_pe]
    lane = lax.broadcasted_iota(jnp.int32, uf.shape, 1)
    mean = jnp.zeros_like(uf)
    for g in range(FN_GROUPS):
        sel = (lane >= g * FN_GD) & (lane < (g + 1) * FN_GD)
        mg = jnp.sum(jnp.where(sel, uf, 0.0), axis=-1, keepdims=True) * (1.0 / FN_GD)
        mean = jnp.where(sel, mg, mean)
    uc = uf - mean
    var = jnp.zeros_like(uf)
    for g in range(FN_GROUPS):
        sel = (lane >= g * FN_GD) & (lane < (g + 1) * FN_GD)
        vg = jnp.sum(jnp.where(sel, uc * uc, 0.0), axis=-1, keepdims=True) * (1.0 / FN_GD)
        var = jnp.where(sel, vg, var)
    ug = (uc * lax.rsqrt(var + LN_EPS) * fng_ref[...] + fnb_ref[...]).astype(BF16)
    p_ref[0] = _dot(ug, cbd_ref[...]).astype(BF16)
    qf_ref[0] = _dot(ug, sbd_ref[...]).astype(BF16)


def _rope_tables(length, lt, scale):
    t = jnp.arange(lt, dtype=jnp.int32)
    rows = (t // GRID_W).astype(F32)
    cols = (t % GRID_W).astype(F32)
    half = MLA_ROPE // 2
    inv = ROPE_THETA ** (-jnp.arange(0, half, 2, dtype=F32) / half)
    ar = rows[:, None] * inv[None, :]
    ac = cols[:, None] * inv[None, :]
    ang = jnp.concatenate([ar, ar, ac, ac], -1)
    is_lat = (t < length)[:, None]
    cos = jnp.where(is_lat, jnp.cos(ang), 1.0)
    sin = jnp.where(is_lat, jnp.sin(ang), 0.0)
    qd = MLA_ROPE // 4
    ones = jnp.ones((lt, MLA_NOPE), F32)
    zeros = jnp.zeros((lt, MLA_NOPE), F32)
    tail1 = jnp.ones((lt, HEAD_PAD - MLA_NOPE - MLA_ROPE), F32)
    tail0 = jnp.zeros((lt, HEAD_PAD - MLA_NOPE - MLA_ROPE), F32)
    z8 = jnp.zeros((lt, qd), F32)
    c = jnp.concatenate([ones, cos, tail1], -1)
    sl = jnp.concatenate([zeros, -sin[:, :qd], z8, -sin[:, 2 * qd:3 * qd], z8, tail0], -1)
    sr = jnp.concatenate([zeros, z8, sin[:, qd:2 * qd], z8, sin[:, 3 * qd:], tail0], -1)
    return c * scale, sl * scale, sr * scale


def _head_slots(w, per_head, take_from, take_n):
    k = w.shape[0]
    w3 = w.reshape(k, MLA_HEADS, per_head)[:, :, take_from:take_from + take_n]
    w3 = jnp.pad(w3, ((0, 0), (0, 0), (0, HEAD_PAD - take_n)))
    return w3.reshape(k, MLA_HEADS * HEAD_PAD)


def _front_cd(xl, xc, modtab, w_in, q_norm, w_uq, kv_norm, w_ukv, fn_g, fn_b):
    b, length, d = xl.shape
    lt = length + xc.shape[1]
    nlat = length // ROW_BLK
    o_kv = MLA_Q_RANK
    o_pe = o_kv + MLA_KV_RANK
    o_fn = o_pe + MLA_ROPE
    hw = MLA_HEADS * HEAD_PAD
    w_perm = jnp.concatenate([w_in[:, :o_pe], w_in[:, o_fn:], w_in[:, o_pe:o_fn],
                              jnp.zeros((d, LANES - MLA_ROPE), w_in.dtype)], -1).astype(BF16)
    wuq = _head_slots(w_uq, MLA_NOPE + MLA_ROPE, 0, MLA_NOPE + MLA_ROPE).astype(BF16)
    wuk = _head_slots(w_ukv, MLA_NOPE + MLA_V, 0, MLA_NOPE).astype(BF16)
    wuv = _head_slots(w_ukv, MLA_NOPE + MLA_V, MLA_NOPE, MLA_V).astype(BF16)
    epe = np.zeros((LANES, hw), np.float32)
    for hd in range(MLA_HEADS):
        for i in range(MLA_ROPE):
            epe[i, hd * HEAD_PAD + MLA_NOPE + i] = 1.0
    epe = jnp.asarray(epe, dtype=BF16)
    ones_col = np.zeros((1, hw), np.float32)
    ones_col[0, MLA_V::HEAD_PAD] = 1.0
    ones_col = jnp.asarray(ones_col)
    cm, sm = _cs(1, FN_GD, FN_GD, FN_GD)
    eye = np.eye(FN_GROUPS)
    cbd = jnp.asarray(np.kron(eye, cm), dtype=BF16)
    sbd = jnp.asarray(np.kron(eye, -sm), dtype=BF16)
    qtab = _rope_tables(length, lt, (MLA_NOPE + MLA_ROPE) ** -0.5 * math.log2(math.e))
    ktab = _rope_tables(length, lt, 1.0)
    row = lambda n: pl.BlockSpec((1, ROW_BLK, n), lambda i, j: (i, j, 0))
    tab = pl.BlockSpec((ROW_BLK, HEAD_PAD), lambda i, j: (j, 0))
    out = lambda n: jax.ShapeDtypeStruct((b, lt, n), BF16)
    return pl.pallas_call(
        functools.partial(_front_cd_kernel, ctx_blk=nlat),
        grid=(b, lt // ROW_BLK),
        in_specs=[pl.BlockSpec((1, ROW_BLK, d), lambda i, j: (i, jnp.minimum(j, nlat - 1), 0)),
                  pl.BlockSpec((1, ROW_BLK, d), lambda i, j: (i, 0, 0)),
                  pl.BlockSpec((1, 1, 6, d), lambda i, j: (i, j // nlat, 0, 0)),
                  _resident(w_perm.shape), _resident((1, MLA_Q_RANK)), _resident((1, MLA_KV_RANK)),
                  _resident(wuq.shape), _resident(wuk.shape), _resident(wuv.shape), _resident(epe.shape),
                  _resident(ones_col.shape),
                  _resident((1, FN_CH)), _resident((1, FN_CH)), _resident(cbd.shape), _resident(sbd.shape),
                  tab, tab, tab, tab, tab, tab],
        out_specs=[row(hw), row(hw), row(hw), row(FN_CH), row(FN_CH)],
        out_shape=[out(hw), out(hw), out(hw), out(FN_CH), out(FN_CH)],
        compiler_params=_cparams(("parallel", "parallel")),
        name="front_cd",
    )(xl, xc, modtab, w_perm, q_norm.reshape(1, -1), kv_norm.reshape(1, -1), wuq, wuk, wuv, epe, ones_col,
      fn_g.reshape(1, -1), fn_b.reshape(1, -1), cbd, sbd, *qtab, *ktab)


def _flash_kernel(q_ref, k_ref, v_ref, o_ref, *, chunk):
    q = q_ref[0]
    nchunk = k_ref.shape[1] // chunk
    m = None
    acc = None
    for c in range(nchunk):
        ks = c * chunk
        s = _dot_nt(q, k_ref[0, ks:ks + chunk, :])
        cm = s.max(axis=-1, keepdims=True)
        m_new = cm if c == 0 else jnp.maximum(m, cm)
        pv = _dot(jnp.exp2(s - m_new).astype(BF16), v_ref[0, ks:ks + chunk, :])
        acc = pv if c == 0 else acc * jnp.exp2(m - m_new) + pv
        m = m_new
    o_ref[0] = (acc / acc[:, MLA_V:MLA_V + 1]).astype(o_ref.dtype)


def _mla_attention(q, k, v, length):
    b, lt, hw = q.shape
    heads = hw // HEAD_PAD
    tq = 256
    chunk = 8448
    assert lt % chunk == 0
    kv = pl.BlockSpec((1, lt, HEAD_PAD), lambda i, h, j: (i, 0, h))
    qs = pl.BlockSpec((1, tq, HEAD_PAD), lambda i, h, j: (i, j, h))
    return pl.pallas_call(
        functools.partial(_flash_kernel, chunk=chunk),
        grid=(b, heads, length // tq),
        in_specs=[qs, kv, kv],
        out_specs=qs,
        out_shape=jax.ShapeDtypeStruct((b, length, hw), BF16),
        compiler_params=_cparams(("parallel", "parallel", "parallel")),
        name="mla_attention",
    )(q, k, v)


def _fn1_kernel(m_ref, zr_ref, zi_ref, o_ref):
    n = zr_ref.shape[1]
    a = _dot(m_ref[...], jnp.concatenate([zr_ref[0], zi_ref[0]], axis=0))
    o_ref[0, 0] = a[:n].astype(o_ref.dtype)
    o_ref[0, 1] = a[n:].astype(o_ref.dtype)


def _fn2_kernel(a_ref, c_ref, s_ref, m_ref, o_ref):
    for i in range(a_ref.shape[2]):
        ar = a_ref[0, 0, i].astype(F32)
        ai = a_ref[0, 1, i].astype(F32)
        c = c_ref[i]
        s = s_ref[i]
        t = jnp.concatenate([ar * c + ai * s, ai * c - ar * s], axis=0).astype(BF16)
        o_ref[0, i] = _dot(m_ref[...], t).astype(o_ref.dtype)


def _fnet(p, qn):
    b, length, c = p.shape
    n1 = 128
    n2 = length // n1
    cm, sm = _cs(1, n1, n1, n1)
    m1 = jnp.asarray(np.block([[cm, sm], [-sm, cm]]), dtype=BF16)
    c2, s2 = _cs(1, n2, n2, n2)
    m2 = jnp.asarray(np.concatenate([c2, s2], 1) / math.sqrt(length * FN_GD), dtype=BF16)
    twc, tws = _twiddle(n1, n2)
    lane_blk = 2048
    zs = pl.BlockSpec((1, n1, lane_blk), lambda i, j: (i, 0, j))
    a = pl.pallas_call(
        _fn1_kernel,
        grid=(b, n2 * c // lane_blk),
        in_specs=[_resident(m1.shape), zs, zs],
        out_specs=pl.BlockSpec((1, 2, n1, lane_blk), lambda i, j: (i, 0, 0, j)),
        out_shape=jax.ShapeDtypeStruct((b, 2, n1, n2 * c), BF16),
        compiler_params=_cparams(("parallel", "parallel")),
        name="fnet_stage1",
    )(m1, p.reshape(b, n1, n2 * c), qn.reshape(b, n1, n2 * c))
    kb = 8
    y = pl.pallas_call(
        _fn2_kernel,
        grid=(b, n1 // kb),
        in_specs=[pl.BlockSpec((1, 2, kb, n2, c), lambda i, j: (i, 0, j, 0, 0)),
                  pl.BlockSpec((kb, n2, 1), lambda i, j: (j, 0, 0)),
                  pl.BlockSpec((kb, n2, 1), lambda i, j: (j, 0, 0)),
                  _resident(m2.shape)],
        out_specs=pl.BlockSpec((1, kb, n2, c), lambda i, j: (i, j, 0, 0)),
        out_shape=jax.ShapeDtypeStruct((b, n1, n2, c), BF16),
        compiler_params=_cparams(("parallel", "parallel")),
        name="fnet_stage2",
    )(a.reshape(b, 2, n1, n2, c), twc, tws, m2)
    return y.transpose(0, 2, 1, 3).reshape(b, length, c)


def kernel(x, c, ctx, c_ctx, mod_w, mod_b, ln_g, ln_b, mlp_w1, mlp_w2,
           ab_w_in, ab_w_out, hy_conv_w, hy_w1, hy_b1, hy_freq, hy_w2, hy_b2, hy_w3, hy_log_decay, hy_skip, na_rpb,
           cd_w_in, cd_w_out, mla_q_norm, mla_w_uq, mla_kv_norm, mla_w_ukv, fn_norm_g, fn_norm_b):
    b, length, d = x.shape
    lc = ctx.shape[1]

    cc = jnp.concatenate([c, c_ctx[None], jnp.zeros((8 - b - 1, d), F32)], 0)
    mods = _mod_vectors(cc, mod_w, mod_b).reshape(DEPTH, 8, 6, d)
    modtab = [jnp.stack([mods[l, :b], jnp.broadcast_to(mods[l, b], (b, 6, d))], axis=1) for l in range(DEPTH)]

    n_hy = 3 * HY_CH
    w_in = ab_w_in[0].astype(BF16)
    uq, ut = _front_ab_lat(x, modtab[0], w_in[:, n_hy:], w_in[:, :n_hy].T)
    uc = _front_ab_ctx(ctx, modtab[0], w_in)
    fargs = (hy_w1[0], hy_b1[0], hy_freq[0], hy_w2[0], hy_b2[0], hy_w3[0], hy_log_decay[0])
    mats = _dft_mats()
    twc, tws = _twiddle2d(FFT_N2)
    spec = _filter_spec_t(_hy_filters_t(length, *fargs, hy_skip[0]), mats[1], mats[2], twc, tws)
    y_hy_t = _hyena_core(_hy_prep_t(ut, hy_conv_w[0].T), spec, mats, twc, tws)
    x1c, x2c, vc = _hy_prep(uc, hy_conv_w[0], 0, lc // ROW_BLK)
    y_hy_c = _hy_ctx(vc, x1c, x2c, _bidir_taps(_hy_filters(lc, *fargs), hy_skip[0], lc))
    y_na = _natten(uq, uc, _natten_bias(na_rpb[0]))
    y_na_c = _ctx_attn(uc)
    w_out = ab_w_out[0].astype(BF16)
    mlp = (ln_g[0], ln_b[0], mlp_w1[0].astype(BF16), mlp_w2[0].astype(BF16))
    xl = _post(x, modtab[0], 0, y_hy_t, y_na, w_out[:HY_CH], w_out[HY_CH:], *mlp, ya_t=True)
    xc = _post(ctx, modtab[0], 1, y_hy_c, y_na_c, w_out[:HY_CH], w_out[HY_CH:], *mlp)

    q, k, vv, p, qn = _front_cd(xl, xc, modtab[1], cd_w_in[0], mla_q_norm[0], mla_w_uq[0], mla_kv_norm[0],
                                mla_w_ukv[0], fn_norm_g[0], fn_norm_b[0])
    o = _mla_attention(q, k, vv, length)
    y_fn = _fnet(p[:, :length], qn[:, :length])
    w_out = cd_w_out[0]
    n_mla = MLA_HEADS * MLA_V
    wa = jnp.pad(w_out[:n_mla].reshape(MLA_HEADS, MLA_V, d), ((0, 0), (0, HEAD_PAD - MLA_V), (0, 0)))
    wa = wa.reshape(MLA_HEADS * HEAD_PAD, d).astype(BF16)
    return _post(xl, modtab[1], 0, o, y_fn, wa, w_out[n_mla:].astype(BF16),
                 ln_g[1], ln_b[1], mlp_w1[1].astype(BF16), mlp_w2[1].astype(BF16))
```

```python
import functools
import math

import numpy as np
import jax
import jax.numpy as jnp
from jax import lax
from jax.experimental import pallas as pl
from jax.experimental.pallas import tpu as pltpu

F32 = jnp.float32
BF16 = jnp.bfloat16

D_MODEL = 1024
DEPTH = 2
GRID_W = 64
HY_CH = 512
HY_EMB = 33
HY_BANDS = (HY_EMB - 1) // 2
NA_HEADS = 8
NA_HD = 64
NA_WIN_R = 8
NA_WIN_C = 16
MLA_HEADS = 8
MLA_Q_RANK = 384
MLA_KV_RANK = 256
MLA_NOPE = 64
MLA_ROPE = 32
MLA_V = 96
ROPE_THETA = 10000.0
FN_CH = 256
FN_GROUPS = 4
FN_GD = FN_CH // FN_GROUPS
D_FF = 4 * D_MODEL
ALPHA = (2.0 * DEPTH) ** 0.25
LN_EPS = 1e-5

LANES = 128
ROW_BLK = 256
HEAD_PAD = 128
FFT_N2 = 128
HY_GROUP = 16
HY_CHAINS = 2
HY_CBLK = HY_GROUP * HY_CHAINS
VMEM_LIMIT = 56 * 1024 * 1024
NEG_BIG = -1e30


def _cparams(sem, vmem=VMEM_LIMIT):
    return pltpu.CompilerParams(dimension_semantics=sem, vmem_limit_bytes=vmem)


def _resident(shape):
    nd = len(shape)
    return pl.BlockSpec(shape, lambda *_: (0,) * nd, pipeline_mode=pl.Buffered(1))


def _norm_rows(x):
    mu = jnp.mean(x, axis=-1, keepdims=True)
    xc = x - mu
    var = jnp.mean(xc * xc, axis=-1, keepdims=True)
    return xc * lax.rsqrt(var + LN_EPS)


def _dot(a, b):
    return jnp.dot(a, b, preferred_element_type=F32)


def _dot_nt(a, b):
    return lax.dot_general(a, b, (((1,), (1,)), ((), ())), preferred_element_type=F32)


def _mod_kernel(c_ref, w_ref, b_ref, o_ref):
    c = c_ref[...]
    s = c * (1.0 / (1.0 + jnp.exp(-c)))
    o_ref[0] = jnp.dot(s, w_ref[0], preferred_element_type=F32,
                       precision=lax.Precision.HIGHEST) + b_ref[0]


def _mod_vectors(cc, mod_w, mod_b):
    depth, d, n = mod_w.shape
    nb = 1024
    return pl.pallas_call(
        _mod_kernel,
        grid=(depth, n // nb),
        in_specs=[pl.BlockSpec((8, d), lambda l, j: (0, 0)),
                  pl.BlockSpec((1, d, nb), lambda l, j: (l, 0, j)),
                  pl.BlockSpec((1, 1, nb), lambda l, j: (l, 0, j))],
        out_specs=pl.BlockSpec((1, 8, nb), lambda l, j: (l, 0, j)),
        out_shape=jax.ShapeDtypeStruct((depth, 8, n), F32),
        compiler_params=_cparams(("parallel", "parallel")),
        name="mod_vectors",
    )(cc, mod_w, mod_b.reshape(depth, 1, n))


def _front_ab_ctx_kernel(x_ref, mod_ref, w_ref, u_ref):
    m = mod_ref[0, 0]
    h = _norm_rows(x_ref[0]) * (1.0 + m[1:2]) + m[0:1]
    u_ref[0] = _dot(h.astype(BF16), w_ref[...]).astype(BF16)


def _front_ab_ctx(xc, modtab, w_in):
    b, lc, d = xc.shape
    n = w_in.shape[1]
    return pl.pallas_call(
        _front_ab_ctx_kernel,
        grid=(b, lc // ROW_BLK),
        in_specs=[pl.BlockSpec((1, ROW_BLK, d), lambda i, j: (i, j, 0)),
                  pl.BlockSpec((1, 1, 6, d), lambda i, j: (i, 1, 0, 0)),
                  _resident((d, n))],
        out_specs=pl.BlockSpec((1, ROW_BLK, n), lambda i, j: (i, j, 0)),
        out_shape=jax.ShapeDtypeStruct((b, lc, n), BF16),
        compiler_params=_cparams(("parallel", "parallel")),
        name="front_ab_ctx",
    )(xc, modtab, w_in)


def _front_ab_lat_kernel(x_ref, mod_ref, wq_ref, wht_ref, u_ref, ut_ref):
    m = mod_ref[0, 0]
    h = (_norm_rows(x_ref[0]) * (1.0 + m[1:2]) + m[0:1]).astype(BF16)
    u_ref[0] = _dot(h, wq_ref[...]).astype(BF16)
    ut_ref[0] = _dot_nt(wht_ref[...], h).astype(BF16)


def _front_ab_lat(x, modtab, w_qkv, w_hy_t):
    b, length, d = x.shape
    nq = w_qkv.shape[1]
    nh = w_hy_t.shape[0]
    return pl.pallas_call(
        _front_ab_lat_kernel,
        grid=(b, length // ROW_BLK),
        in_specs=[pl.BlockSpec((1, ROW_BLK, d), lambda i, j: (i, j, 0)),
                  pl.BlockSpec((1, 1, 6, d), lambda i, j: (i, 0, 0, 0)),
                  _resident((d, nq)), _resident((nh, d))],
        out_specs=[pl.BlockSpec((1, ROW_BLK, nq), lambda i, j: (i, j, 0)),
                   pl.BlockSpec((1, nh, ROW_BLK), lambda i, j: (i, 0, j))],
        out_shape=[jax.ShapeDtypeStruct((b, length, nq), BF16),
                   jax.ShapeDtypeStruct((b, nh, length), BF16)],
        compiler_params=_cparams(("parallel", "parallel")),
        name="front_ab_lat",
    )(x, modtab, w_qkv, w_hy_t)


def _hy_prep_kernel(cur_ref, prev_ref, next_ref, w_ref, x1_ref, x2_ref, v_ref, *, nblk):
    j = pl.program_id(1)
    cur = cur_ref[0].astype(F32)
    rows = cur.shape[0]
    has_prev = (j > 0).astype(F32)
    has_next = (j < nblk - 1).astype(F32)
    prev_row = prev_ref[0][7:8].astype(F32) * has_prev
    next_row = next_ref[0][0:1].astype(F32) * has_next
    rid = lax.broadcasted_iota(jnp.int32, (rows, 1), 0)
    up = jnp.where(rid == 0, prev_row, pltpu.roll(cur, 1, axis=0))
    dn = jnp.where(rid == rows - 1, next_row, pltpu.roll(cur, rows - 1, axis=0))
    w = w_ref[...]
    y = up * w[0:1] + cur * w[1:2] + dn * w[2:3]
    c = HY_CH
    x1_ref[0] = y[:, :c].astype(BF16)
    x2_ref[0] = y[:, c:2 * c].astype(BF16)
    v_ref[0] = y[:, 2 * c:].astype(BF16)


def _hy_prep(u, conv_w, blk0, nblk):
    b, lt, _ = u.shape
    n = 3 * HY_CH
    sub = ROW_BLK // 8
    last8 = lt // 8 - 1
    out = jax.ShapeDtypeStruct((b, nblk * ROW_BLK, HY_CH), BF16)
    ospec = pl.BlockSpec((1, ROW_BLK, HY_CH), lambda i, j: (i, j, 0))
    return pl.pallas_call(
        functools.partial(_hy_prep_kernel, nblk=nblk),
        grid=(b, nblk),
        in_specs=[pl.BlockSpec((1, ROW_BLK, n), lambda i, j: (i, blk0 + j, 0)),
                  pl.BlockSpec((1, 8, n), lambda i, j: (i, jnp.maximum((blk0 + j) * sub - 1, 0), 0)),
                  pl.BlockSpec((1, 8, n), lambda i, j: (i, jnp.minimum((blk0 + j + 1) * sub, last8), 0)),
                  _resident((3, n))],
        out_specs=[ospec, ospec, ospec],
        out_shape=[out, out, out],
        compiler_params=_cparams(("parallel", "parallel")),
        name="hy_prep",
    )(u, u, u, conv_w)


def _hy_filt_kernel(z_ref, w1_ref, b1_ref, fr_ref, w2_ref, b2_ref, w3_ref, ld_ref, o_ref):
    hi = lax.Precision.HIGHEST
    z = z_ref[...]
    fr = fr_ref[...]
    hid = jnp.sin(fr * (jnp.dot(z, w1_ref[...], preferred_element_type=F32, precision=hi) + b1_ref[...]))
    hid = jnp.sin(fr * (jnp.dot(hid, w2_ref[...], preferred_element_type=F32, precision=hi) + b2_ref[...]))
    h = jnp.dot(hid, w3_ref[...], preferred_element_type=F32, precision=hi)
    t = z[:, 0:1]
    o_ref[...] = h * jnp.exp(-t * jnp.exp(ld_ref[...]))


def _pad2(a, rows, cols):
    return jnp.pad(a, ((0, rows - a.shape[0]), (0, cols - a.shape[1])))


def _hy_filters(length, w1, b1, freq, w2, b2, w3, log_decay):
    pos = jnp.arange(length, dtype=F32)
    t = pos / max(length - 1, 1)
    w = 2.0 * math.pi * pos / length
    f = jnp.linspace(1e-4, HY_BANDS - 1, HY_BANDS, dtype=F32)
    ang = w[:, None] * f[None, :]
    z = jnp.concatenate([t[:, None], jnp.cos(ang), -jnp.sin(ang)], -1)
    z = _pad2(z, length, LANES)
    n = w3.shape[1]
    rb = min(length, 512)
    vec = lambda a: _pad2(a.reshape(1, -1), 1, LANES)
    return pl.pallas_call(
        _hy_filt_kernel,
        grid=(length // rb,),
        in_specs=[pl.BlockSpec((rb, LANES), lambda i: (i, 0)),
                  _resident((LANES, LANES)), _resident((1, LANES)), _resident((1, LANES)),
                  _resident((LANES, LANES)), _resident((1, LANES)),
                  _resident((LANES, n)), _resident((1, n))],
        out_specs=pl.BlockSpec((rb, n), lambda i: (i, 0)),
        out_shape=jax.ShapeDtypeStruct((length, n), F32),
        compiler_params=_cparams(("parallel",)),
        name="hy_filters",
    )(z, _pad2(w1, LANES, LANES), vec(b1), vec(freq), _pad2(w2, LANES, LANES), vec(b2),
      _pad2(w3, LANES, n), log_decay.reshape(1, n))


def _bidir_taps(h, skip, length):
    h4 = h.reshape(length, 2, 2, HY_CH)
    cols = []
    for o in range(2):
        hf = h4[:, o, 0].at[0].add(skip[o])
        hb = h4[:, o, 1]
        cols.append(jnp.concatenate([hf, jnp.zeros_like(hf[:1]), hb[:0:-1]], 0))
    return jnp.concatenate(cols, -1)


def _left_mm_kernel(m_ref, x_ref, o_ref):
    o_ref[0] = _dot(m_ref[...], x_ref[0]).astype(o_ref.dtype)


def _left_mm(mat, x, out_dtype, lane_blk):
    g, k, n = x.shape
    m = mat.shape[0]
    lane_blk = min(lane_blk, n)
    return pl.pallas_call(
        _left_mm_kernel, grid=(g, n // lane_blk),
        in_specs=[_resident((m, k)), pl.BlockSpec((1, k, lane_blk), lambda i, j: (i, 0, j))],
        out_specs=pl.BlockSpec((1, m, lane_blk), lambda i, j: (i, 0, j)),
        out_shape=jax.ShapeDtypeStruct((g, m, n), out_dtype),
        compiler_params=_cparams(("parallel", "parallel")),
        name="left_mm",
    )(mat, x)


def _cs(num, den, rows, cols):
    ang = 2.0 * np.pi * np.outer(np.arange(rows), np.arange(cols)) * (num / den)
    return np.cos(ang), np.sin(ang)


def _twiddle(n1, n2):
    k1 = lax.broadcasted_iota(jnp.int32, (n1, n2, 1), 0)
    m2 = lax.broadcasted_iota(jnp.int32, (n1, n2, 1), 1)
    ang = (k1 * m2).astype(F32) * (2.0 * math.pi / (n1 * n2))
    return jnp.cos(ang), jnp.sin(ang)


def _hy_prep_t_kernel(u_ref, w_ref, o_ref):
    u = u_ref[0].astype(F32)
    length = u.shape[1]
    lane = lax.broadcasted_iota(jnp.int32, u.shape, 1)
    up = jnp.where(lane == 0, 0.0, pltpu.roll(u, 1, axis=1))
    dn = jnp.where(lane == length - 1, 0.0, pltpu.roll(u, length - 1, axis=1))
    w = w_ref[...]
    y = up * w[:, 0:1] + u * w[:, 1:2] + dn * w[:, 2:3]
    for n1 in range(length // FFT_N2):
        o_ref[0, :, n1, :] = y[:, n1 * FFT_N2:(n1 + 1) * FFT_N2]


def _hy_prep_t(ut, conv_w_t):
    b, nch, length = ut.shape
    cb = 64
    n1 = length // FFT_N2
    return pl.pallas_call(
        _hy_prep_t_kernel,
        grid=(b, nch // cb),
        in_specs=[pl.BlockSpec((1, cb, length), lambda i, j: (i, j, 0)),
                  pl.BlockSpec((cb, 3), lambda i, j: (j, 0))],
        out_specs=pl.BlockSpec((1, cb, n1, FFT_N2), lambda i, j: (i, j, 0, 0)),
        out_shape=jax.ShapeDtypeStruct((b, nch, n1, FFT_N2), F32),
        compiler_params=_cparams(("parallel", "parallel")),
        name="hy_prep_t",
    )(ut, conv_w_t)


def _hy_filt_t_kernel(z_ref, msk_ref, w1_ref, b1_ref, fr_ref, w2_ref, b2_ref, w3_ref, ld_ref, sk_ref, o_ref):
    hi = lax.Precision.HIGHEST
    z = z_ref[...]
    fr = fr_ref[...]
    hid = jnp.sin(fr * (jnp.dot(w1_ref[...], z, preferred_element_type=F32, precision=hi) + b1_ref[...]))
    hid = jnp.sin(fr * (jnp.dot(w2_ref[...], hid, preferred_element_type=F32, precision=hi) + b2_ref[...]))
    h = jnp.dot(w3_ref[0], hid, preferred_element_type=F32, precision=hi)
    h = h * jnp.exp(-jnp.exp(ld_ref[0]) * z[0:1, :])
    msk = msk_ref[...]
    h = h * msk[0:1, :] + sk_ref[...] * msk[1:2, :]
    for s in range(o_ref.shape[1]):
        o_ref[:, s, :] = h[:, s * FFT_N2:(s + 1) * FFT_N2]


def _hy_filters_t(length, w1, b1, freq, w2, b2, w3, log_decay, skip):
    n = 2 * length
    tt = jnp.arange(n, dtype=jnp.int32)
    pos = jnp.where(tt < length, tt, n - tt).astype(F32)
    t = pos / max(length - 1, 1)
    w = 2.0 * math.pi * pos / length
    f = jnp.linspace(1e-4, HY_BANDS - 1, HY_BANDS, dtype=F32)
    ang = f[:, None] * w[None, :]
    z = jnp.concatenate([t[None, :], jnp.cos(ang), -jnp.sin(ang)], 0)
    z = jnp.pad(z, ((0, LANES - z.shape[0]), (0, 0)))
    msk = jnp.stack([(tt != length).astype(F32), (tt == 0).astype(F32)])
    msk = jnp.pad(msk, ((0, 6), (0, 0)))
    col = lambda a: _pad2(a.reshape(-1, 1), LANES, 1)
    c2 = 2 * HY_CH
    nf = w3.shape[0]
    w3d = w3.reshape(nf, 2, 2, HY_CH).transpose(2, 1, 3, 0).reshape(2, c2, nf)
    w3d = jnp.pad(w3d, ((0, 0), (0, 0), (0, LANES - nf)))
    ldd = log_decay.reshape(2, 2, HY_CH).transpose(1, 0, 2).reshape(2, c2, 1)
    rows = 8
    pb = rows * FFT_N2
    half = length // pb
    return pl.pallas_call(
        _hy_filt_t_kernel,
        grid=(n // pb,),
        in_specs=[pl.BlockSpec((LANES, pb), lambda i: (0, i)),
                  pl.BlockSpec((8, pb), lambda i: (0, i)),
                  _resident((LANES, LANES)), _resident((LANES, 1)), _resident((LANES, 1)),
                  _resident((LANES, LANES)), _resident((LANES, 1)),
                  pl.BlockSpec((1, c2, LANES), lambda i: (i // half, 0, 0)),
                  pl.BlockSpec((1, c2, 1), lambda i: (i // half, 0, 0)),
                  _resident((c2, 1))],
        out_specs=pl.BlockSpec((c2, rows, FFT_N2), lambda i: (0, i, 0)),
        out_shape=jax.ShapeDtypeStruct((c2, n // FFT_N2, FFT_N2), F32),
        compiler_params=_cparams(("parallel",)),
        name="hy_filters_t",
    )(z, msk, _pad2(w1.T, LANES, LANES), col(b1), col(freq), _pad2(w2.T, LANES, LANES), col(b2),
      w3d, ldd, skip.reshape(c2, 1))


def _dft_mats():
    n = FFT_N2
    c, s = _cs(1, n, n, n)
    f1_half = np.concatenate([c[:, :n // 2], -s[:, :n // 2]], 0)
    f1_full = np.concatenate([c, -s], 0)
    m2r = np.block([[c, -s], [s, c]])
    m2i = np.block([[c, s], [-s, c]])
    f1_inv = np.concatenate([c[:n // 2, :], -s[:n // 2, :]], 1) / (n * n)
    cast = lambda a: jnp.asarray(a, dtype=BF16)
    return cast(f1_half), cast(f1_full), cast(m2r), cast(m2i), cast(f1_inv)


def _twiddle2d(n):
    k1 = lax.broadcasted_iota(jnp.int32, (n, n), 0)
    m2 = lax.broadcasted_iota(jnp.int32, (n, n), 1)
    ang = (k1 * m2).astype(F32) * (2.0 * math.pi / (n * n))
    return jnp.cos(ang), jnp.sin(ang)


def _fwd_spectrum(xs, f1, m2r, c, s):
    n = FFT_N2
    a = _dot(f1, jnp.concatenate(xs, axis=1))
    ts = []
    for g in range(len(xs)):
        ar = a[:n, g * n:(g + 1) * n]
        ai = a[n:, g * n:(g + 1) * n]
        ts.append(jnp.concatenate([ar * c + ai * s, ai * c - ar * s], axis=1))
    return _dot(jnp.concatenate(ts, axis=0).astype(BF16), m2r)


def _filter_spec_t_kernel(t_ref, f1_ref, m2r_ref, c_ref, s_ref, o_ref):
    g = t_ref.shape[0]
    xs = [t_ref[i].astype(BF16) for i in range(g)]
    spec = _fwd_spectrum(xs, f1_ref[...], m2r_ref[...], c_ref[...], s_ref[...])
    for i in range(g):
        o_ref[i] = spec[i * FFT_N2:(i + 1) * FFT_N2]


def _filter_spec_t(taps, f1_full, m2r, twc, tws):
    nch, n1, n = taps.shape
    g = HY_GROUP
    return pl.pallas_call(
        _filter_spec_t_kernel,
        grid=(nch // g,),
        in_specs=[pl.BlockSpec((g, n1, n), lambda i: (i, 0, 0)),
                  _resident(f1_full.shape), _resident(m2r.shape), _resident((n, n)), _resident((n, n))],
        out_specs=pl.BlockSpec((g, n, 2 * n), lambda i: (i, 0, 0)),
        out_shape=jax.ShapeDtypeStruct((nch, n, 2 * n), F32),
        compiler_params=_cparams(("parallel",)),
        name="filter_spec_t",
    )(taps, f1_full, m2r, twc, tws)


def _hyena_core_kernel(x1_ref, x2_ref, v_ref, h0_ref, h1_ref, f1_ref, m2r_ref, m2i_ref, f1i_ref, c_ref, s_ref,
                       o_ref):
    n = FFT_N2
    c = c_ref[...]
    s = s_ref[...]

    def conv(xs, h_ref, g0):
        spec = _fwd_spectrum(xs, f1_ref[...], m2r_ref[...], c, s)
        ys = []
        for g in range(HY_GROUP):
            xr = spec[g * n:(g + 1) * n, :n]
            xi = spec[g * n:(g + 1) * n, n:]
            hh = h_ref[g0 + g]
            hr, hi = hh[:, :n], hh[:, n:]
            ys.append(jnp.concatenate([xr * hr - xi * hi, xr * hi + xi * hr], axis=1))
        bm = _dot(jnp.concatenate(ys, axis=0).astype(BF16), m2i_ref[...])
        bs = []
        for g in range(HY_GROUP):
            br = bm[g * n:(g + 1) * n, :n]
            bi = bm[g * n:(g + 1) * n, n:]
            bs.append(jnp.concatenate([br * c - bi * s, bi * c + br * s], axis=0))
        y = _dot(f1i_ref[...], jnp.concatenate(bs, axis=1).astype(BF16))
        return [y[:, g * n:(g + 1) * n] for g in range(HY_GROUP)]

    for chain in range(HY_CHAINS):
        g0 = chain * HY_GROUP
        y1 = conv([v_ref[0, g0 + g].astype(BF16) for g in range(HY_GROUP)], h0_ref, g0)
        z = [(y1[g] * x1_ref[0, g0 + g]).astype(BF16) for g in range(HY_GROUP)]
        y2 = conv(z, h1_ref, g0)
        for g in range(HY_GROUP):
            o_ref[0, :, g0 + g, :] = y2[g] * x2_ref[0, g0 + g]


def _hyena_core(xs, spec, mats, twc, tws):
    f1_half, _, m2r, m2i, f1_inv = mats
    b, _, n1, n = xs.shape
    cb = HY_CBLK
    nblk = HY_CH // cb
    xspec = lambda off: pl.BlockSpec((1, cb, n1, n), lambda j, i: (i, off * nblk + j, 0, 0))
    hspec = lambda off: pl.BlockSpec((cb, n, 2 * n), lambda j, i: (off * nblk + j, 0, 0))
    return pl.pallas_call(
        _hyena_core_kernel,
        grid=(nblk, b),
        in_specs=[xspec(0), xspec(1), xspec(2), hspec(0), hspec(1),
                  _resident(f1_half.shape), _resident(m2r.shape), _resident(m2i.shape), _resident(f1_inv.shape),
                  _resident((n, n)), _resident((n, n))],
        out_specs=pl.BlockSpec((1, n1, cb, n), lambda j, i: (i, 0, j, 0)),
        out_shape=jax.ShapeDtypeStruct((b, n1, HY_CH, n), F32),
        compiler_params=_cparams(("parallel", "parallel")),
        name="hyena_core",
    )(xs, xs, xs, spec, spec, f1_half, m2r, m2i, f1_inv, twc, tws)


def _hy_ctx_kernel(v_ref, x1_ref, x2_ref, f_ref, fi_ref, h_ref, o_ref):
    nf = f_ref.shape[0] // 2
    zin = v_ref[0]
    gates = (x1_ref, x2_ref)
    for o in range(2):
        x = _dot(f_ref[...], zin)
        xr, xi = x[:nf], x[nf:]
        hr = h_ref[o, :nf]
        hi = h_ref[o, nf:]
        y = jnp.concatenate([xr * hr - xi * hi, xr * hi + xi * hr], axis=0).astype(BF16)
        zin = (_dot(fi_ref[...], y) * gates[o][0].astype(F32)).astype(BF16)
    o_ref[0] = zin


def _hy_ctx(v, x1, x2, taps):
    b, lc, c = v.shape
    nf = 2 * lc
    cm, sm = _cs(1, nf, nf, nf)
    fwd = jnp.asarray(np.concatenate([cm[:, :lc], -sm[:, :lc]], 0), dtype=BF16)
    fwd_full = jnp.asarray(np.concatenate([cm, -sm], 0), dtype=BF16)
    inv = jnp.asarray(np.concatenate([cm[:lc, :], -sm[:lc, :]], 1) / nf, dtype=BF16)
    spec = _left_mm(fwd_full, taps.astype(BF16).reshape(1, nf, 2 * c), F32, 2 * c)
    spec = spec.reshape(2 * nf, 2, c).transpose(1, 0, 2)
    blk = pl.BlockSpec((1, lc, c), lambda i: (i, 0, 0))
    return pl.pallas_call(
        _hy_ctx_kernel,
        grid=(b,),
        in_specs=[blk, blk, blk, _resident((2 * nf, lc)), _resident((lc, 2 * nf)),
                  _resident((2, 2 * nf, c))],
        out_specs=blk,
        out_shape=jax.ShapeDtypeStruct((b, lc, c), BF16),
        compiler_params=_cparams(("parallel",)),
        name="hy_ctx",
    )(v, x1, x2, fwd, inv, spec)


def _pair_rows(q2):
    lane = lax.broadcasted_iota(jnp.int32, q2.shape, 1)
    zero = jnp.zeros_like(q2)
    return jnp.concatenate([jnp.where(lane < NA_HD, q2, zero), jnp.where(lane >= NA_HD, q2, zero)], axis=0)


def _unpair_rows(o):
    r = o.shape[0] // 2
    lane = lax.broadcasted_iota(jnp.int32, (r, o.shape[1]), 1)
    return jnp.where(lane < NA_HD, o[:r], o[r:])


def _pair_softmax_pv(scores, values):
    m = scores[0].max(axis=-1, keepdims=True)
    for s in scores[1:]:
        m = jnp.maximum(m, s.max(axis=-1, keepdims=True))
    den = None
    acc = None
    for s, v in zip(scores, values):
        p = jnp.exp(s - m)
        d = p.sum(axis=-1, keepdims=True)
        a = _dot(p.astype(BF16), v)
        den = d if den is None else den + d
        acc = a if acc is None else acc + a
    return acc / den


def _natten_kernel(q_ref, k0, k1, k2, k3, v0, v1, v2, v3, kc_ref, vc_ref, bias_ref, o_ref,
                   kwin, vwin, *, rows):
    g = pl.program_id(1)
    rb = 4 * GRID_W
    for i, (kr, vr) in enumerate(((k0, v0), (k1, v1), (k2, v2), (k3, v3))):
        kwin[i * rb:(i + 1) * rb, :] = kr[0]
        vwin[i * rb:(i + 1) * rb, :] = vr[0]
    base = 4 * jnp.clip(2 * g - 1, 0, rows // 4 - 4)
    nwin = NA_WIN_R * GRID_W
    qscale = jnp.asarray(NA_HD ** -0.5, BF16)

    def row_body(rr, carry):
        r = 8 * g + rr
        rs = jnp.clip(r - NA_WIN_R // 2, 0, rows - NA_WIN_R)
        st = pl.multiple_of((rs - base) * GRID_W, GRID_W)
        d0 = rs - r + NA_WIN_R - 1
        qo = pl.multiple_of(rr * GRID_W, GRID_W)
        for p in range(NA_HEADS // 2):
            ls = slice(p * LANES, (p + 1) * LANES)
            qp = _pair_rows(q_ref[0, pl.ds(qo, GRID_W), ls] * qscale)
            s_lat = _dot_nt(qp, kwin[pl.ds(st, nwin), ls]) + bias_ref[d0, p].astype(F32)
            s_ctx = _dot_nt(qp, kc_ref[0, :, ls])
            o = _pair_softmax_pv([s_lat, s_ctx], [vwin[pl.ds(st, nwin), ls], vc_ref[0, :, ls]])
            o_ref[0, pl.ds(qo, GRID_W), ls] = _unpair_rows(o).astype(o_ref.dtype)
        return carry

    lax.fori_loop(0, 8, row_body, 0)


def _natten_bias(rpb):
    c = np.arange(GRID_W)[:, None]
    kc = np.arange(GRID_W)[None, :]
    cs = np.clip(c - NA_WIN_C // 2, 0, GRID_W - NA_WIN_C)
    valid = (kc >= cs) & (kc < cs + NA_WIN_C)
    dc = np.clip(kc - c + NA_WIN_C - 1, 0, 2 * NA_WIN_C - 2)
    tb = jnp.where(valid[None, None], rpb[:, :, dc], NEG_BIG)
    slabs = []
    for d0 in range(NA_WIN_R):
        s = tb[:, d0:d0 + NA_WIN_R]
        s = s.transpose(0, 2, 1, 3).reshape(NA_HEADS, GRID_W, NA_WIN_R * GRID_W)
        slabs.append(s.reshape(NA_HEADS // 2, 2 * GRID_W, NA_WIN_R * GRID_W))
    return jnp.stack(slabs).astype(BF16)


def _natten(uq, uc, bias):
    b, length, _ = uq.shape
    c = NA_HEADS * NA_HD
    rows = length // GRID_W
    rb = 4 * GRID_W
    nkb = length // rb
    qrows = 8 * GRID_W
    lc = uc.shape[1]

    def kv_spec(col, off):
        return pl.BlockSpec((1, rb, c), lambda i, g: (i, jnp.clip(2 * g - 1, 0, nkb - 4) + off, col))

    return pl.pallas_call(
        functools.partial(_natten_kernel, rows=rows),
        grid=(b, rows // 8),
        in_specs=[pl.BlockSpec((1, qrows, c), lambda i, g: (i, g, 0))]
                 + [kv_spec(1, o) for o in range(4)] + [kv_spec(2, o) for o in range(4)]
                 + [pl.BlockSpec((1, lc, c), lambda i, g: (i, 0, 4)),
                    pl.BlockSpec((1, lc, c), lambda i, g: (i, 0, 5)),
                    _resident(bias.shape)],
        out_specs=pl.BlockSpec((1, qrows, c), lambda i, g: (i, g, 0)),
        out_shape=jax.ShapeDtypeStruct((b, length, c), BF16),
        scratch_shapes=[pltpu.VMEM((4 * rb, c), BF16), pltpu.VMEM((4 * rb, c), BF16)],
        compiler_params=_cparams(("parallel", "parallel")),
        name="natten",
    )(uq, *([uq] * 8), uc, uc, bias)


def _ctx_attn_kernel(q_ref, k_ref, v_ref, o_ref):
    qscale = jnp.asarray(NA_HD ** -0.5, BF16)
    for p in range(NA_HEADS // 2):
        ls = slice(p * LANES, (p + 1) * LANES)
        qp = _pair_rows(q_ref[0, :, ls] * qscale)
        o = _pair_softmax_pv([_dot_nt(qp, k_ref[0, :, ls])], [v_ref[0, :, ls]])
        o_ref[0, :, ls] = _unpair_rows(o).astype(o_ref.dtype)


def _ctx_attn(u):
    b, lc, _ = u.shape
    c = NA_HEADS * NA_HD
    spec = lambda col: pl.BlockSpec((1, lc, c), lambda i: (i, 0, col))
    return pl.pallas_call(
        _ctx_attn_kernel,
        grid=(b,),
        in_specs=[spec(3), spec(4), spec(5)],
        out_specs=pl.BlockSpec((1, lc, c), lambda i: (i, 0, 0)),
        out_shape=jax.ShapeDtypeStruct((b, lc, c), BF16),
        compiler_params=_cparams(("parallel",)),
        name="ctx_attn",
    )(u, u, u)


def _post_kernel(x_ref, mod_ref, ya_ref, yb_ref, wa_ref, wb_ref, lng_ref, lnb_ref, w1_ref, w2_ref, o_ref, *, ya_t):
    m = mod_ref[0, 0]
    lng = lng_ref[...]
    lnb = lnb_ref[...]
    ff = w1_ref.shape[1]
    step = 1024
    sub = x_ref.shape[1]
    for t in range(x_ref.shape[1] // sub):
        rs = slice(t * sub, (t + 1) * sub)
        x = x_ref[0, rs, :]
        if ya_t:
            tiles = range(t * sub // LANES, (t + 1) * sub // LANES)
            ya = jnp.concatenate([ya_ref[0, s].T for s in tiles], axis=0).astype(BF16)
        else:
            ya = ya_ref[0, rs, :]
        y = _dot(ya, wa_ref[...]) + _dot(yb_ref[0, rs, :], wb_ref[...])
        x1 = _norm_rows(ALPHA * x + m[2:3] * y) * lng[0:1] + lnb[0:1]
        h = (_norm_rows(x1) * (1.0 + m[4:5]) + m[3:4]).astype(BF16)
        acc = jnp.zeros_like(x)
        for c in range(ff // step):
            a = _dot(h, w1_ref[:, c * step:(c + 1) * step])
            a = jnp.maximum(a, 0.0)
            acc = acc + _dot((a * a).astype(BF16), w2_ref[c * step:(c + 1) * step, :])
        o_ref[0, rs, :] = _norm_rows(ALPHA * x1 + m[5:6] * acc) * lng[1:2] + lnb[1:2]


def _post(x, modtab, mod_row, ya, yb, wa, wb, lng, lnb, w1, w2, ya_t=False):
    b, r, d = x.shape
    ka, kb = wa.shape[0], wb.shape[0]
    rows = min(r, 2 * ROW_BLK)
    row = lambda k: pl.BlockSpec((1, rows, k), lambda i, j: (i, j, 0))
    ya_spec = pl.BlockSpec((1, rows // LANES, ka, LANES), lambda i, j: (i, j, 0, 0)) if ya_t else row(ka)
    return pl.pallas_call(
        functools.partial(_post_kernel, ya_t=ya_t),
        grid=(b, r // rows),
        in_specs=[row(d), pl.BlockSpec((1, 1, 6, d), lambda i, j: (i, mod_row, 0, 0)), ya_spec, row(kb),
                  _resident(wa.shape), _resident(wb.shape), _resident(lng.shape), _resident(lnb.shape),
                  _resident(w1.shape), _resident(w2.shape)],
        out_specs=row(d),
        out_shape=jax.ShapeDtypeStruct((b, r, d), F32),
        compiler_params=_cparams(("parallel", "parallel")),
        name="post_mixer",
    )(x, modtab, ya, yb, wa, wb, lng, lnb, w1, w2)


def _rope(x, cos, sinl, sinr):
    reps = x.shape[1] // LANES
    tile = lambda t: jnp.concatenate([t] * reps, axis=1)
    n = x.shape[1]
    quarter = MLA_ROPE // 4
    return (x * tile(cos) + pltpu.roll(x, n - quarter, axis=1) * tile(sinl)
            + pltpu.roll(x, quarter, axis=1) * tile(sinr))


def _front_cd_kernel(x_ref, xc_ref, mod_ref, w_ref, qn_ref, kvn_ref, wuq_ref, wuk_ref, wuv_ref, epe_ref, one_ref,
                     fng_ref, fnb_ref, cbd_ref, sbd_ref,
                     cq_ref, slq_ref, srq_ref, ck_ref, slk_ref, srk_ref,
                     q_ref, k_ref, v_ref, p_ref, qf_ref, *, ctx_blk):
    m = mod_ref[0, 0]
    is_ctx = (jnp.zeros((ROW_BLK, 1), jnp.int32) + pl.program_id(1)) == ctx_blk
    x = jnp.where(is_ctx, xc_ref[0], x_ref[0])
    h = _norm_rows(x) * (1.0 + m[1:2]) + m[0:1]
    u = _dot(h.astype(BF16), w_ref[...])
    o_kv = MLA_Q_RANK
    o_fn = o_kv + MLA_KV_RANK
    o_pe = o_fn + FN_CH

    def rms(x, g):
        return x * lax.rsqrt(jnp.mean(x * x, axis=-1, keepdims=True) + LN_EPS) * g

    cq = rms(u[:, :o_kv], qn_ref[...]).astype(BF16)
    q = _dot(cq, wuq_ref[...])
    q_ref[0] = _rope(q, cq_ref[...], slq_ref[...], srq_ref[...]).astype(BF16)

    ckv = rms(u[:, o_kv:o_fn], kvn_ref[...]).astype(BF16)
    kpe = _dot(u[:, o_pe:].astype(BF16), epe_ref[...])
    k = _dot(ckv, wuk_ref[...]) + _rope(kpe, ck_ref[...], slk_ref[...], srk_ref[...])
    k_ref[0] = k.astype(BF16)
    v_ref[0] = (_dot(ckv, wuv_ref[...]) + one_ref[...]).astype(BF16)

    uf = u[:, o_fn:o_pe]
    lane = lax.broadcasted_iota(jnp.int32, uf.shape, 1)
    mean = jnp.zeros_like(uf)
    for g in range(FN_GROUPS):
        sel = (lane >= g * FN_GD) & (lane < (g + 1) * FN_GD)
        mg = jnp.sum(jnp.where(sel, uf, 0.0), axis=-1, keepdims=True) * (1.0 / FN_GD)
        mean = jnp.where(sel, mg, mean)
    uc = uf - mean
    var = jnp.zeros_like(uf)
    for g in range(FN_GROUPS):
        sel = (lane >= g * FN_GD) & (lane < (g + 1) * FN_GD)
        vg = jnp.sum(jnp.where(sel, uc * uc, 0.0), axis=-1, keepdims=True) * (1.0 / FN_GD)
        var = jnp.where(sel, vg, var)
    ug = (uc * lax.rsqrt(var + LN_EPS) * fng_ref[...] + fnb_ref[...]).astype(BF16)
    p_ref[0] = _dot(ug, cbd_ref[...]).astype(BF16)
    qf_ref[0] = _dot(ug, sbd_ref[...]).astype(BF16)


def _rope_tables(length, lt, scale):
    t = jnp.arange(lt, dtype=jnp.int32)
    rows = (t // GRID_W).astype(F32)
    cols = (t % GRID_W).astype(F32)
    half = MLA_ROPE // 2
    inv = ROPE_THETA ** (-jnp.arange(0, half, 2, dtype=F32) / half)
    ar = rows[:, None] * inv[None, :]
    ac = cols[:, None] * inv[None, :]
    ang = jnp.concatenate([ar, ar, ac, ac], -1)
    is_lat = (t < length)[:, None]
    cos = jnp.where(is_lat, jnp.cos(ang), 1.0)
    sin = jnp.where(is_lat, jnp.sin(ang), 0.0)
    qd = MLA_ROPE // 4
    ones = jnp.ones((lt, MLA_NOPE), F32)
    zeros = jnp.zeros((lt, MLA_NOPE), F32)
    tail1 = jnp.ones((lt, HEAD_PAD - MLA_NOPE - MLA_ROPE), F32)
    tail0 = jnp.zeros((lt, HEAD_PAD - MLA_NOPE - MLA_ROPE), F32)
    z8 = jnp.zeros((lt, qd), F32)
    c = jnp.concatenate([ones, cos, tail1], -1)
    sl = jnp.concatenate([zeros, -sin[:, :qd], z8, -sin[:, 2 * qd:3 * qd], z8, tail0], -1)
    sr = jnp.concatenate([zeros, z8, sin[:, qd:2 * qd], z8, sin[:, 3 * qd:], tail0], -1)
    return c * scale, sl * scale, sr * scale


def _head_slots(w, per_head, take_from, take_n):
    k = w.shape[0]
    w3 = w.reshape(k, MLA_HEADS, per_head)[:, :, take_from:take_from + take_n]
    w3 = jnp.pad(w3, ((0, 0), (0, 0), (0, HEAD_PAD - take_n)))
    return w3.reshape(k, MLA_HEADS * HEAD_PAD)


def _front_cd(xl, xc, modtab, w_in, q_norm, w_uq, kv_norm, w_ukv, fn_g, fn_b):
    b, length, d = xl.shape
    lt = length + xc.shape[1]
    nlat = length // ROW_BLK
    o_kv = MLA_Q_RANK
    o_pe = o_kv + MLA_KV_RANK
    o_fn = o_pe + MLA_ROPE
    hw = MLA_HEADS * HEAD_PAD
    w_perm = jnp.concatenate([w_in[:, :o_pe], w_in[:, o_fn:], w_in[:, o_pe:o_fn],
                              jnp.zeros((d, LANES - MLA_ROPE), w_in.dtype)], -1).astype(BF16)
    wuq = _head_slots(w_uq, MLA_NOPE + MLA_ROPE, 0, MLA_NOPE + MLA_ROPE).astype(BF16)
    wuk = _head_slots(w_ukv, MLA_NOPE + MLA_V, 0, MLA_NOPE).astype(BF16)
    wuv = _head_slots(w_ukv, MLA_NOPE + MLA_V, MLA_NOPE, MLA_V).astype(BF16)
    epe = np.zeros((LANES, hw), np.float32)
    for hd in range(MLA_HEADS):
        for i in range(MLA_ROPE):
            epe[i, hd * HEAD_PAD + MLA_NOPE + i] = 1.0
    epe = jnp.asarray(epe, dtype=BF16)
    ones_col = np.zeros((1, hw), np.float32)
    ones_col[0, MLA_V::HEAD_PAD] = 1.0
    ones_col = jnp.asarray(ones_col)
    cm, sm = _cs(1, FN_GD, FN_GD, FN_GD)
    eye = np.eye(FN_GROUPS)
    cbd = jnp.asarray(np.kron(eye, cm), dtype=BF16)
    sbd = jnp.asarray(np.kron(eye, -sm), dtype=BF16)
    qtab = _rope_tables(length, lt, (MLA_NOPE + MLA_ROPE) ** -0.5 * math.log2(math.e))
    ktab = _rope_tables(length, lt, 1.0)
    row = lambda n: pl.BlockSpec((1, ROW_BLK, n), lambda i, j: (i, j, 0))
    tab = pl.BlockSpec((ROW_BLK, HEAD_PAD), lambda i, j: (j, 0))
    out = lambda n: jax.ShapeDtypeStruct((b, lt, n), BF16)
    return pl.pallas_call(
        functools.partial(_front_cd_kernel, ctx_blk=nlat),
        grid=(b, lt // ROW_BLK),
        in_specs=[pl.BlockSpec((1, ROW_BLK, d), lambda i, j: (i, jnp.minimum(j, nlat - 1), 0)),
                  pl.BlockSpec((1, ROW_BLK, d), lambda i, j: (i, 0, 0)),
                  pl.BlockSpec((1, 1, 6, d), lambda i, j: (i, j // nlat, 0, 0)),
                  _resident(w_perm.shape), _resident((1, MLA_Q_RANK)), _resident((1, MLA_KV_RANK)),
                  _resident(wuq.shape), _resident(wuk.shape), _resident(wuv.shape), _resident(epe.shape),
                  _resident(ones_col.shape),
                  _resident((1, FN_CH)), _resident((1, FN_CH)), _resident(cbd.shape), _resident(sbd.shape),
                  tab, tab, tab, tab, tab, tab],
        out_specs=[row(hw), row(hw), row(hw), row(FN_CH), row(FN_CH)],
        out_shape=[out(hw), out(hw), out(hw), out(FN_CH), out(FN_CH)],
        compiler_params=_cparams(("parallel", "parallel")),
        name="front_cd",
    )(xl, xc, modtab, w_perm, q_norm.reshape(1, -1), kv_norm.reshape(1, -1), wuq, wuk, wuv, epe, ones_col,
      fn_g.reshape(1, -1), fn_b.reshape(1, -1), cbd, sbd, *qtab, *ktab)


def _mla_kernel(q_ref, k_ref, v_ref, o_ref, *, rows):
    for i in range(q_ref.shape[1] // rows):
        rs = slice(i * rows, (i + 1) * rows)
        s = _dot_nt(q_ref[0, rs, :], k_ref[0])
        m = s.max(axis=-1, keepdims=True)
        acc = _dot(jnp.exp2(s - m).astype(BF16), v_ref[0])
        o_ref[0, rs, :] = (acc / acc[:, MLA_V:MLA_V + 1]).astype(o_ref.dtype)


def _mla_attention(q, k, v, length):
    b, lt, hw = q.shape
    heads = hw // HEAD_PAD
    tq = 1024
    kv = pl.BlockSpec((1, lt, HEAD_PAD), lambda i, h, j: (i, 0, h))
    qs = pl.BlockSpec((1, tq, HEAD_PAD), lambda i, h, j: (i, j, h))
    return pl.pallas_call(
        functools.partial(_mla_kernel, rows=256),
        grid=(b, heads, length // tq),
        in_specs=[qs, kv, kv],
        out_specs=qs,
        out_shape=jax.ShapeDtypeStruct((b, length, hw), BF16),
        compiler_params=_cparams(("parallel", "parallel", "parallel")),
        name="mla_attention",
    )(q, k, v)


def _fn1_kernel(m_ref, zr_ref, zi_ref, o_ref):
    n = zr_ref.shape[1]
    a = _dot(m_ref[...], jnp.concatenate([zr_ref[0], zi_ref[0]], axis=0))
    o_ref[0, 0] = a[:n].astype(o_ref.dtype)
    o_ref[0, 1] = a[n:].astype(o_ref.dtype)


def _fn2_kernel(a_ref, c_ref, s_ref, m_ref, o_ref):
    for i in range(a_ref.shape[2]):
        ar = a_ref[0, 0, i].astype(F32)
        ai = a_ref[0, 1, i].astype(F32)
        c = c_ref[i]
        s = s_ref[i]
        t = jnp.concatenate([ar * c + ai * s, ai * c - ar * s], axis=0).astype(BF16)
        o_ref[0, i] = _dot(m_ref[...], t).astype(o_ref.dtype)


def _fnet(p, qn):
    b, length, c = p.shape
    n1 = 128
    n2 = length // n1
    cm, sm = _cs(1, n1, n1, n1)
    m1 = jnp.asarray(np.block([[cm, sm], [-sm, cm]]), dtype=BF16)
    c2, s2 = _cs(1, n2, n2, n2)
    m2 = jnp.asarray(np.concatenate([c2, s2], 1) / math.sqrt(length * FN_GD), dtype=BF16)
    twc, tws = _twiddle(n1, n2)
    lane_blk = 2048
    zs = pl.BlockSpec((1, n1, lane_blk), lambda i, j: (i, 0, j))
    a = pl.pallas_call(
        _fn1_kernel,
        grid=(b, n2 * c // lane_blk),
        in_specs=[_resident(m1.shape), zs, zs],
        out_specs=pl.BlockSpec((1, 2, n1, lane_blk), lambda i, j: (i, 0, 0, j)),
        out_shape=jax.ShapeDtypeStruct((b, 2, n1, n2 * c), BF16),
        compiler_params=_cparams(("parallel", "parallel")),
        name="fnet_stage1",
    )(m1, p.reshape(b, n1, n2 * c), qn.reshape(b, n1, n2 * c))
    kb = 8
    y = pl.pallas_call(
        _fn2_kernel,
        grid=(b, n1 // kb),
        in_specs=[pl.BlockSpec((1, 2, kb, n2, c), lambda i, j: (i, 0, j, 0, 0)),
                  pl.BlockSpec((kb, n2, 1), lambda i, j: (j, 0, 0)),
                  pl.BlockSpec((kb, n2, 1), lambda i, j: (j, 0, 0)),
                  _resident(m2.shape)],
        out_specs=pl.BlockSpec((1, kb, n2, c), lambda i, j: (i, j, 0, 0)),
        out_shape=jax.ShapeDtypeStruct((b, n1, n2, c), BF16),
        compiler_params=_cparams(("parallel", "parallel")),
        name="fnet_stage2",
    )(a.reshape(b, 2, n1, n2, c), twc, tws, m2)
    return y.transpose(0, 2, 1, 3).reshape(b, length, c)


def kernel(x, c, ctx, c_ctx, mod_w, mod_b, ln_g, ln_b, mlp_w1, mlp_w2,
           ab_w_in, ab_w_out, hy_conv_w, hy_w1, hy_b1, hy_freq, hy_w2, hy_b2, hy_w3, hy_log_decay, hy_skip, na_rpb,
           cd_w_in, cd_w_out, mla_q_norm, mla_w_uq, mla_kv_norm, mla_w_ukv, fn_norm_g, fn_norm_b):
    b, length, d = x.shape
    lc = ctx.shape[1]

    cc = jnp.concatenate([c, c_ctx[None], jnp.zeros((8 - b - 1, d), F32)], 0)
    mods = _mod_vectors(cc, mod_w, mod_b).reshape(DEPTH, 8, 6, d)
    modtab = [jnp.stack([mods[l, :b], jnp.broadcast_to(mods[l, b], (b, 6, d))], axis=1) for l in range(DEPTH)]

    n_hy = 3 * HY_CH
    w_in = ab_w_in[0].astype(BF16)
    uq, ut = _front_ab_lat(x, modtab[0], w_in[:, n_hy:], w_in[:, :n_hy].T)
    uc = _front_ab_ctx(ctx, modtab[0], w_in)
    fargs = (hy_w1[0], hy_b1[0], hy_freq[0], hy_w2[0], hy_b2[0], hy_w3[0], hy_log_decay[0])
    mats = _dft_mats()
    twc, tws = _twiddle2d(FFT_N2)
    spec = _filter_spec_t(_hy_filters_t(length, *fargs, hy_skip[0]), mats[1], mats[2], twc, tws)
    y_hy_t = _hyena_core(_hy_prep_t(ut, hy_conv_w[0].T), spec, mats, twc, tws)
    x1c, x2c, vc = _hy_prep(uc, hy_conv_w[0], 0, lc // ROW_BLK)
    y_hy_c = _hy_ctx(vc, x1c, x2c, _bidir_taps(_hy_filters(lc, *fargs), hy_skip[0], lc))
    y_na = _natten(uq, uc, _natten_bias(na_rpb[0]))
    y_na_c = _ctx_attn(uc)
    w_out = ab_w_out[0].astype(BF16)
    mlp = (ln_g[0], ln_b[0], mlp_w1[0].astype(BF16), mlp_w2[0].astype(BF16))
    xl = _post(x, modtab[0], 0, y_hy_t, y_na, w_out[:HY_CH], w_out[HY_CH:], *mlp, ya_t=True)
    xc = _post(ctx, modtab[0], 1, y_hy_c, y_na_c, w_out[:HY_CH], w_out[HY_CH:], *mlp)

    q, k, vv, p, qn = _front_cd(xl, xc, modtab[1], cd_w_in[0], mla_q_norm[0], mla_w_uq[0], mla_kv_norm[0],
                                mla_w_ukv[0], fn_norm_g[0], fn_norm_b[0])
    o = _mla_attention(q, k, vv, length)
    y_fn = _fnet(p[:, :length], qn[:, :length])
    w_out = cd_w_out[0]
    n_mla = MLA_HEADS * MLA_V
    wa = jnp.pad(w_out[:n_mla].reshape(MLA_HEADS, MLA_V, d), ((0, 0), (0, HEAD_PAD - MLA_V), (0, 0)))
    wa = wa.reshape(MLA_HEADS * HEAD_PAD, d).astype(BF16)
    return _post(xl, modtab[1], 0, o, y_fn, wa, w_out[n_mla:].astype(BF16),
                 ln_g[1], ln_b[1], mlp_w1[1].astype(BF16), mlp_w2[1].astype(BF16))
```

```python
import functools
import math

import numpy as np
import jax
import jax.numpy as jnp
from jax import lax
from jax.experimental import pallas as pl
from jax.experimental.pallas import tpu as pltpu

F32 = jnp.float32
BF16 = jnp.bfloat16

D_MODEL = 1024
DEPTH = 2
GRID_W = 64
HY_CH = 512
HY_EMB = 33
HY_BANDS = (HY_EMB - 1) // 2
NA_HEADS = 8
NA_HD = 64
NA_WIN_R = 8
NA_WIN_C = 16
MLA_HEADS = 8
MLA_Q_RANK = 384
MLA_KV_RANK = 256
MLA_NOPE = 64
MLA_ROPE = 32
MLA_V = 96
ROPE_THETA = 10000.0
FN_CH = 256
FN_GROUPS = 4
FN_GD = FN_CH // FN_GROUPS
D_FF = 4 * D_MODEL
ALPHA = (2.0 * DEPTH) ** 0.25
LN_EPS = 1e-5

LANES = 128
ROW_BLK = 256
HEAD_PAD = 128
FFT_N2 = 128
NA_ROWS_PER_TRIP = 4
HY_GROUP = 16
HY_CHAINS = 2
HY_CBLK = HY_GROUP * HY_CHAINS
VMEM_LIMIT = 56 * 1024 * 1024
NEG_BIG = -1e30


def _cparams(sem, vmem=VMEM_LIMIT):
    return pltpu.CompilerParams(dimension_semantics=sem, vmem_limit_bytes=vmem)


def _resident(shape):
    nd = len(shape)
    return pl.BlockSpec(shape, lambda *_: (0,) * nd, pipeline_mode=pl.Buffered(1))


def _norm_rows(x):
    mu = jnp.mean(x, axis=-1, keepdims=True)
    xc = x - mu
    var = jnp.mean(xc * xc, axis=-1, keepdims=True)
    return xc * lax.rsqrt(var + LN_EPS)


def _dot(a, b):
    return jnp.dot(a, b, preferred_element_type=F32)


def _dot_nt(a, b):
    return lax.dot_general(a, b, (((1,), (1,)), ((), ())), preferred_element_type=F32)


def _mod_kernel(c_ref, w_ref, b_ref, o_ref):
    c = c_ref[...]
    s = c * (1.0 / (1.0 + jnp.exp(-c)))
    o_ref[0] = jnp.dot(s, w_ref[0], preferred_element_type=F32,
                       precision=lax.Precision.HIGHEST) + b_ref[0]


def _mod_vectors(cc, mod_w, mod_b):
    depth, d, n = mod_w.shape
    nb = 1024
    return pl.pallas_call(
        _mod_kernel,
        grid=(depth, n // nb),
        in_specs=[pl.BlockSpec((8, d), lambda l, j: (0, 0)),
                  pl.BlockSpec((1, d, nb), lambda l, j: (l, 0, j)),
                  pl.BlockSpec((1, 1, nb), lambda l, j: (l, 0, j))],
        out_specs=pl.BlockSpec((1, 8, nb), lambda l, j: (l, 0, j)),
        out_shape=jax.ShapeDtypeStruct((depth, 8, n), F32),
        compiler_params=_cparams(("parallel", "parallel")),
        name="mod_vectors",
    )(cc, mod_w, mod_b.reshape(depth, 1, n))


def _front_ab_ctx_kernel(x_ref, mod_ref, w_ref, u_ref):
    m = mod_ref[0, 0]
    h = _norm_rows(x_ref[0]) * (1.0 + m[1:2]) + m[0:1]
    u_ref[0] = _dot(h.astype(BF16), w_ref[...]).astype(BF16)


def _front_ab_ctx(xc, modtab, w_in):
    b, lc, d = xc.shape
    n = w_in.shape[1]
    return pl.pallas_call(
        _front_ab_ctx_kernel,
        grid=(b, lc // ROW_BLK),
        in_specs=[pl.BlockSpec((1, ROW_BLK, d), lambda i, j: (i, j, 0)),
                  pl.BlockSpec((1, 1, 6, d), lambda i, j: (i, 1, 0, 0)),
                  _resident((d, n))],
        out_specs=pl.BlockSpec((1, ROW_BLK, n), lambda i, j: (i, j, 0)),
        out_shape=jax.ShapeDtypeStruct((b, lc, n), BF16),
        compiler_params=_cparams(("parallel", "parallel")),
        name="front_ab_ctx",
    )(xc, modtab, w_in)


def _front_ab_lat_kernel(x_ref, mod_ref, wq_ref, wht_ref, u_ref, ut_ref):
    m = mod_ref[0, 0]
    h = (_norm_rows(x_ref[0]) * (1.0 + m[1:2]) + m[0:1]).astype(BF16)
    u_ref[0] = _dot(h, wq_ref[...]).astype(BF16)
    ut_ref[0] = _dot_nt(wht_ref[...], h).astype(BF16)


def _front_ab_lat(x, modtab, w_qkv, w_hy_t):
    b, length, d = x.shape
    nq = w_qkv.shape[1]
    nh = w_hy_t.shape[0]
    return pl.pallas_call(
        _front_ab_lat_kernel,
        grid=(b, length // ROW_BLK),
        in_specs=[pl.BlockSpec((1, ROW_BLK, d), lambda i, j: (i, j, 0)),
                  pl.BlockSpec((1, 1, 6, d), lambda i, j: (i, 0, 0, 0)),
                  _resident((d, nq)), _resident((nh, d))],
        out_specs=[pl.BlockSpec((1, ROW_BLK, nq), lambda i, j: (i, j, 0)),
                   pl.BlockSpec((1, nh, ROW_BLK), lambda i, j: (i, 0, j))],
        out_shape=[jax.ShapeDtypeStruct((b, length, nq), BF16),
                   jax.ShapeDtypeStruct((b, nh, length), BF16)],
        compiler_params=_cparams(("parallel", "parallel")),
        name="front_ab_lat",
    )(x, modtab, w_qkv, w_hy_t)


def _hy_prep_kernel(cur_ref, prev_ref, next_ref, w_ref, x1_ref, x2_ref, v_ref, *, nblk):
    j = pl.program_id(1)
    cur = cur_ref[0].astype(F32)
    rows = cur.shape[0]
    has_prev = (j > 0).astype(F32)
    has_next = (j < nblk - 1).astype(F32)
    prev_row = prev_ref[0][7:8].astype(F32) * has_prev
    next_row = next_ref[0][0:1].astype(F32) * has_next
    rid = lax.broadcasted_iota(jnp.int32, (rows, 1), 0)
    up = jnp.where(rid == 0, prev_row, pltpu.roll(cur, 1, axis=0))
    dn = jnp.where(rid == rows - 1, next_row, pltpu.roll(cur, rows - 1, axis=0))
    w = w_ref[...]
    y = up * w[0:1] + cur * w[1:2] + dn * w[2:3]
    c = HY_CH
    x1_ref[0] = y[:, :c].astype(BF16)
    x2_ref[0] = y[:, c:2 * c].astype(BF16)
    v_ref[0] = y[:, 2 * c:].astype(BF16)


def _hy_prep(u, conv_w, blk0, nblk):
    b, lt, _ = u.shape
    n = 3 * HY_CH
    sub = ROW_BLK // 8
    last8 = lt // 8 - 1
    out = jax.ShapeDtypeStruct((b, nblk * ROW_BLK, HY_CH), BF16)
    ospec = pl.BlockSpec((1, ROW_BLK, HY_CH), lambda i, j: (i, j, 0))
    return pl.pallas_call(
        functools.partial(_hy_prep_kernel, nblk=nblk),
        grid=(b, nblk),
        in_specs=[pl.BlockSpec((1, ROW_BLK, n), lambda i, j: (i, blk0 + j, 0)),
                  pl.BlockSpec((1, 8, n), lambda i, j: (i, jnp.maximum((blk0 + j) * sub - 1, 0), 0)),
                  pl.BlockSpec((1, 8, n), lambda i, j: (i, jnp.minimum((blk0 + j + 1) * sub, last8), 0)),
                  _resident((3, n))],
        out_specs=[ospec, ospec, ospec],
        out_shape=[out, out, out],
        compiler_params=_cparams(("parallel", "parallel")),
        name="hy_prep",
    )(u, u, u, conv_w)


def _hy_filt_kernel(z_ref, w1_ref, b1_ref, fr_ref, w2_ref, b2_ref, w3_ref, ld_ref, o_ref):
    hi = lax.Precision.HIGHEST
    z = z_ref[...]
    fr = fr_ref[...]
    hid = jnp.sin(fr * (jnp.dot(z, w1_ref[...], preferred_element_type=F32, precision=hi) + b1_ref[...]))
    hid = jnp.sin(fr * (jnp.dot(hid, w2_ref[...], preferred_element_type=F32, precision=hi) + b2_ref[...]))
    h = jnp.dot(hid, w3_ref[...], preferred_element_type=F32, precision=hi)
    t = z[:, 0:1]
    o_ref[...] = h * jnp.exp(-t * jnp.exp(ld_ref[...]))


def _pad2(a, rows, cols):
    return jnp.pad(a, ((0, rows - a.shape[0]), (0, cols - a.shape[1])))


def _hy_filters(length, w1, b1, freq, w2, b2, w3, log_decay):
    pos = jnp.arange(length, dtype=F32)
    t = pos / max(length - 1, 1)
    w = 2.0 * math.pi * pos / length
    f = jnp.linspace(1e-4, HY_BANDS - 1, HY_BANDS, dtype=F32)
    ang = w[:, None] * f[None, :]
    z = jnp.concatenate([t[:, None], jnp.cos(ang), -jnp.sin(ang)], -1)
    z = _pad2(z, length, LANES)
    n = w3.shape[1]
    rb = min(length, 512)
    vec = lambda a: _pad2(a.reshape(1, -1), 1, LANES)
    return pl.pallas_call(
        _hy_filt_kernel,
        grid=(length // rb,),
        in_specs=[pl.BlockSpec((rb, LANES), lambda i: (i, 0)),
                  _resident((LANES, LANES)), _resident((1, LANES)), _resident((1, LANES)),
                  _resident((LANES, LANES)), _resident((1, LANES)),
                  _resident((LANES, n)), _resident((1, n))],
        out_specs=pl.BlockSpec((rb, n), lambda i: (i, 0)),
        out_shape=jax.ShapeDtypeStruct((length, n), F32),
        compiler_params=_cparams(("parallel",)),
        name="hy_filters",
    )(z, _pad2(w1, LANES, LANES), vec(b1), vec(freq), _pad2(w2, LANES, LANES), vec(b2),
      _pad2(w3, LANES, n), log_decay.reshape(1, n))


def _bidir_taps(h, skip, length):
    h4 = h.reshape(length, 2, 2, HY_CH)
    cols = []
    for o in range(2):
        hf = h4[:, o, 0].at[0].add(skip[o])
        hb = h4[:, o, 1]
        cols.append(jnp.concatenate([hf, jnp.zeros_like(hf[:1]), hb[:0:-1]], 0))
    return jnp.concatenate(cols, -1)


def _left_mm_kernel(m_ref, x_ref, o_ref):
    o_ref[0] = _dot(m_ref[...], x_ref[0]).astype(o_ref.dtype)


def _left_mm(mat, x, out_dtype, lane_blk):
    g, k, n = x.shape
    m = mat.shape[0]
    lane_blk = min(lane_blk, n)
    return pl.pallas_call(
        _left_mm_kernel, grid=(g, n // lane_blk),
        in_specs=[_resident((m, k)), pl.BlockSpec((1, k, lane_blk), lambda i, j: (i, 0, j))],
        out_specs=pl.BlockSpec((1, m, lane_blk), lambda i, j: (i, 0, j)),
        out_shape=jax.ShapeDtypeStruct((g, m, n), out_dtype),
        compiler_params=_cparams(("parallel", "parallel")),
        name="left_mm",
    )(mat, x)


def _cs(num, den, rows, cols):
    ang = 2.0 * np.pi * np.outer(np.arange(rows), np.arange(cols)) * (num / den)
    return np.cos(ang), np.sin(ang)


def _twiddle(n1, n2):
    k1 = lax.broadcasted_iota(jnp.int32, (n1, n2, 1), 0)
    m2 = lax.broadcasted_iota(jnp.int32, (n1, n2, 1), 1)
    ang = (k1 * m2).astype(F32) * (2.0 * math.pi / (n1 * n2))
    return jnp.cos(ang), jnp.sin(ang)


def _hy_prep_t_kernel(u_ref, w_ref, o_ref):
    u = u_ref[0].astype(F32)
    length = u.shape[1]
    lane = lax.broadcasted_iota(jnp.int32, u.shape, 1)
    up = jnp.where(lane == 0, 0.0, pltpu.roll(u, 1, axis=1))
    dn = jnp.where(lane == length - 1, 0.0, pltpu.roll(u, length - 1, axis=1))
    w = w_ref[...]
    y = up * w[:, 0:1] + u * w[:, 1:2] + dn * w[:, 2:3]
    for n1 in range(length // FFT_N2):
        o_ref[0, :, n1, :] = y[:, n1 * FFT_N2:(n1 + 1) * FFT_N2]


def _hy_prep_t(ut, conv_w_t):
    b, nch, length = ut.shape
    cb = 64
    n1 = length // FFT_N2
    return pl.pallas_call(
        _hy_prep_t_kernel,
        grid=(b, nch // cb),
        in_specs=[pl.BlockSpec((1, cb, length), lambda i, j: (i, j, 0)),
                  pl.BlockSpec((cb, 3), lambda i, j: (j, 0))],
        out_specs=pl.BlockSpec((1, cb, n1, FFT_N2), lambda i, j: (i, j, 0, 0)),
        out_shape=jax.ShapeDtypeStruct((b, nch, n1, FFT_N2), F32),
        compiler_params=_cparams(("parallel", "parallel")),
        name="hy_prep_t",
    )(ut, conv_w_t)


def _hy_filt_t_kernel(z_ref, msk_ref, w1_ref, b1_ref, fr_ref, w2_ref, b2_ref, w3_ref, ld_ref, sk_ref, o_ref):
    hi = lax.Precision.HIGHEST
    z = z_ref[...]
    fr = fr_ref[...]
    hid = jnp.sin(fr * (jnp.dot(w1_ref[...], z, preferred_element_type=F32, precision=hi) + b1_ref[...]))
    hid = jnp.sin(fr * (jnp.dot(w2_ref[...], hid, preferred_element_type=F32, precision=hi) + b2_ref[...]))
    h = jnp.dot(w3_ref[0], hid, preferred_element_type=F32, precision=hi)
    h = h * jnp.exp(-jnp.exp(ld_ref[0]) * z[0:1, :])
    msk = msk_ref[...]
    h = h * msk[0:1, :] + sk_ref[...] * msk[1:2, :]
    for s in range(o_ref.shape[1]):
        o_ref[:, s, :] = h[:, s * FFT_N2:(s + 1) * FFT_N2]


def _hy_filters_t(length, w1, b1, freq, w2, b2, w3, log_decay, skip):
    n = 2 * length
    tt = jnp.arange(n, dtype=jnp.int32)
    pos = jnp.where(tt < length, tt, n - tt).astype(F32)
    t = pos / max(length - 1, 1)
    w = 2.0 * math.pi * pos / length
    f = jnp.linspace(1e-4, HY_BANDS - 1, HY_BANDS, dtype=F32)
    ang = f[:, None] * w[None, :]
    z = jnp.concatenate([t[None, :], jnp.cos(ang), -jnp.sin(ang)], 0)
    z = jnp.pad(z, ((0, LANES - z.shape[0]), (0, 0)))
    msk = jnp.stack([(tt != length).astype(F32), (tt == 0).astype(F32)])
    msk = jnp.pad(msk, ((0, 6), (0, 0)))
    col = lambda a: _pad2(a.reshape(-1, 1), LANES, 1)
    c2 = 2 * HY_CH
    nf = w3.shape[0]
    w3d = w3.reshape(nf, 2, 2, HY_CH).transpose(2, 1, 3, 0).reshape(2, c2, nf)
    w3d = jnp.pad(w3d, ((0, 0), (0, 0), (0, LANES - nf)))
    ldd = log_decay.reshape(2, 2, HY_CH).transpose(1, 0, 2).reshape(2, c2, 1)
    rows = 8
    pb = rows * FFT_N2
    half = length // pb
    return pl.pallas_call(
        _hy_filt_t_kernel,
        grid=(n // pb,),
        in_specs=[pl.BlockSpec((LANES, pb), lambda i: (0, i)),
                  pl.BlockSpec((8, pb), lambda i: (0, i)),
                  _resident((LANES, LANES)), _resident((LANES, 1)), _resident((LANES, 1)),
                  _resident((LANES, LANES)), _resident((LANES, 1)),
                  pl.BlockSpec((1, c2, LANES), lambda i: (i // half, 0, 0)),
                  pl.BlockSpec((1, c2, 1), lambda i: (i // half, 0, 0)),
                  _resident((c2, 1))],
        out_specs=pl.BlockSpec((c2, rows, FFT_N2), lambda i: (0, i, 0)),
        out_shape=jax.ShapeDtypeStruct((c2, n // FFT_N2, FFT_N2), F32),
        compiler_params=_cparams(("parallel",)),
        name="hy_filters_t",
    )(z, msk, _pad2(w1.T, LANES, LANES), col(b1), col(freq), _pad2(w2.T, LANES, LANES), col(b2),
      w3d, ldd, skip.reshape(c2, 1))


def _dft_mats():
    n = FFT_N2
    c, s = _cs(1, n, n, n)
    f1_half = np.concatenate([c[:, :n // 2], -s[:, :n // 2]], 0)
    f1_full = np.concatenate([c, -s], 0)
    m2r = np.block([[c, -s], [s, c]])
    m2i = np.block([[c, s], [-s, c]])
    f1_inv = np.concatenate([c[:n // 2, :], -s[:n // 2, :]], 1) / (n * n)
    cast = lambda a: jnp.asarray(a, dtype=BF16)
    return cast(f1_half), cast(f1_full), cast(m2r), cast(m2i), cast(f1_inv)


def _twiddle2d(n):
    k1 = lax.broadcasted_iota(jnp.int32, (n, n), 0)
    m2 = lax.broadcasted_iota(jnp.int32, (n, n), 1)
    ang = (k1 * m2).astype(F32) * (2.0 * math.pi / (n * n))
    return jnp.cos(ang), jnp.sin(ang)


def _fwd_spectrum(xs, f1, m2r, c, s):
    n = FFT_N2
    a = _dot(f1, jnp.concatenate(xs, axis=1))
    ts = []
    for g in range(len(xs)):
        ar = a[:n, g * n:(g + 1) * n]
        ai = a[n:, g * n:(g + 1) * n]
        ts.append(jnp.concatenate([ar * c + ai * s, ai * c - ar * s], axis=1))
    return _dot(jnp.concatenate(ts, axis=0).astype(BF16), m2r)


def _filter_spec_t_kernel(t_ref, f1_ref, m2r_ref, c_ref, s_ref, o_ref):
    g = t_ref.shape[0]
    xs = [t_ref[i].astype(BF16) for i in range(g)]
    spec = _fwd_spectrum(xs, f1_ref[...], m2r_ref[...], c_ref[...], s_ref[...])
    for i in range(g):
        o_ref[i] = spec[i * FFT_N2:(i + 1) * FFT_N2]


def _filter_spec_t(taps, f1_full, m2r, twc, tws):
    nch, n1, n = taps.shape
    g = HY_GROUP
    return pl.pallas_call(
        _filter_spec_t_kernel,
        grid=(nch // g,),
        in_specs=[pl.BlockSpec((g, n1, n), lambda i: (i, 0, 0)),
                  _resident(f1_full.shape), _resident(m2r.shape), _resident((n, n)), _resident((n, n))],
        out_specs=pl.BlockSpec((g, n, 2 * n), lambda i: (i, 0, 0)),
        out_shape=jax.ShapeDtypeStruct((nch, n, 2 * n), F32),
        compiler_params=_cparams(("parallel",)),
        name="filter_spec_t",
    )(taps, f1_full, m2r, twc, tws)


def _hyena_core_kernel(x1_ref, x2_ref, v_ref, h0_ref, h1_ref, f1_ref, m2r_ref, m2i_ref, f1i_ref, c_ref, s_ref,
                       o_ref):
    n = FFT_N2
    c = c_ref[...]
    s = s_ref[...]

    def conv(xs, h_ref, g0):
        spec = _fwd_spectrum(xs, f1_ref[...], m2r_ref[...], c, s)
        ys = []
        for g in range(HY_GROUP):
            xr = spec[g * n:(g + 1) * n, :n]
            xi = spec[g * n:(g + 1) * n, n:]
            hh = h_ref[g0 + g]
            hr, hi = hh[:, :n], hh[:, n:]
            ys.append(jnp.concatenate([xr * hr - xi * hi, xr * hi + xi * hr], axis=1))
        bm = _dot(jnp.concatenate(ys, axis=0).astype(BF16), m2i_ref[...])
        bs = []
        for g in range(HY_GROUP):
            br = bm[g * n:(g + 1) * n, :n]
            bi = bm[g * n:(g + 1) * n, n:]
            bs.append(jnp.concatenate([br * c - bi * s, bi * c + br * s], axis=0))
        y = _dot(f1i_ref[...], jnp.concatenate(bs, axis=1).astype(BF16))
        return [y[:, g * n:(g + 1) * n] for g in range(HY_GROUP)]

    for chain in range(HY_CHAINS):
        g0 = chain * HY_GROUP
        y1 = conv([v_ref[0, g0 + g].astype(BF16) for g in range(HY_GROUP)], h0_ref, g0)
        z = [(y1[g] * x1_ref[0, g0 + g]).astype(BF16) for g in range(HY_GROUP)]
        y2 = conv(z, h1_ref, g0)
        for g in range(HY_GROUP):
            o_ref[0, :, g0 + g, :] = y2[g] * x2_ref[0, g0 + g]


def _hyena_core(xs, spec, mats, twc, tws):
    f1_half, _, m2r, m2i, f1_inv = mats
    b, _, n1, n = xs.shape
    cb = HY_CBLK
    nblk = HY_CH // cb
    xspec = lambda off: pl.BlockSpec((1, cb, n1, n), lambda j, i: (i, off * nblk + j, 0, 0))
    hspec = lambda off: pl.BlockSpec((cb, n, 2 * n), lambda j, i: (off * nblk + j, 0, 0))
    return pl.pallas_call(
        _hyena_core_kernel,
        grid=(nblk, b),
        in_specs=[xspec(0), xspec(1), xspec(2), hspec(0), hspec(1),
                  _resident(f1_half.shape), _resident(m2r.shape), _resident(m2i.shape), _resident(f1_inv.shape),
                  _resident((n, n)), _resident((n, n))],
        out_specs=pl.BlockSpec((1, n1, cb, n), lambda j, i: (i, 0, j, 0)),
        out_shape=jax.ShapeDtypeStruct((b, n1, HY_CH, n), F32),
        compiler_params=_cparams(("parallel", "parallel")),
        name="hyena_core",
    )(xs, xs, xs, spec, spec, f1_half, m2r, m2i, f1_inv, twc, tws)


def _hy_ctx_kernel(v_ref, x1_ref, x2_ref, f_ref, fi_ref, h_ref, o_ref):
    nf = f_ref.shape[0] // 2
    zin = v_ref[0]
    gates = (x1_ref, x2_ref)
    for o in range(2):
        x = _dot(f_ref[...], zin)
        xr, xi = x[:nf], x[nf:]
        hr = h_ref[o, :nf]
        hi = h_ref[o, nf:]
        y = jnp.concatenate([xr * hr - xi * hi, xr * hi + xi * hr], axis=0).astype(BF16)
        zin = (_dot(fi_ref[...], y) * gates[o][0].astype(F32)).astype(BF16)
    o_ref[0] = zin


def _hy_ctx(v, x1, x2, taps):
    b, lc, c = v.shape
    nf = 2 * lc
    cm, sm = _cs(1, nf, nf, nf)
    fwd = jnp.asarray(np.concatenate([cm[:, :lc], -sm[:, :lc]], 0), dtype=BF16)
    fwd_full = jnp.asarray(np.concatenate([cm, -sm], 0), dtype=BF16)
    inv = jnp.asarray(np.concatenate([cm[:lc, :], -sm[:lc, :]], 1) / nf, dtype=BF16)
    spec = _left_mm(fwd_full, taps.astype(BF16).reshape(1, nf, 2 * c), F32, 2 * c)
    spec = spec.reshape(2 * nf, 2, c).transpose(1, 0, 2)
    blk = pl.BlockSpec((1, lc, c), lambda i: (i, 0, 0))
    return pl.pallas_call(
        _hy_ctx_kernel,
        grid=(b,),
        in_specs=[blk, blk, blk, _resident((2 * nf, lc)), _resident((lc, 2 * nf)),
                  _resident((2, 2 * nf, c))],
        out_specs=blk,
        out_shape=jax.ShapeDtypeStruct((b, lc, c), BF16),
        compiler_params=_cparams(("parallel",)),
        name="hy_ctx",
    )(v, x1, x2, fwd, inv, spec)


def _pair_rows(q2):
    lane = lax.broadcasted_iota(jnp.int32, q2.shape, 1)
    zero = jnp.zeros_like(q2)
    return jnp.concatenate([jnp.where(lane < NA_HD, q2, zero), jnp.where(lane >= NA_HD, q2, zero)], axis=0)


def _unpair_rows(o):
    r = o.shape[0] // 2
    lane = lax.broadcasted_iota(jnp.int32, (r, o.shape[1]), 1)
    return jnp.where(lane < NA_HD, o[:r], o[r:])


def _pair_softmax_pv(scores, values):
    m = scores[0].max(axis=-1, keepdims=True)
    for s in scores[1:]:
        m = jnp.maximum(m, s.max(axis=-1, keepdims=True))
    den = None
    acc = None
    for s, v in zip(scores, values):
        p = jnp.exp(s - m)
        d = p.sum(axis=-1, keepdims=True)
        a = _dot(p.astype(BF16), v)
        den = d if den is None else den + d
        acc = a if acc is None else acc + a
    return acc / den


def _natten_kernel(q_ref, k0, k1, k2, k3, v0, v1, v2, v3, kc_ref, vc_ref, bias_ref, o_ref,
                   kwin, vwin, *, rows):
    g = pl.program_id(1)
    rb = 4 * GRID_W
    for i, (kr, vr) in enumerate(((k0, v0), (k1, v1), (k2, v2), (k3, v3))):
        kwin[i * rb:(i + 1) * rb, :] = kr[0]
        vwin[i * rb:(i + 1) * rb, :] = vr[0]
    base = 4 * jnp.clip(2 * g - 1, 0, rows // 4 - 4)
    nwin = NA_WIN_R * GRID_W
    qscale = jnp.asarray(NA_HD ** -0.5, BF16)

    ones_lat = jnp.ones((nwin, LANES), BF16)
    ones_ctx = jnp.ones((kc_ref.shape[1], LANES), BF16)

    def rows_body(it, carry):
        work = []
        for u in range(NA_ROWS_PER_TRIP):
            rr = it * NA_ROWS_PER_TRIP + u
            r = 8 * g + rr
            rs = jnp.clip(r - NA_WIN_R // 2, 0, rows - NA_WIN_R)
            st = pl.multiple_of((rs - base) * GRID_W, GRID_W)
            qo = pl.multiple_of(rr * GRID_W, GRID_W)
            work += [(qo, st, rs - r + NA_WIN_R - 1, p) for p in range(NA_HEADS // 2)]
        scores = []
        for qo, st, d0, p in work:
            ls = slice(p * LANES, (p + 1) * LANES)
            qp = _pair_rows(q_ref[0, pl.ds(qo, GRID_W), ls] * qscale)
            scores.append((_dot_nt(qp, kwin[pl.ds(st, nwin), ls]) + bias_ref[d0, p].astype(F32),
                           _dot_nt(qp, kc_ref[0, :, ls])))
        maxima = [jnp.maximum(a.max(axis=-1, keepdims=True), b.max(axis=-1, keepdims=True)) for a, b in scores]
        for (qo, st, d0, p), (s_lat, s_ctx), m in zip(work, scores, maxima):
            ls = slice(p * LANES, (p + 1) * LANES)
            v_lat = jnp.concatenate([vwin[pl.ds(st, nwin), ls], ones_lat], axis=1)
            v_ctx = jnp.concatenate([vc_ref[0, :, ls], ones_ctx], axis=1)
            acc = _dot(jnp.exp(s_lat - m).astype(BF16), v_lat) + _dot(jnp.exp(s_ctx - m).astype(BF16), v_ctx)
            o = acc[:, :LANES] / acc[:, LANES:LANES + 1]
            o_ref[0, pl.ds(qo, GRID_W), ls] = _unpair_rows(o).astype(o_ref.dtype)
        return carry

    lax.fori_loop(0, 8 // NA_ROWS_PER_TRIP, rows_body, 0)


def _natten_bias(rpb):
    c = np.arange(GRID_W)[:, None]
    kc = np.arange(GRID_W)[None, :]
    cs = np.clip(c - NA_WIN_C // 2, 0, GRID_W - NA_WIN_C)
    valid = (kc >= cs) & (kc < cs + NA_WIN_C)
    dc = np.clip(kc - c + NA_WIN_C - 1, 0, 2 * NA_WIN_C - 2)
    tb = jnp.where(valid[None, None], rpb[:, :, dc], NEG_BIG)
    slabs = []
    for d0 in range(NA_WIN_R):
        s = tb[:, d0:d0 + NA_WIN_R]
        s = s.transpose(0, 2, 1, 3).reshape(NA_HEADS, GRID_W, NA_WIN_R * GRID_W)
        slabs.append(s.reshape(NA_HEADS // 2, 2 * GRID_W, NA_WIN_R * GRID_W))
    return jnp.stack(slabs).astype(BF16)


def _natten(uq, uc, bias):
    b, length, _ = uq.shape
    c = NA_HEADS * NA_HD
    rows = length // GRID_W
    rb = 4 * GRID_W
    nkb = length // rb
    qrows = 8 * GRID_W
    lc = uc.shape[1]

    def kv_spec(col, off):
        return pl.BlockSpec((1, rb, c), lambda i, g: (i, jnp.clip(2 * g - 1, 0, nkb - 4) + off, col))

    return pl.pallas_call(
        functools.partial(_natten_kernel, rows=rows),
        grid=(b, rows // 8),
        in_specs=[pl.BlockSpec((1, qrows, c), lambda i, g: (i, g, 0))]
                 + [kv_spec(1, o) for o in range(4)] + [kv_spec(2, o) for o in range(4)]
                 + [pl.BlockSpec((1, lc, c), lambda i, g: (i, 0, 4)),
                    pl.BlockSpec((1, lc, c), lambda i, g: (i, 0, 5)),
                    _resident(bias.shape)],
        out_specs=pl.BlockSpec((1, qrows, c), lambda i, g: (i, g, 0)),
        out_shape=jax.ShapeDtypeStruct((b, length, c), BF16),
        scratch_shapes=[pltpu.VMEM((4 * rb, c), BF16), pltpu.VMEM((4 * rb, c), BF16)],
        compiler_params=_cparams(("parallel", "parallel")),
        name="natten",
    )(uq, *([uq] * 8), uc, uc, bias)


def _ctx_attn_kernel(q_ref, k_ref, v_ref, o_ref):
    qscale = jnp.asarray(NA_HD ** -0.5, BF16)
    for p in range(NA_HEADS // 2):
        ls = slice(p * LANES, (p + 1) * LANES)
        qp = _pair_rows(q_ref[0, :, ls] * qscale)
        o = _pair_softmax_pv([_dot_nt(qp, k_ref[0, :, ls])], [v_ref[0, :, ls]])
        o_ref[0, :, ls] = _unpair_rows(o).astype(o_ref.dtype)


def _ctx_attn(u):
    b, lc, _ = u.shape
    c = NA_HEADS * NA_HD
    spec = lambda col: pl.BlockSpec((1, lc, c), lambda i: (i, 0, col))
    return pl.pallas_call(
        _ctx_attn_kernel,
        grid=(b,),
        in_specs=[spec(3), spec(4), spec(5)],
        out_specs=pl.BlockSpec((1, lc, c), lambda i: (i, 0, 0)),
        out_shape=jax.ShapeDtypeStruct((b, lc, c), BF16),
        compiler_params=_cparams(("parallel",)),
        name="ctx_attn",
    )(u, u, u)


def _post_kernel(x_ref, mod_ref, ya_ref, yb_ref, wa_ref, wb_ref, lng_ref, lnb_ref, w1_ref, w2_ref, o_ref, *, ya_t):
    m = mod_ref[0, 0]
    lng = lng_ref[...]
    lnb = lnb_ref[...]
    ff = w1_ref.shape[1]
    step = 1024
    sub = x_ref.shape[1]
    for t in range(x_ref.shape[1] // sub):
        rs = slice(t * sub, (t + 1) * sub)
        x = x_ref[0, rs, :]
        if ya_t:
            tiles = range(t * sub // LANES, (t + 1) * sub // LANES)
            ya = jnp.concatenate([ya_ref[0, s].T for s in tiles], axis=0).astype(BF16)
        else:
            ya = ya_ref[0, rs, :]
        y = _dot(ya, wa_ref[...]) + _dot(yb_ref[0, rs, :], wb_ref[...])
        x1 = _norm_rows(ALPHA * x + m[2:3] * y) * lng[0:1] + lnb[0:1]
        h = (_norm_rows(x1) * (1.0 + m[4:5]) + m[3:4]).astype(BF16)
        acc = jnp.zeros_like(x)
        for c in range(ff // step):
            a = _dot(h, w1_ref[:, c * step:(c + 1) * step])
            a = jnp.maximum(a, 0.0)
            acc = acc + _dot((a * a).astype(BF16), w2_ref[c * step:(c + 1) * step, :])
        o_ref[0, rs, :] = _norm_rows(ALPHA * x1 + m[5:6] * acc) * lng[1:2] + lnb[1:2]


def _post(x, modtab, mod_row, ya, yb, wa, wb, lng, lnb, w1, w2, ya_t=False):
    b, r, d = x.shape
    ka, kb = wa.shape[0], wb.shape[0]
    rows = min(r, 2 * ROW_BLK)
    row = lambda k: pl.BlockSpec((1, rows, k), lambda i, j: (i, j, 0))
    ya_spec = pl.BlockSpec((1, rows // LANES, ka, LANES), lambda i, j: (i, j, 0, 0)) if ya_t else row(ka)
    return pl.pallas_call(
        functools.partial(_post_kernel, ya_t=ya_t),
        grid=(b, r // rows),
        in_specs=[row(d), pl.BlockSpec((1, 1, 6, d), lambda i, j: (i, mod_row, 0, 0)), ya_spec, row(kb),
                  _resident(wa.shape), _resident(wb.shape), _resident(lng.shape), _resident(lnb.shape),
                  _resident(w1.shape), _resident(w2.shape)],
        out_specs=row(d),
        out_shape=jax.ShapeDtypeStruct((b, r, d), F32),
        compiler_params=_cparams(("parallel", "parallel")),
        name="post_mixer",
    )(x, modtab, ya, yb, wa, wb, lng, lnb, w1, w2)


def _rope(x, cos, sinl, sinr):
    reps = x.shape[1] // LANES
    tile = lambda t: jnp.concatenate([t] * reps, axis=1)
    n = x.shape[1]
    quarter = MLA_ROPE // 4
    return (x * tile(cos) + pltpu.roll(x, n - quarter, axis=1) * tile(sinl)
            + pltpu.roll(x, quarter, axis=1) * tile(sinr))


def _front_cd_kernel(x_ref, xc_ref, mod_ref, w_ref, qn_ref, kvn_ref, wuq_ref, wuk_ref, wuv_ref, epe_ref, one_ref,
                     fng_ref, fnb_ref, cbd_ref, sbd_ref,
                     cq_ref, slq_ref, srq_ref, ck_ref, slk_ref, srk_ref,
                     q_ref, k_ref, v_ref, p_ref, qf_ref, *, ctx_blk):
    m = mod_ref[0, 0]
    is_ctx = (jnp.zeros((ROW_BLK, 1), jnp.int32) + pl.program_id(1)) == ctx_blk
    x = jnp.where(is_ctx, xc_ref[0], x_ref[0])
    h = _norm_rows(x) * (1.0 + m[1:2]) + m[0:1]
    u = _dot(h.astype(BF16), w_ref[...])
    o_kv = MLA_Q_RANK
    o_fn = o_kv + MLA_KV_RANK
    o_pe = o_fn + FN_CH

    def rms(x, g):
        return x * lax.rsqrt(jnp.mean(x * x, axis=-1, keepdims=True) + LN_EPS) * g

    cq = rms(u[:, :o_kv], qn_ref[...]).astype(BF16)
    q = _dot(cq, wuq_ref[...])
    q_ref[0] = _rope(q, cq_ref[...], slq_ref[...], srq_ref[...]).astype(BF16)

    ckv = rms(u[:, o_kv:o_fn], kvn_ref[...]).astype(BF16)
    kpe = _dot(u[:, o_pe:].astype(BF16), epe_ref[...])
    k = _dot(ckv, wuk_ref[...]) + _rope(kpe, ck_ref[...], slk_ref[...], srk_ref[...])
    k_ref[0] = k.astype(BF16)
    v_ref[0] = (_dot(ckv, wuv_ref[...]) + one_ref[...]).astype(BF16)

    uf = u[:, o_fn:o_pe]
    lane = lax.broadcasted_iota(jnp.int32, uf.shape, 1)
    mean = jnp.zeros_like(uf)
    for g in range(FN_GROUPS):
        sel = (lane >= g * FN_GD) & (lane < (g + 1) * FN_GD)
        mg = jnp.sum(jnp.where(sel, uf, 0.0), axis=-1, keepdims=True) * (1.0 / FN_GD)
        mean = jnp.where(sel, mg, mean)
    uc = uf - mean
    var = jnp.zeros_like(uf)
    for g in range(FN_GROUPS):
        sel = (lane >= g * FN_GD) & (lane < (g + 1) * FN_GD)
        vg = jnp.sum(jnp.where(sel, uc * uc, 0.0), axis=-1, keepdims=True) * (1.0 / FN_GD)
        var = jnp.where(sel, vg, var)
    ug = (uc * lax.rsqrt(var + LN_EPS) * fng_ref[...] + fnb_ref[...]).astype(BF16)
    p_ref[0] = _dot(ug, cbd_ref[...]).astype(BF16)
    qf_ref[0] = _dot(ug, sbd_ref[...]).astype(BF16)


def _rope_tables(length, lt, scale):
    t = jnp.arange(lt, dtype=jnp.int32)
    rows = (t // GRID_W).astype(F32)
    cols = (t % GRID_W).astype(F32)
    half = MLA_ROPE // 2
    inv = ROPE_THETA ** (-jnp.arange(0, half, 2, dtype=F32) / half)
    ar = rows[:, None] * inv[None, :]
    ac = cols[:, None] * inv[None, :]
    ang = jnp.concatenate([ar, ar, ac, ac], -1)
    is_lat = (t < length)[:, None]
    cos = jnp.where(is_lat, jnp.cos(ang), 1.0)
    sin = jnp.where(is_lat, jnp.sin(ang), 0.0)
    qd = MLA_ROPE // 4
    ones = jnp.ones((lt, MLA_NOPE), F32)
    zeros = jnp.zeros((lt, MLA_NOPE), F32)
    tail1 = jnp.ones((lt, HEAD_PAD - MLA_NOPE - MLA_ROPE), F32)
    tail0 = jnp.zeros((lt, HEAD_PAD - MLA_NOPE - MLA_ROPE), F32)
    z8 = jnp.zeros((lt, qd), F32)
    c = jnp.concatenate([ones, cos, tail1], -1)
    sl = jnp.concatenate([zeros, -sin[:, :qd], z8, -sin[:, 2 * qd:3 * qd], z8, tail0], -1)
    sr = jnp.concatenate([zeros, z8, sin[:, qd:2 * qd], z8, sin[:, 3 * qd:], tail0], -1)
    return c * scale, sl * scale, sr * scale


def _head_slots(w, per_head, take_from, take_n):
    k = w.shape[0]
    w3 = w.reshape(k, MLA_HEADS, per_head)[:, :, take_from:take_from + take_n]
    w3 = jnp.pad(w3, ((0, 0), (0, 0), (0, HEAD_PAD - take_n)))
    return w3.reshape(k, MLA_HEADS * HEAD_PAD)


def _front_cd(xl, xc, modtab, w_in, q_norm, w_uq, kv_norm, w_ukv, fn_g, fn_b):
    b, length, d = xl.shape
    lt = length + xc.shape[1]
    nlat = length // ROW_BLK
    o_kv = MLA_Q_RANK
    o_pe = o_kv + MLA_KV_RANK
    o_fn = o_pe + MLA_ROPE
    hw = MLA_HEADS * HEAD_PAD
    w_perm = jnp.concatenate([w_in[:, :o_pe], w_in[:, o_fn:], w_in[:, o_pe:o_fn],
                              jnp.zeros((d, LANES - MLA_ROPE), w_in.dtype)], -1).astype(BF16)
    wuq = _head_slots(w_uq, MLA_NOPE + MLA_ROPE, 0, MLA_NOPE + MLA_ROPE).astype(BF16)
    wuk = _head_slots(w_ukv, MLA_NOPE + MLA_V, 0, MLA_NOPE).astype(BF16)
    wuv = _head_slots(w_ukv, MLA_NOPE + MLA_V, MLA_NOPE, MLA_V).astype(BF16)
    epe = np.zeros((LANES, hw), np.float32)
    for hd in range(MLA_HEADS):
        for i in range(MLA_ROPE):
            epe[i, hd * HEAD_PAD + MLA_NOPE + i] = 1.0
    epe = jnp.asarray(epe, dtype=BF16)
    ones_col = np.zeros((1, hw), np.float32)
    ones_col[0, MLA_V::HEAD_PAD] = 1.0
    ones_col = jnp.asarray(ones_col)
    cm, sm = _cs(1, FN_GD, FN_GD, FN_GD)
    eye = np.eye(FN_GROUPS)
    cbd = jnp.asarray(np.kron(eye, cm), dtype=BF16)
    sbd = jnp.asarray(np.kron(eye, -sm), dtype=BF16)
    qtab = _rope_tables(length, lt, (MLA_NOPE + MLA_ROPE) ** -0.5 * math.log2(math.e))
    ktab = _rope_tables(length, lt, 1.0)
    row = lambda n: pl.BlockSpec((1, ROW_BLK, n), lambda i, j: (i, j, 0))
    tab = pl.BlockSpec((ROW_BLK, HEAD_PAD), lambda i, j: (j, 0))
    out = lambda n: jax.ShapeDtypeStruct((b, lt, n), BF16)
    return pl.pallas_call(
        functools.partial(_front_cd_kernel, ctx_blk=nlat),
        grid=(b, lt // ROW_BLK),
        in_specs=[pl.BlockSpec((1, ROW_BLK, d), lambda i, j: (i, jnp.minimum(j, nlat - 1), 0)),
                  pl.BlockSpec((1, ROW_BLK, d), lambda i, j: (i, 0, 0)),
                  pl.BlockSpec((1, 1, 6, d), lambda i, j: (i, j // nlat, 0, 0)),
                  _resident(w_perm.shape), _resident((1, MLA_Q_RANK)), _resident((1, MLA_KV_RANK)),
                  _resident(wuq.shape), _resident(wuk.shape), _resident(wuv.shape), _resident(epe.shape),
                  _resident(ones_col.shape),
                  _resident((1, FN_CH)), _resident((1, FN_CH)), _resident(cbd.shape), _resident(sbd.shape),
                  tab, tab, tab, tab, tab, tab],
        out_specs=[row(hw), row(hw), row(hw), row(FN_CH), row(FN_CH)],
        out_shape=[out(hw), out(hw), out(hw), out(FN_CH), out(FN_CH)],
        compiler_params=_cparams(("parallel", "parallel")),
        name="front_cd",
    )(xl, xc, modtab, w_perm, q_norm.reshape(1, -1), kv_norm.reshape(1, -1), wuq, wuk, wuv, epe, ones_col,
      fn_g.reshape(1, -1), fn_b.reshape(1, -1), cbd, sbd, *qtab, *ktab)


def _mla_kernel(q_ref, k_ref, v_ref, o_ref, *, rows):
    n = q_ref.shape[1] // rows

    def scores(i):
        s = _dot_nt(q_ref[0, i * rows:(i + 1) * rows, :], k_ref[0])
        return s, s.max(axis=-1, keepdims=True)

    def finish(i, s, m):
        acc = _dot(jnp.exp2(s - m).astype(BF16), v_ref[0])
        o_ref[0, i * rows:(i + 1) * rows, :] = (acc / acc[:, MLA_V:MLA_V + 1]).astype(o_ref.dtype)

    pending = scores(0)
    for i in range(n):
        nxt = scores(i + 1) if i + 1 < n else None
        finish(i, *pending)
        pending = nxt


def _mla_attention(q, k, v, length):
    b, lt, hw = q.shape
    heads = hw // HEAD_PAD
    tq = 1024
    kv = pl.BlockSpec((1, lt, HEAD_PAD), lambda i, h, j: (i, 0, h))
    qs = pl.BlockSpec((1, tq, HEAD_PAD), lambda i, h, j: (i, j, h))
    return pl.pallas_call(
        functools.partial(_mla_kernel, rows=256),
        grid=(b, heads, length // tq),
        in_specs=[qs, kv, kv],
        out_specs=qs,
        out_shape=jax.ShapeDtypeStruct((b, length, hw), BF16),
        compiler_params=_cparams(("parallel", "parallel", "parallel")),
        name="mla_attention",
    )(q, k, v)


def _fn1_kernel(m_ref, zr_ref, zi_ref, o_ref):
    n = zr_ref.shape[1]
    a = _dot(m_ref[...], jnp.concatenate([zr_ref[0], zi_ref[0]], axis=0))
    o_ref[0, 0] = a[:n].astype(o_ref.dtype)
    o_ref[0, 1] = a[n:].astype(o_ref.dtype)


def _fn2_kernel(a_ref, c_ref, s_ref, m_ref, o_ref):
    for i in range(a_ref.shape[2]):
        ar = a_ref[0, 0, i].astype(F32)
        ai = a_ref[0, 1, i].astype(F32)
        c = c_ref[i]
        s = s_ref[i]
        t = jnp.concatenate([ar * c + ai * s, ai * c - ar * s], axis=0).astype(BF16)
        o_ref[0, i] = _dot(m_ref[...], t).astype(o_ref.dtype)


def _fnet(p, qn):
    b, length, c = p.shape
    n1 = 128
    n2 = length // n1
    cm, sm = _cs(1, n1, n1, n1)
    m1 = jnp.asarray(np.block([[cm, sm], [-sm, cm]]), dtype=BF16)
    c2, s2 = _cs(1, n2, n2, n2)
    m2 = jnp.asarray(np.concatenate([c2, s2], 1) / math.sqrt(length * FN_GD), dtype=BF16)
    twc, tws = _twiddle(n1, n2)
    lane_blk = 2048
    zs = pl.BlockSpec((1, n1, lane_blk), lambda i, j: (i, 0, j))
    a = pl.pallas_call(
        _fn1_kernel,
        grid=(b, n2 * c // lane_blk),
        in_specs=[_resident(m1.shape), zs, zs],
        out_specs=pl.BlockSpec((1, 2, n1, lane_blk), lambda i, j: (i, 0, 0, j)),
        out_shape=jax.ShapeDtypeStruct((b, 2, n1, n2 * c), BF16),
        compiler_params=_cparams(("parallel", "parallel")),
        name="fnet_stage1",
    )(m1, p.reshape(b, n1, n2 * c), qn.reshape(b, n1, n2 * c))
    kb = 8
    y = pl.pallas_call(
        _fn2_kernel,
        grid=(b, n1 // kb),
        in_specs=[pl.BlockSpec((1, 2, kb, n2, c), lambda i, j: (i, 0, j, 0, 0)),
                  pl.BlockSpec((kb, n2, 1), lambda i, j: (j, 0, 0)),
                  pl.BlockSpec((kb, n2, 1), lambda i, j: (j, 0, 0)),
                  _resident(m2.shape)],
        out_specs=pl.BlockSpec((1, kb, n2, c), lambda i, j: (i, j, 0, 0)),
        out_shape=jax.ShapeDtypeStruct((b, n1, n2, c), BF16),
        compiler_params=_cparams(("parallel", "parallel")),
        name="fnet_stage2",
    )(a.reshape(b, 2, n1, n2, c), twc, tws, m2)
    return y.transpose(0, 2, 1, 3).reshape(b, length, c)


def kernel(x, c, ctx, c_ctx, mod_w, mod_b, ln_g, ln_b, mlp_w1, mlp_w2,
           ab_w_in, ab_w_out, hy_conv_w, hy_w1, hy_b1, hy_freq, hy_w2, hy_b2, hy_w3, hy_log_decay, hy_skip, na_rpb,
           cd_w_in, cd_w_out, mla_q_norm, mla_w_uq, mla_kv_norm, mla_w_ukv, fn_norm_g, fn_norm_b):
    b, length, d = x.shape
    lc = ctx.shape[1]

    cc = jnp.concatenate([c, c_ctx[None], jnp.zeros((8 - b - 1, d), F32)], 0)
    mods = _mod_vectors(cc, mod_w, mod_b).reshape(DEPTH, 8, 6, d)
    modtab = [jnp.stack([mods[l, :b], jnp.broadcast_to(mods[l, b], (b, 6, d))], axis=1) for l in range(DEPTH)]

    n_hy = 3 * HY_CH
    w_in = ab_w_in[0].astype(BF16)
    uq, ut = _front_ab_lat(x, modtab[0], w_in[:, n_hy:], w_in[:, :n_hy].T)
    uc = _front_ab_ctx(ctx, modtab[0], w_in)
    fargs = (hy_w1[0], hy_b1[0], hy_freq[0], hy_w2[0], hy_b2[0], hy_w3[0], hy_log_decay[0])
    mats = _dft_mats()
    twc, tws = _twiddle2d(FFT_N2)
    spec = _filter_spec_t(_hy_filters_t(length, *fargs, hy_skip[0]), mats[1], mats[2], twc, tws)
    y_hy_t = _hyena_core(_hy_prep_t(ut, hy_conv_w[0].T), spec, mats, twc, tws)
    x1c, x2c, vc = _hy_prep(uc, hy_conv_w[0], 0, lc // ROW_BLK)
    y_hy_c = _hy_ctx(vc, x1c, x2c, _bidir_taps(_hy_filters(lc, *fargs), hy_skip[0], lc))
    y_na = _natten(uq, uc, _natten_bias(na_rpb[0]))
    y_na_c = _ctx_attn(uc)
    w_out = ab_w_out[0].astype(BF16)
    mlp = (ln_g[0], ln_b[0], mlp_w1[0].astype(BF16), mlp_w2[0].astype(BF16))
    xl = _post(x, modtab[0], 0, y_hy_t, y_na, w_out[:HY_CH], w_out[HY_CH:], *mlp, ya_t=True)
    xc = _post(ctx, modtab[0], 1, y_hy_c, y_na_c, w_out[:HY_CH], w_out[HY_CH:], *mlp)

    q, k, vv, p, qn = _front_cd(xl, xc, modtab[1], cd_w_in[0], mla_q_norm[0], mla_w_uq[0], mla_kv_norm[0],
                                mla_w_ukv[0], fn_norm_g[0], fn_norm_b[0])
    o = _mla_attention(q, k, vv, length)
    y_fn = _fnet(p[:, :length], qn[:, :length])
    w_out = cd_w_out[0]
    n_mla = MLA_HEADS * MLA_V
    wa = jnp.pad(w_out[:n_mla].reshape(MLA_HEADS, MLA_V, d), ((0, 0), (0, HEAD_PAD - MLA_V), (0, 0)))
    wa = wa.reshape(MLA_HEADS * HEAD_PAD, d).astype(BF16)
    return _post(xl, modtab[1], 0, o, y_fn, wa, w_out[n_mla:].astype(BF16),
                 ln_g[1], ln_b[1], mlp_w1[1].astype(BF16), mlp_w2[1].astype(BF16))
```

```python
import functools
import math

import numpy as np
import jax
import jax.numpy as jnp
from jax import lax
from jax.experimental import pallas as pl
from jax.experimental.pallas import tpu as pltpu

F32 = jnp.float32
BF16 = jnp.bfloat16

D_MODEL = 1024
DEPTH = 2
GRID_W = 64
HY_CH = 512
HY_EMB = 33
HY_BANDS = (HY_EMB - 1) // 2
NA_HEADS = 8
NA_HD = 64
NA_WIN_R = 8
NA_WIN_C = 16
MLA_HEADS = 8
MLA_Q_RANK = 384
MLA_KV_RANK = 256
MLA_NOPE = 64
MLA_ROPE = 32
MLA_V = 96
ROPE_THETA = 10000.0
FN_CH = 256
FN_GROUPS = 4
FN_GD = FN_CH // FN_GROUPS
D_FF = 4 * D_MODEL
ALPHA = (2.0 * DEPTH) ** 0.25
LN_EPS = 1e-5

LANES = 128
ROW_BLK = 256
HEAD_PAD = 128
FFT_N2 = 128
NA_ROWS_PER_TRIP = 4
HY_GROUP = 16
HY_CHAINS = 2
HY_CBLK = HY_GROUP * HY_CHAINS
VMEM_LIMIT = 56 * 1024 * 1024
NEG_BIG = -1e30


def _cparams(sem, vmem=VMEM_LIMIT):
    return pltpu.CompilerParams(dimension_semantics=sem, vmem_limit_bytes=vmem)


def _resident(shape):
    nd = len(shape)
    return pl.BlockSpec(shape, lambda *_: (0,) * nd, pipeline_mode=pl.Buffered(1))


def _norm_rows(x):
    mu = jnp.mean(x, axis=-1, keepdims=True)
    xc = x - mu
    var = jnp.mean(xc * xc, axis=-1, keepdims=True)
    return xc * lax.rsqrt(var + LN_EPS)


def _dot(a, b):
    return jnp.dot(a, b, preferred_element_type=F32)


def _dot_nt(a, b):
    return lax.dot_general(a, b, (((1,), (1,)), ((), ())), preferred_element_type=F32)


def _mod_kernel(c_ref, w_ref, b_ref, o_ref):
    c = c_ref[...]
    s = c * (1.0 / (1.0 + jnp.exp(-c)))
    o_ref[0] = jnp.dot(s, w_ref[0], preferred_element_type=F32,
                       precision=lax.Precision.HIGHEST) + b_ref[0]


def _mod_vectors(cc, mod_w, mod_b):
    depth, d, n = mod_w.shape
    nb = 1024
    return pl.pallas_call(
        _mod_kernel,
        grid=(depth, n // nb),
        in_specs=[pl.BlockSpec((8, d), lambda l, j: (0, 0)),
                  pl.BlockSpec((1, d, nb), lambda l, j: (l, 0, j)),
                  pl.BlockSpec((1, 1, nb), lambda l, j: (l, 0, j))],
        out_specs=pl.BlockSpec((1, 8, nb), lambda l, j: (l, 0, j)),
        out_shape=jax.ShapeDtypeStruct((depth, 8, n), F32),
        compiler_params=_cparams(("parallel", "parallel")),
        name="mod_vectors",
    )(cc, mod_w, mod_b.reshape(depth, 1, n))


def _front_ab_ctx_kernel(x_ref, mod_ref, w_ref, u_ref):
    m = mod_ref[0, 0]
    h = _norm_rows(x_ref[0]) * (1.0 + m[1:2]) + m[0:1]
    u_ref[0] = _dot(h.astype(BF16), w_ref[...]).astype(BF16)


def _front_ab_ctx(xc, modtab, w_in):
    b, lc, d = xc.shape
    n = w_in.shape[1]
    return pl.pallas_call(
        _front_ab_ctx_kernel,
        grid=(b, lc // ROW_BLK),
        in_specs=[pl.BlockSpec((1, ROW_BLK, d), lambda i, j: (i, j, 0)),
                  pl.BlockSpec((1, 1, 6, d), lambda i, j: (i, 1, 0, 0)),
                  _resident((d, n))],
        out_specs=pl.BlockSpec((1, ROW_BLK, n), lambda i, j: (i, j, 0)),
        out_shape=jax.ShapeDtypeStruct((b, lc, n), BF16),
        compiler_params=_cparams(("parallel", "parallel")),
        name="front_ab_ctx",
    )(xc, modtab, w_in)


def _front_ab_lat_kernel(x_ref, mod_ref, wq_ref, wht_ref, u_ref, ut_ref):
    m = mod_ref[0, 0]
    h = (_norm_rows(x_ref[0]) * (1.0 + m[1:2]) + m[0:1]).astype(BF16)
    u_ref[0] = _dot(h, wq_ref[...]).astype(BF16)
    ut_ref[0] = _dot_nt(wht_ref[...], h).astype(BF16)


def _front_ab_lat(x, modtab, w_qkv, w_hy_t):
    b, length, d = x.shape
    nq = w_qkv.shape[1]
    nh = w_hy_t.shape[0]
    return pl.pallas_call(
        _front_ab_lat_kernel,
        grid=(b, length // ROW_BLK),
        in_specs=[pl.BlockSpec((1, ROW_BLK, d), lambda i, j: (i, j, 0)),
                  pl.BlockSpec((1, 1, 6, d), lambda i, j: (i, 0, 0, 0)),
                  _resident((d, nq)), _resident((nh, d))],
        out_specs=[pl.BlockSpec((1, ROW_BLK, nq), lambda i, j: (i, j, 0)),
                   pl.BlockSpec((1, nh, ROW_BLK), lambda i, j: (i, 0, j))],
        out_shape=[jax.ShapeDtypeStruct((b, length, nq), BF16),
                   jax.ShapeDtypeStruct((b, nh, length), BF16)],
        compiler_params=_cparams(("parallel", "parallel")),
        name="front_ab_lat",
    )(x, modtab, w_qkv, w_hy_t)


def _hy_prep_kernel(cur_ref, prev_ref, next_ref, w_ref, x1_ref, x2_ref, v_ref, *, nblk):
    j = pl.program_id(1)
    cur = cur_ref[0].astype(F32)
    rows = cur.shape[0]
    has_prev = (j > 0).astype(F32)
    has_next = (j < nblk - 1).astype(F32)
    prev_row = prev_ref[0][7:8].astype(F32) * has_prev
    next_row = next_ref[0][0:1].astype(F32) * has_next
    rid = lax.broadcasted_iota(jnp.int32, (rows, 1), 0)
    up = jnp.where(rid == 0, prev_row, pltpu.roll(cur, 1, axis=0))
    dn = jnp.where(rid == rows - 1, next_row, pltpu.roll(cur, rows - 1, axis=0))
    w = w_ref[...]
    y = up * w[0:1] + cur * w[1:2] + dn * w[2:3]
    c = HY_CH
    x1_ref[0] = y[:, :c].astype(BF16)
    x2_ref[0] = y[:, c:2 * c].astype(BF16)
    v_ref[0] = y[:, 2 * c:].astype(BF16)


def _hy_prep(u, conv_w, blk0, nblk):
    b, lt, _ = u.shape
    n = 3 * HY_CH
    sub = ROW_BLK // 8
    last8 = lt // 8 - 1
    out = jax.ShapeDtypeStruct((b, nblk * ROW_BLK, HY_CH), BF16)
    ospec = pl.BlockSpec((1, ROW_BLK, HY_CH), lambda i, j: (i, j, 0))
    return pl.pallas_call(
        functools.partial(_hy_prep_kernel, nblk=nblk),
        grid=(b, nblk),
        in_specs=[pl.BlockSpec((1, ROW_BLK, n), lambda i, j: (i, blk0 + j, 0)),
                  pl.BlockSpec((1, 8, n), lambda i, j: (i, jnp.maximum((blk0 + j) * sub - 1, 0), 0)),
                  pl.BlockSpec((1, 8, n), lambda i, j: (i, jnp.minimum((blk0 + j + 1) * sub, last8), 0)),
                  _resident((3, n))],
        out_specs=[ospec, ospec, ospec],
        out_shape=[out, out, out],
        compiler_params=_cparams(("parallel", "parallel")),
        name="hy_prep",
    )(u, u, u, conv_w)


def _hy_filt_kernel(z_ref, w1_ref, b1_ref, fr_ref, w2_ref, b2_ref, w3_ref, ld_ref, o_ref):
    hi = lax.Precision.HIGHEST
    z = z_ref[...]
    fr = fr_ref[...]
    hid = jnp.sin(fr * (jnp.dot(z, w1_ref[...], preferred_element_type=F32, precision=hi) + b1_ref[...]))
    hid = jnp.sin(fr * (jnp.dot(hid, w2_ref[...], preferred_element_type=F32, precision=hi) + b2_ref[...]))
    h = jnp.dot(hid, w3_ref[...], preferred_element_type=F32, precision=hi)
    t = z[:, 0:1]
    o_ref[...] = h * jnp.exp(-t * jnp.exp(ld_ref[...]))


def _pad2(a, rows, cols):
    return jnp.pad(a, ((0, rows - a.shape[0]), (0, cols - a.shape[1])))


def _hy_filters(length, w1, b1, freq, w2, b2, w3, log_decay):
    pos = jnp.arange(length, dtype=F32)
    t = pos / max(length - 1, 1)
    w = 2.0 * math.pi * pos / length
    f = jnp.linspace(1e-4, HY_BANDS - 1, HY_BANDS, dtype=F32)
    ang = w[:, None] * f[None, :]
    z = jnp.concatenate([t[:, None], jnp.cos(ang), -jnp.sin(ang)], -1)
    z = _pad2(z, length, LANES)
    n = w3.shape[1]
    rb = min(length, 512)
    vec = lambda a: _pad2(a.reshape(1, -1), 1, LANES)
    return pl.pallas_call(
        _hy_filt_kernel,
        grid=(length // rb,),
        in_specs=[pl.BlockSpec((rb, LANES), lambda i: (i, 0)),
                  _resident((LANES, LANES)), _resident((1, LANES)), _resident((1, LANES)),
                  _resident((LANES, LANES)), _resident((1, LANES)),
                  _resident((LANES, n)), _resident((1, n))],
        out_specs=pl.BlockSpec((rb, n), lambda i: (i, 0)),
        out_shape=jax.ShapeDtypeStruct((length, n), F32),
        compiler_params=_cparams(("parallel",)),
        name="hy_filters",
    )(z, _pad2(w1, LANES, LANES), vec(b1), vec(freq), _pad2(w2, LANES, LANES), vec(b2),
      _pad2(w3, LANES, n), log_decay.reshape(1, n))


def _bidir_taps(h, skip, length):
    h4 = h.reshape(length, 2, 2, HY_CH)
    cols = []
    for o in range(2):
        hf = h4[:, o, 0].at[0].add(skip[o])
        hb = h4[:, o, 1]
        cols.append(jnp.concatenate([hf, jnp.zeros_like(hf[:1]), hb[:0:-1]], 0))
    return jnp.concatenate(cols, -1)


def _left_mm_kernel(m_ref, x_ref, o_ref):
    o_ref[0] = _dot(m_ref[...], x_ref[0]).astype(o_ref.dtype)


def _left_mm(mat, x, out_dtype, lane_blk):
    g, k, n = x.shape
    m = mat.shape[0]
    lane_blk = min(lane_blk, n)
    return pl.pallas_call(
        _left_mm_kernel, grid=(g, n // lane_blk),
        in_specs=[_resident((m, k)), pl.BlockSpec((1, k, lane_blk), lambda i, j: (i, 0, j))],
        out_specs=pl.BlockSpec((1, m, lane_blk), lambda i, j: (i, 0, j)),
        out_shape=jax.ShapeDtypeStruct((g, m, n), out_dtype),
        compiler_params=_cparams(("parallel", "parallel")),
        name="left_mm",
    )(mat, x)


def _cs(num, den, rows, cols):
    ang = 2.0 * np.pi * np.outer(np.arange(rows), np.arange(cols)) * (num / den)
    return np.cos(ang), np.sin(ang)


def _twiddle(n1, n2):
    k1 = lax.broadcasted_iota(jnp.int32, (n1, n2, 1), 0)
    m2 = lax.broadcasted_iota(jnp.int32, (n1, n2, 1), 1)
    ang = (k1 * m2).astype(F32) * (2.0 * math.pi / (n1 * n2))
    return jnp.cos(ang), jnp.sin(ang)


def _hy_prep_t_kernel(u_ref, w_ref, o_ref):
    u = u_ref[0].astype(F32)
    length = u.shape[1]
    lane = lax.broadcasted_iota(jnp.int32, u.shape, 1)
    up = jnp.where(lane == 0, 0.0, pltpu.roll(u, 1, axis=1))
    dn = jnp.where(lane == length - 1, 0.0, pltpu.roll(u, length - 1, axis=1))
    w = w_ref[...]
    y = up * w[:, 0:1] + u * w[:, 1:2] + dn * w[:, 2:3]
    for n1 in range(length // FFT_N2):
        o_ref[0, :, n1, :] = y[:, n1 * FFT_N2:(n1 + 1) * FFT_N2]


def _hy_prep_t(ut, conv_w_t):
    b, nch, length = ut.shape
    cb = 64
    n1 = length // FFT_N2
    return pl.pallas_call(
        _hy_prep_t_kernel,
        grid=(b, nch // cb),
        in_specs=[pl.BlockSpec((1, cb, length), lambda i, j: (i, j, 0)),
                  pl.BlockSpec((cb, 3), lambda i, j: (j, 0))],
        out_specs=pl.BlockSpec((1, cb, n1, FFT_N2), lambda i, j: (i, j, 0, 0)),
        out_shape=jax.ShapeDtypeStruct((b, nch, n1, FFT_N2), F32),
        compiler_params=_cparams(("parallel", "parallel")),
        name="hy_prep_t",
    )(ut, conv_w_t)


def _hy_filt_t_kernel(z_ref, msk_ref, w1_ref, b1_ref, fr_ref, w2_ref, b2_ref, w3_ref, ld_ref, sk_ref, o_ref):
    hi = lax.Precision.HIGHEST
    z = z_ref[...]
    fr = fr_ref[...]
    hid = jnp.sin(fr * (jnp.dot(w1_ref[...], z, preferred_element_type=F32, precision=hi) + b1_ref[...]))
    hid = jnp.sin(fr * (jnp.dot(w2_ref[...], hid, preferred_element_type=F32, precision=hi) + b2_ref[...]))
    h = jnp.dot(w3_ref[0], hid, preferred_element_type=F32, precision=hi)
    h = h * jnp.exp(-jnp.exp(ld_ref[0]) * z[0:1, :])
    msk = msk_ref[...]
    h = h * msk[0:1, :] + sk_ref[...] * msk[1:2, :]
    for s in range(o_ref.shape[1]):
        o_ref[:, s, :] = h[:, s * FFT_N2:(s + 1) * FFT_N2]


def _hy_filters_t(length, w1, b1, freq, w2, b2, w3, log_decay, skip):
    n = 2 * length
    tt = jnp.arange(n, dtype=jnp.int32)
    pos = jnp.where(tt < length, tt, n - tt).astype(F32)
    t = pos / max(length - 1, 1)
    w = 2.0 * math.pi * pos / length
    f = jnp.linspace(1e-4, HY_BANDS - 1, HY_BANDS, dtype=F32)
    ang = f[:, None] * w[None, :]
    z = jnp.concatenate([t[None, :], jnp.cos(ang), -jnp.sin(ang)], 0)
    z = jnp.pad(z, ((0, LANES - z.shape[0]), (0, 0)))
    msk = jnp.stack([(tt != length).astype(F32), (tt == 0).astype(F32)])
    msk = jnp.pad(msk, ((0, 6), (0, 0)))
    col = lambda a: a.reshape(-1, 1)
    c2 = 2 * HY_CH
    nf = w3.shape[0]
    w3d = w3.reshape(nf, 2, 2, HY_CH).transpose(2, 1, 3, 0).reshape(2, c2, nf)
    ldd = log_decay.reshape(2, 2, HY_CH).transpose(1, 0, 2).reshape(2, c2, 1)
    rows = 8
    pb = rows * FFT_N2
    half = length // pb
    return pl.pallas_call(
        _hy_filt_t_kernel,
        grid=(n // pb,),
        in_specs=[pl.BlockSpec((LANES, pb), lambda i: (0, i)),
                  pl.BlockSpec((8, pb), lambda i: (0, i)),
                  _resident((nf, LANES)), _resident((nf, 1)), _resident((nf, 1)),
                  _resident((nf, nf)), _resident((nf, 1)),
                  pl.BlockSpec((1, c2, nf), lambda i: (i // half, 0, 0)),
                  pl.BlockSpec((1, c2, 1), lambda i: (i // half, 0, 0)),
                  _resident((c2, 1))],
        out_specs=pl.BlockSpec((c2, rows, FFT_N2), lambda i: (0, i, 0)),
        out_shape=jax.ShapeDtypeStruct((c2, n // FFT_N2, FFT_N2), F32),
        compiler_params=_cparams(("parallel",)),
        name="hy_filters_t",
    )(z, msk, _pad2(w1.T, nf, LANES), col(b1), col(freq), w2.T, col(b2), w3d, ldd, skip.reshape(c2, 1))


def _dft_mats():
    n = FFT_N2
    c, s = _cs(1, n, n, n)
    f1_half = np.concatenate([c[:, :n // 2], -s[:, :n // 2]], 0)
    f1_full = np.concatenate([c, -s], 0)
    m2r = np.block([[c, -s], [s, c]])
    m2i = np.block([[c, s], [-s, c]])
    f1_inv = np.concatenate([c[:n // 2, :], -s[:n // 2, :]], 1) / (n * n)
    cast = lambda a: jnp.asarray(a, dtype=BF16)
    return cast(f1_half), cast(f1_full), cast(m2r), cast(m2i), cast(f1_inv)


def _twiddle2d(n):
    k1 = lax.broadcasted_iota(jnp.int32, (n, n), 0)
    m2 = lax.broadcasted_iota(jnp.int32, (n, n), 1)
    ang = (k1 * m2).astype(F32) * (2.0 * math.pi / (n * n))
    return jnp.cos(ang), jnp.sin(ang)


def _fwd_spectrum(xs, f1, m2r, c, s):
    n = FFT_N2
    a = _dot(f1, jnp.concatenate(xs, axis=1))
    ts = []
    for g in range(len(xs)):
        ar = a[:n, g * n:(g + 1) * n]
        ai = a[n:, g * n:(g + 1) * n]
        ts.append(jnp.concatenate([ar * c + ai * s, ai * c - ar * s], axis=1))
    return _dot(jnp.concatenate(ts, axis=0).astype(BF16), m2r)


def _filter_spec_t_kernel(t_ref, f1_ref, m2r_ref, c_ref, s_ref, o_ref):
    g = t_ref.shape[0]
    xs = [t_ref[i].astype(BF16) for i in range(g)]
    spec = _fwd_spectrum(xs, f1_ref[...], m2r_ref[...], c_ref[...], s_ref[...])
    for i in range(g):
        o_ref[i] = spec[i * FFT_N2:(i + 1) * FFT_N2]


def _filter_spec_t(taps, f1_full, m2r, twc, tws):
    nch, n1, n = taps.shape
    g = HY_GROUP
    return pl.pallas_call(
        _filter_spec_t_kernel,
        grid=(nch // g,),
        in_specs=[pl.BlockSpec((g, n1, n), lambda i: (i, 0, 0)),
                  _resident(f1_full.shape), _resident(m2r.shape), _resident((n, n)), _resident((n, n))],
        out_specs=pl.BlockSpec((g, n, 2 * n), lambda i: (i, 0, 0)),
        out_shape=jax.ShapeDtypeStruct((nch, n, 2 * n), F32),
        compiler_params=_cparams(("parallel",)),
        name="filter_spec_t",
    )(taps, f1_full, m2r, twc, tws)


def _hyena_core_kernel(x1_ref, x2_ref, v_ref, h0_ref, h1_ref, f1_ref, m2r_ref, m2i_ref, f1i_ref, c_ref, s_ref,
                       o_ref):
    n = FFT_N2
    c = c_ref[...]
    s = s_ref[...]

    def conv(xs, h_ref, g0):
        spec = _fwd_spectrum(xs, f1_ref[...], m2r_ref[...], c, s)
        ys = []
        for g in range(HY_GROUP):
            xr = spec[g * n:(g + 1) * n, :n]
            xi = spec[g * n:(g + 1) * n, n:]
            hh = h_ref[g0 + g]
            hr, hi = hh[:, :n], hh[:, n:]
            ys.append(jnp.concatenate([xr * hr - xi * hi, xr * hi + xi * hr], axis=1))
        bm = _dot(jnp.concatenate(ys, axis=0).astype(BF16), m2i_ref[...])
        bs = []
        for g in range(HY_GROUP):
            br = bm[g * n:(g + 1) * n, :n]
            bi = bm[g * n:(g + 1) * n, n:]
            bs.append(jnp.concatenate([br * c - bi * s, bi * c + br * s], axis=0))
        y = _dot(f1i_ref[...], jnp.concatenate(bs, axis=1).astype(BF16))
        return [y[:, g * n:(g + 1) * n] for g in range(HY_GROUP)]

    for chain in range(HY_CHAINS):
        g0 = chain * HY_GROUP
        y1 = conv([v_ref[0, g0 + g].astype(BF16) for g in range(HY_GROUP)], h0_ref, g0)
        z = [(y1[g] * x1_ref[0, g0 + g]).astype(BF16) for g in range(HY_GROUP)]
        y2 = conv(z, h1_ref, g0)
        for g in range(HY_GROUP):
            o_ref[0, :, g0 + g, :] = y2[g] * x2_ref[0, g0 + g]


def _hyena_core(xs, spec, mats, twc, tws):
    f1_half, _, m2r, m2i, f1_inv = mats
    b, _, n1, n = xs.shape
    cb = HY_CBLK
    nblk = HY_CH // cb
    xspec = lambda off: pl.BlockSpec((1, cb, n1, n), lambda j, i: (i, off * nblk + j, 0, 0))
    hspec = lambda off: pl.BlockSpec((cb, n, 2 * n), lambda j, i: (off * nblk + j, 0, 0))
    return pl.pallas_call(
        _hyena_core_kernel,
        grid=(nblk, b),
        in_specs=[xspec(0), xspec(1), xspec(2), hspec(0), hspec(1),
                  _resident(f1_half.shape), _resident(m2r.shape), _resident(m2i.shape), _resident(f1_inv.shape),
                  _resident((n, n)), _resident((n, n))],
        out_specs=pl.BlockSpec((1, n1, cb, n), lambda j, i: (i, 0, j, 0)),
        out_shape=jax.ShapeDtypeStruct((b, n1, HY_CH, n), F32),
        compiler_params=_cparams(("parallel", "parallel")),
        name="hyena_core",
    )(xs, xs, xs, spec, spec, f1_half, m2r, m2i, f1_inv, twc, tws)


def _hy_ctx_kernel(v_ref, x1_ref, x2_ref, f_ref, fi_ref, h_ref, o_ref):
    nf = f_ref.shape[0] // 2
    zin = v_ref[0]
    gates = (x1_ref, x2_ref)
    for o in range(2):
        x = _dot(f_ref[...], zin)
        xr, xi = x[:nf], x[nf:]
        hr = h_ref[o, :nf]
        hi = h_ref[o, nf:]
        y = jnp.concatenate([xr * hr - xi * hi, xr * hi + xi * hr], axis=0).astype(BF16)
        zin = (_dot(fi_ref[...], y) * gates[o][0].astype(F32)).astype(BF16)
    o_ref[0] = zin


def _hy_ctx(v, x1, x2, taps):
    b, lc, c = v.shape
    nf = 2 * lc
    cm, sm = _cs(1, nf, nf, nf)
    fwd = jnp.asarray(np.concatenate([cm[:, :lc], -sm[:, :lc]], 0), dtype=BF16)
    fwd_full = jnp.asarray(np.concatenate([cm, -sm], 0), dtype=BF16)
    inv = jnp.asarray(np.concatenate([cm[:lc, :], -sm[:lc, :]], 1) / nf, dtype=BF16)
    spec = _left_mm(fwd_full, taps.astype(BF16).reshape(1, nf, 2 * c), F32, 2 * c)
    spec = spec.reshape(2 * nf, 2, c).transpose(1, 0, 2)
    blk = pl.BlockSpec((1, lc, c), lambda i: (i, 0, 0))
    return pl.pallas_call(
        _hy_ctx_kernel,
        grid=(b,),
        in_specs=[blk, blk, blk, _resident((2 * nf, lc)), _resident((lc, 2 * nf)),
                  _resident((2, 2 * nf, c))],
        out_specs=blk,
        out_shape=jax.ShapeDtypeStruct((b, lc, c), BF16),
        compiler_params=_cparams(("parallel",)),
        name="hy_ctx",
    )(v, x1, x2, fwd, inv, spec)


def _pair_rows(q2):
    lane = lax.broadcasted_iota(jnp.int32, q2.shape, 1)
    zero = jnp.zeros_like(q2)
    return jnp.concatenate([jnp.where(lane < NA_HD, q2, zero), jnp.where(lane >= NA_HD, q2, zero)], axis=0)


def _unpair_rows(o):
    r = o.shape[0] // 2
    lane = lax.broadcasted_iota(jnp.int32, (r, o.shape[1]), 1)
    return jnp.where(lane < NA_HD, o[:r], o[r:])


def _pair_softmax_pv(scores, values):
    m = scores[0].max(axis=-1, keepdims=True)
    for s in scores[1:]:
        m = jnp.maximum(m, s.max(axis=-1, keepdims=True))
    den = None
    acc = None
    for s, v in zip(scores, values):
        p = jnp.exp(s - m)
        d = p.sum(axis=-1, keepdims=True)
        a = _dot(p.astype(BF16), v)
        den = d if den is None else den + d
        acc = a if acc is None else acc + a
    return acc / den


def _natten_kernel(q_ref, k0, k1, k2, k3, v0, v1, v2, v3, kc_ref, vc_ref, bias_ref, o_ref,
                   kwin, vwin, *, rows):
    g = pl.program_id(1)
    rb = 4 * GRID_W
    for i, (kr, vr) in enumerate(((k0, v0), (k1, v1), (k2, v2), (k3, v3))):
        kwin[i * rb:(i + 1) * rb, :] = kr[0]
        vwin[i * rb:(i + 1) * rb, :] = vr[0]
    base = 4 * jnp.clip(2 * g - 1, 0, rows // 4 - 4)
    nwin = NA_WIN_R * GRID_W
    qscale = jnp.asarray(NA_HD ** -0.5, BF16)

    ones_lat = jnp.ones((nwin, LANES), BF16)
    ones_ctx = jnp.ones((kc_ref.shape[1], LANES), BF16)

    def rows_body(it, carry):
        work = []
        for u in range(NA_ROWS_PER_TRIP):
            rr = it * NA_ROWS_PER_TRIP + u
            r = 8 * g + rr
            rs = jnp.clip(r - NA_WIN_R // 2, 0, rows - NA_WIN_R)
            st = pl.multiple_of((rs - base) * GRID_W, GRID_W)
            qo = pl.multiple_of(rr * GRID_W, GRID_W)
            work += [(qo, st, rs - r + NA_WIN_R - 1, p) for p in range(NA_HEADS // 2)]
        scores = []
        for qo, st, d0, p in work:
            ls = slice(p * LANES, (p + 1) * LANES)
            qp = _pair_rows(q_ref[0, pl.ds(qo, GRID_W), ls] * qscale)
            scores.append((_dot_nt(qp, kwin[pl.ds(st, nwin), ls]) + bias_ref[d0, p].astype(F32),
                           _dot_nt(qp, kc_ref[0, :, ls])))
        maxima = [jnp.maximum(a.max(axis=-1, keepdims=True), b.max(axis=-1, keepdims=True)) for a, b in scores]
        for (qo, st, d0, p), (s_lat, s_ctx), m in zip(work, scores, maxima):
            ls = slice(p * LANES, (p + 1) * LANES)
            v_lat = jnp.concatenate([vwin[pl.ds(st, nwin), ls], ones_lat], axis=1)
            v_ctx = jnp.concatenate([vc_ref[0, :, ls], ones_ctx], axis=1)
            acc = _dot(jnp.exp(s_lat - m).astype(BF16), v_lat) + _dot(jnp.exp(s_ctx - m).astype(BF16), v_ctx)
            o = acc[:, :LANES] / acc[:, LANES:LANES + 1]
            o_ref[0, pl.ds(qo, GRID_W), ls] = _unpair_rows(o).astype(o_ref.dtype)
        return carry

    lax.fori_loop(0, 8 // NA_ROWS_PER_TRIP, rows_body, 0)


def _natten_bias(rpb):
    c = np.arange(GRID_W)[:, None]
    kc = np.arange(GRID_W)[None, :]
    cs = np.clip(c - NA_WIN_C // 2, 0, GRID_W - NA_WIN_C)
    valid = (kc >= cs) & (kc < cs + NA_WIN_C)
    dc = np.clip(kc - c + NA_WIN_C - 1, 0, 2 * NA_WIN_C - 2)
    tb = jnp.where(valid[None, None], rpb[:, :, dc], NEG_BIG)
    slabs = []
    for d0 in range(NA_WIN_R):
        s = tb[:, d0:d0 + NA_WIN_R]
        s = s.transpose(0, 2, 1, 3).reshape(NA_HEADS, GRID_W, NA_WIN_R * GRID_W)
        slabs.append(s.reshape(NA_HEADS // 2, 2 * GRID_W, NA_WIN_R * GRID_W))
    return jnp.stack(slabs).astype(BF16)


def _natten(uq, uc, bias):
    b, length, _ = uq.shape
    c = NA_HEADS * NA_HD
    rows = length // GRID_W
    rb = 4 * GRID_W
    nkb = length // rb
    qrows = 8 * GRID_W
    lc = uc.shape[1]

    def kv_spec(col, off):
        return pl.BlockSpec((1, rb, c), lambda i, g: (i, jnp.clip(2 * g - 1, 0, nkb - 4) + off, col))

    return pl.pallas_call(
        functools.partial(_natten_kernel, rows=rows),
        grid=(b, rows // 8),
        in_specs=[pl.BlockSpec((1, qrows, c), lambda i, g: (i, g, 0))]
                 + [kv_spec(1, o) for o in range(4)] + [kv_spec(2, o) for o in range(4)]
                 + [pl.BlockSpec((1, lc, c), lambda i, g: (i, 0, 4)),
                    pl.BlockSpec((1, lc, c), lambda i, g: (i, 0, 5)),
                    _resident(bias.shape)],
        out_specs=pl.BlockSpec((1, qrows, c), lambda i, g: (i, g, 0)),
        out_shape=jax.ShapeDtypeStruct((b, length, c), BF16),
        scratch_shapes=[pltpu.VMEM((4 * rb, c), BF16), pltpu.VMEM((4 * rb, c), BF16)],
        compiler_params=_cparams(("parallel", "parallel")),
        name="natten",
    )(uq, *([uq] * 8), uc, uc, bias)


def _ctx_attn_kernel(q_ref, k_ref, v_ref, o_ref):
    qscale = jnp.asarray(NA_HD ** -0.5, BF16)
    for p in range(NA_HEADS // 2):
        ls = slice(p * LANES, (p + 1) * LANES)
        qp = _pair_rows(q_ref[0, :, ls] * qscale)
        o = _pair_softmax_pv([_dot_nt(qp, k_ref[0, :, ls])], [v_ref[0, :, ls]])
        o_ref[0, :, ls] = _unpair_rows(o).astype(o_ref.dtype)


def _ctx_attn(u):
    b, lc, _ = u.shape
    c = NA_HEADS * NA_HD
    spec = lambda col: pl.BlockSpec((1, lc, c), lambda i: (i, 0, col))
    return pl.pallas_call(
        _ctx_attn_kernel,
        grid=(b,),
        in_specs=[spec(3), spec(4), spec(5)],
        out_specs=pl.BlockSpec((1, lc, c), lambda i: (i, 0, 0)),
        out_shape=jax.ShapeDtypeStruct((b, lc, c), BF16),
        compiler_params=_cparams(("parallel",)),
        name="ctx_attn",
    )(u, u, u)


def _post_kernel(x_ref, mod_ref, ya_ref, yb_ref, wa_ref, wb_ref, lng_ref, lnb_ref, w1_ref, w2_ref, o_ref, *, ya_t):
    m = mod_ref[0, 0]
    lng = lng_ref[...]
    lnb = lnb_ref[...]
    ff = w1_ref.shape[1]
    step = 1024
    sub = ROW_BLK
    nsub = x_ref.shape[1] // sub

    def head(t):
        rs = slice(t * sub, (t + 1) * sub)
        if ya_t:
            tiles = range(t * sub // LANES, (t + 1) * sub // LANES)
            ya = jnp.concatenate([ya_ref[0, s].T for s in tiles], axis=0).astype(BF16)
        else:
            ya = ya_ref[0, rs, :]
        y = _dot(ya, wa_ref[...]) + _dot(yb_ref[0, rs, :], wb_ref[...])
        x1 = _norm_rows(ALPHA * x_ref[0, rs, :] + m[2:3] * y) * lng[0:1] + lnb[0:1]
        return x1, (_norm_rows(x1) * (1.0 + m[4:5]) + m[3:4]).astype(BF16)

    def mlp(h):
        acc = None
        for c in range(ff // step):
            a = jnp.maximum(_dot(h, w1_ref[:, c * step:(c + 1) * step]), 0.0)
            d = _dot((a * a).astype(BF16), w2_ref[c * step:(c + 1) * step, :])
            acc = d if acc is None else acc + d
        return acc

    cur = head(0)
    for t in range(nsub):
        nxt = head(t + 1) if t + 1 < nsub else None
        acc = mlp(cur[1])
        o_ref[0, t * sub:(t + 1) * sub, :] = _norm_rows(ALPHA * cur[0] + m[5:6] * acc) * lng[1:2] + lnb[1:2]
        cur = nxt


def _post(x, modtab, mod_row, ya, yb, wa, wb, lng, lnb, w1, w2, ya_t=False):
    b, r, d = x.shape
    ka, kb = wa.shape[0], wb.shape[0]
    rows = min(r, 2 * ROW_BLK)
    row = lambda k: pl.BlockSpec((1, rows, k), lambda i, j: (i, j, 0))
    ya_spec = pl.BlockSpec((1, rows // LANES, ka, LANES), lambda i, j: (i, j, 0, 0)) if ya_t else row(ka)
    return pl.pallas_call(
        functools.partial(_post_kernel, ya_t=ya_t),
        grid=(b, r // rows),
        in_specs=[row(d), pl.BlockSpec((1, 1, 6, d), lambda i, j: (i, mod_row, 0, 0)), ya_spec, row(kb),
                  _resident(wa.shape), _resident(wb.shape), _resident(lng.shape), _resident(lnb.shape),
                  _resident(w1.shape), _resident(w2.shape)],
        out_specs=row(d),
        out_shape=jax.ShapeDtypeStruct((b, r, d), F32),
        compiler_params=_cparams(("parallel", "parallel")),
        name="post_mixer",
    )(x, modtab, ya, yb, wa, wb, lng, lnb, w1, w2)


def _rope(x, cos, sinl, sinr):
    reps = x.shape[1] // LANES
    tile = lambda t: jnp.concatenate([t] * reps, axis=1)
    n = x.shape[1]
    quarter = MLA_ROPE // 4
    return (x * tile(cos) + pltpu.roll(x, n - quarter, axis=1) * tile(sinl)
            + pltpu.roll(x, quarter, axis=1) * tile(sinr))


def _front_cd_kernel(x_ref, xc_ref, mod_ref, w_ref, qn_ref, kvn_ref, wuq_ref, wuk_ref, wuv_ref, epe_ref, one_ref,
                     fng_ref, fnb_ref, cbd_ref, sbd_ref,
                     cq_ref, slq_ref, srq_ref, ck_ref, slk_ref, srk_ref,
                     q_ref, k_ref, v_ref, p_ref, qf_ref, *, ctx_blk):
    m = mod_ref[0, 0]
    is_ctx = (jnp.zeros((ROW_BLK, 1), jnp.int32) + pl.program_id(1)) == ctx_blk
    x = jnp.where(is_ctx, xc_ref[0], x_ref[0])
    h = _norm_rows(x) * (1.0 + m[1:2]) + m[0:1]
    u = _dot(h.astype(BF16), w_ref[...])
    o_kv = MLA_Q_RANK
    o_fn = o_kv + MLA_KV_RANK
    o_pe = o_fn + FN_CH

    def rms(x, g):
        return x * lax.rsqrt(jnp.mean(x * x, axis=-1, keepdims=True) + LN_EPS) * g

    cq = rms(u[:, :o_kv], qn_ref[...]).astype(BF16)
    q = _dot(cq, wuq_ref[...])
    q_ref[0] = _rope(q, cq_ref[...], slq_ref[...], srq_ref[...]).astype(BF16)

    ckv = rms(u[:, o_kv:o_fn], kvn_ref[...]).astype(BF16)
    kpe = _dot(u[:, o_pe:].astype(BF16), epe_ref[...])
    k = _dot(ckv, wuk_ref[...]) + _rope(kpe, ck_ref[...], slk_ref[...], srk_ref[...])
    k_ref[0] = k.astype(BF16)
    v_ref[0] = (_dot(ckv, wuv_ref[...]) + one_ref[...]).astype(BF16)

    uf = u[:, o_fn:o_pe]
    lane = lax.broadcasted_iota(jnp.int32, uf.shape, 1)
    mean = jnp.zeros_like(uf)
    for g in range(FN_GROUPS):
        sel = (lane >= g * FN_GD) & (lane < (g + 1) * FN_GD)
        mg = jnp.sum(jnp.where(sel, uf, 0.0), axis=-1, keepdims=True) * (1.0 / FN_GD)
        mean = jnp.where(sel, mg, mean)
    uc = uf - mean
    var = jnp.zeros_like(uf)
    for g in range(FN_GROUPS):
        sel = (lane >= g * FN_GD) & (lane < (g + 1) * FN_GD)
        vg = jnp.sum(jnp.where(sel, uc * uc, 0.0), axis=-1, keepdims=True) * (1.0 / FN_GD)
        var = jnp.where(sel, vg, var)
    ug = (uc * lax.rsqrt(var + LN_EPS) * fng_ref[...] + fnb_ref[...]).astype(BF16)
    p_ref[0] = _dot(ug, cbd_ref[...]).astype(BF16)
    qf_ref[0] = _dot(ug, sbd_ref[...]).astype(BF16)


def _rope_tables(length, lt, scale):
    t = jnp.arange(lt, dtype=jnp.int32)
    rows = (t // GRID_W).astype(F32)
    cols = (t % GRID_W).astype(F32)
    half = MLA_ROPE // 2
    inv = ROPE_THETA ** (-jnp.arange(0, half, 2, dtype=F32) / half)
    ar = rows[:, None] * inv[None, :]
    ac = cols[:, None] * inv[None, :]
    ang = jnp.concatenate([ar, ar, ac, ac], -1)
    is_lat = (t < length)[:, None]
    cos = jnp.where(is_lat, jnp.cos(ang), 1.0)
    sin = jnp.where(is_lat, jnp.sin(ang), 0.0)
    qd = MLA_ROPE // 4
    ones = jnp.ones((lt, MLA_NOPE), F32)
    zeros = jnp.zeros((lt, MLA_NOPE), F32)
    tail1 = jnp.ones((lt, HEAD_PAD - MLA_NOPE - MLA_ROPE), F32)
    tail0 = jnp.zeros((lt, HEAD_PAD - MLA_NOPE - MLA_ROPE), F32)
    z8 = jnp.zeros((lt, qd), F32)
    c = jnp.concatenate([ones, cos, tail1], -1)
    sl = jnp.concatenate([zeros, -sin[:, :qd], z8, -sin[:, 2 * qd:3 * qd], z8, tail0], -1)
    sr = jnp.concatenate([zeros, z8, sin[:, qd:2 * qd], z8, sin[:, 3 * qd:], tail0], -1)
    return c * scale, sl * scale, sr * scale


def _head_slots(w, per_head, take_from, take_n):
    k = w.shape[0]
    w3 = w.reshape(k, MLA_HEADS, per_head)[:, :, take_from:take_from + take_n]
    w3 = jnp.pad(w3, ((0, 0), (0, 0), (0, HEAD_PAD - take_n)))
    return w3.reshape(k, MLA_HEADS * HEAD_PAD)


def _front_cd(xl, xc, modtab, w_in, q_norm, w_uq, kv_norm, w_ukv, fn_g, fn_b):
    b, length, d = xl.shape
    lt = length + xc.shape[1]
    nlat = length // ROW_BLK
    o_kv = MLA_Q_RANK
    o_pe = o_kv + MLA_KV_RANK
    o_fn = o_pe + MLA_ROPE
    hw = MLA_HEADS * HEAD_PAD
    w_perm = jnp.concatenate([w_in[:, :o_pe], w_in[:, o_fn:], w_in[:, o_pe:o_fn],
                              jnp.zeros((d, LANES - MLA_ROPE), w_in.dtype)], -1).astype(BF16)
    wuq = _head_slots(w_uq, MLA_NOPE + MLA_ROPE, 0, MLA_NOPE + MLA_ROPE).astype(BF16)
    wuk = _head_slots(w_ukv, MLA_NOPE + MLA_V, 0, MLA_NOPE).astype(BF16)
    wuv = _head_slots(w_ukv, MLA_NOPE + MLA_V, MLA_NOPE, MLA_V).astype(BF16)
    epe = np.zeros((LANES, hw), np.float32)
    for hd in range(MLA_HEADS):
        for i in range(MLA_ROPE):
            epe[i, hd * HEAD_PAD + MLA_NOPE + i] = 1.0
    epe = jnp.asarray(epe, dtype=BF16)
    ones_col = np.zeros((1, hw), np.float32)
    ones_col[0, MLA_V::HEAD_PAD] = 1.0
    ones_col = jnp.asarray(ones_col)
    cm, sm = _cs(1, FN_GD, FN_GD, FN_GD)
    eye = np.eye(FN_GROUPS)
    cbd = jnp.asarray(np.kron(eye, cm), dtype=BF16)
    sbd = jnp.asarray(np.kron(eye, -sm), dtype=BF16)
    qtab = _rope_tables(length, lt, (MLA_NOPE + MLA_ROPE) ** -0.5 * math.log2(math.e))
    ktab = _rope_tables(length, lt, 1.0)
    row = lambda n: pl.BlockSpec((1, ROW_BLK, n), lambda i, j: (i, j, 0))
    tab = pl.BlockSpec((ROW_BLK, HEAD_PAD), lambda i, j: (j, 0))
    out = lambda n: jax.ShapeDtypeStruct((b, lt, n), BF16)
    return pl.pallas_call(
        functools.partial(_front_cd_kernel, ctx_blk=nlat),
        grid=(b, lt // ROW_BLK),
        in_specs=[pl.BlockSpec((1, ROW_BLK, d), lambda i, j: (i, jnp.minimum(j, nlat - 1), 0)),
                  pl.BlockSpec((1, ROW_BLK, d), lambda i, j: (i, 0, 0)),
                  pl.BlockSpec((1, 1, 6, d), lambda i, j: (i, j // nlat, 0, 0)),
                  _resident(w_perm.shape), _resident((1, MLA_Q_RANK)), _resident((1, MLA_KV_RANK)),
                  _resident(wuq.shape), _resident(wuk.shape), _resident(wuv.shape), _resident(epe.shape),
                  _resident(ones_col.shape),
                  _resident((1, FN_CH)), _resident((1, FN_CH)), _resident(cbd.shape), _resident(sbd.shape),
                  tab, tab, tab, tab, tab, tab],
        out_specs=[row(hw), row(hw), row(hw), row(FN_CH), row(FN_CH)],
        out_shape=[out(hw), out(hw), out(hw), out(FN_CH), out(FN_CH)],
        compiler_params=_cparams(("parallel", "parallel")),
        name="front_cd",
    )(xl, xc, modtab, w_perm, q_norm.reshape(1, -1), kv_norm.reshape(1, -1), wuq, wuk, wuv, epe, ones_col,
      fn_g.reshape(1, -1), fn_b.reshape(1, -1), cbd, sbd, *qtab, *ktab)


def _mla_kernel(q_ref, k_ref, v_ref, o_ref, *, rows):
    n = q_ref.shape[1] // rows

    def scores(i):
        s = _dot_nt(q_ref[0, i * rows:(i + 1) * rows, :], k_ref[0])
        return s, s.max(axis=-1, keepdims=True)

    def finish(i, s, m):
        acc = _dot(jnp.exp2(s - m).astype(BF16), v_ref[0])
        o_ref[0, i * rows:(i + 1) * rows, :] = (acc / acc[:, MLA_V:MLA_V + 1]).astype(o_ref.dtype)

    pending = scores(0)
    for i in range(n):
        nxt = scores(i + 1) if i + 1 < n else None
        finish(i, *pending)
        pending = nxt


def _mla_attention(q, k, v, length):
    b, lt, hw = q.shape
    heads = hw // HEAD_PAD
    tq = 1024
    kv = pl.BlockSpec((1, lt, HEAD_PAD), lambda i, h, j: (i, 0, h))
    qs = pl.BlockSpec((1, tq, HEAD_PAD), lambda i, h, j: (i, j, h))
    return pl.pallas_call(
        functools.partial(_mla_kernel, rows=256),
        grid=(b, heads, length // tq),
        in_specs=[qs, kv, kv],
        out_specs=qs,
        out_shape=jax.ShapeDtypeStruct((b, length, hw), BF16),
        compiler_params=_cparams(("parallel", "parallel", "parallel")),
        name="mla_attention",
    )(q, k, v)


def _fn1_kernel(m_ref, zr_ref, zi_ref, o_ref):
    n = zr_ref.shape[1]
    a = _dot(m_ref[...], jnp.concatenate([zr_ref[0], zi_ref[0]], axis=0))
    o_ref[0, 0] = a[:n].astype(o_ref.dtype)
    o_ref[0, 1] = a[n:].astype(o_ref.dtype)


def _fn2_kernel(a_ref, c_ref, s_ref, m_ref, o_ref):
    for i in range(a_ref.shape[2]):
        ar = a_ref[0, 0, i].astype(F32)
        ai = a_ref[0, 1, i].astype(F32)
        c = c_ref[i]
        s = s_ref[i]
        t = jnp.concatenate([ar * c + ai * s, ai * c - ar * s], axis=0).astype(BF16)
        o_ref[0, i] = _dot(m_ref[...], t).astype(o_ref.dtype)


def _fnet(p, qn):
    b, length, c = p.shape
    n1 = 128
    n2 = length // n1
    cm, sm = _cs(1, n1, n1, n1)
    m1 = jnp.asarray(np.block([[cm, sm], [-sm, cm]]), dtype=BF16)
    c2, s2 = _cs(1, n2, n2, n2)
    m2 = jnp.asarray(np.concatenate([c2, s2], 1) / math.sqrt(length * FN_GD), dtype=BF16)
    twc, tws = _twiddle(n1, n2)
    lane_blk = 2048
    zs = pl.BlockSpec((1, n1, lane_blk), lambda i, j: (i, 0, j))
    a = pl.pallas_call(
        _fn1_kernel,
        grid=(b, n2 * c // lane_blk),
        in_specs=[_resident(m1.shape), zs, zs],
        out_specs=pl.BlockSpec((1, 2, n1, lane_blk), lambda i, j: (i, 0, 0, j)),
        out_shape=jax.ShapeDtypeStruct((b, 2, n1, n2 * c), BF16),
        compiler_params=_cparams(("parallel", "parallel")),
        name="fnet_stage1",
    )(m1, p.reshape(b, n1, n2 * c), qn.reshape(b, n1, n2 * c))
    kb = 8
    y = pl.pallas_call(
        _fn2_kernel,
        grid=(b, n1 // kb),
        in_specs=[pl.BlockSpec((1, 2, kb, n2, c), lambda i, j: (i, 0, j, 0, 0)),
                  pl.BlockSpec((kb, n2, 1), lambda i, j: (j, 0, 0)),
                  pl.BlockSpec((kb, n2, 1), lambda i, j: (j, 0, 0)),
                  _resident(m2.shape)],
        out_specs=pl.BlockSpec((1, kb, n2, c), lambda i, j: (i, j, 0, 0)),
        out_shape=jax.ShapeDtypeStruct((b, n1, n2, c), BF16),
        compiler_params=_cparams(("parallel", "parallel")),
        name="fnet_stage2",
    )(a.reshape(b, 2, n1, n2, c), twc, tws, m2)
    return y.transpose(0, 2, 1, 3).reshape(b, length, c)


def kernel(x, c, ctx, c_ctx, mod_w, mod_b, ln_g, ln_b, mlp_w1, mlp_w2,
           ab_w_in, ab_w_out, hy_conv_w, hy_w1, hy_b1, hy_freq, hy_w2, hy_b2, hy_w3, hy_log_decay, hy_skip, na_rpb,
           cd_w_in, cd_w_out, mla_q_norm, mla_w_uq, mla_kv_norm, mla_w_ukv, fn_norm_g, fn_norm_b):
    b, length, d = x.shape
    lc = ctx.shape[1]

    cc = jnp.concatenate([c, c_ctx[None], jnp.zeros((8 - b - 1, d), F32)], 0)
    mods = _mod_vectors(cc, mod_w, mod_b).reshape(DEPTH, 8, 6, d)
    modtab = [jnp.stack([mods[l, :b], jnp.broadcast_to(mods[l, b], (b, 6, d))], axis=1) for l in range(DEPTH)]

    n_hy = 3 * HY_CH
    w_in = ab_w_in[0].astype(BF16)
    uq, ut = _front_ab_lat(x, modtab[0], w_in[:, n_hy:], w_in[:, :n_hy].T)
    uc = _front_ab_ctx(ctx, modtab[0], w_in)
    fargs = (hy_w1[0], hy_b1[0], hy_freq[0], hy_w2[0], hy_b2[0], hy_w3[0], hy_log_decay[0])
    mats = _dft_mats()
    twc, tws = _twiddle2d(FFT_N2)
    spec = _filter_spec_t(_hy_filters_t(length, *fargs, hy_skip[0]), mats[1], mats[2], twc, tws)
    y_hy_t = _hyena_core(_hy_prep_t(ut, hy_conv_w[0].T), spec, mats, twc, tws)
    x1c, x2c, vc = _hy_prep(uc, hy_conv_w[0], 0, lc // ROW_BLK)
    y_hy_c = _hy_ctx(vc, x1c, x2c, _bidir_taps(_hy_filters(lc, *fargs), hy_skip[0], lc))
    y_na = _natten(uq, uc, _natten_bias(na_rpb[0]))
    y_na_c = _ctx_attn(uc)
    w_out = ab_w_out[0].astype(BF16)
    mlp = (ln_g[0], ln_b[0], mlp_w1[0].astype(BF16), mlp_w2[0].astype(BF16))
    xl = _post(x, modtab[0], 0, y_hy_t, y_na, w_out[:HY_CH], w_out[HY_CH:], *mlp, ya_t=True)
    xc = _post(ctx, modtab[0], 1, y_hy_c, y_na_c, w_out[:HY_CH], w_out[HY_CH:], *mlp)

    q, k, vv, p, qn = _front_cd(xl, xc, modtab[1], cd_w_in[0], mla_q_norm[0], mla_w_uq[0], mla_kv_norm[0],
                                mla_w_ukv[0], fn_norm_g[0], fn_norm_b[0])
    o = _mla_attention(q, k, vv, length)
    y_fn = _fnet(p[:, :length], qn[:, :length])
    w_out = cd_w_out[0]
    n_mla = MLA_HEADS * MLA_V
    wa = jnp.pad(w_out[:n_mla].reshape(MLA_HEADS, MLA_V, d), ((0, 0), (0, HEAD_PAD - MLA_V), (0, 0)))
    wa = wa.reshape(MLA_HEADS * HEAD_PAD, d).astype(BF16)
    return _post(xl, modtab[1], 0, o, y_fn, wa, w_out[n_mla:].astype(BF16),
                 ln_g[1], ln_b[1], mlp_w1[1].astype(BF16), mlp_w2[1].astype(BF16))
```

```python
import functools
import math

import numpy as np
import jax
import jax.numpy as jnp
from jax import lax
from jax.experimental import pallas as pl
from jax.experimental.pallas import tpu as pltpu

F32 = jnp.float32
BF16 = jnp.bfloat16

D_MODEL = 1024
DEPTH = 2
GRID_W = 64
HY_CH = 512
HY_EMB = 33
HY_BANDS = (HY_EMB - 1) // 2
NA_HEADS = 8
NA_HD = 64
NA_WIN_R = 8
NA_WIN_C = 16
MLA_HEADS = 8
MLA_Q_RANK = 384
MLA_KV_RANK = 256
MLA_NOPE = 64
MLA_ROPE = 32
MLA_V = 96
ROPE_THETA = 10000.0
FN_CH = 256
FN_GROUPS = 4
FN_GD = FN_CH // FN_GROUPS
D_FF = 4 * D_MODEL
ALPHA = (2.0 * DEPTH) ** 0.25
LN_EPS = 1e-5

LANES = 128
ROW_BLK = 256
HEAD_PAD = 128
FFT_N2 = 128
NA_ROWS_PER_TRIP = 4
HY_GROUP = 16
HY_CHAINS = 2
HY_CBLK = HY_GROUP * HY_CHAINS
VMEM_LIMIT = 56 * 1024 * 1024
NEG_BIG = -1e30


def _cparams(sem, vmem=VMEM_LIMIT):
    return pltpu.CompilerParams(dimension_semantics=sem, vmem_limit_bytes=vmem)


def _resident(shape):
    nd = len(shape)
    return pl.BlockSpec(shape, lambda *_: (0,) * nd, pipeline_mode=pl.Buffered(1))


def _norm_rows(x):
    mu = jnp.mean(x, axis=-1, keepdims=True)
    xc = x - mu
    var = jnp.mean(xc * xc, axis=-1, keepdims=True)
    return xc * lax.rsqrt(var + LN_EPS)


def _dot(a, b):
    return jnp.dot(a, b, preferred_element_type=F32)


def _dot_nt(a, b):
    return lax.dot_general(a, b, (((1,), (1,)), ((), ())), preferred_element_type=F32)


def _mod_kernel(c_ref, w_ref, b_ref, o_ref):
    c = c_ref[...]
    s = c * (1.0 / (1.0 + jnp.exp(-c)))
    o_ref[0] = jnp.dot(s, w_ref[0], preferred_element_type=F32,
                       precision=lax.Precision.HIGHEST) + b_ref[0]


def _mod_vectors(cc, mod_w, mod_b):
    depth, d, n = mod_w.shape
    nb = 1024
    return pl.pallas_call(
        _mod_kernel,
        grid=(depth, n // nb),
        in_specs=[pl.BlockSpec((8, d), lambda l, j: (0, 0)),
                  pl.BlockSpec((1, d, nb), lambda l, j: (l, 0, j)),
                  pl.BlockSpec((1, 1, nb), lambda l, j: (l, 0, j))],
        out_specs=pl.BlockSpec((1, 8, nb), lambda l, j: (l, 0, j)),
        out_shape=jax.ShapeDtypeStruct((depth, 8, n), F32),
        compiler_params=_cparams(("parallel", "parallel")),
        name="mod_vectors",
    )(cc, mod_w, mod_b.reshape(depth, 1, n))


def _front_ab_ctx_kernel(x_ref, mod_ref, w_ref, u_ref):
    m = mod_ref[0, 0]
    h = _norm_rows(x_ref[0]) * (1.0 + m[1:2]) + m[0:1]
    u_ref[0] = _dot(h.astype(BF16), w_ref[...]).astype(BF16)


def _front_ab_ctx(xc, modtab, w_in):
    b, lc, d = xc.shape
    n = w_in.shape[1]
    return pl.pallas_call(
        _front_ab_ctx_kernel,
        grid=(b, lc // ROW_BLK),
        in_specs=[pl.BlockSpec((1, ROW_BLK, d), lambda i, j: (i, j, 0)),
                  pl.BlockSpec((1, 1, 6, d), lambda i, j: (i, 1, 0, 0)),
                  _resident((d, n))],
        out_specs=pl.BlockSpec((1, ROW_BLK, n), lambda i, j: (i, j, 0)),
        out_shape=jax.ShapeDtypeStruct((b, lc, n), BF16),
        compiler_params=_cparams(("parallel", "parallel")),
        name="front_ab_ctx",
    )(xc, modtab, w_in)


def _front_ab_lat_kernel(x_ref, mod_ref, wq_ref, wht_ref, u_ref, ut_ref):
    m = mod_ref[0, 0]
    per = ROW_BLK // FFT_N2
    for t in range(x_ref.shape[1] // ROW_BLK):
        rs = slice(t * ROW_BLK, (t + 1) * ROW_BLK)
        h = (_norm_rows(x_ref[0, rs, :]) * (1.0 + m[1:2]) + m[0:1]).astype(BF16)
        u_ref[0, rs, :] = _dot(h, wq_ref[...]).astype(BF16)
        ut = _dot_nt(wht_ref[...], h)
        for s in range(per):
            ut_ref[0, :, t * per + s, :] = ut[:, s * FFT_N2:(s + 1) * FFT_N2]


def _front_ab_lat(x, modtab, w_qkv, w_hy_t):
    b, length, d = x.shape
    nq = w_qkv.shape[1]
    nh = w_hy_t.shape[0]
    rows = 8 * FFT_N2
    return pl.pallas_call(
        _front_ab_lat_kernel,
        grid=(b, length // rows),
        in_specs=[pl.BlockSpec((1, rows, d), lambda i, j: (i, j, 0)),
                  pl.BlockSpec((1, 1, 6, d), lambda i, j: (i, 0, 0, 0)),
                  _resident((d, nq)), _resident((nh, d))],
        out_specs=[pl.BlockSpec((1, rows, nq), lambda i, j: (i, j, 0)),
                   pl.BlockSpec((1, nh, 8, FFT_N2), lambda i, j: (i, 0, j, 0))],
        out_shape=[jax.ShapeDtypeStruct((b, length, nq), BF16),
                   jax.ShapeDtypeStruct((b, nh, length // FFT_N2, FFT_N2), F32)],
        compiler_params=_cparams(("parallel", "parallel")),
        name="front_ab_lat",
    )(x, modtab, w_qkv, w_hy_t)


def _hy_prep_kernel(cur_ref, prev_ref, next_ref, w_ref, x1_ref, x2_ref, v_ref, *, nblk):
    j = pl.program_id(1)
    cur = cur_ref[0].astype(F32)
    rows = cur.shape[0]
    has_prev = (j > 0).astype(F32)
    has_next = (j < nblk - 1).astype(F32)
    prev_row = prev_ref[0][7:8].astype(F32) * has_prev
    next_row = next_ref[0][0:1].astype(F32) * has_next
    rid = lax.broadcasted_iota(jnp.int32, (rows, 1), 0)
    up = jnp.where(rid == 0, prev_row, pltpu.roll(cur, 1, axis=0))
    dn = jnp.where(rid == rows - 1, next_row, pltpu.roll(cur, rows - 1, axis=0))
    w = w_ref[...]
    y = up * w[0:1] + cur * w[1:2] + dn * w[2:3]
    c = HY_CH
    x1_ref[0] = y[:, :c].astype(BF16)
    x2_ref[0] = y[:, c:2 * c].astype(BF16)
    v_ref[0] = y[:, 2 * c:].astype(BF16)


def _hy_prep(u, conv_w, blk0, nblk):
    b, lt, _ = u.shape
    n = 3 * HY_CH
    sub = ROW_BLK // 8
    last8 = lt // 8 - 1
    out = jax.ShapeDtypeStruct((b, nblk * ROW_BLK, HY_CH), BF16)
    ospec = pl.BlockSpec((1, ROW_BLK, HY_CH), lambda i, j: (i, j, 0))
    return pl.pallas_call(
        functools.partial(_hy_prep_kernel, nblk=nblk),
        grid=(b, nblk),
        in_specs=[pl.BlockSpec((1, ROW_BLK, n), lambda i, j: (i, blk0 + j, 0)),
                  pl.BlockSpec((1, 8, n), lambda i, j: (i, jnp.maximum((blk0 + j) * sub - 1, 0), 0)),
                  pl.BlockSpec((1, 8, n), lambda i, j: (i, jnp.minimum((blk0 + j + 1) * sub, last8), 0)),
                  _resident((3, n))],
        out_specs=[ospec, ospec, ospec],
        out_shape=[out, out, out],
        compiler_params=_cparams(("parallel", "parallel")),
        name="hy_prep",
    )(u, u, u, conv_w)


def _hy_filt_kernel(z_ref, w1_ref, b1_ref, fr_ref, w2_ref, b2_ref, w3_ref, ld_ref, o_ref):
    hi = lax.Precision.HIGHEST
    z = z_ref[...]
    fr = fr_ref[...]
    hid = jnp.sin(fr * (jnp.dot(z, w1_ref[...], preferred_element_type=F32, precision=hi) + b1_ref[...]))
    hid = jnp.sin(fr * (jnp.dot(hid, w2_ref[...], preferred_element_type=F32, precision=hi) + b2_ref[...]))
    h = jnp.dot(hid, w3_ref[...], preferred_element_type=F32, precision=hi)
    t = z[:, 0:1]
    o_ref[...] = h * jnp.exp(-t * jnp.exp(ld_ref[...]))


def _pad2(a, rows, cols):
    return jnp.pad(a, ((0, rows - a.shape[0]), (0, cols - a.shape[1])))


def _hy_filters(length, w1, b1, freq, w2, b2, w3, log_decay):
    pos = jnp.arange(length, dtype=F32)
    t = pos / max(length - 1, 1)
    w = 2.0 * math.pi * pos / length
    f = jnp.linspace(1e-4, HY_BANDS - 1, HY_BANDS, dtype=F32)
    ang = w[:, None] * f[None, :]
    z = jnp.concatenate([t[:, None], jnp.cos(ang), -jnp.sin(ang)], -1)
    z = _pad2(z, length, LANES)
    n = w3.shape[1]
    rb = min(length, 512)
    vec = lambda a: _pad2(a.reshape(1, -1), 1, LANES)
    return pl.pallas_call(
        _hy_filt_kernel,
        grid=(length // rb,),
        in_specs=[pl.BlockSpec((rb, LANES), lambda i: (i, 0)),
                  _resident((LANES, LANES)), _resident((1, LANES)), _resident((1, LANES)),
                  _resident((LANES, LANES)), _resident((1, LANES)),
                  _resident((LANES, n)), _resident((1, n))],
        out_specs=pl.BlockSpec((rb, n), lambda i: (i, 0)),
        out_shape=jax.ShapeDtypeStruct((length, n), F32),
        compiler_params=_cparams(("parallel",)),
        name="hy_filters",
    )(z, _pad2(w1, LANES, LANES), vec(b1), vec(freq), _pad2(w2, LANES, LANES), vec(b2),
      _pad2(w3, LANES, n), log_decay.reshape(1, n))


def _bidir_taps(h, skip, length):
    h4 = h.reshape(length, 2, 2, HY_CH)
    cols = []
    for o in range(2):
        hf = h4[:, o, 0].at[0].add(skip[o])
        hb = h4[:, o, 1]
        cols.append(jnp.concatenate([hf, jnp.zeros_like(hf[:1]), hb[:0:-1]], 0))
    return jnp.concatenate(cols, -1)


def _left_mm_kernel(m_ref, x_ref, o_ref):
    o_ref[0] = _dot(m_ref[...], x_ref[0]).astype(o_ref.dtype)


def _left_mm(mat, x, out_dtype, lane_blk):
    g, k, n = x.shape
    m = mat.shape[0]
    lane_blk = min(lane_blk, n)
    return pl.pallas_call(
        _left_mm_kernel, grid=(g, n // lane_blk),
        in_specs=[_resident((m, k)), pl.BlockSpec((1, k, lane_blk), lambda i, j: (i, 0, j))],
        out_specs=pl.BlockSpec((1, m, lane_blk), lambda i, j: (i, 0, j)),
        out_shape=jax.ShapeDtypeStruct((g, m, n), out_dtype),
        compiler_params=_cparams(("parallel", "parallel")),
        name="left_mm",
    )(mat, x)


def _cs(num, den, rows, cols):
    ang = 2.0 * np.pi * np.outer(np.arange(rows), np.arange(cols)) * (num / den)
    return np.cos(ang), np.sin(ang)


def _twiddle(n1, n2):
    k1 = lax.broadcasted_iota(jnp.int32, (n1, n2, 1), 0)
    m2 = lax.broadcasted_iota(jnp.int32, (n1, n2, 1), 1)
    ang = (k1 * m2).astype(F32) * (2.0 * math.pi / (n1 * n2))
    return jnp.cos(ang), jnp.sin(ang)


def _hy_filt_t_kernel(z_ref, msk_ref, w1_ref, b1_ref, fr_ref, w2_ref, b2_ref, w3_ref, ld_ref, sk_ref, o_ref):
    hi = lax.Precision.HIGHEST
    z = z_ref[...]
    fr = fr_ref[...]
    hid = jnp.sin(fr * (jnp.dot(w1_ref[...], z, preferred_element_type=F32, precision=hi) + b1_ref[...]))
    hid = jnp.sin(fr * (jnp.dot(w2_ref[...], hid, preferred_element_type=F32, precision=hi) + b2_ref[...]))
    h = jnp.dot(w3_ref[0], hid, preferred_element_type=F32, precision=hi)
    h = h * jnp.exp(-jnp.exp(ld_ref[0]) * z[0:1, :])
    msk = msk_ref[...]
    h = h * msk[0:1, :] + sk_ref[...] * msk[1:2, :]
    for s in range(o_ref.shape[1]):
        o_ref[:, s, :] = h[:, s * FFT_N2:(s + 1) * FFT_N2]


def _hy_filters_t(length, w1, b1, freq, w2, b2, w3, log_decay, skip):
    n = 2 * length
    tt = jnp.arange(n, dtype=jnp.int32)
    pos = jnp.where(tt < length, tt, n - tt).astype(F32)
    t = pos / max(length - 1, 1)
    w = 2.0 * math.pi * pos / length
    f = jnp.linspace(1e-4, HY_BANDS - 1, HY_BANDS, dtype=F32)
    ang = f[:, None] * w[None, :]
    z = jnp.concatenate([t[None, :], jnp.cos(ang), -jnp.sin(ang)], 0)
    z = jnp.pad(z, ((0, LANES - z.shape[0]), (0, 0)))
    msk = jnp.stack([(tt != length).astype(F32), (tt == 0).astype(F32)])
    msk = jnp.pad(msk, ((0, 6), (0, 0)))
    col = lambda a: a.reshape(-1, 1)
    c2 = 2 * HY_CH
    nf = w3.shape[0]
    w3d = w3.reshape(nf, 2, 2, HY_CH).transpose(2, 1, 3, 0).reshape(2, c2, nf)
    ldd = log_decay.reshape(2, 2, HY_CH).transpose(1, 0, 2).reshape(2, c2, 1)
    rows = 8
    pb = rows * FFT_N2
    half = length // pb
    return pl.pallas_call(
        _hy_filt_t_kernel,
        grid=(n // pb,),
        in_specs=[pl.BlockSpec((LANES, pb), lambda i: (0, i)),
                  pl.BlockSpec((8, pb), lambda i: (0, i)),
                  _resident((nf, LANES)), _resident((nf, 1)), _resident((nf, 1)),
                  _resident((nf, nf)), _resident((nf, 1)),
                  pl.BlockSpec((1, c2, nf), lambda i: (i // half, 0, 0)),
                  pl.BlockSpec((1, c2, 1), lambda i: (i // half, 0, 0)),
                  _resident((c2, 1))],
        out_specs=pl.BlockSpec((c2, rows, FFT_N2), lambda i: (0, i, 0)),
        out_shape=jax.ShapeDtypeStruct((c2, n // FFT_N2, FFT_N2), F32),
        compiler_params=_cparams(("parallel",)),
        name="hy_filters_t",
    )(z, msk, _pad2(w1.T, nf, LANES), col(b1), col(freq), w2.T, col(b2), w3d, ldd, skip.reshape(c2, 1))


def _dft_mats():
    n = FFT_N2
    c, s = _cs(1, n, n, n)
    f1_half = np.concatenate([c[:, :n // 2], -s[:, :n // 2]], 0)
    f1_full = np.concatenate([c, -s], 0)
    m2r = np.block([[c, -s], [s, c]])
    m2i = np.block([[c, s], [-s, c]])
    f1_inv = np.concatenate([c[:n // 2, :], -s[:n // 2, :]], 1) / (n * n)
    cast = lambda a: jnp.asarray(a, dtype=BF16)
    return cast(f1_half), cast(f1_full), cast(m2r), cast(m2i), cast(f1_inv)


def _twiddle2d(n):
    k1 = lax.broadcasted_iota(jnp.int32, (n, n), 0)
    m2 = lax.broadcasted_iota(jnp.int32, (n, n), 1)
    ang = (k1 * m2).astype(F32) * (2.0 * math.pi / (n * n))
    return jnp.cos(ang), jnp.sin(ang)


def _fwd_spectrum(xs, f1, m2r, c, s):
    n = FFT_N2
    a = _dot(f1, jnp.concatenate(xs, axis=1))
    ts = []
    for g in range(len(xs)):
        ar = a[:n, g * n:(g + 1) * n]
        ai = a[n:, g * n:(g + 1) * n]
        ts.append(jnp.concatenate([ar * c + ai * s, ai * c - ar * s], axis=1))
    return _dot(jnp.concatenate(ts, axis=0).astype(BF16), m2r)


def _filter_spec_t_kernel(t_ref, f1_ref, m2r_ref, c_ref, s_ref, o_ref):
    g = t_ref.shape[0]
    xs = [t_ref[i].astype(BF16) for i in range(g)]
    spec = _fwd_spectrum(xs, f1_ref[...], m2r_ref[...], c_ref[...], s_ref[...])
    for i in range(g):
        o_ref[i] = spec[i * FFT_N2:(i + 1) * FFT_N2]


def _filter_spec_t(taps, f1_full, m2r, twc, tws):
    nch, n1, n = taps.shape
    g = HY_GROUP
    return pl.pallas_call(
        _filter_spec_t_kernel,
        grid=(nch // g,),
        in_specs=[pl.BlockSpec((g, n1, n), lambda i: (i, 0, 0)),
                  _resident(f1_full.shape), _resident(m2r.shape), _resident((n, n)), _resident((n, n))],
        out_specs=pl.BlockSpec((g, n, 2 * n), lambda i: (i, 0, 0)),
        out_shape=jax.ShapeDtypeStruct((nch, n, 2 * n), F32),
        compiler_params=_cparams(("parallel",)),
        name="filter_spec_t",
    )(taps, f1_full, m2r, twc, tws)


def _hyena_core_kernel(x1_ref, x2_ref, v_ref, w1_ref, w2_ref, wv_ref, h0_ref, h1_ref,
                       f1_ref, m2r_ref, m2i_ref, f1i_ref, c_ref, s_ref, o_ref):
    n = FFT_N2
    n1 = v_ref.shape[2]
    c = c_ref[...]
    s = s_ref[...]
    lane = lax.broadcasted_iota(jnp.int32, (n1, n), 1)
    row = lax.broadcasted_iota(jnp.int32, (n1, n), 0)
    first_lane, last_lane = lane == 0, lane == n - 1
    seq_start, seq_end = first_lane & (row == 0), last_lane & (row == n1 - 1)

    def short_conv(x, w):
        r = pltpu.roll(x, 1, axis=1)
        up = jnp.where(first_lane, pltpu.roll(r, 1, axis=0), r)
        up = jnp.where(seq_start, 0.0, up)
        l = pltpu.roll(x, n - 1, axis=1)
        dn = jnp.where(last_lane, pltpu.roll(l, n1 - 1, axis=0), l)
        dn = jnp.where(seq_end, 0.0, dn)
        return up * w[0:1] + x * w[1:2] + dn * w[2:3]

    def conv_all(xss, h_ref):
        spec_s = [_fwd_spectrum(xs, f1_ref[...], m2r_ref[...], c, s) for xs in xss]
        y_s = []
        for chain, spec in enumerate(spec_s):
            ys = []
            for g in range(HY_GROUP):
                xr = spec[g * n:(g + 1) * n, :n]
                xi = spec[g * n:(g + 1) * n, n:]
                hh = h_ref[chain * HY_GROUP + g]
                hr, hi = hh[:, :n], hh[:, n:]
                ys.append(jnp.concatenate([xr * hr - xi * hi, xr * hi + xi * hr], axis=1))
            y_s.append(jnp.concatenate(ys, axis=0).astype(BF16))
        bm_s = [_dot(y, m2i_ref[...]) for y in y_s]
        b_s = []
        for bm in bm_s:
            bs = []
            for g in range(HY_GROUP):
                br = bm[g * n:(g + 1) * n, :n]
                bi = bm[g * n:(g + 1) * n, n:]
                bs.append(jnp.concatenate([br * c - bi * s, bi * c + br * s], axis=0))
            b_s.append(jnp.concatenate(bs, axis=1).astype(BF16))
        outs = [_dot(f1i_ref[...], bc) for bc in b_s]
        return [[y[:, g * n:(g + 1) * n] for g in range(HY_GROUP)] for y in outs]

    chans = [[chain * HY_GROUP + g for g in range(HY_GROUP)] for chain in range(HY_CHAINS)]
    y1 = conv_all([[short_conv(v_ref[0, ch], wv_ref[ch]).astype(BF16) for ch in grp] for grp in chans], h0_ref)
    z = [[(y1[k][g] * short_conv(x1_ref[0, ch], w1_ref[ch])).astype(BF16) for g, ch in enumerate(grp)]
         for k, grp in enumerate(chans)]
    y2 = conv_all(z, h1_ref)
    for k, grp in enumerate(chans):
        for g, ch in enumerate(grp):
            o_ref[0, :, ch, :] = y2[k][g] * short_conv(x2_ref[0, ch], w2_ref[ch])


def _hyena_core(ut, conv_w, spec, mats, twc, tws):
    f1_half, _, m2r, m2i, f1_inv = mats
    b, nch, n1, n = ut.shape
    cb = HY_CBLK
    nblk = HY_CH // cb
    wt = jnp.broadcast_to(conv_w.T[:, :, None], (nch, conv_w.shape[0], n))
    xspec = lambda off: pl.BlockSpec((1, cb, n1, n), lambda j, i: (i, off * nblk + j, 0, 0))
    wspec = lambda off: pl.BlockSpec((cb, conv_w.shape[0], n), lambda j, i: (off * nblk + j, 0, 0))
    hspec = lambda off: pl.BlockSpec((cb, n, 2 * n), lambda j, i: (off * nblk + j, 0, 0))
    return pl.pallas_call(
        _hyena_core_kernel,
        grid=(nblk, b),
        in_specs=[xspec(0), xspec(1), xspec(2), wspec(0), wspec(1), wspec(2), hspec(0), hspec(1),
                  _resident(f1_half.shape), _resident(m2r.shape), _resident(m2i.shape), _resident(f1_inv.shape),
                  _resident((n, n)), _resident((n, n))],
        out_specs=pl.BlockSpec((1, n1, cb, n), lambda j, i: (i, 0, j, 0)),
        out_shape=jax.ShapeDtypeStruct((b, n1, HY_CH, n), F32),
        compiler_params=_cparams(("parallel", "parallel")),
        name="hyena_core",
    )(ut, ut, ut, wt, wt, wt, spec, spec, f1_half, m2r, m2i, f1_inv, twc, tws)


def _hy_ctx_kernel(v_ref, x1_ref, x2_ref, f_ref, fi_ref, h_ref, o_ref):
    nf = f_ref.shape[0] // 2
    zin = v_ref[0]
    gates = (x1_ref, x2_ref)
    for o in range(2):
        x = _dot(f_ref[...], zin)
        xr, xi = x[:nf], x[nf:]
        hr = h_ref[o, :nf]
        hi = h_ref[o, nf:]
        y = jnp.concatenate([xr * hr - xi * hi, xr * hi + xi * hr], axis=0).astype(BF16)
        zin = (_dot(fi_ref[...], y) * gates[o][0].astype(F32)).astype(BF16)
    o_ref[0] = zin


def _hy_ctx(v, x1, x2, taps):
    b, lc, c = v.shape
    nf = 2 * lc
    cm, sm = _cs(1, nf, nf, nf)
    fwd = jnp.asarray(np.concatenate([cm[:, :lc], -sm[:, :lc]], 0), dtype=BF16)
    fwd_full = jnp.asarray(np.concatenate([cm, -sm], 0), dtype=BF16)
    inv = jnp.asarray(np.concatenate([cm[:lc, :], -sm[:lc, :]], 1) / nf, dtype=BF16)
    spec = _left_mm(fwd_full, taps.astype(BF16).reshape(1, nf, 2 * c), F32, 2 * c)
    spec = spec.reshape(2 * nf, 2, c).transpose(1, 0, 2)
    blk = pl.BlockSpec((1, lc, c), lambda i: (i, 0, 0))
    return pl.pallas_call(
        _hy_ctx_kernel,
        grid=(b,),
        in_specs=[blk, blk, blk, _resident((2 * nf, lc)), _resident((lc, 2 * nf)),
                  _resident((2, 2 * nf, c))],
        out_specs=blk,
        out_shape=jax.ShapeDtypeStruct((b, lc, c), BF16),
        compiler_params=_cparams(("parallel",)),
        name="hy_ctx",
    )(v, x1, x2, fwd, inv, spec)


def _pair_rows(q2):
    lane = lax.broadcasted_iota(jnp.int32, q2.shape, 1)
    zero = jnp.zeros_like(q2)
    return jnp.concatenate([jnp.where(lane < NA_HD, q2, zero), jnp.where(lane >= NA_HD, q2, zero)], axis=0)


def _unpair_rows(o):
    r = o.shape[0] // 2
    lane = lax.broadcasted_iota(jnp.int32, (r, o.shape[1]), 1)
    return jnp.where(lane < NA_HD, o[:r], o[r:])


def _pair_softmax_pv(scores, values):
    m = scores[0].max(axis=-1, keepdims=True)
    for s in scores[1:]:
        m = jnp.maximum(m, s.max(axis=-1, keepdims=True))
    den = None
    acc = None
    for s, v in zip(scores, values):
        p = jnp.exp(s - m)
        d = p.sum(axis=-1, keepdims=True)
        a = _dot(p.astype(BF16), v)
        den = d if den is None else den + d
        acc = a if acc is None else acc + a
    return acc / den


def _natten_kernel(q_ref, k0, k1, k2, k3, v0, v1, v2, v3, kc_ref, vc_ref, bias_ref, o_ref,
                   kwin, vwin, *, rows):
    g = pl.program_id(1)
    rb = 4 * GRID_W
    for i, (kr, vr) in enumerate(((k0, v0), (k1, v1), (k2, v2), (k3, v3))):
        kwin[i * rb:(i + 1) * rb, :] = kr[0]
        vwin[i * rb:(i + 1) * rb, :] = vr[0]
    base = 4 * jnp.clip(2 * g - 1, 0, rows // 4 - 4)
    nwin = NA_WIN_R * GRID_W
    qscale = jnp.asarray(NA_HD ** -0.5, BF16)

    ones_lat = jnp.ones((nwin, LANES), BF16)
    ones_ctx = jnp.ones((kc_ref.shape[1], LANES), BF16)

    def rows_body(it, carry):
        work = []
        for u in range(NA_ROWS_PER_TRIP):
            rr = it * NA_ROWS_PER_TRIP + u
            r = 8 * g + rr
            rs = jnp.clip(r - NA_WIN_R // 2, 0, rows - NA_WIN_R)
            st = pl.multiple_of((rs - base) * GRID_W, GRID_W)
            qo = pl.multiple_of(rr * GRID_W, GRID_W)
            work += [(qo, st, rs - r + NA_WIN_R - 1, p) for p in range(NA_HEADS // 2)]
        scores = []
        for qo, st, d0, p in work:
            ls = slice(p * LANES, (p + 1) * LANES)
            qp = _pair_rows(q_ref[0, pl.ds(qo, GRID_W), ls] * qscale)
            scores.append((_dot_nt(qp, kwin[pl.ds(st, nwin), ls]) + bias_ref[d0, p].astype(F32),
                           _dot_nt(qp, kc_ref[0, :, ls])))
        maxima = [jnp.maximum(a.max(axis=-1, keepdims=True), b.max(axis=-1, keepdims=True)) for a, b in scores]
        for (qo, st, d0, p), (s_lat, s_ctx), m in zip(work, scores, maxima):
            ls = slice(p * LANES, (p + 1) * LANES)
            v_lat = jnp.concatenate([vwin[pl.ds(st, nwin), ls], ones_lat], axis=1)
            v_ctx = jnp.concatenate([vc_ref[0, :, ls], ones_ctx], axis=1)
            acc = _dot(jnp.exp(s_lat - m).astype(BF16), v_lat) + _dot(jnp.exp(s_ctx - m).astype(BF16), v_ctx)
            o = acc[:, :LANES] / acc[:, LANES:LANES + 1]
            o_ref[0, pl.ds(qo, GRID_W), ls] = _unpair_rows(o).astype(o_ref.dtype)
        return carry

    lax.fori_loop(0, 8 // NA_ROWS_PER_TRIP, rows_body, 0)


def _natten_bias(rpb):
    c = np.arange(GRID_W)[:, None]
    kc = np.arange(GRID_W)[None, :]
    cs = np.clip(c - NA_WIN_C // 2, 0, GRID_W - NA_WIN_C)
    valid = (kc >= cs) & (kc < cs + NA_WIN_C)
    dc = np.clip(kc - c + NA_WIN_C - 1, 0, 2 * NA_WIN_C - 2)
    tb = jnp.where(valid[None, None], rpb[:, :, dc], NEG_BIG)
    slabs = []
    for d0 in range(NA_WIN_R):
        s = tb[:, d0:d0 + NA_WIN_R]
        s = s.transpose(0, 2, 1, 3).reshape(NA_HEADS, GRID_W, NA_WIN_R * GRID_W)
        slabs.append(s.reshape(NA_HEADS // 2, 2 * GRID_W, NA_WIN_R * GRID_W))
    return jnp.stack(slabs).astype(BF16)


def _natten(uq, uc, bias):
    b, length, _ = uq.shape
    c = NA_HEADS * NA_HD
    rows = length // GRID_W
    rb = 4 * GRID_W
    nkb = length // rb
    qrows = 8 * GRID_W
    lc = uc.shape[1]

    def kv_spec(col, off):
        return pl.BlockSpec((1, rb, c), lambda i, g: (i, jnp.clip(2 * g - 1, 0, nkb - 4) + off, col))

    return pl.pallas_call(
        functools.partial(_natten_kernel, rows=rows),
        grid=(b, rows // 8),
        in_specs=[pl.BlockSpec((1, qrows, c), lambda i, g: (i, g, 0))]
                 + [kv_spec(1, o) for o in range(4)] + [kv_spec(2, o) for o in range(4)]
                 + [pl.BlockSpec((1, lc, c), lambda i, g: (i, 0, 4)),
                    pl.BlockSpec((1, lc, c), lambda i, g: (i, 0, 5)),
                    _resident(bias.shape)],
        out_specs=pl.BlockSpec((1, qrows, c), lambda i, g: (i, g, 0)),
        out_shape=jax.ShapeDtypeStruct((b, length, c), BF16),
        scratch_shapes=[pltpu.VMEM((4 * rb, c), BF16), pltpu.VMEM((4 * rb, c), BF16)],
        compiler_params=_cparams(("parallel", "parallel")),
        name="natten",
    )(uq, *([uq] * 8), uc, uc, bias)


def _ctx_attn_kernel(q_ref, k_ref, v_ref, o_ref):
    qscale = jnp.asarray(NA_HD ** -0.5, BF16)
    for p in range(NA_HEADS // 2):
        ls = slice(p * LANES, (p + 1) * LANES)
        qp = _pair_rows(q_ref[0, :, ls] * qscale)
        o = _pair_softmax_pv([_dot_nt(qp, k_ref[0, :, ls])], [v_ref[0, :, ls]])
        o_ref[0, :, ls] = _unpair_rows(o).astype(o_ref.dtype)


def _ctx_attn(u):
    b, lc, _ = u.shape
    c = NA_HEADS * NA_HD
    spec = lambda col: pl.BlockSpec((1, lc, c), lambda i: (i, 0, col))
    return pl.pallas_call(
        _ctx_attn_kernel,
        grid=(b,),
        in_specs=[spec(3), spec(4), spec(5)],
        out_specs=pl.BlockSpec((1, lc, c), lambda i: (i, 0, 0)),
        out_shape=jax.ShapeDtypeStruct((b, lc, c), BF16),
        compiler_params=_cparams(("parallel",)),
        name="ctx_attn",
    )(u, u, u)


def _post_kernel(x_ref, mod_ref, ya_ref, yb_ref, wa_ref, wb_ref, lng_ref, lnb_ref, w1_ref, w2_ref, o_ref, *, ya_t):
    m = mod_ref[0, 0]
    lng = lng_ref[...]
    lnb = lnb_ref[...]
    ff = w1_ref.shape[1]
    step = 1024
    sub = ROW_BLK
    nsub = x_ref.shape[1] // sub

    def head(t):
        rs = slice(t * sub, (t + 1) * sub)
        if ya_t:
            tiles = range(t * sub // LANES, (t + 1) * sub // LANES)
            ya = jnp.concatenate([ya_ref[0, s].T for s in tiles], axis=0).astype(BF16)
        else:
            ya = ya_ref[0, rs, :]
        y = _dot(ya, wa_ref[...]) + _dot(yb_ref[0, rs, :], wb_ref[...])
        x1 = _norm_rows(ALPHA * x_ref[0, rs, :] + m[2:3] * y) * lng[0:1] + lnb[0:1]
        return x1, (_norm_rows(x1) * (1.0 + m[4:5]) + m[3:4]).astype(BF16)

    def mlp(h):
        acc = None
        for c in range(ff // step):
            a = jnp.maximum(_dot(h, w1_ref[:, c * step:(c + 1) * step]), 0.0)
            d = _dot((a * a).astype(BF16), w2_ref[c * step:(c + 1) * step, :])
            acc = d if acc is None else acc + d
        return acc

    cur = head(0)
    for t in range(nsub):
        nxt = head(t + 1) if t + 1 < nsub else None
        acc = mlp(cur[1])
        o_ref[0, t * sub:(t + 1) * sub, :] = _norm_rows(ALPHA * cur[0] + m[5:6] * acc) * lng[1:2] + lnb[1:2]
        cur = nxt


def _post(x, modtab, mod_row, ya, yb, wa, wb, lng, lnb, w1, w2, ya_t=False):
    b, r, d = x.shape
    ka, kb = wa.shape[0], wb.shape[0]
    rows = min(r, 2 * ROW_BLK)
    row = lambda k: pl.BlockSpec((1, rows, k), lambda i, j: (i, j, 0))
    ya_spec = pl.BlockSpec((1, rows // LANES, ka, LANES), lambda i, j: (i, j, 0, 0)) if ya_t else row(ka)
    return pl.pallas_call(
        functools.partial(_post_kernel, ya_t=ya_t),
        grid=(b, r // rows),
        in_specs=[row(d), pl.BlockSpec((1, 1, 6, d), lambda i, j: (i, mod_row, 0, 0)), ya_spec, row(kb),
                  _resident(wa.shape), _resident(wb.shape), _resident(lng.shape), _resident(lnb.shape),
                  _resident(w1.shape), _resident(w2.shape)],
        out_specs=row(d),
        out_shape=jax.ShapeDtypeStruct((b, r, d), F32),
        compiler_params=_cparams(("parallel", "parallel")),
        name="post_mixer",
    )(x, modtab, ya, yb, wa, wb, lng, lnb, w1, w2)


def _rope(x, cos, sinl, sinr):
    reps = x.shape[1] // LANES
    tile = lambda t: jnp.concatenate([t] * reps, axis=1)
    n = x.shape[1]
    quarter = MLA_ROPE // 4
    return (x * tile(cos) + pltpu.roll(x, n - quarter, axis=1) * tile(sinl)
            + pltpu.roll(x, quarter, axis=1) * tile(sinr))


def _front_cd_kernel(x_ref, xc_ref, mod_ref, w_ref, qn_ref, kvn_ref, wuq_ref, wuk_ref, wuv_ref, epe_ref, one_ref,
                     fng_ref, fnb_ref, cbd_ref, sbd_ref,
                     cq_ref, slq_ref, srq_ref, ck_ref, slk_ref, srk_ref,
                     q_ref, k_ref, v_ref, p_ref, qf_ref, *, ctx_blk):
    m = mod_ref[0, 0]
    is_ctx = (jnp.zeros((ROW_BLK, 1), jnp.int32) + pl.program_id(1)) == ctx_blk
    x = jnp.where(is_ctx, xc_ref[0], x_ref[0])
    h = _norm_rows(x) * (1.0 + m[1:2]) + m[0:1]
    u = _dot(h.astype(BF16), w_ref[...])
    o_kv = MLA_Q_RANK
    o_fn = o_kv + MLA_KV_RANK
    o_pe = o_fn + FN_CH

    def rms(x, g):
        return x * lax.rsqrt(jnp.mean(x * x, axis=-1, keepdims=True) + LN_EPS) * g

    cq = rms(u[:, :o_kv], qn_ref[...]).astype(BF16)
    q = _dot(cq, wuq_ref[...])
    q_ref[0] = _rope(q, cq_ref[...], slq_ref[...], srq_ref[...]).astype(BF16)

    ckv = rms(u[:, o_kv:o_fn], kvn_ref[...]).astype(BF16)
    kpe = _dot(u[:, o_pe:].astype(BF16), epe_ref[...])
    k = _dot(ckv, wuk_ref[...]) + _rope(kpe, ck_ref[...], slk_ref[...], srk_ref[...])
    k_ref[0] = k.astype(BF16)
    v_ref[0] = (_dot(ckv, wuv_ref[...]) + one_ref[...]).astype(BF16)

    uf = u[:, o_fn:o_pe]
    lane = lax.broadcasted_iota(jnp.int32, uf.shape, 1)
    mean = jnp.zeros_like(uf)
    for g in range(FN_GROUPS):
        sel = (lane >= g * FN_GD) & (lane < (g + 1) * FN_GD)
        mg = jnp.sum(jnp.where(sel, uf, 0.0), axis=-1, keepdims=True) * (1.0 / FN_GD)
        mean = jnp.where(sel, mg, mean)
    uc = uf - mean
    var = jnp.zeros_like(uf)
    for g in range(FN_GROUPS):
        sel = (lane >= g * FN_GD) & (lane < (g + 1) * FN_GD)
        vg = jnp.sum(jnp.where(sel, uc * uc, 0.0), axis=-1, keepdims=True) * (1.0 / FN_GD)
        var = jnp.where(sel, vg, var)
    ug = (uc * lax.rsqrt(var + LN_EPS) * fng_ref[...] + fnb_ref[...]).astype(BF16)
    p_ref[0] = _dot(ug, cbd_ref[...]).astype(BF16)
    qf_ref[0] = _dot(ug, sbd_ref[...]).astype(BF16)


def _rope_tables(length, lt, scale):
    t = jnp.arange(lt, dtype=jnp.int32)
    rows = (t // GRID_W).astype(F32)
    cols = (t % GRID_W).astype(F32)
    half = MLA_ROPE // 2
    inv = ROPE_THETA ** (-jnp.arange(0, half, 2, dtype=F32) / half)
    ar = rows[:, None] * inv[None, :]
    ac = cols[:, None] * inv[None, :]
    ang = jnp.concatenate([ar, ar, ac, ac], -1)
    is_lat = (t < length)[:, None]
    cos = jnp.where(is_lat, jnp.cos(ang), 1.0)
    sin = jnp.where(is_lat, jnp.sin(ang), 0.0)
    qd = MLA_ROPE // 4
    ones = jnp.ones((lt, MLA_NOPE), F32)
    zeros = jnp.zeros((lt, MLA_NOPE), F32)
    tail1 = jnp.ones((lt, HEAD_PAD - MLA_NOPE - MLA_ROPE), F32)
    tail0 = jnp.zeros((lt, HEAD_PAD - MLA_NOPE - MLA_ROPE), F32)
    z8 = jnp.zeros((lt, qd), F32)
    c = jnp.concatenate([ones, cos, tail1], -1)
    sl = jnp.concatenate([zeros, -sin[:, :qd], z8, -sin[:, 2 * qd:3 * qd], z8, tail0], -1)
    sr = jnp.concatenate([zeros, z8, sin[:, qd:2 * qd], z8, sin[:, 3 * qd:], tail0], -1)
    return c * scale, sl * scale, sr * scale


def _head_slots(w, per_head, take_from, take_n):
    k = w.shape[0]
    w3 = w.reshape(k, MLA_HEADS, per_head)[:, :, take_from:take_from + take_n]
    w3 = jnp.pad(w3, ((0, 0), (0, 0), (0, HEAD_PAD - take_n)))
    return w3.reshape(k, MLA_HEADS * HEAD_PAD)


def _front_cd(xl, xc, modtab, w_in, q_norm, w_uq, kv_norm, w_ukv, fn_g, fn_b):
    b, length, d = xl.shape
    lt = length + xc.shape[1]
    nlat = length // ROW_BLK
    o_kv = MLA_Q_RANK
    o_pe = o_kv + MLA_KV_RANK
    o_fn = o_pe + MLA_ROPE
    hw = MLA_HEADS * HEAD_PAD
    w_perm = jnp.concatenate([w_in[:, :o_pe], w_in[:, o_fn:], w_in[:, o_pe:o_fn],
                              jnp.zeros((d, LANES - MLA_ROPE), w_in.dtype)], -1).astype(BF16)
    wuq = _head_slots(w_uq, MLA_NOPE + MLA_ROPE, 0, MLA_NOPE + MLA_ROPE).astype(BF16)
    wuk = _head_slots(w_ukv, MLA_NOPE + MLA_V, 0, MLA_NOPE).astype(BF16)
    wuv = _head_slots(w_ukv, MLA_NOPE + MLA_V, MLA_NOPE, MLA_V).astype(BF16)
    epe = np.zeros((LANES, hw), np.float32)
    for hd in range(MLA_HEADS):
        for i in range(MLA_ROPE):
            epe[i, hd * HEAD_PAD + MLA_NOPE + i] = 1.0
    epe = jnp.asarray(epe, dtype=BF16)
    ones_col = np.zeros((1, hw), np.float32)
    ones_col[0, MLA_V::HEAD_PAD] = 1.0
    ones_col = jnp.asarray(ones_col)
    cm, sm = _cs(1, FN_GD, FN_GD, FN_GD)
    eye = np.eye(FN_GROUPS)
    cbd = jnp.asarray(np.kron(eye, cm), dtype=BF16)
    sbd = jnp.asarray(np.kron(eye, -sm), dtype=BF16)
    qtab = _rope_tables(length, lt, (MLA_NOPE + MLA_ROPE) ** -0.5 * math.log2(math.e))
    ktab = _rope_tables(length, lt, 1.0)
    row = lambda n: pl.BlockSpec((1, ROW_BLK, n), lambda i, j: (i, j, 0))
    tab = pl.BlockSpec((ROW_BLK, HEAD_PAD), lambda i, j: (j, 0))
    out = lambda n: jax.ShapeDtypeStruct((b, lt, n), BF16)
    return pl.pallas_call(
        functools.partial(_front_cd_kernel, ctx_blk=nlat),
        grid=(b, lt // ROW_BLK),
        in_specs=[pl.BlockSpec((1, ROW_BLK, d), lambda i, j: (i, jnp.minimum(j, nlat - 1), 0)),
                  pl.BlockSpec((1, ROW_BLK, d), lambda i, j: (i, 0, 0)),
                  pl.BlockSpec((1, 1, 6, d), lambda i, j: (i, j // nlat, 0, 0)),
                  _resident(w_perm.shape), _resident((1, MLA_Q_RANK)), _resident((1, MLA_KV_RANK)),
                  _resident(wuq.shape), _resident(wuk.shape), _resident(wuv.shape), _resident(epe.shape),
                  _resident(ones_col.shape),
                  _resident((1, FN_CH)), _resident((1, FN_CH)), _resident(cbd.shape), _resident(sbd.shape),
                  tab, tab, tab, tab, tab, tab],
        out_specs=[row(hw), row(hw), row(hw), row(FN_CH), row(FN_CH)],
        out_shape=[out(hw), out(hw), out(hw), out(FN_CH), out(FN_CH)],
        compiler_params=_cparams(("parallel", "parallel")),
        name="front_cd",
    )(xl, xc, modtab, w_perm, q_norm.reshape(1, -1), kv_norm.reshape(1, -1), wuq, wuk, wuv, epe, ones_col,
      fn_g.reshape(1, -1), fn_b.reshape(1, -1), cbd, sbd, *qtab, *ktab)


def _mla_kernel(q_ref, k_ref, v_ref, o_ref, *, rows):
    n = q_ref.shape[1] // rows

    def scores(i):
        s = _dot_nt(q_ref[0, i * rows:(i + 1) * rows, :], k_ref[0])
        return s, s.max(axis=-1, keepdims=True)

    def finish(i, s, m):
        acc = _dot(jnp.exp2(s - m).astype(BF16), v_ref[0])
        o_ref[0, i * rows:(i + 1) * rows, :] = (acc / acc[:, MLA_V:MLA_V + 1]).astype(o_ref.dtype)

    pending = scores(0)
    for i in range(n):
        nxt = scores(i + 1) if i + 1 < n else None
        finish(i, *pending)
        pending = nxt


def _mla_attention(q, k, v, length):
    b, lt, hw = q.shape
    heads = hw // HEAD_PAD
    tq = 1024
    kv = pl.BlockSpec((1, lt, HEAD_PAD), lambda i, h, j: (i, 0, h))
    qs = pl.BlockSpec((1, tq, HEAD_PAD), lambda i, h, j: (i, j, h))
    return pl.pallas_call(
        functools.partial(_mla_kernel, rows=256),
        grid=(b, heads, length // tq),
        in_specs=[qs, kv, kv],
        out_specs=qs,
        out_shape=jax.ShapeDtypeStruct((b, length, hw), BF16),
        compiler_params=_cparams(("parallel", "parallel", "parallel")),
        name="mla_attention",
    )(q, k, v)


def _fn1_kernel(m_ref, zr_ref, zi_ref, o_ref):
    n = zr_ref.shape[1]
    a = _dot(m_ref[...], jnp.concatenate([zr_ref[0], zi_ref[0]], axis=0))
    o_ref[0, 0] = a[:n].astype(o_ref.dtype)
    o_ref[0, 1] = a[n:].astype(o_ref.dtype)


def _fn2_kernel(a_ref, c_ref, s_ref, m_ref, o_ref):
    for i in range(a_ref.shape[2]):
        ar = a_ref[0, 0, i].astype(F32)
        ai = a_ref[0, 1, i].astype(F32)
        c = c_ref[i]
        s = s_ref[i]
        t = jnp.concatenate([ar * c + ai * s, ai * c - ar * s], axis=0).astype(BF16)
        o_ref[0, i] = _dot(m_ref[...], t).astype(o_ref.dtype)


def _fnet(p, qn):
    b, length, c = p.shape
    n1 = 128
    n2 = length // n1
    cm, sm = _cs(1, n1, n1, n1)
    m1 = jnp.asarray(np.block([[cm, sm], [-sm, cm]]), dtype=BF16)
    c2, s2 = _cs(1, n2, n2, n2)
    m2 = jnp.asarray(np.concatenate([c2, s2], 1) / math.sqrt(length * FN_GD), dtype=BF16)
    twc, tws = _twiddle(n1, n2)
    lane_blk = 2048
    zs = pl.BlockSpec((1, n1, lane_blk), lambda i, j: (i, 0, j))
    a = pl.pallas_call(
        _fn1_kernel,
        grid=(b, n2 * c // lane_blk),
        in_specs=[_resident(m1.shape), zs, zs],
        out_specs=pl.BlockSpec((1, 2, n1, lane_blk), lambda i, j: (i, 0, 0, j)),
        out_shape=jax.ShapeDtypeStruct((b, 2, n1, n2 * c), BF16),
        compiler_params=_cparams(("parallel", "parallel")),
        name="fnet_stage1",
    )(m1, p.reshape(b, n1, n2 * c), qn.reshape(b, n1, n2 * c))
    kb = 8
    y = pl.pallas_call(
        _fn2_kernel,
        grid=(b, n1 // kb),
        in_specs=[pl.BlockSpec((1, 2, kb, n2, c), lambda i, j: (i, 0, j, 0, 0)),
                  pl.BlockSpec((kb, n2, 1), lambda i, j: (j, 0, 0)),
                  pl.BlockSpec((kb, n2, 1), lambda i, j: (j, 0, 0)),
                  _resident(m2.shape)],
        out_specs=pl.BlockSpec((1, kb, n2, c), lambda i, j: (i, j, 0, 0)),
        out_shape=jax.ShapeDtypeStruct((b, n1, n2, c), BF16),
        compiler_params=_cparams(("parallel", "parallel")),
        name="fnet_stage2",
    )(a.reshape(b, 2, n1, n2, c), twc, tws, m2)
    return y.transpose(0, 2, 1, 3).reshape(b, length, c)


def kernel(x, c, ctx, c_ctx, mod_w, mod_b, ln_g, ln_b, mlp_w1, mlp_w2,
           ab_w_in, ab_w_out, hy_conv_w, hy_w1, hy_b1, hy_freq, hy_w2, hy_b2, hy_w3, hy_log_decay, hy_skip, na_rpb,
           cd_w_in, cd_w_out, mla_q_norm, mla_w_uq, mla_kv_norm, mla_w_ukv, fn_norm_g, fn_norm_b):
    b, length, d = x.shape
    lc = ctx.shape[1]

    cc = jnp.concatenate([c, c_ctx[None], jnp.zeros((8 - b - 1, d), F32)], 0)
    mods = _mod_vectors(cc, mod_w, mod_b).reshape(DEPTH, 8, 6, d)
    modtab = [jnp.stack([mods[l, :b], jnp.broadcast_to(mods[l, b], (b, 6, d))], axis=1) for l in range(DEPTH)]

    n_hy = 3 * HY_CH
    w_in = ab_w_in[0].astype(BF16)
    uq, ut = _front_ab_lat(x, modtab[0], w_in[:, n_hy:], w_in[:, :n_hy].T)
    uc = _front_ab_ctx(ctx, modtab[0], w_in)
    fargs = (hy_w1[0], hy_b1[0], hy_freq[0], hy_w2[0], hy_b2[0], hy_w3[0], hy_log_decay[0])
    mats = _dft_mats()
    twc, tws = _twiddle2d(FFT_N2)
    spec = _filter_spec_t(_hy_filters_t(length, *fargs, hy_skip[0]), mats[1], mats[2], twc, tws)
    y_hy_t = _hyena_core(ut, hy_conv_w[0], spec, mats, twc, tws)
    x1c, x2c, vc = _hy_prep(uc, hy_conv_w[0], 0, lc // ROW_BLK)
    y_hy_c = _hy_ctx(vc, x1c, x2c, _bidir_taps(_hy_filters(lc, *fargs), hy_skip[0], lc))
    y_na = _natten(uq, uc, _natten_bias(na_rpb[0]))
    y_na_c = _ctx_attn(uc)
    w_out = ab_w_out[0].astype(BF16)
    mlp = (ln_g[0], ln_b[0], mlp_w1[0].astype(BF16), mlp_w2[0].astype(BF16))
    xl = _post(x, modtab[0], 0, y_hy_t, y_na, w_out[:HY_CH], w_out[HY_CH:], *mlp, ya_t=True)
    xc = _post(ctx, modtab[0], 1, y_hy_c, y_na_c, w_out[:HY_CH], w_out[HY_CH:], *mlp)

    q, k, vv, p, qn = _front_cd(xl, xc, modtab[1], cd_w_in[0], mla_q_norm[0], mla_w_uq[0], mla_kv_norm[0],
                                mla_w_ukv[0], fn_norm_g[0], fn_norm_b[0])
    o = _mla_attention(q, k, vv, length)
    y_fn = _fnet(p[:, :length], qn[:, :length])
    w_out = cd_w_out[0]
    n_mla = MLA_HEADS * MLA_V
    wa = jnp.pad(w_out[:n_mla].reshape(MLA_HEADS, MLA_V, d), ((0, 0), (0, HEAD_PAD - MLA_V), (0, 0)))
    wa = wa.reshape(MLA_HEADS * HEAD_PAD, d).astype(BF16)
    return _post(xl, modtab[1], 0, o, y_fn, wa, w_out[n_mla:].astype(BF16),
                 ln_g[1], ln_b[1], mlp_w1[1].astype(BF16), mlp_w2[1].astype(BF16))
```

```python
import functools
import math

import numpy as np
import jax
import jax.numpy as jnp
from jax import lax
from jax.experimental import pallas as pl
from jax.experimental.pallas import tpu as pltpu

F32 = jnp.float32
BF16 = jnp.bfloat16

D_MODEL = 1024
DEPTH = 2
GRID_W = 64
HY_CH = 512
HY_EMB = 33
HY_BANDS = (HY_EMB - 1) // 2
NA_HEADS = 8
NA_HD = 64
NA_WIN_R = 8
NA_WIN_C = 16
MLA_HEADS = 8
MLA_Q_RANK = 384
MLA_KV_RANK = 256
MLA_NOPE = 64
MLA_ROPE = 32
MLA_V = 96
ROPE_THETA = 10000.0
FN_CH = 256
FN_GROUPS = 4
FN_GD = FN_CH // FN_GROUPS
D_FF = 4 * D_MODEL
ALPHA = (2.0 * DEPTH) ** 0.25
LN_EPS = 1e-5

LANES = 128
ROW_BLK = 256
HEAD_PAD = 128
FFT_N2 = 128
NA_ROWS_PER_TRIP = 4
HY_GROUP = 16
HY_CHAINS = 2
HY_CBLK = HY_GROUP * HY_CHAINS
VMEM_LIMIT = 56 * 1024 * 1024
NEG_BIG = -1e30


def _cparams(sem, vmem=VMEM_LIMIT):
    return pltpu.CompilerParams(dimension_semantics=sem, vmem_limit_bytes=vmem)


def _resident(shape):
    nd = len(shape)
    return pl.BlockSpec(shape, lambda *_: (0,) * nd, pipeline_mode=pl.Buffered(1))


def _norm_rows(x):
    mu = jnp.mean(x, axis=-1, keepdims=True)
    xc = x - mu
    var = jnp.mean(xc * xc, axis=-1, keepdims=True)
    return xc * lax.rsqrt(var + LN_EPS)


def _dot(a, b):
    return jnp.dot(a, b, preferred_element_type=F32)


def _dot_nt(a, b):
    return lax.dot_general(a, b, (((1,), (1,)), ((), ())), preferred_element_type=F32)


def _mod_kernel(c_ref, w_ref, b_ref, o_ref):
    c = c_ref[...]
    s = c * (1.0 / (1.0 + jnp.exp(-c)))
    o_ref[0] = jnp.dot(s, w_ref[0], preferred_element_type=F32,
                       precision=lax.Precision.HIGHEST) + b_ref[0]


def _mod_vectors(cc, mod_w, mod_b):
    depth, d, n = mod_w.shape
    nb = 1024
    return pl.pallas_call(
        _mod_kernel,
        grid=(depth, n // nb),
        in_specs=[pl.BlockSpec((8, d), lambda l, j: (0, 0)),
                  pl.BlockSpec((1, d, nb), lambda l, j: (l, 0, j)),
                  pl.BlockSpec((1, 1, nb), lambda l, j: (l, 0, j))],
        out_specs=pl.BlockSpec((1, 8, nb), lambda l, j: (l, 0, j)),
        out_shape=jax.ShapeDtypeStruct((depth, 8, n), F32),
        compiler_params=_cparams(("parallel", "parallel")),
        name="mod_vectors",
    )(cc, mod_w, mod_b.reshape(depth, 1, n))


def _front_ab_ctx_kernel(x_ref, mod_ref, w_ref, u_ref):
    m = mod_ref[0, 0]
    h = _norm_rows(x_ref[0]) * (1.0 + m[1:2]) + m[0:1]
    u_ref[0] = _dot(h.astype(BF16), w_ref[...]).astype(BF16)


def _front_ab_ctx(xc, modtab, w_in):
    b, lc, d = xc.shape
    n = w_in.shape[1]
    return pl.pallas_call(
        _front_ab_ctx_kernel,
        grid=(b, lc // ROW_BLK),
        in_specs=[pl.BlockSpec((1, ROW_BLK, d), lambda i, j: (i, j, 0)),
                  pl.BlockSpec((1, 1, 6, d), lambda i, j: (i, 1, 0, 0)),
                  _resident((d, n))],
        out_specs=pl.BlockSpec((1, ROW_BLK, n), lambda i, j: (i, j, 0)),
        out_shape=jax.ShapeDtypeStruct((b, lc, n), BF16),
        compiler_params=_cparams(("parallel", "parallel")),
        name="front_ab_ctx",
    )(xc, modtab, w_in)


def _front_ab_lat_kernel(x_ref, mod_ref, wq_ref, wht_ref, u_ref, ut_ref):
    m = mod_ref[0, 0]
    per = ROW_BLK // FFT_N2
    for t in range(x_ref.shape[1] // ROW_BLK):
        rs = slice(t * ROW_BLK, (t + 1) * ROW_BLK)
        h = (_norm_rows(x_ref[0, rs, :]) * (1.0 + m[1:2]) + m[0:1]).astype(BF16)
        u_ref[0, rs, :] = _dot(h, wq_ref[...]).astype(BF16)
        ut = _dot_nt(wht_ref[...], h)
        for s in range(per):
            ut_ref[0, :, t * per + s, :] = ut[:, s * FFT_N2:(s + 1) * FFT_N2]


def _front_ab_lat(x, modtab, w_qkv, w_hy_t):
    b, length, d = x.shape
    nq = w_qkv.shape[1]
    nh = w_hy_t.shape[0]
    rows = 8 * FFT_N2
    return pl.pallas_call(
        _front_ab_lat_kernel,
        grid=(b, length // rows),
        in_specs=[pl.BlockSpec((1, rows, d), lambda i, j: (i, j, 0)),
                  pl.BlockSpec((1, 1, 6, d), lambda i, j: (i, 0, 0, 0)),
                  _resident((d, nq)), _resident((nh, d))],
        out_specs=[pl.BlockSpec((1, rows, nq), lambda i, j: (i, j, 0)),
                   pl.BlockSpec((1, nh, 8, FFT_N2), lambda i, j: (i, 0, j, 0))],
        out_shape=[jax.ShapeDtypeStruct((b, length, nq), BF16),
                   jax.ShapeDtypeStruct((b, nh, length // FFT_N2, FFT_N2), F32)],
        compiler_params=_cparams(("parallel", "parallel")),
        name="front_ab_lat",
    )(x, modtab, w_qkv, w_hy_t)


def _hy_prep_kernel(cur_ref, prev_ref, next_ref, w_ref, x1_ref, x2_ref, v_ref, *, nblk):
    j = pl.program_id(1)
    cur = cur_ref[0].astype(F32)
    rows = cur.shape[0]
    has_prev = (j > 0).astype(F32)
    has_next = (j < nblk - 1).astype(F32)
    prev_row = prev_ref[0][7:8].astype(F32) * has_prev
    next_row = next_ref[0][0:1].astype(F32) * has_next
    rid = lax.broadcasted_iota(jnp.int32, (rows, 1), 0)
    up = jnp.where(rid == 0, prev_row, pltpu.roll(cur, 1, axis=0))
    dn = jnp.where(rid == rows - 1, next_row, pltpu.roll(cur, rows - 1, axis=0))
    w = w_ref[...]
    y = up * w[0:1] + cur * w[1:2] + dn * w[2:3]
    c = HY_CH
    x1_ref[0] = y[:, :c].astype(BF16)
    x2_ref[0] = y[:, c:2 * c].astype(BF16)
    v_ref[0] = y[:, 2 * c:].astype(BF16)


def _hy_prep(u, conv_w, blk0, nblk):
    b, lt, _ = u.shape
    n = 3 * HY_CH
    sub = ROW_BLK // 8
    last8 = lt // 8 - 1
    out = jax.ShapeDtypeStruct((b, nblk * ROW_BLK, HY_CH), BF16)
    ospec = pl.BlockSpec((1, ROW_BLK, HY_CH), lambda i, j: (i, j, 0))
    return pl.pallas_call(
        functools.partial(_hy_prep_kernel, nblk=nblk),
        grid=(b, nblk),
        in_specs=[pl.BlockSpec((1, ROW_BLK, n), lambda i, j: (i, blk0 + j, 0)),
                  pl.BlockSpec((1, 8, n), lambda i, j: (i, jnp.maximum((blk0 + j) * sub - 1, 0), 0)),
                  pl.BlockSpec((1, 8, n), lambda i, j: (i, jnp.minimum((blk0 + j + 1) * sub, last8), 0)),
                  _resident((3, n))],
        out_specs=[ospec, ospec, ospec],
        out_shape=[out, out, out],
        compiler_params=_cparams(("parallel", "parallel")),
        name="hy_prep",
    )(u, u, u, conv_w)


def _hy_filt_kernel(z_ref, w1_ref, b1_ref, fr_ref, w2_ref, b2_ref, w3_ref, ld_ref, o_ref):
    hi = lax.Precision.HIGHEST
    z = z_ref[...]
    fr = fr_ref[...]
    hid = jnp.sin(fr * (jnp.dot(z, w1_ref[...], preferred_element_type=F32, precision=hi) + b1_ref[...]))
    hid = jnp.sin(fr * (jnp.dot(hid, w2_ref[...], preferred_element_type=F32, precision=hi) + b2_ref[...]))
    h = jnp.dot(hid, w3_ref[...], preferred_element_type=F32, precision=hi)
    t = z[:, 0:1]
    o_ref[...] = h * jnp.exp(-t * jnp.exp(ld_ref[...]))


def _pad2(a, rows, cols):
    return jnp.pad(a, ((0, rows - a.shape[0]), (0, cols - a.shape[1])))


def _hy_filters(length, w1, b1, freq, w2, b2, w3, log_decay):
    pos = jnp.arange(length, dtype=F32)
    t = pos / max(length - 1, 1)
    w = 2.0 * math.pi * pos / length
    f = jnp.linspace(1e-4, HY_BANDS - 1, HY_BANDS, dtype=F32)
    ang = w[:, None] * f[None, :]
    z = jnp.concatenate([t[:, None], jnp.cos(ang), -jnp.sin(ang)], -1)
    z = _pad2(z, length, LANES)
    n = w3.shape[1]
    rb = min(length, 512)
    vec = lambda a: _pad2(a.reshape(1, -1), 1, LANES)
    return pl.pallas_call(
        _hy_filt_kernel,
        grid=(length // rb,),
        in_specs=[pl.BlockSpec((rb, LANES), lambda i: (i, 0)),
                  _resident((LANES, LANES)), _resident((1, LANES)), _resident((1, LANES)),
                  _resident((LANES, LANES)), _resident((1, LANES)),
                  _resident((LANES, n)), _resident((1, n))],
        out_specs=pl.BlockSpec((rb, n), lambda i: (i, 0)),
        out_shape=jax.ShapeDtypeStruct((length, n), F32),
        compiler_params=_cparams(("parallel",)),
        name="hy_filters",
    )(z, _pad2(w1, LANES, LANES), vec(b1), vec(freq), _pad2(w2, LANES, LANES), vec(b2),
      _pad2(w3, LANES, n), log_decay.reshape(1, n))


def _bidir_taps(h, skip, length):
    h4 = h.reshape(length, 2, 2, HY_CH)
    cols = []
    for o in range(2):
        hf = h4[:, o, 0].at[0].add(skip[o])
        hb = h4[:, o, 1]
        cols.append(jnp.concatenate([hf, jnp.zeros_like(hf[:1]), hb[:0:-1]], 0))
    return jnp.concatenate(cols, -1)


def _left_mm_kernel(m_ref, x_ref, o_ref):
    o_ref[0] = _dot(m_ref[...], x_ref[0]).astype(o_ref.dtype)


def _left_mm(mat, x, out_dtype, lane_blk):
    g, k, n = x.shape
    m = mat.shape[0]
    lane_blk = min(lane_blk, n)
    return pl.pallas_call(
        _left_mm_kernel, grid=(g, n // lane_blk),
        in_specs=[_resident((m, k)), pl.BlockSpec((1, k, lane_blk), lambda i, j: (i, 0, j))],
        out_specs=pl.BlockSpec((1, m, lane_blk), lambda i, j: (i, 0, j)),
        out_shape=jax.ShapeDtypeStruct((g, m, n), out_dtype),
        compiler_params=_cparams(("parallel", "parallel")),
        name="left_mm",
    )(mat, x)


def _cs(num, den, rows, cols):
    ang = 2.0 * np.pi * np.outer(np.arange(rows), np.arange(cols)) * (num / den)
    return np.cos(ang), np.sin(ang)


def _twiddle(n1, n2):
    k1 = lax.broadcasted_iota(jnp.int32, (n1, n2, 1), 0)
    m2 = lax.broadcasted_iota(jnp.int32, (n1, n2, 1), 1)
    ang = (k1 * m2).astype(F32) * (2.0 * math.pi / (n1 * n2))
    return jnp.cos(ang), jnp.sin(ang)


def _hy_filt_t_kernel(z_ref, msk_ref, w1_ref, b1_ref, fr_ref, w2_ref, b2_ref, w3_ref, ld_ref, sk_ref, o_ref):
    hi = lax.Precision.HIGHEST
    z = z_ref[...]
    fr = fr_ref[...]
    hid = jnp.sin(fr * (jnp.dot(w1_ref[...], z, preferred_element_type=F32, precision=hi) + b1_ref[...]))
    hid = jnp.sin(fr * (jnp.dot(w2_ref[...], hid, preferred_element_type=F32, precision=hi) + b2_ref[...]))
    h = jnp.dot(w3_ref[0], hid, preferred_element_type=F32, precision=hi)
    h = h * jnp.exp(-jnp.exp(ld_ref[0]) * z[0:1, :])
    msk = msk_ref[...]
    h = h * msk[0:1, :] + sk_ref[...] * msk[1:2, :]
    for s in range(o_ref.shape[1]):
        o_ref[:, s, :] = h[:, s * FFT_N2:(s + 1) * FFT_N2]


def _hy_filters_t(length, w1, b1, freq, w2, b2, w3, log_decay, skip):
    n = 2 * length
    tt = jnp.arange(n, dtype=jnp.int32)
    pos = jnp.where(tt < length, tt, n - tt).astype(F32)
    t = pos / max(length - 1, 1)
    w = 2.0 * math.pi * pos / length
    f = jnp.linspace(1e-4, HY_BANDS - 1, HY_BANDS, dtype=F32)
    ang = f[:, None] * w[None, :]
    z = jnp.concatenate([t[None, :], jnp.cos(ang), -jnp.sin(ang)], 0)
    z = jnp.pad(z, ((0, LANES - z.shape[0]), (0, 0)))
    msk = jnp.stack([(tt != length).astype(F32), (tt == 0).astype(F32)])
    msk = jnp.pad(msk, ((0, 6), (0, 0)))
    col = lambda a: a.reshape(-1, 1)
    c2 = 2 * HY_CH
    nf = w3.shape[0]
    w3d = w3.reshape(nf, 2, 2, HY_CH).transpose(2, 1, 3, 0).reshape(2, c2, nf)
    ldd = log_decay.reshape(2, 2, HY_CH).transpose(1, 0, 2).reshape(2, c2, 1)
    rows = 8
    pb = rows * FFT_N2
    half = length // pb
    return pl.pallas_call(
        _hy_filt_t_kernel,
        grid=(n // pb,),
        in_specs=[pl.BlockSpec((LANES, pb), lambda i: (0, i)),
                  pl.BlockSpec((8, pb), lambda i: (0, i)),
                  _resident((nf, LANES)), _resident((nf, 1)), _resident((nf, 1)),
                  _resident((nf, nf)), _resident((nf, 1)),
                  pl.BlockSpec((1, c2, nf), lambda i: (i // half, 0, 0)),
                  pl.BlockSpec((1, c2, 1), lambda i: (i // half, 0, 0)),
                  _resident((c2, 1))],
        out_specs=pl.BlockSpec((c2, rows, FFT_N2), lambda i: (0, i, 0)),
        out_shape=jax.ShapeDtypeStruct((c2, n // FFT_N2, FFT_N2), F32),
        compiler_params=_cparams(("parallel",)),
        name="hy_filters_t",
    )(z, msk, _pad2(w1.T, nf, LANES), col(b1), col(freq), w2.T, col(b2), w3d, ldd, skip.reshape(c2, 1))


def _dft_mats():
    n = FFT_N2
    c, s = _cs(1, n, n, n)
    ch, sh = c[:, :n // 2], s[:, :n // 2]
    f1_pair = np.block([[ch, sh], [-sh, ch]])
    f1_full = np.concatenate([c, -s], 0)
    m2r = np.block([[c, -s], [s, c]])
    m2i = np.block([[c, s], [-s, c]])
    f1_inv = np.block([[ch.T, -sh.T], [sh.T, ch.T]]) / (n * n)
    cast = lambda a: jnp.asarray(a, dtype=BF16)
    return cast(f1_pair), cast(f1_full), cast(m2r), cast(m2i), cast(f1_inv)


def _twiddle2d(n):
    k1 = lax.broadcasted_iota(jnp.int32, (n, n), 0)
    m2 = lax.broadcasted_iota(jnp.int32, (n, n), 1)
    ang = (k1 * m2).astype(F32) * (2.0 * math.pi / (n * n))
    return jnp.cos(ang), jnp.sin(ang)


def _fwd_spectrum(xs, f1, m2r, c, s):
    n = FFT_N2
    a = _dot(f1, jnp.concatenate(xs, axis=1))
    ts = []
    for g in range(len(xs)):
        ar = a[:n, g * n:(g + 1) * n]
        ai = a[n:, g * n:(g + 1) * n]
        ts.append(jnp.concatenate([ar * c + ai * s, ai * c - ar * s], axis=1))
    return _dot(jnp.concatenate(ts, axis=0).astype(BF16), m2r)


def _filter_spec_t_kernel(t_ref, f1_ref, m2r_ref, c_ref, s_ref, o_ref):
    g = t_ref.shape[0]
    xs = [t_ref[i].astype(BF16) for i in range(g)]
    spec = _fwd_spectrum(xs, f1_ref[...], m2r_ref[...], c_ref[...], s_ref[...])
    for i in range(g):
        o_ref[i] = spec[i * FFT_N2:(i + 1) * FFT_N2]


def _filter_spec_t(taps, f1_full, m2r, twc, tws):
    nch, n1, n = taps.shape
    g = HY_GROUP
    return pl.pallas_call(
        _filter_spec_t_kernel,
        grid=(nch // g,),
        in_specs=[pl.BlockSpec((g, n1, n), lambda i: (i, 0, 0)),
                  _resident(f1_full.shape), _resident(m2r.shape), _resident((n, n)), _resident((n, n))],
        out_specs=pl.BlockSpec((g, n, 2 * n), lambda i: (i, 0, 0)),
        out_shape=jax.ShapeDtypeStruct((nch, n, 2 * n), F32),
        compiler_params=_cparams(("parallel",)),
        name="filter_spec_t",
    )(taps, f1_full, m2r, twc, tws)


def _hyena_core_kernel(x1_ref, x2_ref, v_ref, w1_ref, w2_ref, wv_ref, h0_ref, h1_ref,
                       f1_ref, m2r_ref, m2i_ref, f1i_ref, c_ref, s_ref, o_ref):
    n = FFT_N2
    n1 = v_ref.shape[2]
    c = c_ref[...]
    s = s_ref[...]
    lane = lax.broadcasted_iota(jnp.int32, (n1, n), 1)
    row = lax.broadcasted_iota(jnp.int32, (n1, n), 0)
    first_lane, last_lane = lane == 0, lane == n - 1
    seq_start, seq_end = first_lane & (row == 0), last_lane & (row == n1 - 1)

    def short_conv(x, w):
        r = pltpu.roll(x, 1, axis=1)
        up = jnp.where(first_lane, pltpu.roll(r, 1, axis=0), r)
        up = jnp.where(seq_start, 0.0, up)
        l = pltpu.roll(x, n - 1, axis=1)
        dn = jnp.where(last_lane, pltpu.roll(l, n1 - 1, axis=0), l)
        dn = jnp.where(seq_end, 0.0, dn)
        return up * w[0:1] + x * w[1:2] + dn * w[2:3]

    def conv_all(xss, h_ref):
        spec_s = [_fwd_spectrum(xs, f1_ref[...], m2r_ref[...], c, s) for xs in xss]
        y_s = []
        for chain, spec in enumerate(spec_s):
            ys = []
            for g in range(HY_GROUP):
                xr = spec[g * n:(g + 1) * n, :n]
                xi = spec[g * n:(g + 1) * n, n:]
                hh = h_ref[chain * HY_GROUP + g]
                hr, hi = hh[:, :n], hh[:, n:]
                ys.append(jnp.concatenate([xr * hr - xi * hi, xr * hi + xi * hr], axis=1))
            y_s.append(jnp.concatenate(ys, axis=0).astype(BF16))
        bm_s = [_dot(y, m2i_ref[...]) for y in y_s]
        b_s = []
        for bm in bm_s:
            bs = []
            for g in range(HY_GROUP):
                br = bm[g * n:(g + 1) * n, :n]
                bi = bm[g * n:(g + 1) * n, n:]
                bs.append(jnp.concatenate([br * c - bi * s, bi * c + br * s], axis=0))
            b_s.append(jnp.concatenate(bs, axis=1).astype(BF16))
        outs = [_dot(f1i_ref[...], bc) for bc in b_s]
        return [[y[:, g * n:(g + 1) * n] for g in range(HY_GROUP)] for y in outs]

    def pair(ref, w_ref, ch):
        return jnp.concatenate([short_conv(ref[0, ch], w_ref[ch]), short_conv(ref[1, ch], w_ref[ch])], axis=0)

    chans = [[chain * HY_GROUP + g for g in range(HY_GROUP)] for chain in range(HY_CHAINS)]
    y1 = conv_all([[pair(v_ref, wv_ref, ch).astype(BF16) for ch in grp] for grp in chans], h0_ref)
    z = [[(y1[k][g] * pair(x1_ref, w1_ref, ch)).astype(BF16) for g, ch in enumerate(grp)]
         for k, grp in enumerate(chans)]
    y2 = conv_all(z, h1_ref)
    for k, grp in enumerate(chans):
        for g, ch in enumerate(grp):
            y = y2[k][g] * pair(x2_ref, w2_ref, ch)
            o_ref[0, :, ch, :] = y[:n1]
            o_ref[1, :, ch, :] = y[n1:]


def _hyena_core(ut, conv_w, spec, mats, twc, tws):
    f1_pair, _, m2r, m2i, f1_inv = mats
    b, nch, n1, n = ut.shape
    assert b % 2 == 0
    cb = HY_CBLK
    nblk = HY_CH // cb
    wt = jnp.broadcast_to(conv_w.T[:, :, None], (nch, conv_w.shape[0], n))
    xspec = lambda off: pl.BlockSpec((2, cb, n1, n), lambda j, i: (i, off * nblk + j, 0, 0))
    wspec = lambda off: pl.BlockSpec((cb, conv_w.shape[0], n), lambda j, i: (off * nblk + j, 0, 0))
    hspec = lambda off: pl.BlockSpec((cb, n, 2 * n), lambda j, i: (off * nblk + j, 0, 0))
    return pl.pallas_call(
        _hyena_core_kernel,
        grid=(nblk, b // 2),
        in_specs=[xspec(0), xspec(1), xspec(2), wspec(0), wspec(1), wspec(2), hspec(0), hspec(1),
                  _resident(f1_pair.shape), _resident(m2r.shape), _resident(m2i.shape), _resident(f1_inv.shape),
                  _resident((n, n)), _resident((n, n))],
        out_specs=pl.BlockSpec((2, n1, cb, n), lambda j, i: (i, 0, j, 0)),
        out_shape=jax.ShapeDtypeStruct((b, n1, HY_CH, n), F32),
        compiler_params=_cparams(("parallel", "parallel")),
        name="hyena_core",
    )(ut, ut, ut, wt, wt, wt, spec, spec, f1_pair, m2r, m2i, f1_inv, twc, tws)


def _hy_ctx_kernel(v_ref, x1_ref, x2_ref, f_ref, fi_ref, h_ref, o_ref):
    nf = f_ref.shape[0] // 2
    zin = v_ref[0]
    gates = (x1_ref, x2_ref)
    for o in range(2):
        x = _dot(f_ref[...], zin)
        xr, xi = x[:nf], x[nf:]
        hr = h_ref[o, :nf]
        hi = h_ref[o, nf:]
        y = jnp.concatenate([xr * hr - xi * hi, xr * hi + xi * hr], axis=0).astype(BF16)
        zin = (_dot(fi_ref[...], y) * gates[o][0].astype(F32)).astype(BF16)
    o_ref[0] = zin


def _hy_ctx(v, x1, x2, taps):
    b, lc, c = v.shape
    nf = 2 * lc
    cm, sm = _cs(1, nf, nf, nf)
    fwd = jnp.asarray(np.concatenate([cm[:, :lc], -sm[:, :lc]], 0), dtype=BF16)
    fwd_full = jnp.asarray(np.concatenate([cm, -sm], 0), dtype=BF16)
    inv = jnp.asarray(np.concatenate([cm[:lc, :], -sm[:lc, :]], 1) / nf, dtype=BF16)
    spec = _left_mm(fwd_full, taps.astype(BF16).reshape(1, nf, 2 * c), F32, 2 * c)
    spec = spec.reshape(2 * nf, 2, c).transpose(1, 0, 2)
    blk = pl.BlockSpec((1, lc, c), lambda i: (i, 0, 0))
    return pl.pallas_call(
        _hy_ctx_kernel,
        grid=(b,),
        in_specs=[blk, blk, blk, _resident((2 * nf, lc)), _resident((lc, 2 * nf)),
                  _resident((2, 2 * nf, c))],
        out_specs=blk,
        out_shape=jax.ShapeDtypeStruct((b, lc, c), BF16),
        compiler_params=_cparams(("parallel",)),
        name="hy_ctx",
    )(v, x1, x2, fwd, inv, spec)


def _pair_rows(q2):
    lane = lax.broadcasted_iota(jnp.int32, q2.shape, 1)
    zero = jnp.zeros_like(q2)
    return jnp.concatenate([jnp.where(lane < NA_HD, q2, zero), jnp.where(lane >= NA_HD, q2, zero)], axis=0)


def _unpair_rows(o):
    r = o.shape[0] // 2
    lane = lax.broadcasted_iota(jnp.int32, (r, o.shape[1]), 1)
    return jnp.where(lane < NA_HD, o[:r], o[r:])


def _pair_softmax_pv(scores, values):
    m = scores[0].max(axis=-1, keepdims=True)
    for s in scores[1:]:
        m = jnp.maximum(m, s.max(axis=-1, keepdims=True))
    den = None
    acc = None
    for s, v in zip(scores, values):
        p = jnp.exp(s - m)
        d = p.sum(axis=-1, keepdims=True)
        a = _dot(p.astype(BF16), v)
        den = d if den is None else den + d
        acc = a if acc is None else acc + a
    return acc / den


def _natten_kernel(q_ref, k0, k1, k2, k3, v0, v1, v2, v3, kc_ref, vc_ref, bias_ref, o_ref,
                   kwin, vwin, *, rows):
    g = pl.program_id(1)
    rb = 4 * GRID_W
    for i, (kr, vr) in enumerate(((k0, v0), (k1, v1), (k2, v2), (k3, v3))):
        kwin[i * rb:(i + 1) * rb, :] = kr[0]
        vwin[i * rb:(i + 1) * rb, :] = vr[0]
    base = 4 * jnp.clip(2 * g - 1, 0, rows // 4 - 4)
    nwin = NA_WIN_R * GRID_W
    qscale = jnp.asarray(NA_HD ** -0.5, BF16)

    ones_lat = jnp.ones((nwin, LANES), BF16)
    ones_ctx = jnp.ones((kc_ref.shape[1], LANES), BF16)

    def rows_body(it, carry):
        work = []
        for u in range(NA_ROWS_PER_TRIP):
            rr = it * NA_ROWS_PER_TRIP + u
            r = 8 * g + rr
            rs = jnp.clip(r - NA_WIN_R // 2, 0, rows - NA_WIN_R)
            st = pl.multiple_of((rs - base) * GRID_W, GRID_W)
            qo = pl.multiple_of(rr * GRID_W, GRID_W)
            work += [(qo, st, rs - r + NA_WIN_R - 1, p) for p in range(NA_HEADS // 2)]
        scores = []
        for qo, st, d0, p in work:
            ls = slice(p * LANES, (p + 1) * LANES)
            qp = _pair_rows(q_ref[0, pl.ds(qo, GRID_W), ls] * qscale)
            scores.append((_dot_nt(qp, kwin[pl.ds(st, nwin), ls]) + bias_ref[d0, p].astype(F32),
                           _dot_nt(qp, kc_ref[0, :, ls])))
        maxima = [jnp.maximum(a.max(axis=-1, keepdims=True), b.max(axis=-1, keepdims=True)) for a, b in scores]
        for (qo, st, d0, p), (s_lat, s_ctx), m in zip(work, scores, maxima):
            ls = slice(p * LANES, (p + 1) * LANES)
            v_lat = jnp.concatenate([vwin[pl.ds(st, nwin), ls], ones_lat], axis=1)
            v_ctx = jnp.concatenate([vc_ref[0, :, ls], ones_ctx], axis=1)
            acc = _dot(jnp.exp(s_lat - m).astype(BF16), v_lat) + _dot(jnp.exp(s_ctx - m).astype(BF16), v_ctx)
            o = acc[:, :LANES] / acc[:, LANES:LANES + 1]
            o_ref[0, pl.ds(qo, GRID_W), ls] = _unpair_rows(o).astype(o_ref.dtype)
        return carry

    lax.fori_loop(0, 8 // NA_ROWS_PER_TRIP, rows_body, 0)


def _natten_bias(rpb):
    c = np.arange(GRID_W)[:, None]
    kc = np.arange(GRID_W)[None, :]
    cs = np.clip(c - NA_WIN_C // 2, 0, GRID_W - NA_WIN_C)
    valid = (kc >= cs) & (kc < cs + NA_WIN_C)
    dc = np.clip(kc - c + NA_WIN_C - 1, 0, 2 * NA_WIN_C - 2)
    tb = jnp.where(valid[None, None], rpb[:, :, dc], NEG_BIG)
    slabs = []
    for d0 in range(NA_WIN_R):
        s = tb[:, d0:d0 + NA_WIN_R]
        s = s.transpose(0, 2, 1, 3).reshape(NA_HEADS, GRID_W, NA_WIN_R * GRID_W)
        slabs.append(s.reshape(NA_HEADS // 2, 2 * GRID_W, NA_WIN_R * GRID_W))
    return jnp.stack(slabs).astype(BF16)


def _natten(uq, uc, bias):
    b, length, _ = uq.shape
    c = NA_HEADS * NA_HD
    rows = length // GRID_W
    rb = 4 * GRID_W
    nkb = length // rb
    qrows = 8 * GRID_W
    lc = uc.shape[1]

    def kv_spec(col, off):
        return pl.BlockSpec((1, rb, c), lambda i, g: (i, jnp.clip(2 * g - 1, 0, nkb - 4) + off, col))

    return pl.pallas_call(
        functools.partial(_natten_kernel, rows=rows),
        grid=(b, rows // 8),
        in_specs=[pl.BlockSpec((1, qrows, c), lambda i, g: (i, g, 0))]
                 + [kv_spec(1, o) for o in range(4)] + [kv_spec(2, o) for o in range(4)]
                 + [pl.BlockSpec((1, lc, c), lambda i, g: (i, 0, 4)),
                    pl.BlockSpec((1, lc, c), lambda i, g: (i, 0, 5)),
                    _resident(bias.shape)],
        out_specs=pl.BlockSpec((1, qrows, c), lambda i, g: (i, g, 0)),
        out_shape=jax.ShapeDtypeStruct((b, length, c), BF16),
        scratch_shapes=[pltpu.VMEM((4 * rb, c), BF16), pltpu.VMEM((4 * rb, c), BF16)],
        compiler_params=_cparams(("parallel", "parallel")),
        name="natten",
    )(uq, *([uq] * 8), uc, uc, bias)


def _ctx_attn_kernel(q_ref, k_ref, v_ref, o_ref):
    qscale = jnp.asarray(NA_HD ** -0.5, BF16)
    for p in range(NA_HEADS // 2):
        ls = slice(p * LANES, (p + 1) * LANES)
        qp = _pair_rows(q_ref[0, :, ls] * qscale)
        o = _pair_softmax_pv([_dot_nt(qp, k_ref[0, :, ls])], [v_ref[0, :, ls]])
        o_ref[0, :, ls] = _unpair_rows(o).astype(o_ref.dtype)


def _ctx_attn(u):
    b, lc, _ = u.shape
    c = NA_HEADS * NA_HD
    spec = lambda col: pl.BlockSpec((1, lc, c), lambda i: (i, 0, col))
    return pl.pallas_call(
        _ctx_attn_kernel,
        grid=(b,),
        in_specs=[spec(3), spec(4), spec(5)],
        out_specs=pl.BlockSpec((1, lc, c), lambda i: (i, 0, 0)),
        out_shape=jax.ShapeDtypeStruct((b, lc, c), BF16),
        compiler_params=_cparams(("parallel",)),
        name="ctx_attn",
    )(u, u, u)


def _post_kernel(x_ref, mod_ref, ya_ref, yb_ref, wa_ref, wb_ref, lng_ref, lnb_ref, w1_ref, w2_ref, o_ref, *, ya_t):
    m = mod_ref[0, 0]
    lng = lng_ref[...]
    lnb = lnb_ref[...]
    ff = w1_ref.shape[1]
    step = 1024
    sub = ROW_BLK
    nsub = x_ref.shape[1] // sub

    def head(t):
        rs = slice(t * sub, (t + 1) * sub)
        if ya_t:
            tiles = range(t * sub // LANES, (t + 1) * sub // LANES)
            ya = jnp.concatenate([ya_ref[0, s].T for s in tiles], axis=0).astype(BF16)
        else:
            ya = ya_ref[0, rs, :]
        y = _dot(ya, wa_ref[...]) + _dot(yb_ref[0, rs, :], wb_ref[...])
        x1 = _norm_rows(ALPHA * x_ref[0, rs, :] + m[2:3] * y) * lng[0:1] + lnb[0:1]
        return x1, (_norm_rows(x1) * (1.0 + m[4:5]) + m[3:4]).astype(BF16)

    def mlp(h):
        acc = None
        for c in range(ff // step):
            a = jnp.maximum(_dot(h, w1_ref[:, c * step:(c + 1) * step]), 0.0)
            d = _dot((a * a).astype(BF16), w2_ref[c * step:(c + 1) * step, :])
            acc = d if acc is None else acc + d
        return acc

    cur = head(0)
    for t in range(nsub):
        nxt = head(t + 1) if t + 1 < nsub else None
        acc = mlp(cur[1])
        o_ref[0, t * sub:(t + 1) * sub, :] = _norm_rows(ALPHA * cur[0] + m[5:6] * acc) * lng[1:2] + lnb[1:2]
        cur = nxt


def _post(x, modtab, mod_row, ya, yb, wa, wb, lng, lnb, w1, w2, ya_t=False):
    b, r, d = x.shape
    ka, kb = wa.shape[0], wb.shape[0]
    rows = min(r, 2 * ROW_BLK)
    row = lambda k: pl.BlockSpec((1, rows, k), lambda i, j: (i, j, 0))
    ya_spec = pl.BlockSpec((1, rows // LANES, ka, LANES), lambda i, j: (i, j, 0, 0)) if ya_t else row(ka)
    return pl.pallas_call(
        functools.partial(_post_kernel, ya_t=ya_t),
        grid=(b, r // rows),
        in_specs=[row(d), pl.BlockSpec((1, 1, 6, d), lambda i, j: (i, mod_row, 0, 0)), ya_spec, row(kb),
                  _resident(wa.shape), _resident(wb.shape), _resident(lng.shape), _resident(lnb.shape),
                  _resident(w1.shape), _resident(w2.shape)],
        out_specs=row(d),
        out_shape=jax.ShapeDtypeStruct((b, r, d), F32),
        compiler_params=_cparams(("parallel", "parallel")),
        name="post_mixer",
    )(x, modtab, ya, yb, wa, wb, lng, lnb, w1, w2)


def _rope(x, cos, sinl, sinr):
    reps = x.shape[1] // LANES
    tile = lambda t: jnp.concatenate([t] * reps, axis=1)
    n = x.shape[1]
    quarter = MLA_ROPE // 4
    return (x * tile(cos) + pltpu.roll(x, n - quarter, axis=1) * tile(sinl)
            + pltpu.roll(x, quarter, axis=1) * tile(sinr))


def _front_cd_kernel(x_ref, xc_ref, mod_ref, w_ref, qn_ref, kvn_ref, wuq_ref, wuk_ref, wuv_ref, epe_ref, one_ref,
                     fng_ref, fnb_ref, cbd_ref, sbd_ref,
                     cq_ref, slq_ref, srq_ref, ck_ref, slk_ref, srk_ref,
                     q_ref, k_ref, v_ref, p_ref, qf_ref, *, ctx_blk):
    m = mod_ref[0, 0]
    is_ctx = (jnp.zeros((ROW_BLK, 1), jnp.int32) + pl.program_id(1)) == ctx_blk
    x = jnp.where(is_ctx, xc_ref[0], x_ref[0])
    h = _norm_rows(x) * (1.0 + m[1:2]) + m[0:1]
    u = _dot(h.astype(BF16), w_ref[...])
    o_kv = MLA_Q_RANK
    o_fn = o_kv + MLA_KV_RANK
    o_pe = o_fn + FN_CH

    def rms(x, g):
        return x * lax.rsqrt(jnp.mean(x * x, axis=-1, keepdims=True) + LN_EPS) * g

    cq = rms(u[:, :o_kv], qn_ref[...]).astype(BF16)
    q = _dot(cq, wuq_ref[...])
    q_ref[0] = _rope(q, cq_ref[...], slq_ref[...], srq_ref[...]).astype(BF16)

    ckv = rms(u[:, o_kv:o_fn], kvn_ref[...]).astype(BF16)
    kpe = _dot(u[:, o_pe:].astype(BF16), epe_ref[...])
    k = _dot(ckv, wuk_ref[...]) + _rope(kpe, ck_ref[...], slk_ref[...], srk_ref[...])
    k_ref[0] = k.astype(BF16)
    v_ref[0] = (_dot(ckv, wuv_ref[...]) + one_ref[...]).astype(BF16)

    uf = u[:, o_fn:o_pe]
    lane = lax.broadcasted_iota(jnp.int32, uf.shape, 1)
    mean = jnp.zeros_like(uf)
    for g in range(FN_GROUPS):
        sel = (lane >= g * FN_GD) & (lane < (g + 1) * FN_GD)
        mg = jnp.sum(jnp.where(sel, uf, 0.0), axis=-1, keepdims=True) * (1.0 / FN_GD)
        mean = jnp.where(sel, mg, mean)
    uc = uf - mean
    var = jnp.zeros_like(uf)
    for g in range(FN_GROUPS):
        sel = (lane >= g * FN_GD) & (lane < (g + 1) * FN_GD)
        vg = jnp.sum(jnp.where(sel, uc * uc, 0.0), axis=-1, keepdims=True) * (1.0 / FN_GD)
        var = jnp.where(sel, vg, var)
    ug = (uc * lax.rsqrt(var + LN_EPS) * fng_ref[...] + fnb_ref[...]).astype(BF16)
    p_ref[0] = _dot(ug, cbd_ref[...]).astype(BF16)
    qf_ref[0] = _dot(ug, sbd_ref[...]).astype(BF16)


def _rope_tables(length, lt, scale):
    t = jnp.arange(lt, dtype=jnp.int32)
    rows = (t // GRID_W).astype(F32)
    cols = (t % GRID_W).astype(F32)
    half = MLA_ROPE // 2
    inv = ROPE_THETA ** (-jnp.arange(0, half, 2, dtype=F32) / half)
    ar = rows[:, None] * inv[None, :]
    ac = cols[:, None] * inv[None, :]
    ang = jnp.concatenate([ar, ar, ac, ac], -1)
    is_lat = (t < length)[:, None]
    cos = jnp.where(is_lat, jnp.cos(ang), 1.0)
    sin = jnp.where(is_lat, jnp.sin(ang), 0.0)
    qd = MLA_ROPE // 4
    ones = jnp.ones((lt, MLA_NOPE), F32)
    zeros = jnp.zeros((lt, MLA_NOPE), F32)
    tail1 = jnp.ones((lt, HEAD_PAD - MLA_NOPE - MLA_ROPE), F32)
    tail0 = jnp.zeros((lt, HEAD_PAD - MLA_NOPE - MLA_ROPE), F32)
    z8 = jnp.zeros((lt, qd), F32)
    c = jnp.concatenate([ones, cos, tail1], -1)
    sl = jnp.concatenate([zeros, -sin[:, :qd], z8, -sin[:, 2 * qd:3 * qd], z8, tail0], -1)
    sr = jnp.concatenate([zeros, z8, sin[:, qd:2 * qd], z8, sin[:, 3 * qd:], tail0], -1)
    return c * scale, sl * scale, sr * scale


def _head_slots(w, per_head, take_from, take_n):
    k = w.shape[0]
    w3 = w.reshape(k, MLA_HEADS, per_head)[:, :, take_from:take_from + take_n]
    w3 = jnp.pad(w3, ((0, 0), (0, 0), (0, HEAD_PAD - take_n)))
    return w3.reshape(k, MLA_HEADS * HEAD_PAD)


def _front_cd(xl, xc, modtab, w_in, q_norm, w_uq, kv_norm, w_ukv, fn_g, fn_b):
    b, length, d = xl.shape
    lt = length + xc.shape[1]
    nlat = length // ROW_BLK
    o_kv = MLA_Q_RANK
    o_pe = o_kv + MLA_KV_RANK
    o_fn = o_pe + MLA_ROPE
    hw = MLA_HEADS * HEAD_PAD
    w_perm = jnp.concatenate([w_in[:, :o_pe], w_in[:, o_fn:], w_in[:, o_pe:o_fn],
                              jnp.zeros((d, LANES - MLA_ROPE), w_in.dtype)], -1).astype(BF16)
    wuq = _head_slots(w_uq, MLA_NOPE + MLA_ROPE, 0, MLA_NOPE + MLA_ROPE).astype(BF16)
    wuk = _head_slots(w_ukv, MLA_NOPE + MLA_V, 0, MLA_NOPE).astype(BF16)
    wuv = _head_slots(w_ukv, MLA_NOPE + MLA_V, MLA_NOPE, MLA_V).astype(BF16)
    epe = np.zeros((LANES, hw), np.float32)
    for hd in range(MLA_HEADS):
        for i in range(MLA_ROPE):
            epe[i, hd * HEAD_PAD + MLA_NOPE + i] = 1.0
    epe = jnp.asarray(epe, dtype=BF16)
    ones_col = np.zeros((1, hw), np.float32)
    ones_col[0, MLA_V::HEAD_PAD] = 1.0
    ones_col = jnp.asarray(ones_col)
    cm, sm = _cs(1, FN_GD, FN_GD, FN_GD)
    eye = np.eye(FN_GROUPS)
    cbd = jnp.asarray(np.kron(eye, cm), dtype=BF16)
    sbd = jnp.asarray(np.kron(eye, -sm), dtype=BF16)
    qtab = _rope_tables(length, lt, (MLA_NOPE + MLA_ROPE) ** -0.5 * math.log2(math.e))
    ktab = _rope_tables(length, lt, 1.0)
    row = lambda n: pl.BlockSpec((1, ROW_BLK, n), lambda i, j: (i, j, 0))
    tab = pl.BlockSpec((ROW_BLK, HEAD_PAD), lambda i, j: (j, 0))
    out = lambda n: jax.ShapeDtypeStruct((b, lt, n), BF16)
    return pl.pallas_call(
        functools.partial(_front_cd_kernel, ctx_blk=nlat),
        grid=(b, lt // ROW_BLK),
        in_specs=[pl.BlockSpec((1, ROW_BLK, d), lambda i, j: (i, jnp.minimum(j, nlat - 1), 0)),
                  pl.BlockSpec((1, ROW_BLK, d), lambda i, j: (i, 0, 0)),
                  pl.BlockSpec((1, 1, 6, d), lambda i, j: (i, j // nlat, 0, 0)),
                  _resident(w_perm.shape), _resident((1, MLA_Q_RANK)), _resident((1, MLA_KV_RANK)),
                  _resident(wuq.shape), _resident(wuk.shape), _resident(wuv.shape), _resident(epe.shape),
                  _resident(ones_col.shape),
                  _resident((1, FN_CH)), _resident((1, FN_CH)), _resident(cbd.shape), _resident(sbd.shape),
                  tab, tab, tab, tab, tab, tab],
        out_specs=[row(hw), row(hw), row(hw), row(FN_CH), row(FN_CH)],
        out_shape=[out(hw), out(hw), out(hw), out(FN_CH), out(FN_CH)],
        compiler_params=_cparams(("parallel", "parallel")),
        name="front_cd",
    )(xl, xc, modtab, w_perm, q_norm.reshape(1, -1), kv_norm.reshape(1, -1), wuq, wuk, wuv, epe, ones_col,
      fn_g.reshape(1, -1), fn_b.reshape(1, -1), cbd, sbd, *qtab, *ktab)


def _mla_kernel(q_ref, k_ref, v_ref, o_ref, *, rows):
    n = q_ref.shape[1] // rows

    def scores(i):
        s = _dot_nt(q_ref[0, i * rows:(i + 1) * rows, :], k_ref[0])
        return s, s.max(axis=-1, keepdims=True)

    def finish(i, s, m):
        acc = _dot(jnp.exp2(s - m).astype(BF16), v_ref[0])
        o_ref[0, i * rows:(i + 1) * rows, :] = (acc / acc[:, MLA_V:MLA_V + 1]).astype(o_ref.dtype)

    pending = scores(0)
    for i in range(n):
        nxt = scores(i + 1) if i + 1 < n else None
        finish(i, *pending)
        pending = nxt


def _mla_attention(q, k, v, length):
    b, lt, hw = q.shape
    heads = hw // HEAD_PAD
    tq = 1024
    kv = pl.BlockSpec((1, lt, HEAD_PAD), lambda i, h, j: (i, 0, h))
    qs = pl.BlockSpec((1, tq, HEAD_PAD), lambda i, h, j: (i, j, h))
    return pl.pallas_call(
        functools.partial(_mla_kernel, rows=256),
        grid=(b, heads, length // tq),
        in_specs=[qs, kv, kv],
        out_specs=qs,
        out_shape=jax.ShapeDtypeStruct((b, length, hw), BF16),
        compiler_params=_cparams(("parallel", "parallel", "parallel")),
        name="mla_attention",
    )(q, k, v)


def _fn1_kernel(m_ref, zr_ref, zi_ref, o_ref):
    n = zr_ref.shape[1]
    a = _dot(m_ref[...], jnp.concatenate([zr_ref[0], zi_ref[0]], axis=0))
    o_ref[0, 0] = a[:n].astype(o_ref.dtype)
    o_ref[0, 1] = a[n:].astype(o_ref.dtype)


def _fn2_kernel(a_ref, c_ref, s_ref, m_ref, o_ref):
    for i in range(a_ref.shape[2]):
        ar = a_ref[0, 0, i].astype(F32)
        ai = a_ref[0, 1, i].astype(F32)
        c = c_ref[i]
        s = s_ref[i]
        t = jnp.concatenate([ar * c + ai * s, ai * c - ar * s], axis=0).astype(BF16)
        o_ref[0, i] = _dot(m_ref[...], t).astype(o_ref.dtype)


def _fnet(p, qn):
    b, length, c = p.shape
    n1 = 128
    n2 = length // n1
    cm, sm = _cs(1, n1, n1, n1)
    m1 = jnp.asarray(np.block([[cm, sm], [-sm, cm]]), dtype=BF16)
    c2, s2 = _cs(1, n2, n2, n2)
    m2 = jnp.asarray(np.concatenate([c2, s2], 1) / math.sqrt(length * FN_GD), dtype=BF16)
    twc, tws = _twiddle(n1, n2)
    lane_blk = 2048
    zs = pl.BlockSpec((1, n1, lane_blk), lambda i, j: (i, 0, j))
    a = pl.pallas_call(
        _fn1_kernel,
        grid=(b, n2 * c // lane_blk),
        in_specs=[_resident(m1.shape), zs, zs],
        out_specs=pl.BlockSpec((1, 2, n1, lane_blk), lambda i, j: (i, 0, 0, j)),
        out_shape=jax.ShapeDtypeStruct((b, 2, n1, n2 * c), BF16),
        compiler_params=_cparams(("parallel", "parallel")),
        name="fnet_stage1",
    )(m1, p.reshape(b, n1, n2 * c), qn.reshape(b, n1, n2 * c))
    kb = 8
    y = pl.pallas_call(
        _fn2_kernel,
        grid=(b, n1 // kb),
        in_specs=[pl.BlockSpec((1, 2, kb, n2, c), lambda i, j: (i, 0, j, 0, 0)),
                  pl.BlockSpec((kb, n2, 1), lambda i, j: (j, 0, 0)),
                  pl.BlockSpec((kb, n2, 1), lambda i, j: (j, 0, 0)),
                  _resident(m2.shape)],
        out_specs=pl.BlockSpec((1, kb, n2, c), lambda i, j: (i, j, 0, 0)),
        out_shape=jax.ShapeDtypeStruct((b, n1, n2, c), BF16),
        compiler_params=_cparams(("parallel", "parallel")),
        name="fnet_stage2",
    )(a.reshape(b, 2, n1, n2, c), twc, tws, m2)
    return y.transpose(0, 2, 1, 3).reshape(b, length, c)


def kernel(x, c, ctx, c_ctx, mod_w, mod_b, ln_g, ln_b, mlp_w1, mlp_w2,
           ab_w_in, ab_w_out, hy_conv_w, hy_w1, hy_b1, hy_freq, hy_w2, hy_b2, hy_w3, hy_log_decay, hy_skip, na_rpb,
           cd_w_in, cd_w_out, mla_q_norm, mla_w_uq, mla_kv_norm, mla_w_ukv, fn_norm_g, fn_norm_b):
    b, length, d = x.shape
    lc = ctx.shape[1]

    cc = jnp.concatenate([c, c_ctx[None], jnp.zeros((8 - b - 1, d), F32)], 0)
    mods = _mod_vectors(cc, mod_w, mod_b).reshape(DEPTH, 8, 6, d)
    modtab = [jnp.stack([mods[l, :b], jnp.broadcast_to(mods[l, b], (b, 6, d))], axis=1) for l in range(DEPTH)]

    n_hy = 3 * HY_CH
    w_in = ab_w_in[0].astype(BF16)
    uq, ut = _front_ab_lat(x, modtab[0], w_in[:, n_hy:], w_in[:, :n_hy].T)
    uc = _front_ab_ctx(ctx, modtab[0], w_in)
    fargs = (hy_w1[0], hy_b1[0], hy_freq[0], hy_w2[0], hy_b2[0], hy_w3[0], hy_log_decay[0])
    mats = _dft_mats()
    twc, tws = _twiddle2d(FFT_N2)
    spec = _filter_spec_t(_hy_filters_t(length, *fargs, hy_skip[0]), mats[1], mats[2], twc, tws)
    y_hy_t = _hyena_core(ut, hy_conv_w[0], spec, mats, twc, tws)
    x1c, x2c, vc = _hy_prep(uc, hy_conv_w[0], 0, lc // ROW_BLK)
    y_hy_c = _hy_ctx(vc, x1c, x2c, _bidir_taps(_hy_filters(lc, *fargs), hy_skip[0], lc))
    y_na = _natten(uq, uc, _natten_bias(na_rpb[0]))
    y_na_c = _ctx_attn(uc)
    w_out = ab_w_out[0].astype(BF16)
    mlp = (ln_g[0], ln_b[0], mlp_w1[0].astype(BF16), mlp_w2[0].astype(BF16))
    xl = _post(x, modtab[0], 0, y_hy_t, y_na, w_out[:HY_CH], w_out[HY_CH:], *mlp, ya_t=True)
    xc = _post(ctx, modtab[0], 1, y_hy_c, y_na_c, w_out[:HY_CH], w_out[HY_CH:], *mlp)

    q, k, vv, p, qn = _front_cd(xl, xc, modtab[1], cd_w_in[0], mla_q_norm[0], mla_w_uq[0], mla_kv_norm[0],
                                mla_w_ukv[0], fn_norm_g[0], fn_norm_b[0])
    o = _mla_attention(q, k, vv, length)
    y_fn = _fnet(p[:, :length], qn[:, :length])
    w_out = cd_w_out[0]
    n_mla = MLA_HEADS * MLA_V
    wa = jnp.pad(w_out[:n_mla].reshape(MLA_HEADS, MLA_V, d), ((0, 0), (0, HEAD_PAD - MLA_V), (0, 0)))
    wa = wa.reshape(MLA_HEADS * HEAD_PAD, d).astype(BF16)
    return _post(xl, modtab[1], 0, o, y_fn, wa, w_out[n_mla:].astype(BF16),
                 ln_g[1], ln_b[1], mlp_w1[1].astype(BF16), mlp_w2[1].astype(BF16))
```

```python
import functools
import math

import numpy as np
import jax
import jax.numpy as jnp
from jax import lax
from jax.experimental import pallas as pl
from jax.experimental.pallas import tpu as pltpu

F32 = jnp.float32
BF16 = jnp.bfloat16

D_MODEL = 1024
DEPTH = 2
GRID_W = 64
HY_CH = 512
HY_EMB = 33
HY_BANDS = (HY_EMB - 1) // 2
NA_HEADS = 8
NA_HD = 64
NA_WIN_R = 8
NA_WIN_C = 16
MLA_HEADS = 8
MLA_Q_RANK = 384
MLA_KV_RANK = 256
MLA_NOPE = 64
MLA_ROPE = 32
MLA_V = 96
ROPE_THETA = 10000.0
FN_CH = 256
FN_GROUPS = 4
FN_GD = FN_CH // FN_GROUPS
D_FF = 4 * D_MODEL
ALPHA = (2.0 * DEPTH) ** 0.25
LN_EPS = 1e-5

LANES = 128
ROW_BLK = 256
HEAD_PAD = 128
FFT_N2 = 128
NA_ROWS_PER_TRIP = 8
MLP_CHUNK = 1024
MLA_Q_BLK = 1024
MLA_Q_SUB = 256
HY_GROUP = 16
HY_CHAINS = 2
HY_CBLK = HY_GROUP * HY_CHAINS
VMEM_LIMIT = 56 * 1024 * 1024
NEG_BIG = -1e30


def _cparams(sem, vmem=VMEM_LIMIT):
    return pltpu.CompilerParams(dimension_semantics=sem, vmem_limit_bytes=vmem)


def _resident(shape):
    nd = len(shape)
    return pl.BlockSpec(shape, lambda *_: (0,) * nd, pipeline_mode=pl.Buffered(1))


def _norm_rows(x):
    mu = jnp.mean(x, axis=-1, keepdims=True)
    xc = x - mu
    var = jnp.mean(xc * xc, axis=-1, keepdims=True)
    return xc * lax.rsqrt(var + LN_EPS)


def _dot(a, b):
    return jnp.dot(a, b, preferred_element_type=F32)


def _dot_nt(a, b):
    return lax.dot_general(a, b, (((1,), (1,)), ((), ())), preferred_element_type=F32)


def _mod_kernel(c_ref, w_ref, b_ref, o_ref):
    c = c_ref[...]
    s = c * (1.0 / (1.0 + jnp.exp(-c)))
    o_ref[0] = jnp.dot(s, w_ref[0], preferred_element_type=F32,
                       precision=lax.Precision.HIGHEST) + b_ref[0]


def _mod_vectors(cc, mod_w, mod_b):
    depth, d, n = mod_w.shape
    nb = 1024
    return pl.pallas_call(
        _mod_kernel,
        grid=(depth, n // nb),
        in_specs=[pl.BlockSpec((8, d), lambda l, j: (0, 0)),
                  pl.BlockSpec((1, d, nb), lambda l, j: (l, 0, j)),
                  pl.BlockSpec((1, 1, nb), lambda l, j: (l, 0, j))],
        out_specs=pl.BlockSpec((1, 8, nb), lambda l, j: (l, 0, j)),
        out_shape=jax.ShapeDtypeStruct((depth, 8, n), F32),
        compiler_params=_cparams(("parallel", "parallel")),
        name="mod_vectors",
    )(cc, mod_w, mod_b.reshape(depth, 1, n))


def _front_ab_ctx_kernel(x_ref, mod_ref, w_ref, u_ref):
    m = mod_ref[0, 0]
    h = _norm_rows(x_ref[0]) * (1.0 + m[1:2]) + m[0:1]
    u_ref[0] = _dot(h.astype(BF16), w_ref[...]).astype(BF16)


def _front_ab_ctx(xc, modtab, w_in):
    b, lc, d = xc.shape
    n = w_in.shape[1]
    return pl.pallas_call(
        _front_ab_ctx_kernel,
        grid=(b, lc // ROW_BLK),
        in_specs=[pl.BlockSpec((1, ROW_BLK, d), lambda i, j: (i, j, 0)),
                  pl.BlockSpec((1, 1, 6, d), lambda i, j: (i, 1, 0, 0)),
                  _resident((d, n))],
        out_specs=pl.BlockSpec((1, ROW_BLK, n), lambda i, j: (i, j, 0)),
        out_shape=jax.ShapeDtypeStruct((b, lc, n), BF16),
        compiler_params=_cparams(("parallel", "parallel")),
        name="front_ab_ctx",
    )(xc, modtab, w_in)


def _front_ab_lat_kernel(x_ref, mod_ref, wq_ref, wht_ref, u_ref, ut_ref):
    m = mod_ref[0, 0]
    per = ROW_BLK // FFT_N2
    for t in range(x_ref.shape[1] // ROW_BLK):
        rs = slice(t * ROW_BLK, (t + 1) * ROW_BLK)
        h = (_norm_rows(x_ref[0, rs, :]) * (1.0 + m[1:2]) + m[0:1]).astype(BF16)
        u_ref[0, rs, :] = _dot(h, wq_ref[...]).astype(BF16)
        ut = _dot_nt(wht_ref[...], h)
        for s in range(per):
            ut_ref[0, :, t * per + s, :] = ut[:, s * FFT_N2:(s + 1) * FFT_N2]


def _front_ab_lat(x, modtab, w_qkv, w_hy_t):
    b, length, d = x.shape
    nq = w_qkv.shape[1]
    nh = w_hy_t.shape[0]
    rows = 8 * FFT_N2
    return pl.pallas_call(
        _front_ab_lat_kernel,
        grid=(b, length // rows),
        in_specs=[pl.BlockSpec((1, rows, d), lambda i, j: (i, j, 0)),
                  pl.BlockSpec((1, 1, 6, d), lambda i, j: (i, 0, 0, 0)),
                  _resident((d, nq)), _resident((nh, d))],
        out_specs=[pl.BlockSpec((1, rows, nq), lambda i, j: (i, j, 0)),
                   pl.BlockSpec((1, nh, 8, FFT_N2), lambda i, j: (i, 0, j, 0))],
        out_shape=[jax.ShapeDtypeStruct((b, length, nq), BF16),
                   jax.ShapeDtypeStruct((b, nh, length // FFT_N2, FFT_N2), F32)],
        compiler_params=_cparams(("parallel", "parallel")),
        name="front_ab_lat",
    )(x, modtab, w_qkv, w_hy_t)


def _hy_prep_kernel(cur_ref, prev_ref, next_ref, w_ref, x1_ref, x2_ref, v_ref, *, nblk):
    j = pl.program_id(1)
    cur = cur_ref[0].astype(F32)
    rows = cur.shape[0]
    has_prev = (j > 0).astype(F32)
    has_next = (j < nblk - 1).astype(F32)
    prev_row = prev_ref[0][7:8].astype(F32) * has_prev
    next_row = next_ref[0][0:1].astype(F32) * has_next
    rid = lax.broadcasted_iota(jnp.int32, (rows, 1), 0)
    up = jnp.where(rid == 0, prev_row, pltpu.roll(cur, 1, axis=0))
    dn = jnp.where(rid == rows - 1, next_row, pltpu.roll(cur, rows - 1, axis=0))
    w = w_ref[...]
    y = up * w[0:1] + cur * w[1:2] + dn * w[2:3]
    c = HY_CH
    x1_ref[0] = y[:, :c].astype(BF16)
    x2_ref[0] = y[:, c:2 * c].astype(BF16)
    v_ref[0] = y[:, 2 * c:].astype(BF16)


def _hy_prep(u, conv_w, blk0, nblk):
    b, lt, _ = u.shape
    n = 3 * HY_CH
    sub = ROW_BLK // 8
    last8 = lt // 8 - 1
    out = jax.ShapeDtypeStruct((b, nblk * ROW_BLK, HY_CH), BF16)
    ospec = pl.BlockSpec((1, ROW_BLK, HY_CH), lambda i, j: (i, j, 0))
    return pl.pallas_call(
        functools.partial(_hy_prep_kernel, nblk=nblk),
        grid=(b, nblk),
        in_specs=[pl.BlockSpec((1, ROW_BLK, n), lambda i, j: (i, blk0 + j, 0)),
                  pl.BlockSpec((1, 8, n), lambda i, j: (i, jnp.maximum((blk0 + j) * sub - 1, 0), 0)),
                  pl.BlockSpec((1, 8, n), lambda i, j: (i, jnp.minimum((blk0 + j + 1) * sub, last8), 0)),
                  _resident((3, n))],
        out_specs=[ospec, ospec, ospec],
        out_shape=[out, out, out],
        compiler_params=_cparams(("parallel", "parallel")),
        name="hy_prep",
    )(u, u, u, conv_w)


def _hy_filt_kernel(z_ref, w1_ref, b1_ref, fr_ref, w2_ref, b2_ref, w3_ref, ld_ref, o_ref):
    hi = lax.Precision.HIGHEST
    z = z_ref[...]
    fr = fr_ref[...]
    hid = jnp.sin(fr * (jnp.dot(z, w1_ref[...], preferred_element_type=F32, precision=hi) + b1_ref[...]))
    hid = jnp.sin(fr * (jnp.dot(hid, w2_ref[...], preferred_element_type=F32, precision=hi) + b2_ref[...]))
    h = jnp.dot(hid, w3_ref[...], preferred_element_type=F32, precision=hi)
    t = z[:, 0:1]
    o_ref[...] = h * jnp.exp(-t * jnp.exp(ld_ref[...]))


def _pad2(a, rows, cols):
    return jnp.pad(a, ((0, rows - a.shape[0]), (0, cols - a.shape[1])))


def _hy_filters(length, w1, b1, freq, w2, b2, w3, log_decay):
    pos = jnp.arange(length, dtype=F32)
    t = pos / max(length - 1, 1)
    w = 2.0 * math.pi * pos / length
    f = jnp.linspace(1e-4, HY_BANDS - 1, HY_BANDS, dtype=F32)
    ang = w[:, None] * f[None, :]
    z = jnp.concatenate([t[:, None], jnp.cos(ang), -jnp.sin(ang)], -1)
    z = _pad2(z, length, LANES)
    n = w3.shape[1]
    rb = min(length, 512)
    vec = lambda a: _pad2(a.reshape(1, -1), 1, LANES)
    return pl.pallas_call(
        _hy_filt_kernel,
        grid=(length // rb,),
        in_specs=[pl.BlockSpec((rb, LANES), lambda i: (i, 0)),
                  _resident((LANES, LANES)), _resident((1, LANES)), _resident((1, LANES)),
                  _resident((LANES, LANES)), _resident((1, LANES)),
                  _resident((LANES, n)), _resident((1, n))],
        out_specs=pl.BlockSpec((rb, n), lambda i: (i, 0)),
        out_shape=jax.ShapeDtypeStruct((length, n), F32),
        compiler_params=_cparams(("parallel",)),
        name="hy_filters",
    )(z, _pad2(w1, LANES, LANES), vec(b1), vec(freq), _pad2(w2, LANES, LANES), vec(b2),
      _pad2(w3, LANES, n), log_decay.reshape(1, n))


def _bidir_taps(h, skip, length):
    h4 = h.reshape(length, 2, 2, HY_CH)
    cols = []
    for o in range(2):
        hf = h4[:, o, 0].at[0].add(skip[o])
        hb = h4[:, o, 1]
        cols.append(jnp.concatenate([hf, jnp.zeros_like(hf[:1]), hb[:0:-1]], 0))
    return jnp.concatenate(cols, -1)


def _left_mm_kernel(m_ref, x_ref, o_ref):
    o_ref[0] = _dot(m_ref[...], x_ref[0]).astype(o_ref.dtype)


def _left_mm(mat, x, out_dtype, lane_blk):
    g, k, n = x.shape
    m = mat.shape[0]
    lane_blk = min(lane_blk, n)
    return pl.pallas_call(
        _left_mm_kernel, grid=(g, n // lane_blk),
        in_specs=[_resident((m, k)), pl.BlockSpec((1, k, lane_blk), lambda i, j: (i, 0, j))],
        out_specs=pl.BlockSpec((1, m, lane_blk), lambda i, j: (i, 0, j)),
        out_shape=jax.ShapeDtypeStruct((g, m, n), out_dtype),
        compiler_params=_cparams(("parallel", "parallel")),
        name="left_mm",
    )(mat, x)


def _cs(num, den, rows, cols):
    ang = 2.0 * np.pi * np.outer(np.arange(rows), np.arange(cols)) * (num / den)
    return np.cos(ang), np.sin(ang)


def _twiddle(n1, n2):
    k1 = lax.broadcasted_iota(jnp.int32, (n1, n2, 1), 0)
    m2 = lax.broadcasted_iota(jnp.int32, (n1, n2, 1), 1)
    ang = (k1 * m2).astype(F32) * (2.0 * math.pi / (n1 * n2))
    return jnp.cos(ang), jnp.sin(ang)


def _hy_filt_t_kernel(z_ref, msk_ref, w1_ref, b1_ref, fr_ref, w2_ref, b2_ref, w3_ref, ld_ref, sk_ref, o_ref):
    hi = lax.Precision.HIGHEST
    z = z_ref[...]
    fr = fr_ref[...]
    hid = jnp.sin(fr * (jnp.dot(w1_ref[...], z, preferred_element_type=F32, precision=hi) + b1_ref[...]))
    hid = jnp.sin(fr * (jnp.dot(w2_ref[...], hid, preferred_element_type=F32, precision=hi) + b2_ref[...]))
    h = jnp.dot(w3_ref[0], hid, preferred_element_type=F32, precision=hi)
    h = h * jnp.exp(-jnp.exp(ld_ref[0]) * z[0:1, :])
    msk = msk_ref[...]
    h = h * msk[0:1, :] + sk_ref[...] * msk[1:2, :]
    for s in range(o_ref.shape[1]):
        o_ref[:, s, :] = h[:, s * FFT_N2:(s + 1) * FFT_N2]


def _hy_filters_t(length, w1, b1, freq, w2, b2, w3, log_decay, skip):
    n = 2 * length
    tt = jnp.arange(n, dtype=jnp.int32)
    pos = jnp.where(tt < length, tt, n - tt).astype(F32)
    t = pos / max(length - 1, 1)
    w = 2.0 * math.pi * pos / length
    f = jnp.linspace(1e-4, HY_BANDS - 1, HY_BANDS, dtype=F32)
    ang = f[:, None] * w[None, :]
    z = jnp.concatenate([t[None, :], jnp.cos(ang), -jnp.sin(ang)], 0)
    z = jnp.pad(z, ((0, LANES - z.shape[0]), (0, 0)))
    msk = jnp.stack([(tt != length).astype(F32), (tt == 0).astype(F32)])
    msk = jnp.pad(msk, ((0, 6), (0, 0)))
    col = lambda a: a.reshape(-1, 1)
    c2 = 2 * HY_CH
    nf = w3.shape[0]
    w3d = w3.reshape(nf, 2, 2, HY_CH).transpose(2, 1, 3, 0).reshape(2, c2, nf)
    ldd = log_decay.reshape(2, 2, HY_CH).transpose(1, 0, 2).reshape(2, c2, 1)
    rows = 8
    pb = rows * FFT_N2
    half = length // pb
    return pl.pallas_call(
        _hy_filt_t_kernel,
        grid=(n // pb,),
        in_specs=[pl.BlockSpec((LANES, pb), lambda i: (0, i)),
                  pl.BlockSpec((8, pb), lambda i: (0, i)),
                  _resident((nf, LANES)), _resident((nf, 1)), _resident((nf, 1)),
                  _resident((nf, nf)), _resident((nf, 1)),
                  pl.BlockSpec((1, c2, nf), lambda i: (i // half, 0, 0)),
                  pl.BlockSpec((1, c2, 1), lambda i: (i // half, 0, 0)),
                  _resident((c2, 1))],
        out_specs=pl.BlockSpec((c2, rows, FFT_N2), lambda i: (0, i, 0)),
        out_shape=jax.ShapeDtypeStruct((c2, n // FFT_N2, FFT_N2), F32),
        compiler_params=_cparams(("parallel",)),
        name="hy_filters_t",
    )(z, msk, _pad2(w1.T, nf, LANES), col(b1), col(freq), w2.T, col(b2), w3d, ldd, skip.reshape(c2, 1))


def _dft_mats():
    n = FFT_N2
    c, s = _cs(1, n, n, n)
    ch, sh = c[:, :n // 2], s[:, :n // 2]
    f1_pair = np.block([[ch, sh], [-sh, ch]])
    f1_full = np.concatenate([c, -s], 0)
    m2r = np.block([[c, -s], [s, c]])
    m2i = np.block([[c, s], [-s, c]])
    f1_inv = np.block([[ch.T, -sh.T], [sh.T, ch.T]]) / (n * n)
    cast = lambda a: jnp.asarray(a, dtype=BF16)
    return cast(f1_pair), cast(f1_full), cast(m2r), cast(m2i), cast(f1_inv)


def _twiddle2d(n):
    k1 = lax.broadcasted_iota(jnp.int32, (n, n), 0)
    m2 = lax.broadcasted_iota(jnp.int32, (n, n), 1)
    ang = (k1 * m2).astype(F32) * (2.0 * math.pi / (n * n))
    return jnp.cos(ang), jnp.sin(ang)


def _fwd_spectrum(xs, f1, m2r, c, s):
    n = FFT_N2
    a = _dot(f1, jnp.concatenate(xs, axis=1))
    ts = []
    for g in range(len(xs)):
        ar = a[:n, g * n:(g + 1) * n]
        ai = a[n:, g * n:(g + 1) * n]
        ts.append(jnp.concatenate([ar * c + ai * s, ai * c - ar * s], axis=1))
    return _dot(jnp.concatenate(ts, axis=0).astype(BF16), m2r)


def _filter_spec_t_kernel(t_ref, f1_ref, m2r_ref, c_ref, s_ref, o_ref):
    g = t_ref.shape[0]
    xs = [t_ref[i].astype(BF16) for i in range(g)]
    spec = _fwd_spectrum(xs, f1_ref[...], m2r_ref[...], c_ref[...], s_ref[...])
    for i in range(g):
        o_ref[i] = spec[i * FFT_N2:(i + 1) * FFT_N2]


def _filter_spec_t(taps, f1_full, m2r, twc, tws):
    nch, n1, n = taps.shape
    g = HY_GROUP
    return pl.pallas_call(
        _filter_spec_t_kernel,
        grid=(nch // g,),
        in_specs=[pl.BlockSpec((g, n1, n), lambda i: (i, 0, 0)),
                  _resident(f1_full.shape), _resident(m2r.shape), _resident((n, n)), _resident((n, n))],
        out_specs=pl.BlockSpec((g, n, 2 * n), lambda i: (i, 0, 0)),
        out_shape=jax.ShapeDtypeStruct((nch, n, 2 * n), F32),
        compiler_params=_cparams(("parallel",)),
        name="filter_spec_t",
    )(taps, f1_full, m2r, twc, tws)


def _hyena_core_kernel(x1_ref, x2_ref, v_ref, w1_ref, w2_ref, wv_ref, h0_ref, h1_ref,
                       f1_ref, m2r_ref, m2i_ref, f1i_ref, c_ref, s_ref, o_ref):
    n = FFT_N2
    n1 = v_ref.shape[2]
    c = c_ref[...]
    s = s_ref[...]
    lane = lax.broadcasted_iota(jnp.int32, (n1, n), 1)
    row = lax.broadcasted_iota(jnp.int32, (n1, n), 0)
    first_lane, last_lane = lane == 0, lane == n - 1
    seq_start, seq_end = first_lane & (row == 0), last_lane & (row == n1 - 1)

    def short_conv(x, w):
        r = pltpu.roll(x, 1, axis=1)
        up = jnp.where(first_lane, pltpu.roll(r, 1, axis=0), r)
        up = jnp.where(seq_start, 0.0, up)
        l = pltpu.roll(x, n - 1, axis=1)
        dn = jnp.where(last_lane, pltpu.roll(l, n1 - 1, axis=0), l)
        dn = jnp.where(seq_end, 0.0, dn)
        return up * w[0:1] + x * w[1:2] + dn * w[2:3]

    def conv_all(xss, h_ref):
        spec_s = [_fwd_spectrum(xs, f1_ref[...], m2r_ref[...], c, s) for xs in xss]
        y_s = []
        for chain, spec in enumerate(spec_s):
            ys = []
            for g in range(HY_GROUP):
                xr = spec[g * n:(g + 1) * n, :n]
                xi = spec[g * n:(g + 1) * n, n:]
                hh = h_ref[chain * HY_GROUP + g]
                hr, hi = hh[:, :n], hh[:, n:]
                ys.append(jnp.concatenate([xr * hr - xi * hi, xr * hi + xi * hr], axis=1))
            y_s.append(jnp.concatenate(ys, axis=0).astype(BF16))
        bm_s = [_dot(y, m2i_ref[...]) for y in y_s]
        b_s = []
        for bm in bm_s:
            bs = []
            for g in range(HY_GROUP):
                br = bm[g * n:(g + 1) * n, :n]
                bi = bm[g * n:(g + 1) * n, n:]
                bs.append(jnp.concatenate([br * c - bi * s, bi * c + br * s], axis=0))
            b_s.append(jnp.concatenate(bs, axis=1).astype(BF16))
        outs = [_dot(f1i_ref[...], bc) for bc in b_s]
        return [[y[:, g * n:(g + 1) * n] for g in range(HY_GROUP)] for y in outs]

    def pair(ref, w_ref, ch):
        return jnp.concatenate([short_conv(ref[0, ch], w_ref[ch]), short_conv(ref[1, ch], w_ref[ch])], axis=0)

    chans = [[chain * HY_GROUP + g for g in range(HY_GROUP)] for chain in range(HY_CHAINS)]
    y1 = conv_all([[pair(v_ref, wv_ref, ch).astype(BF16) for ch in grp] for grp in chans], h0_ref)
    z = [[(y1[k][g] * pair(x1_ref, w1_ref, ch)).astype(BF16) for g, ch in enumerate(grp)]
         for k, grp in enumerate(chans)]
    y2 = conv_all(z, h1_ref)
    for k, grp in enumerate(chans):
        for g, ch in enumerate(grp):
            y = y2[k][g] * pair(x2_ref, w2_ref, ch)
            o_ref[0, :, ch, :] = y[:n1]
            o_ref[1, :, ch, :] = y[n1:]


def _hyena_core(ut, conv_w, spec, mats, twc, tws):
    f1_pair, _, m2r, m2i, f1_inv = mats
    b, nch, n1, n = ut.shape
    assert b % 2 == 0
    cb = HY_CBLK
    nblk = HY_CH // cb
    wt = jnp.broadcast_to(conv_w.T[:, :, None], (nch, conv_w.shape[0], n))
    xspec = lambda off: pl.BlockSpec((2, cb, n1, n), lambda j, i: (i, off * nblk + j, 0, 0))
    wspec = lambda off: pl.BlockSpec((cb, conv_w.shape[0], n), lambda j, i: (off * nblk + j, 0, 0))
    hspec = lambda off: pl.BlockSpec((cb, n, 2 * n), lambda j, i: (off * nblk + j, 0, 0))
    return pl.pallas_call(
        _hyena_core_kernel,
        grid=(nblk, b // 2),
        in_specs=[xspec(0), xspec(1), xspec(2), wspec(0), wspec(1), wspec(2), hspec(0), hspec(1),
                  _resident(f1_pair.shape), _resident(m2r.shape), _resident(m2i.shape), _resident(f1_inv.shape),
                  _resident((n, n)), _resident((n, n))],
        out_specs=pl.BlockSpec((2, n1, cb, n), lambda j, i: (i, 0, j, 0)),
        out_shape=jax.ShapeDtypeStruct((b, n1, HY_CH, n), F32),
        compiler_params=_cparams(("parallel", "parallel")),
        name="hyena_core",
    )(ut, ut, ut, wt, wt, wt, spec, spec, f1_pair, m2r, m2i, f1_inv, twc, tws)


def _hy_ctx_kernel(v_ref, x1_ref, x2_ref, f_ref, fi_ref, h_ref, o_ref):
    nf = f_ref.shape[0] // 2
    zin = v_ref[0]
    gates = (x1_ref, x2_ref)
    for o in range(2):
        x = _dot(f_ref[...], zin)
        xr, xi = x[:nf], x[nf:]
        hr = h_ref[o, :nf]
        hi = h_ref[o, nf:]
        y = jnp.concatenate([xr * hr - xi * hi, xr * hi + xi * hr], axis=0).astype(BF16)
        zin = (_dot(fi_ref[...], y) * gates[o][0].astype(F32)).astype(BF16)
    o_ref[0] = zin


def _hy_ctx(v, x1, x2, taps):
    b, lc, c = v.shape
    nf = 2 * lc
    cm, sm = _cs(1, nf, nf, nf)
    fwd = jnp.asarray(np.concatenate([cm[:, :lc], -sm[:, :lc]], 0), dtype=BF16)
    fwd_full = jnp.asarray(np.concatenate([cm, -sm], 0), dtype=BF16)
    inv = jnp.asarray(np.concatenate([cm[:lc, :], -sm[:lc, :]], 1) / nf, dtype=BF16)
    spec = _left_mm(fwd_full, taps.astype(BF16).reshape(1, nf, 2 * c), F32, 2 * c)
    spec = spec.reshape(2 * nf, 2, c).transpose(1, 0, 2)
    blk = pl.BlockSpec((1, lc, c), lambda i: (i, 0, 0))
    return pl.pallas_call(
        _hy_ctx_kernel,
        grid=(b,),
        in_specs=[blk, blk, blk, _resident((2 * nf, lc)), _resident((lc, 2 * nf)),
                  _resident((2, 2 * nf, c))],
        out_specs=blk,
        out_shape=jax.ShapeDtypeStruct((b, lc, c), BF16),
        compiler_params=_cparams(("parallel",)),
        name="hy_ctx",
    )(v, x1, x2, fwd, inv, spec)


def _pair_rows(q2):
    lane = lax.broadcasted_iota(jnp.int32, q2.shape, 1)
    zero = jnp.zeros_like(q2)
    return jnp.concatenate([jnp.where(lane < NA_HD, q2, zero), jnp.where(lane >= NA_HD, q2, zero)], axis=0)


def _unpair_rows(o):
    r = o.shape[0] // 2
    lane = lax.broadcasted_iota(jnp.int32, (r, o.shape[1]), 1)
    return jnp.where(lane < NA_HD, o[:r], o[r:])


def _pair_softmax_pv(scores, values):
    m = scores[0].max(axis=-1, keepdims=True)
    for s in scores[1:]:
        m = jnp.maximum(m, s.max(axis=-1, keepdims=True))
    den = None
    acc = None
    for s, v in zip(scores, values):
        p = jnp.exp(s - m)
        d = p.sum(axis=-1, keepdims=True)
        a = _dot(p.astype(BF16), v)
        den = d if den is None else den + d
        acc = a if acc is None else acc + a
    return acc / den


def _natten_kernel(q_ref, k0, k1, k2, k3, v0, v1, v2, v3, kc_ref, vc_ref, bias_ref, o_ref,
                   kwin, vwin, *, rows):
    g = pl.program_id(1)
    rb = 4 * GRID_W
    for i, (kr, vr) in enumerate(((k0, v0), (k1, v1), (k2, v2), (k3, v3))):
        kwin[i * rb:(i + 1) * rb, :] = kr[0]
        vwin[i * rb:(i + 1) * rb, :] = vr[0]
    base = 4 * jnp.clip(2 * g - 1, 0, rows // 4 - 4)
    nwin = NA_WIN_R * GRID_W
    qscale = jnp.asarray(NA_HD ** -0.5, BF16)

    ones_lat = jnp.ones((nwin, LANES), BF16)
    ones_ctx = jnp.ones((kc_ref.shape[1], LANES), BF16)

    def rows_body(it, carry):
        work = []
        for u in range(NA_ROWS_PER_TRIP):
            rr = it * NA_ROWS_PER_TRIP + u
            r = 8 * g + rr
            rs = jnp.clip(r - NA_WIN_R // 2, 0, rows - NA_WIN_R)
            st = pl.multiple_of((rs - base) * GRID_W, GRID_W)
            qo = pl.multiple_of(rr * GRID_W, GRID_W)
            work += [(qo, st, rs - r + NA_WIN_R - 1, p) for p in range(NA_HEADS // 2)]
        scores = []
        for qo, st, d0, p in work:
            ls = slice(p * LANES, (p + 1) * LANES)
            qp = _pair_rows(q_ref[0, pl.ds(qo, GRID_W), ls] * qscale)
            scores.append((_dot_nt(qp, kwin[pl.ds(st, nwin), ls]) + bias_ref[d0, p].astype(F32),
                           _dot_nt(qp, kc_ref[0, :, ls])))
        maxima = [jnp.maximum(a.max(axis=-1, keepdims=True), b.max(axis=-1, keepdims=True)) for a, b in scores]
        for (qo, st, d0, p), (s_lat, s_ctx), m in zip(work, scores, maxima):
            ls = slice(p * LANES, (p + 1) * LANES)
            v_lat = jnp.concatenate([vwin[pl.ds(st, nwin), ls], ones_lat], axis=1)
            v_ctx = jnp.concatenate([vc_ref[0, :, ls], ones_ctx], axis=1)
            acc = _dot(jnp.exp(s_lat - m).astype(BF16), v_lat) + _dot(jnp.exp(s_ctx - m).astype(BF16), v_ctx)
            o = acc[:, :LANES] / acc[:, LANES:LANES + 1]
            o_ref[0, pl.ds(qo, GRID_W), ls] = _unpair_rows(o).astype(o_ref.dtype)
        return carry

    lax.fori_loop(0, 8 // NA_ROWS_PER_TRIP, rows_body, 0)


def _natten_bias(rpb):
    c = np.arange(GRID_W)[:, None]
    kc = np.arange(GRID_W)[None, :]
    cs = np.clip(c - NA_WIN_C // 2, 0, GRID_W - NA_WIN_C)
    valid = (kc >= cs) & (kc < cs + NA_WIN_C)
    dc = np.clip(kc - c + NA_WIN_C - 1, 0, 2 * NA_WIN_C - 2)
    tb = jnp.where(valid[None, None], rpb[:, :, dc], NEG_BIG)
    slabs = []
    for d0 in range(NA_WIN_R):
        s = tb[:, d0:d0 + NA_WIN_R]
        s = s.transpose(0, 2, 1, 3).reshape(NA_HEADS, GRID_W, NA_WIN_R * GRID_W)
        slabs.append(s.reshape(NA_HEADS // 2, 2 * GRID_W, NA_WIN_R * GRID_W))
    return jnp.stack(slabs).astype(BF16)


def _natten(uq, uc, bias):
    b, length, _ = uq.shape
    c = NA_HEADS * NA_HD
    rows = length // GRID_W
    rb = 4 * GRID_W
    nkb = length // rb
    qrows = 8 * GRID_W
    lc = uc.shape[1]

    def kv_spec(col, off):
        return pl.BlockSpec((1, rb, c), lambda i, g: (i, jnp.clip(2 * g - 1, 0, nkb - 4) + off, col))

    return pl.pallas_call(
        functools.partial(_natten_kernel, rows=rows),
        grid=(b, rows // 8),
        in_specs=[pl.BlockSpec((1, qrows, c), lambda i, g: (i, g, 0))]
                 + [kv_spec(1, o) for o in range(4)] + [kv_spec(2, o) for o in range(4)]
                 + [pl.BlockSpec((1, lc, c), lambda i, g: (i, 0, 4)),
                    pl.BlockSpec((1, lc, c), lambda i, g: (i, 0, 5)),
                    _resident(bias.shape)],
        out_specs=pl.BlockSpec((1, qrows, c), lambda i, g: (i, g, 0)),
        out_shape=jax.ShapeDtypeStruct((b, length, c), BF16),
        scratch_shapes=[pltpu.VMEM((4 * rb, c), BF16), pltpu.VMEM((4 * rb, c), BF16)],
        compiler_params=_cparams(("parallel", "parallel")),
        name="natten",
    )(uq, *([uq] * 8), uc, uc, bias)


def _ctx_attn_kernel(q_ref, k_ref, v_ref, o_ref):
    qscale = jnp.asarray(NA_HD ** -0.5, BF16)
    for p in range(NA_HEADS // 2):
        ls = slice(p * LANES, (p + 1) * LANES)
        qp = _pair_rows(q_ref[0, :, ls] * qscale)
        o = _pair_softmax_pv([_dot_nt(qp, k_ref[0, :, ls])], [v_ref[0, :, ls]])
        o_ref[0, :, ls] = _unpair_rows(o).astype(o_ref.dtype)


def _ctx_attn(u):
    b, lc, _ = u.shape
    c = NA_HEADS * NA_HD
    spec = lambda col: pl.BlockSpec((1, lc, c), lambda i: (i, 0, col))
    return pl.pallas_call(
        _ctx_attn_kernel,
        grid=(b,),
        in_specs=[spec(3), spec(4), spec(5)],
        out_specs=pl.BlockSpec((1, lc, c), lambda i: (i, 0, 0)),
        out_shape=jax.ShapeDtypeStruct((b, lc, c), BF16),
        compiler_params=_cparams(("parallel",)),
        name="ctx_attn",
    )(u, u, u)


def _post_kernel(x_ref, mod_ref, ya_ref, yb_ref, wa_ref, wb_ref, lng_ref, lnb_ref, w1_ref, w2_ref, o_ref, *, ya_t):
    m = mod_ref[0, 0]
    lng = lng_ref[...]
    lnb = lnb_ref[...]
    ff = w1_ref.shape[1]
    step = MLP_CHUNK
    sub = ROW_BLK
    nsub = x_ref.shape[1] // sub

    def head(t):
        rs = slice(t * sub, (t + 1) * sub)
        if ya_t:
            tiles = range(t * sub // LANES, (t + 1) * sub // LANES)
            ya = jnp.concatenate([ya_ref[0, s].T for s in tiles], axis=0).astype(BF16)
        else:
            ya = ya_ref[0, rs, :]
        y = _dot(ya, wa_ref[...]) + _dot(yb_ref[0, rs, :], wb_ref[...])
        x1 = _norm_rows(ALPHA * x_ref[0, rs, :] + m[2:3] * y) * lng[0:1] + lnb[0:1]
        return x1, (_norm_rows(x1) * (1.0 + m[4:5]) + m[3:4]).astype(BF16)

    def mlp(h):
        acc = None
        for c in range(ff // step):
            a = jnp.maximum(_dot(h, w1_ref[:, c * step:(c + 1) * step]), 0.0)
            d = _dot((a * a).astype(BF16), w2_ref[c * step:(c + 1) * step, :])
            acc = d if acc is None else acc + d
        return acc

    cur = head(0)
    for t in range(nsub):
        nxt = head(t + 1) if t + 1 < nsub else None
        acc = mlp(cur[1])
        o_ref[0, t * sub:(t + 1) * sub, :] = _norm_rows(ALPHA * cur[0] + m[5:6] * acc) * lng[1:2] + lnb[1:2]
        cur = nxt


def _post(x, modtab, mod_row, ya, yb, wa, wb, lng, lnb, w1, w2, ya_t=False):
    b, r, d = x.shape
    ka, kb = wa.shape[0], wb.shape[0]
    rows = min(r, 2 * ROW_BLK)
    row = lambda k: pl.BlockSpec((1, rows, k), lambda i, j: (i, j, 0))
    ya_spec = pl.BlockSpec((1, rows // LANES, ka, LANES), lambda i, j: (i, j, 0, 0)) if ya_t else row(ka)
    return pl.pallas_call(
        functools.partial(_post_kernel, ya_t=ya_t),
        grid=(b, r // rows),
        in_specs=[row(d), pl.BlockSpec((1, 1, 6, d), lambda i, j: (i, mod_row, 0, 0)), ya_spec, row(kb),
                  _resident(wa.shape), _resident(wb.shape), _resident(lng.shape), _resident(lnb.shape),
                  _resident(w1.shape), _resident(w2.shape)],
        out_specs=row(d),
        out_shape=jax.ShapeDtypeStruct((b, r, d), F32),
        compiler_params=_cparams(("parallel", "parallel")),
        name="post_mixer",
    )(x, modtab, ya, yb, wa, wb, lng, lnb, w1, w2)


def _rope(x, cos, sinl, sinr):
    reps = x.shape[1] // LANES
    tile = lambda t: jnp.concatenate([t] * reps, axis=1)
    n = x.shape[1]
    quarter = MLA_ROPE // 4
    return (x * tile(cos) + pltpu.roll(x, n - quarter, axis=1) * tile(sinl)
            + pltpu.roll(x, quarter, axis=1) * tile(sinr))


def _front_cd_kernel(x_ref, xc_ref, mod_ref, w_ref, qn_ref, kvn_ref, wuq_ref, wuk_ref, wuv_ref, epe_ref, one_ref,
                     fng_ref, fnb_ref, avg_ref, cbd_ref, sbd_ref,
                     cq_ref, slq_ref, srq_ref, ck_ref, slk_ref, srk_ref,
                     q_ref, k_ref, v_ref, p_ref, qf_ref, *, ctx_blk):
    m = mod_ref[0, 0]
    is_ctx = (jnp.zeros((ROW_BLK, 1), jnp.int32) + pl.program_id(1)) == ctx_blk
    x = jnp.where(is_ctx, xc_ref[0], x_ref[0])
    h = _norm_rows(x) * (1.0 + m[1:2]) + m[0:1]
    u = _dot(h.astype(BF16), w_ref[...])
    o_kv = MLA_Q_RANK
    o_fn = o_kv + MLA_KV_RANK
    o_pe = o_fn + FN_CH

    def rms(x, g):
        return x * lax.rsqrt(jnp.mean(x * x, axis=-1, keepdims=True) + LN_EPS) * g

    cq = rms(u[:, :o_kv], qn_ref[...]).astype(BF16)
    q = _dot(cq, wuq_ref[...])
    q_ref[0] = _rope(q, cq_ref[...], slq_ref[...], srq_ref[...]).astype(BF16)

    ckv = rms(u[:, o_kv:o_fn], kvn_ref[...]).astype(BF16)
    kpe = _dot(u[:, o_pe:].astype(BF16), epe_ref[...])
    k = _dot(ckv, wuk_ref[...]) + _rope(kpe, ck_ref[...], slk_ref[...], srk_ref[...])
    k_ref[0] = k.astype(BF16)
    v_ref[0] = (_dot(ckv, wuv_ref[...]) + one_ref[...]).astype(BF16)

    uf = u[:, o_fn:o_pe]
    avg = avg_ref[...]
    uc = uf - _dot(uf.astype(BF16), avg)
    var = _dot((uc * uc).astype(BF16), avg)
    ug = (uc * lax.rsqrt(var + LN_EPS) * fng_ref[...] + fnb_ref[...]).astype(BF16)
    p_ref[0] = _dot(ug, cbd_ref[...]).astype(BF16)
    qf_ref[0] = _dot(ug, sbd_ref[...]).astype(BF16)


def _rope_tables(length, lt, scale):
    t = jnp.arange(lt, dtype=jnp.int32)
    rows = (t // GRID_W).astype(F32)
    cols = (t % GRID_W).astype(F32)
    half = MLA_ROPE // 2
    inv = ROPE_THETA ** (-jnp.arange(0, half, 2, dtype=F32) / half)
    ar = rows[:, None] * inv[None, :]
    ac = cols[:, None] * inv[None, :]
    ang = jnp.concatenate([ar, ar, ac, ac], -1)
    is_lat = (t < length)[:, None]
    cos = jnp.where(is_lat, jnp.cos(ang), 1.0)
    sin = jnp.where(is_lat, jnp.sin(ang), 0.0)
    qd = MLA_ROPE // 4
    ones = jnp.ones((lt, MLA_NOPE), F32)
    zeros = jnp.zeros((lt, MLA_NOPE), F32)
    tail1 = jnp.ones((lt, HEAD_PAD - MLA_NOPE - MLA_ROPE), F32)
    tail0 = jnp.zeros((lt, HEAD_PAD - MLA_NOPE - MLA_ROPE), F32)
    z8 = jnp.zeros((lt, qd), F32)
    c = jnp.concatenate([ones, cos, tail1], -1)
    sl = jnp.concatenate([zeros, -sin[:, :qd], z8, -sin[:, 2 * qd:3 * qd], z8, tail0], -1)
    sr = jnp.concatenate([zeros, z8, sin[:, qd:2 * qd], z8, sin[:, 3 * qd:], tail0], -1)
    return c * scale, sl * scale, sr * scale


def _head_slots(w, per_head, take_from, take_n):
    k = w.shape[0]
    w3 = w.reshape(k, MLA_HEADS, per_head)[:, :, take_from:take_from + take_n]
    w3 = jnp.pad(w3, ((0, 0), (0, 0), (0, HEAD_PAD - take_n)))
    return w3.reshape(k, MLA_HEADS * HEAD_PAD)


def _front_cd(xl, xc, modtab, w_in, q_norm, w_uq, kv_norm, w_ukv, fn_g, fn_b):
    b, length, d = xl.shape
    lt = length + xc.shape[1]
    nlat = length // ROW_BLK
    o_kv = MLA_Q_RANK
    o_pe = o_kv + MLA_KV_RANK
    o_fn = o_pe + MLA_ROPE
    hw = MLA_HEADS * HEAD_PAD
    w_perm = jnp.concatenate([w_in[:, :o_pe], w_in[:, o_fn:], w_in[:, o_pe:o_fn],
                              jnp.zeros((d, LANES - MLA_ROPE), w_in.dtype)], -1).astype(BF16)
    wuq = _head_slots(w_uq, MLA_NOPE + MLA_ROPE, 0, MLA_NOPE + MLA_ROPE).astype(BF16)
    wuk = _head_slots(w_ukv, MLA_NOPE + MLA_V, 0, MLA_NOPE).astype(BF16)
    wuv = _head_slots(w_ukv, MLA_NOPE + MLA_V, MLA_NOPE, MLA_V).astype(BF16)
    epe = np.zeros((LANES, hw), np.float32)
    for hd in range(MLA_HEADS):
        for i in range(MLA_ROPE):
            epe[i, hd * HEAD_PAD + MLA_NOPE + i] = 1.0
    epe = jnp.asarray(epe, dtype=BF16)
    ones_col = np.zeros((1, hw), np.float32)
    ones_col[0, MLA_V::HEAD_PAD] = 1.0
    ones_col = jnp.asarray(ones_col)
    cm, sm = _cs(1, FN_GD, FN_GD, FN_GD)
    eye = np.eye(FN_GROUPS)
    cbd = jnp.asarray(np.kron(eye, cm), dtype=BF16)
    sbd = jnp.asarray(np.kron(eye, -sm), dtype=BF16)
    avg = jnp.asarray(np.kron(eye, np.full((FN_GD, FN_GD), 1.0 / FN_GD)), dtype=BF16)
    qtab = _rope_tables(length, lt, (MLA_NOPE + MLA_ROPE) ** -0.5 * math.log2(math.e))
    ktab = _rope_tables(length, lt, 1.0)
    row = lambda n: pl.BlockSpec((1, ROW_BLK, n), lambda i, j: (i, j, 0))
    tab = pl.BlockSpec((ROW_BLK, HEAD_PAD), lambda i, j: (j, 0))
    out = lambda n: jax.ShapeDtypeStruct((b, lt, n), BF16)
    return pl.pallas_call(
        functools.partial(_front_cd_kernel, ctx_blk=nlat),
        grid=(b, lt // ROW_BLK),
        in_specs=[pl.BlockSpec((1, ROW_BLK, d), lambda i, j: (i, jnp.minimum(j, nlat - 1), 0)),
                  pl.BlockSpec((1, ROW_BLK, d), lambda i, j: (i, 0, 0)),
                  pl.BlockSpec((1, 1, 6, d), lambda i, j: (i, j // nlat, 0, 0)),
                  _resident(w_perm.shape), _resident((1, MLA_Q_RANK)), _resident((1, MLA_KV_RANK)),
                  _resident(wuq.shape), _resident(wuk.shape), _resident(wuv.shape), _resident(epe.shape),
                  _resident(ones_col.shape),
                  _resident((1, FN_CH)), _resident((1, FN_CH)), _resident(avg.shape), _resident(cbd.shape),
                  _resident(sbd.shape),
                  tab, tab, tab, tab, tab, tab],
        out_specs=[row(hw), row(hw), row(hw), row(FN_CH), row(FN_CH)],
        out_shape=[out(hw), out(hw), out(hw), out(FN_CH), out(FN_CH)],
        compiler_params=_cparams(("parallel", "parallel")),
        name="front_cd",
    )(xl, xc, modtab, w_perm, q_norm.reshape(1, -1), kv_norm.reshape(1, -1), wuq, wuk, wuv, epe, ones_col,
      fn_g.reshape(1, -1), fn_b.reshape(1, -1), avg, cbd, sbd, *qtab, *ktab)


def _mla_kernel(q_ref, k_ref, v_ref, o_ref, *, rows):
    n = q_ref.shape[1] // rows

    def scores(i):
        s = _dot_nt(q_ref[0, i * rows:(i + 1) * rows, :], k_ref[0])
        return s, s.max(axis=-1, keepdims=True)

    def finish(i, s, m):
        acc = _dot(jnp.exp2(s - m).astype(BF16), v_ref[0])
        o_ref[0, i * rows:(i + 1) * rows, :] = (acc / acc[:, MLA_V:MLA_V + 1]).astype(o_ref.dtype)

    pending = scores(0)
    for i in range(n):
        nxt = scores(i + 1) if i + 1 < n else None
        finish(i, *pending)
        pending = nxt


def _mla_attention(q, k, v, length):
    b, lt, hw = q.shape
    heads = hw // HEAD_PAD
    tq = MLA_Q_BLK
    kv = pl.BlockSpec((1, lt, HEAD_PAD), lambda i, h, j: (i, 0, h))
    qs = pl.BlockSpec((1, tq, HEAD_PAD), lambda i, h, j: (i, j, h))
    return pl.pallas_call(
        functools.partial(_mla_kernel, rows=MLA_Q_SUB),
        grid=(b, heads, length // tq),
        in_specs=[qs, kv, kv],
        out_specs=qs,
        out_shape=jax.ShapeDtypeStruct((b, length, hw), BF16),
        compiler_params=_cparams(("parallel", "parallel", "parallel")),
        name="mla_attention",
    )(q, k, v)


def _fn1_kernel(m_ref, zr_ref, zi_ref, o_ref):
    n = zr_ref.shape[1]
    a = _dot(m_ref[...], jnp.concatenate([zr_ref[0], zi_ref[0]], axis=0))
    o_ref[0, 0] = a[:n].astype(o_ref.dtype)
    o_ref[0, 1] = a[n:].astype(o_ref.dtype)


def _fn2_kernel(a_ref, c_ref, s_ref, m_ref, o_ref):
    for i in range(a_ref.shape[2]):
        ar = a_ref[0, 0, i].astype(F32)
        ai = a_ref[0, 1, i].astype(F32)
        c = c_ref[i]
        s = s_ref[i]
        t = jnp.concatenate([ar * c + ai * s, ai * c - ar * s], axis=0).astype(BF16)
        o_ref[0, i] = _dot(m_ref[...], t).astype(o_ref.dtype)


def _fnet(p, qn):
    b, length, c = p.shape
    n1 = 128
    n2 = length // n1
    cm, sm = _cs(1, n1, n1, n1)
    m1 = jnp.asarray(np.block([[cm, sm], [-sm, cm]]), dtype=BF16)
    c2, s2 = _cs(1, n2, n2, n2)
    m2 = jnp.asarray(np.concatenate([c2, s2], 1) / math.sqrt(length * FN_GD), dtype=BF16)
    twc, tws = _twiddle(n1, n2)
    lane_blk = 2048
    zs = pl.BlockSpec((1, n1, lane_blk), lambda i, j: (i, 0, j))
    a = pl.pallas_call(
        _fn1_kernel,
        grid=(b, n2 * c // lane_blk),
        in_specs=[_resident(m1.shape), zs, zs],
        out_specs=pl.BlockSpec((1, 2, n1, lane_blk), lambda i, j: (i, 0, 0, j)),
        out_shape=jax.ShapeDtypeStruct((b, 2, n1, n2 * c), BF16),
        compiler_params=_cparams(("parallel", "parallel")),
        name="fnet_stage1",
    )(m1, p.reshape(b, n1, n2 * c), qn.reshape(b, n1, n2 * c))
    kb = 8
    y = pl.pallas_call(
        _fn2_kernel,
        grid=(b, n1 // kb),
        in_specs=[pl.BlockSpec((1, 2, kb, n2, c), lambda i, j: (i, 0, j, 0, 0)),
                  pl.BlockSpec((kb, n2, 1), lambda i, j: (j, 0, 0)),
                  pl.BlockSpec((kb, n2, 1), lambda i, j: (j, 0, 0)),
                  _resident(m2.shape)],
        out_specs=pl.BlockSpec((1, kb, n2, c), lambda i, j: (i, j, 0, 0)),
        out_shape=jax.ShapeDtypeStruct((b, n1, n2, c), BF16),
        compiler_params=_cparams(("parallel", "parallel")),
        name="fnet_stage2",
    )(a.reshape(b, 2, n1, n2, c), twc, tws, m2)
    return y.transpose(0, 2, 1, 3).reshape(b, length, c)


def kernel(x, c, ctx, c_ctx, mod_w, mod_b, ln_g, ln_b, mlp_w1, mlp_w2,
           ab_w_in, ab_w_out, hy_conv_w, hy_w1, hy_b1, hy_freq, hy_w2, hy_b2, hy_w3, hy_log_decay, hy_skip, na_rpb,
           cd_w_in, cd_w_out, mla_q_norm, mla_w_uq, mla_kv_norm, mla_w_ukv, fn_norm_g, fn_norm_b):
    b, length, d = x.shape
    lc = ctx.shape[1]

    cc = jnp.concatenate([c, c_ctx[None], jnp.zeros((8 - b - 1, d), F32)], 0)
    mods = _mod_vectors(cc, mod_w, mod_b).reshape(DEPTH, 8, 6, d)
    modtab = [jnp.stack([mods[l, :b], jnp.broadcast_to(mods[l, b], (b, 6, d))], axis=1) for l in range(DEPTH)]

    n_hy = 3 * HY_CH
    w_in = ab_w_in[0].astype(BF16)
    uq, ut = _front_ab_lat(x, modtab[0], w_in[:, n_hy:], w_in[:, :n_hy].T)
    uc = _front_ab_ctx(ctx, modtab[0], w_in)
    fargs = (hy_w1[0], hy_b1[0], hy_freq[0], hy_w2[0], hy_b2[0], hy_w3[0], hy_log_decay[0])
    mats = _dft_mats()
    twc, tws = _twiddle2d(FFT_N2)
    spec = _filter_spec_t(_hy_filters_t(length, *fargs, hy_skip[0]), mats[1], mats[2], twc, tws)
    y_hy_t = _hyena_core(ut, hy_conv_w[0], spec, mats, twc, tws)
    x1c, x2c, vc = _hy_prep(uc, hy_conv_w[0], 0, lc // ROW_BLK)
    y_hy_c = _hy_ctx(vc, x1c, x2c, _bidir_taps(_hy_filters(lc, *fargs), hy_skip[0], lc))
    y_na = _natten(uq, uc, _natten_bias(na_rpb[0]))
    y_na_c = _ctx_attn(uc)
    w_out = ab_w_out[0].astype(BF16)
    mlp = (ln_g[0], ln_b[0], mlp_w1[0].astype(BF16), mlp_w2[0].astype(BF16))
    xl = _post(x, modtab[0], 0, y_hy_t, y_na, w_out[:HY_CH], w_out[HY_CH:], *mlp, ya_t=True)
    xc = _post(ctx, modtab[0], 1, y_hy_c, y_na_c, w_out[:HY_CH], w_out[HY_CH:], *mlp)

    q, k, vv, p, qn = _front_cd(xl, xc, modtab[1], cd_w_in[0], mla_q_norm[0], mla_w_uq[0], mla_kv_norm[0],
                                mla_w_ukv[0], fn_norm_g[0], fn_norm_b[0])
    o = _mla_attention(q, k, vv, length)
    y_fn = _fnet(p[:, :length], qn[:, :length])
    w_out = cd_w_out[0]
    n_mla = MLA_HEADS * MLA_V
    wa = jnp.pad(w_out[:n_mla].reshape(MLA_HEADS, MLA_V, d), ((0, 0), (0, HEAD_PAD - MLA_V), (0, 0)))
    wa = wa.reshape(MLA_HEADS * HEAD_PAD, d).astype(BF16)
    return _post(xl, modtab[1], 0, o, y_fn, wa, w_out[n_mla:].astype(BF16),
                 ln_g[1], ln_b[1], mlp_w1[1].astype(BF16), mlp_w2[1].astype(BF16))
```

```python
import functools
import math

import numpy as np
import jax
import jax.numpy as jnp
from jax import lax
from jax.experimental import pallas as pl
from jax.experimental.pallas import tpu as pltpu

F32 = jnp.float32
BF16 = jnp.bfloat16

D_MODEL = 1024
DEPTH = 2
GRID_W = 64
HY_CH = 512
HY_EMB = 33
HY_BANDS = (HY_EMB - 1) // 2
NA_HEADS = 8
NA_HD = 64
NA_WIN_R = 8
NA_WIN_C = 16
MLA_HEADS = 8
MLA_Q_RANK = 384
MLA_KV_RANK = 256
MLA_NOPE = 64
MLA_ROPE = 32
MLA_V = 96
ROPE_THETA = 10000.0
FN_CH = 256
FN_GROUPS = 4
FN_GD = FN_CH // FN_GROUPS
D_FF = 4 * D_MODEL
ALPHA = (2.0 * DEPTH) ** 0.25
LN_EPS = 1e-5

LANES = 128
ROW_BLK = 256
HEAD_PAD = 128
FFT_N2 = 128
NA_ROWS_PER_TRIP = 8
MLP_CHUNK = 1024
MLA_Q_BLK = 1024
MLA_Q_SUB = 256
HY_GROUP = 16
HY_CHAINS = 2
HY_CBLK = HY_GROUP * HY_CHAINS
VMEM_LIMIT = 56 * 1024 * 1024
NEG_BIG = -1e30


def _cparams(sem, vmem=VMEM_LIMIT):
    return pltpu.CompilerParams(dimension_semantics=sem, vmem_limit_bytes=vmem)


def _resident(shape):
    nd = len(shape)
    return pl.BlockSpec(shape, lambda *_: (0,) * nd, pipeline_mode=pl.Buffered(1))


def _norm_rows(x):
    mu = jnp.mean(x, axis=-1, keepdims=True)
    xc = x - mu
    var = jnp.mean(xc * xc, axis=-1, keepdims=True)
    return xc * lax.rsqrt(var + LN_EPS)


def _dot(a, b):
    return jnp.dot(a, b, preferred_element_type=F32)


def _dot_nt(a, b):
    return lax.dot_general(a, b, (((1,), (1,)), ((), ())), preferred_element_type=F32)


def _mod_kernel(c_ref, w_ref, b_ref, o_ref):
    c = c_ref[...]
    s = c * (1.0 / (1.0 + jnp.exp(-c)))
    o_ref[0] = jnp.dot(s, w_ref[0], preferred_element_type=F32,
                       precision=lax.Precision.HIGHEST) + b_ref[0]


def _mod_vectors(cc, mod_w, mod_b):
    depth, d, n = mod_w.shape
    nb = 1024
    return pl.pallas_call(
        _mod_kernel,
        grid=(depth, n // nb),
        in_specs=[pl.BlockSpec((8, d), lambda l, j: (0, 0)),
                  pl.BlockSpec((1, d, nb), lambda l, j: (l, 0, j)),
                  pl.BlockSpec((1, 1, nb), lambda l, j: (l, 0, j))],
        out_specs=pl.BlockSpec((1, 8, nb), lambda l, j: (l, 0, j)),
        out_shape=jax.ShapeDtypeStruct((depth, 8, n), F32),
        compiler_params=_cparams(("parallel", "parallel")),
        name="mod_vectors",
    )(cc, mod_w, mod_b.reshape(depth, 1, n))


def _front_ab_ctx_kernel(x_ref, mod_ref, w_ref, u_ref):
    m = mod_ref[0, 0]
    h = _norm_rows(x_ref[0]) * (1.0 + m[1:2]) + m[0:1]
    u_ref[0] = _dot(h.astype(BF16), w_ref[...]).astype(BF16)


def _front_ab_ctx(xc, modtab, w_in):
    b, lc, d = xc.shape
    n = w_in.shape[1]
    return pl.pallas_call(
        _front_ab_ctx_kernel,
        grid=(b, lc // ROW_BLK),
        in_specs=[pl.BlockSpec((1, ROW_BLK, d), lambda i, j: (i, j, 0)),
                  pl.BlockSpec((1, 1, 6, d), lambda i, j: (i, 1, 0, 0)),
                  _resident((d, n))],
        out_specs=pl.BlockSpec((1, ROW_BLK, n), lambda i, j: (i, j, 0)),
        out_shape=jax.ShapeDtypeStruct((b, lc, n), BF16),
        compiler_params=_cparams(("parallel", "parallel")),
        name="front_ab_ctx",
    )(xc, modtab, w_in)


def _front_ab_lat_kernel(x_ref, mod_ref, wq_ref, wht_ref, u_ref, ut_ref):
    m = mod_ref[0, 0]
    hs = []
    for t in range(x_ref.shape[1] // ROW_BLK):
        rs = slice(t * ROW_BLK, (t + 1) * ROW_BLK)
        hs.append((_norm_rows(x_ref[0, rs, :]) * (1.0 + m[1:2]) + m[0:1]).astype(BF16))
        u_ref[0, rs, :] = _dot(hs[-1], wq_ref[...]).astype(BF16)
    h_all = jnp.concatenate(hs, axis=0)
    for c0 in range(0, wht_ref.shape[0], HY_CH):
        ut = _dot_nt(wht_ref[c0:c0 + HY_CH, :], h_all)
        for s in range(ut_ref.shape[2]):
            ut_ref[0, c0:c0 + HY_CH, s, :] = ut[:, s * FFT_N2:(s + 1) * FFT_N2]


def _front_ab_lat(x, modtab, w_qkv, w_hy_t):
    b, length, d = x.shape
    nq = w_qkv.shape[1]
    nh = w_hy_t.shape[0]
    rows = 8 * FFT_N2
    return pl.pallas_call(
        _front_ab_lat_kernel,
        grid=(b, length // rows),
        in_specs=[pl.BlockSpec((1, rows, d), lambda i, j: (i, j, 0)),
                  pl.BlockSpec((1, 1, 6, d), lambda i, j: (i, 0, 0, 0)),
                  _resident((d, nq)), _resident((nh, d))],
        out_specs=[pl.BlockSpec((1, rows, nq), lambda i, j: (i, j, 0)),
                   pl.BlockSpec((1, nh, 8, FFT_N2), lambda i, j: (i, 0, j, 0))],
        out_shape=[jax.ShapeDtypeStruct((b, length, nq), BF16),
                   jax.ShapeDtypeStruct((b, nh, length // FFT_N2, FFT_N2), F32)],
        compiler_params=_cparams(("parallel", "parallel")),
        name="front_ab_lat",
    )(x, modtab, w_qkv, w_hy_t)


def _hy_prep_kernel(cur_ref, prev_ref, next_ref, w_ref, x1_ref, x2_ref, v_ref, *, nblk):
    j = pl.program_id(1)
    cur = cur_ref[0].astype(F32)
    rows = cur.shape[0]
    has_prev = (j > 0).astype(F32)
    has_next = (j < nblk - 1).astype(F32)
    prev_row = prev_ref[0][7:8].astype(F32) * has_prev
    next_row = next_ref[0][0:1].astype(F32) * has_next
    rid = lax.broadcasted_iota(jnp.int32, (rows, 1), 0)
    up = jnp.where(rid == 0, prev_row, pltpu.roll(cur, 1, axis=0))
    dn = jnp.where(rid == rows - 1, next_row, pltpu.roll(cur, rows - 1, axis=0))
    w = w_ref[...]
    y = up * w[0:1] + cur * w[1:2] + dn * w[2:3]
    c = HY_CH
    x1_ref[0] = y[:, :c].astype(BF16)
    x2_ref[0] = y[:, c:2 * c].astype(BF16)
    v_ref[0] = y[:, 2 * c:].astype(BF16)


def _hy_prep(u, conv_w, blk0, nblk):
    b, lt, _ = u.shape
    n = 3 * HY_CH
    sub = ROW_BLK // 8
    last8 = lt // 8 - 1
    out = jax.ShapeDtypeStruct((b, nblk * ROW_BLK, HY_CH), BF16)
    ospec = pl.BlockSpec((1, ROW_BLK, HY_CH), lambda i, j: (i, j, 0))
    return pl.pallas_call(
        functools.partial(_hy_prep_kernel, nblk=nblk),
        grid=(b, nblk),
        in_specs=[pl.BlockSpec((1, ROW_BLK, n), lambda i, j: (i, blk0 + j, 0)),
                  pl.BlockSpec((1, 8, n), lambda i, j: (i, jnp.maximum((blk0 + j) * sub - 1, 0), 0)),
                  pl.BlockSpec((1, 8, n), lambda i, j: (i, jnp.minimum((blk0 + j + 1) * sub, last8), 0)),
                  _resident((3, n))],
        out_specs=[ospec, ospec, ospec],
        out_shape=[out, out, out],
        compiler_params=_cparams(("parallel", "parallel")),
        name="hy_prep",
    )(u, u, u, conv_w)


def _hy_filt_kernel(z_ref, w1_ref, b1_ref, fr_ref, w2_ref, b2_ref, w3_ref, ld_ref, o_ref):
    hi = lax.Precision.HIGHEST
    z = z_ref[...]
    fr = fr_ref[...]
    hid = jnp.sin(fr * (jnp.dot(z, w1_ref[...], preferred_element_type=F32, precision=hi) + b1_ref[...]))
    hid = jnp.sin(fr * (jnp.dot(hid, w2_ref[...], preferred_element_type=F32, precision=hi) + b2_ref[...]))
    h = jnp.dot(hid, w3_ref[...], preferred_element_type=F32, precision=hi)
    t = z[:, 0:1]
    o_ref[...] = h * jnp.exp(-t * jnp.exp(ld_ref[...]))


def _pad2(a, rows, cols):
    return jnp.pad(a, ((0, rows - a.shape[0]), (0, cols - a.shape[1])))


def _hy_filters(length, w1, b1, freq, w2, b2, w3, log_decay):
    pos = jnp.arange(length, dtype=F32)
    t = pos / max(length - 1, 1)
    w = 2.0 * math.pi * pos / length
    f = jnp.linspace(1e-4, HY_BANDS - 1, HY_BANDS, dtype=F32)
    ang = w[:, None] * f[None, :]
    z = jnp.concatenate([t[:, None], jnp.cos(ang), -jnp.sin(ang)], -1)
    z = _pad2(z, length, LANES)
    n = w3.shape[1]
    rb = min(length, 512)
    vec = lambda a: _pad2(a.reshape(1, -1), 1, LANES)
    return pl.pallas_call(
        _hy_filt_kernel,
        grid=(length // rb,),
        in_specs=[pl.BlockSpec((rb, LANES), lambda i: (i, 0)),
                  _resident((LANES, LANES)), _resident((1, LANES)), _resident((1, LANES)),
                  _resident((LANES, LANES)), _resident((1, LANES)),
                  _resident((LANES, n)), _resident((1, n))],
        out_specs=pl.BlockSpec((rb, n), lambda i: (i, 0)),
        out_shape=jax.ShapeDtypeStruct((length, n), F32),
        compiler_params=_cparams(("parallel",)),
        name="hy_filters",
    )(z, _pad2(w1, LANES, LANES), vec(b1), vec(freq), _pad2(w2, LANES, LANES), vec(b2),
      _pad2(w3, LANES, n), log_decay.reshape(1, n))


def _bidir_taps(h, skip, length):
    h4 = h.reshape(length, 2, 2, HY_CH)
    cols = []
    for o in range(2):
        hf = h4[:, o, 0].at[0].add(skip[o])
        hb = h4[:, o, 1]
        cols.append(jnp.concatenate([hf, jnp.zeros_like(hf[:1]), hb[:0:-1]], 0))
    return jnp.concatenate(cols, -1)


def _left_mm_kernel(m_ref, x_ref, o_ref):
    o_ref[0] = _dot(m_ref[...], x_ref[0]).astype(o_ref.dtype)


def _left_mm(mat, x, out_dtype, lane_blk):
    g, k, n = x.shape
    m = mat.shape[0]
    lane_blk = min(lane_blk, n)
    return pl.pallas_call(
        _left_mm_kernel, grid=(g, n // lane_blk),
        in_specs=[_resident((m, k)), pl.BlockSpec((1, k, lane_blk), lambda i, j: (i, 0, j))],
        out_specs=pl.BlockSpec((1, m, lane_blk), lambda i, j: (i, 0, j)),
        out_shape=jax.ShapeDtypeStruct((g, m, n), out_dtype),
        compiler_params=_cparams(("parallel", "parallel")),
        name="left_mm",
    )(mat, x)


def _cs(num, den, rows, cols):
    ang = 2.0 * np.pi * np.outer(np.arange(rows), np.arange(cols)) * (num / den)
    return np.cos(ang), np.sin(ang)


def _twiddle(n1, n2):
    k1 = lax.broadcasted_iota(jnp.int32, (n1, n2, 1), 0)
    m2 = lax.broadcasted_iota(jnp.int32, (n1, n2, 1), 1)
    ang = (k1 * m2).astype(F32) * (2.0 * math.pi / (n1 * n2))
    return jnp.cos(ang), jnp.sin(ang)


def _hy_filt_t_kernel(z_ref, msk_ref, w1_ref, b1_ref, fr_ref, w2_ref, b2_ref, w3_ref, ld_ref, sk_ref, o_ref):
    hi = lax.Precision.HIGHEST
    z = z_ref[...]
    fr = fr_ref[...]
    hid = jnp.sin(fr * (jnp.dot(w1_ref[...], z, preferred_element_type=F32, precision=hi) + b1_ref[...]))
    hid = jnp.sin(fr * (jnp.dot(w2_ref[...], hid, preferred_element_type=F32, precision=hi) + b2_ref[...]))
    h = jnp.dot(w3_ref[0], hid, preferred_element_type=F32, precision=hi)
    h = h * jnp.exp(-jnp.exp(ld_ref[0]) * z[0:1, :])
    msk = msk_ref[...]
    h = h * msk[0:1, :] + sk_ref[...] * msk[1:2, :]
    for s in range(o_ref.shape[1]):
        o_ref[:, s, :] = h[:, s * FFT_N2:(s + 1) * FFT_N2]


def _hy_filters_t(length, w1, b1, freq, w2, b2, w3, log_decay, skip):
    n = 2 * length
    tt = jnp.arange(n, dtype=jnp.int32)
    pos = jnp.where(tt < length, tt, n - tt).astype(F32)
    t = pos / max(length - 1, 1)
    w = 2.0 * math.pi * pos / length
    f = jnp.linspace(1e-4, HY_BANDS - 1, HY_BANDS, dtype=F32)
    ang = f[:, None] * w[None, :]
    z = jnp.concatenate([t[None, :], jnp.cos(ang), -jnp.sin(ang)], 0)
    z = jnp.pad(z, ((0, LANES - z.shape[0]), (0, 0)))
    msk = jnp.stack([(tt != length).astype(F32), (tt == 0).astype(F32)])
    msk = jnp.pad(msk, ((0, 6), (0, 0)))
    col = lambda a: a.reshape(-1, 1)
    c2 = 2 * HY_CH
    nf = w3.shape[0]
    w3d = w3.reshape(nf, 2, 2, HY_CH).transpose(2, 1, 3, 0).reshape(2, c2, nf)
    ldd = log_decay.reshape(2, 2, HY_CH).transpose(1, 0, 2).reshape(2, c2, 1)
    rows = 8
    pb = rows * FFT_N2
    half = length // pb
    return pl.pallas_call(
        _hy_filt_t_kernel,
        grid=(n // pb,),
        in_specs=[pl.BlockSpec((LANES, pb), lambda i: (0, i)),
                  pl.BlockSpec((8, pb), lambda i: (0, i)),
                  _resident((nf, LANES)), _resident((nf, 1)), _resident((nf, 1)),
                  _resident((nf, nf)), _resident((nf, 1)),
                  pl.BlockSpec((1, c2, nf), lambda i: (i // half, 0, 0)),
                  pl.BlockSpec((1, c2, 1), lambda i: (i // half, 0, 0)),
                  _resident((c2, 1))],
        out_specs=pl.BlockSpec((c2, rows, FFT_N2), lambda i: (0, i, 0)),
        out_shape=jax.ShapeDtypeStruct((c2, n // FFT_N2, FFT_N2), F32),
        compiler_params=_cparams(("parallel",)),
        name="hy_filters_t",
    )(z, msk, _pad2(w1.T, nf, LANES), col(b1), col(freq), w2.T, col(b2), w3d, ldd, skip.reshape(c2, 1))


def _dft_mats():
    n = FFT_N2
    c, s = _cs(1, n, n, n)
    ch, sh = c[:, :n // 2], s[:, :n // 2]
    f1_pair = np.block([[ch, sh], [-sh, ch]])
    f1_full = np.concatenate([c, -s], 0)
    m2r = np.block([[c, -s], [s, c]])
    m2i = np.block([[c, s], [-s, c]])
    f1_inv = np.block([[ch.T, -sh.T], [sh.T, ch.T]]) / (n * n)
    cast = lambda a: jnp.asarray(a, dtype=BF16)
    return cast(f1_pair), cast(f1_full), cast(m2r), cast(m2i), cast(f1_inv)


def _twiddle2d(n):
    k1 = lax.broadcasted_iota(jnp.int32, (n, n), 0)
    m2 = lax.broadcasted_iota(jnp.int32, (n, n), 1)
    ang = (k1 * m2).astype(F32) * (2.0 * math.pi / (n * n))
    return jnp.cos(ang), jnp.sin(ang)


def _fwd_spectrum(xs, f1, m2r, c, s):
    n = FFT_N2
    a = _dot(f1, jnp.concatenate(xs, axis=1))
    ts = []
    for g in range(len(xs)):
        ar = a[:n, g * n:(g + 1) * n]
        ai = a[n:, g * n:(g + 1) * n]
        ts.append(jnp.concatenate([ar * c + ai * s, ai * c - ar * s], axis=1))
    return _dot(jnp.concatenate(ts, axis=0).astype(BF16), m2r)


def _filter_spec_t_kernel(t_ref, f1_ref, m2r_ref, c_ref, s_ref, o_ref):
    g = t_ref.shape[0]
    xs = [t_ref[i].astype(BF16) for i in range(g)]
    spec = _fwd_spectrum(xs, f1_ref[...], m2r_ref[...], c_ref[...], s_ref[...])
    for i in range(g):
        o_ref[i] = spec[i * FFT_N2:(i + 1) * FFT_N2]


def _filter_spec_t(taps, f1_full, m2r, twc, tws):
    nch, n1, n = taps.shape
    g = HY_GROUP
    return pl.pallas_call(
        _filter_spec_t_kernel,
        grid=(nch // g,),
        in_specs=[pl.BlockSpec((g, n1, n), lambda i: (i, 0, 0)),
                  _resident(f1_full.shape), _resident(m2r.shape), _resident((n, n)), _resident((n, n))],
        out_specs=pl.BlockSpec((g, n, 2 * n), lambda i: (i, 0, 0)),
        out_shape=jax.ShapeDtypeStruct((nch, n, 2 * n), F32),
        compiler_params=_cparams(("parallel",)),
        name="filter_spec_t",
    )(taps, f1_full, m2r, twc, tws)


def _hyena_core_kernel(x1_ref, x2_ref, v_ref, w1_ref, w2_ref, wv_ref, h0_ref, h1_ref,
                       f1_ref, m2r_ref, m2i_ref, f1i_ref, c_ref, s_ref, o_ref):
    n = FFT_N2
    n1 = v_ref.shape[2]
    c = c_ref[...]
    s = s_ref[...]
    lane = lax.broadcasted_iota(jnp.int32, (n1, n), 1)
    row = lax.broadcasted_iota(jnp.int32, (n1, n), 0)
    first_lane, last_lane = lane == 0, lane == n - 1
    seq_start, seq_end = first_lane & (row == 0), last_lane & (row == n1 - 1)

    def short_conv(x, w):
        r = pltpu.roll(x, 1, axis=1)
        up = jnp.where(first_lane, pltpu.roll(r, 1, axis=0), r)
        up = jnp.where(seq_start, 0.0, up)
        l = pltpu.roll(x, n - 1, axis=1)
        dn = jnp.where(last_lane, pltpu.roll(l, n1 - 1, axis=0), l)
        dn = jnp.where(seq_end, 0.0, dn)
        return up * w[0:1] + x * w[1:2] + dn * w[2:3]

    def conv_all(xss, h_ref):
        spec_s = [_fwd_spectrum(xs, f1_ref[...], m2r_ref[...], c, s) for xs in xss]
        y_s = []
        for chain, spec in enumerate(spec_s):
            ys = []
            for g in range(HY_GROUP):
                xr = spec[g * n:(g + 1) * n, :n]
                xi = spec[g * n:(g + 1) * n, n:]
                hh = h_ref[chain * HY_GROUP + g]
                hr, hi = hh[:, :n], hh[:, n:]
                ys.append(jnp.concatenate([xr * hr - xi * hi, xr * hi + xi * hr], axis=1))
            y_s.append(jnp.concatenate(ys, axis=0).astype(BF16))
        bm_s = [_dot(y, m2i_ref[...]) for y in y_s]
        b_s = []
        for bm in bm_s:
            bs = []
            for g in range(HY_GROUP):
                br = bm[g * n:(g + 1) * n, :n]
                bi = bm[g * n:(g + 1) * n, n:]
                bs.append(jnp.concatenate([br * c - bi * s, bi * c + br * s], axis=0))
            b_s.append(jnp.concatenate(bs, axis=1).astype(BF16))
        outs = [_dot(f1i_ref[...], bc) for bc in b_s]
        return [[y[:, g * n:(g + 1) * n] for g in range(HY_GROUP)] for y in outs]

    def pair(ref, w_ref, ch):
        return jnp.concatenate([short_conv(ref[0, ch], w_ref[ch]), short_conv(ref[1, ch], w_ref[ch])], axis=0)

    chans = [[chain * HY_GROUP + g for g in range(HY_GROUP)] for chain in range(HY_CHAINS)]
    y1 = conv_all([[pair(v_ref, wv_ref, ch).astype(BF16) for ch in grp] for grp in chans], h0_ref)
    z = [[(y1[k][g] * pair(x1_ref, w1_ref, ch)).astype(BF16) for g, ch in enumerate(grp)]
         for k, grp in enumerate(chans)]
    y2 = conv_all(z, h1_ref)
    for k, grp in enumerate(chans):
        for g, ch in enumerate(grp):
            y = y2[k][g] * pair(x2_ref, w2_ref, ch)
            o_ref[0, :, ch, :] = y[:n1]
            o_ref[1, :, ch, :] = y[n1:]


def _hyena_core(ut, conv_w, spec, mats, twc, tws):
    f1_pair, _, m2r, m2i, f1_inv = mats
    b, nch, n1, n = ut.shape
    assert b % 2 == 0
    cb = HY_CBLK
    nblk = HY_CH // cb
    wt = jnp.broadcast_to(conv_w.T[:, :, None], (nch, conv_w.shape[0], n))
    xspec = lambda off: pl.BlockSpec((2, cb, n1, n), lambda j, i: (i, off * nblk + j, 0, 0))
    wspec = lambda off: pl.BlockSpec((cb, conv_w.shape[0], n), lambda j, i: (off * nblk + j, 0, 0))
    hspec = lambda off: pl.BlockSpec((cb, n, 2 * n), lambda j, i: (off * nblk + j, 0, 0))
    return pl.pallas_call(
        _hyena_core_kernel,
        grid=(nblk, b // 2),
        in_specs=[xspec(0), xspec(1), xspec(2), wspec(0), wspec(1), wspec(2), hspec(0), hspec(1),
                  _resident(f1_pair.shape), _resident(m2r.shape), _resident(m2i.shape), _resident(f1_inv.shape),
                  _resident((n, n)), _resident((n, n))],
        out_specs=pl.BlockSpec((2, n1, cb, n), lambda j, i: (i, 0, j, 0)),
        out_shape=jax.ShapeDtypeStruct((b, n1, HY_CH, n), F32),
        compiler_params=_cparams(("parallel", "parallel")),
        name="hyena_core",
    )(ut, ut, ut, wt, wt, wt, spec, spec, f1_pair, m2r, m2i, f1_inv, twc, tws)


def _hy_ctx_kernel(v_ref, x1_ref, x2_ref, f_ref, fi_ref, h_ref, o_ref):
    nf = f_ref.shape[0] // 2
    zin = v_ref[0]
    gates = (x1_ref, x2_ref)
    for o in range(2):
        x = _dot(f_ref[...], zin)
        xr, xi = x[:nf], x[nf:]
        hr = h_ref[o, :nf]
        hi = h_ref[o, nf:]
        y = jnp.concatenate([xr * hr - xi * hi, xr * hi + xi * hr], axis=0).astype(BF16)
        zin = (_dot(fi_ref[...], y) * gates[o][0].astype(F32)).astype(BF16)
    o_ref[0] = zin


def _hy_ctx(v, x1, x2, taps):
    b, lc, c = v.shape
    nf = 2 * lc
    cm, sm = _cs(1, nf, nf, nf)
    fwd = jnp.asarray(np.concatenate([cm[:, :lc], -sm[:, :lc]], 0), dtype=BF16)
    fwd_full = jnp.asarray(np.concatenate([cm, -sm], 0), dtype=BF16)
    inv = jnp.asarray(np.concatenate([cm[:lc, :], -sm[:lc, :]], 1) / nf, dtype=BF16)
    spec = _left_mm(fwd_full, taps.astype(BF16).reshape(1, nf, 2 * c), F32, 2 * c)
    spec = spec.reshape(2 * nf, 2, c).transpose(1, 0, 2)
    blk = pl.BlockSpec((1, lc, c), lambda i: (i, 0, 0))
    return pl.pallas_call(
        _hy_ctx_kernel,
        grid=(b,),
        in_specs=[blk, blk, blk, _resident((2 * nf, lc)), _resident((lc, 2 * nf)),
                  _resident((2, 2 * nf, c))],
        out_specs=blk,
        out_shape=jax.ShapeDtypeStruct((b, lc, c), BF16),
        compiler_params=_cparams(("parallel",)),
        name="hy_ctx",
    )(v, x1, x2, fwd, inv, spec)


def _pair_rows(q2):
    lane = lax.broadcasted_iota(jnp.int32, q2.shape, 1)
    zero = jnp.zeros_like(q2)
    return jnp.concatenate([jnp.where(lane < NA_HD, q2, zero), jnp.where(lane >= NA_HD, q2, zero)], axis=0)


def _unpair_rows(o):
    r = o.shape[0] // 2
    lane = lax.broadcasted_iota(jnp.int32, (r, o.shape[1]), 1)
    return jnp.where(lane < NA_HD, o[:r], o[r:])


def _pair_softmax_pv(scores, values):
    m = scores[0].max(axis=-1, keepdims=True)
    for s in scores[1:]:
        m = jnp.maximum(m, s.max(axis=-1, keepdims=True))
    den = None
    acc = None
    for s, v in zip(scores, values):
        p = jnp.exp(s - m)
        d = p.sum(axis=-1, keepdims=True)
        a = _dot(p.astype(BF16), v)
        den = d if den is None else den + d
        acc = a if acc is None else acc + a
    return acc / den


def _natten_kernel(q_ref, k0, k1, k2, k3, v0, v1, v2, v3, kc_ref, vc_ref, bias_ref, o_ref,
                   kwin, vwin, *, rows):
    g = pl.program_id(1)
    rb = 4 * GRID_W
    for i, (kr, vr) in enumerate(((k0, v0), (k1, v1), (k2, v2), (k3, v3))):
        kwin[i * rb:(i + 1) * rb, :] = kr[0]
        vwin[i * rb:(i + 1) * rb, :] = vr[0]
    base = 4 * jnp.clip(2 * g - 1, 0, rows // 4 - 4)
    nwin = NA_WIN_R * GRID_W
    qscale = jnp.asarray(NA_HD ** -0.5, BF16)

    ones_lat = jnp.ones((nwin, LANES), BF16)
    ones_ctx = jnp.ones((kc_ref.shape[1], LANES), BF16)

    def rows_body(it, carry):
        work = []
        for u in range(NA_ROWS_PER_TRIP):
            rr = it * NA_ROWS_PER_TRIP + u
            r = 8 * g + rr
            rs = jnp.clip(r - NA_WIN_R // 2, 0, rows - NA_WIN_R)
            st = pl.multiple_of((rs - base) * GRID_W, GRID_W)
            qo = pl.multiple_of(rr * GRID_W, GRID_W)
            work += [(qo, st, rs - r + NA_WIN_R - 1, p) for p in range(NA_HEADS // 2)]
        scores = []
        for qo, st, d0, p in work:
            ls = slice(p * LANES, (p + 1) * LANES)
            qp = _pair_rows(q_ref[0, pl.ds(qo, GRID_W), ls] * qscale)
            scores.append((_dot_nt(qp, kwin[pl.ds(st, nwin), ls]) + bias_ref[d0, p].astype(F32),
                           _dot_nt(qp, kc_ref[0, :, ls])))
        maxima = [jnp.maximum(a.max(axis=-1, keepdims=True), b.max(axis=-1, keepdims=True)) for a, b in scores]
        for (qo, st, d0, p), (s_lat, s_ctx), m in zip(work, scores, maxima):
            ls = slice(p * LANES, (p + 1) * LANES)
            v_lat = jnp.concatenate([vwin[pl.ds(st, nwin), ls], ones_lat], axis=1)
            v_ctx = jnp.concatenate([vc_ref[0, :, ls], ones_ctx], axis=1)
            acc = _dot(jnp.exp(s_lat - m).astype(BF16), v_lat) + _dot(jnp.exp(s_ctx - m).astype(BF16), v_ctx)
            o = acc[:, :LANES] / acc[:, LANES:LANES + 1]
            o_ref[0, pl.ds(qo, GRID_W), ls] = _unpair_rows(o).astype(o_ref.dtype)
        return carry

    lax.fori_loop(0, 8 // NA_ROWS_PER_TRIP, rows_body, 0)


def _natten_bias(rpb):
    c = np.arange(GRID_W)[:, None]
    kc = np.arange(GRID_W)[None, :]
    cs = np.clip(c - NA_WIN_C // 2, 0, GRID_W - NA_WIN_C)
    valid = (kc >= cs) & (kc < cs + NA_WIN_C)
    dc = np.clip(kc - c + NA_WIN_C - 1, 0, 2 * NA_WIN_C - 2)
    tb = jnp.where(valid[None, None], rpb[:, :, dc], NEG_BIG)
    slabs = []
    for d0 in range(NA_WIN_R):
        s = tb[:, d0:d0 + NA_WIN_R]
        s = s.transpose(0, 2, 1, 3).reshape(NA_HEADS, GRID_W, NA_WIN_R * GRID_W)
        slabs.append(s.reshape(NA_HEADS // 2, 2 * GRID_W, NA_WIN_R * GRID_W))
    return jnp.stack(slabs).astype(BF16)


def _natten(uq, uc, bias):
    b, length, _ = uq.shape
    c = NA_HEADS * NA_HD
    rows = length // GRID_W
    rb = 4 * GRID_W
    nkb = length // rb
    qrows = 8 * GRID_W
    lc = uc.shape[1]

    def kv_spec(col, off):
        return pl.BlockSpec((1, rb, c), lambda i, g: (i, jnp.clip(2 * g - 1, 0, nkb - 4) + off, col))

    return pl.pallas_call(
        functools.partial(_natten_kernel, rows=rows),
        grid=(b, rows // 8),
        in_specs=[pl.BlockSpec((1, qrows, c), lambda i, g: (i, g, 0))]
                 + [kv_spec(1, o) for o in range(4)] + [kv_spec(2, o) for o in range(4)]
                 + [pl.BlockSpec((1, lc, c), lambda i, g: (i, 0, 4)),
                    pl.BlockSpec((1, lc, c), lambda i, g: (i, 0, 5)),
                    _resident(bias.shape)],
        out_specs=pl.BlockSpec((1, qrows, c), lambda i, g: (i, g, 0)),
        out_shape=jax.ShapeDtypeStruct((b, length, c), BF16),
        scratch_shapes=[pltpu.VMEM((4 * rb, c), BF16), pltpu.VMEM((4 * rb, c), BF16)],
        compiler_params=_cparams(("parallel", "parallel")),
        name="natten",
    )(uq, *([uq] * 8), uc, uc, bias)


def _ctx_attn_kernel(q_ref, k_ref, v_ref, o_ref):
    qscale = jnp.asarray(NA_HD ** -0.5, BF16)
    for p in range(NA_HEADS // 2):
        ls = slice(p * LANES, (p + 1) * LANES)
        qp = _pair_rows(q_ref[0, :, ls] * qscale)
        o = _pair_softmax_pv([_dot_nt(qp, k_ref[0, :, ls])], [v_ref[0, :, ls]])
        o_ref[0, :, ls] = _unpair_rows(o).astype(o_ref.dtype)


def _ctx_attn(u):
    b, lc, _ = u.shape
    c = NA_HEADS * NA_HD
    spec = lambda col: pl.BlockSpec((1, lc, c), lambda i: (i, 0, col))
    return pl.pallas_call(
        _ctx_attn_kernel,
        grid=(b,),
        in_specs=[spec(3), spec(4), spec(5)],
        out_specs=pl.BlockSpec((1, lc, c), lambda i: (i, 0, 0)),
        out_shape=jax.ShapeDtypeStruct((b, lc, c), BF16),
        compiler_params=_cparams(("parallel",)),
        name="ctx_attn",
    )(u, u, u)


def _post_kernel(x_ref, mod_ref, ya_ref, yb_ref, wa_ref, wb_ref, lng_ref, lnb_ref, w1_ref, w2_ref, o_ref, *, ya_t):
    m = mod_ref[0, 0]
    lng = lng_ref[...]
    lnb = lnb_ref[...]
    ff = w1_ref.shape[1]
    step = MLP_CHUNK
    sub = ROW_BLK
    nsub = x_ref.shape[1] // sub

    def head(t):
        rs = slice(t * sub, (t + 1) * sub)
        if ya_t:
            tiles = range(t * sub // LANES, (t + 1) * sub // LANES)
            ya = jnp.concatenate([ya_ref[0, s].T for s in tiles], axis=0).astype(BF16)
        else:
            ya = ya_ref[0, rs, :]
        y = _dot(ya, wa_ref[...]) + _dot(yb_ref[0, rs, :], wb_ref[...])
        x1 = _norm_rows(ALPHA * x_ref[0, rs, :] + m[2:3] * y) * lng[0:1] + lnb[0:1]
        return x1, (_norm_rows(x1) * (1.0 + m[4:5]) + m[3:4]).astype(BF16)

    def mlp(h):
        acc = None
        for c in range(ff // step):
            a = jnp.maximum(_dot(h, w1_ref[:, c * step:(c + 1) * step]), 0.0)
            d = _dot((a * a).astype(BF16), w2_ref[c * step:(c + 1) * step, :])
            acc = d if acc is None else acc + d
        return acc

    cur = head(0)
    for t in range(nsub):
        nxt = head(t + 1) if t + 1 < nsub else None
        acc = mlp(cur[1])
        o_ref[0, t * sub:(t + 1) * sub, :] = _norm_rows(ALPHA * cur[0] + m[5:6] * acc) * lng[1:2] + lnb[1:2]
        cur = nxt


def _post(x, modtab, mod_row, ya, yb, wa, wb, lng, lnb, w1, w2, ya_t=False):
    b, r, d = x.shape
    ka, kb = wa.shape[0], wb.shape[0]
    rows = min(r, 2 * ROW_BLK)
    row = lambda k: pl.BlockSpec((1, rows, k), lambda i, j: (i, j, 0))
    ya_spec = pl.BlockSpec((1, rows // LANES, ka, LANES), lambda i, j: (i, j, 0, 0)) if ya_t else row(ka)
    return pl.pallas_call(
        functools.partial(_post_kernel, ya_t=ya_t),
        grid=(b, r // rows),
        in_specs=[row(d), pl.BlockSpec((1, 1, 6, d), lambda i, j: (i, mod_row, 0, 0)), ya_spec, row(kb),
                  _resident(wa.shape), _resident(wb.shape), _resident(lng.shape), _resident(lnb.shape),
                  _resident(w1.shape), _resident(w2.shape)],
        out_specs=row(d),
        out_shape=jax.ShapeDtypeStruct((b, r, d), F32),
        compiler_params=_cparams(("parallel", "parallel")),
        name="post_mixer",
    )(x, modtab, ya, yb, wa, wb, lng, lnb, w1, w2)


def _rope(x, cos, sinl, sinr):
    reps = x.shape[1] // LANES
    tile = lambda t: jnp.concatenate([t] * reps, axis=1)
    n = x.shape[1]
    quarter = MLA_ROPE // 4
    return (x * tile(cos) + pltpu.roll(x, n - quarter, axis=1) * tile(sinl)
            + pltpu.roll(x, quarter, axis=1) * tile(sinr))


def _front_cd_kernel(x_ref, xc_ref, mod_ref, w_ref, qn_ref, kvn_ref, wuq_ref, wuk_ref, wuv_ref, epe_ref, one_ref,
                     fng_ref, fnb_ref, avg_ref, cbd_ref, sbd_ref,
                     cq_ref, slq_ref, srq_ref, ck_ref, slk_ref, srk_ref,
                     q_ref, k_ref, v_ref, p_ref, qf_ref, *, ctx_blk):
    m = mod_ref[0, 0]
    is_ctx = (jnp.zeros((ROW_BLK, 1), jnp.int32) + pl.program_id(1)) == ctx_blk
    x = jnp.where(is_ctx, xc_ref[0], x_ref[0])
    h = _norm_rows(x) * (1.0 + m[1:2]) + m[0:1]
    u = _dot(h.astype(BF16), w_ref[...])
    o_kv = MLA_Q_RANK
    o_fn = o_kv + MLA_KV_RANK
    o_pe = o_fn + FN_CH

    def rms(x, g):
        return x * lax.rsqrt(jnp.mean(x * x, axis=-1, keepdims=True) + LN_EPS) * g

    cq = rms(u[:, :o_kv], qn_ref[...]).astype(BF16)
    q = _dot(cq, wuq_ref[...])
    q_ref[0] = _rope(q, cq_ref[...], slq_ref[...], srq_ref[...]).astype(BF16)

    ckv = rms(u[:, o_kv:o_fn], kvn_ref[...]).astype(BF16)
    kpe = _dot(u[:, o_pe:].astype(BF16), epe_ref[...])
    k = _dot(ckv, wuk_ref[...]) + _rope(kpe, ck_ref[...], slk_ref[...], srk_ref[...])
    k_ref[0] = k.astype(BF16)
    v_ref[0] = (_dot(ckv, wuv_ref[...]) + one_ref[...]).astype(BF16)

    uf = u[:, o_fn:o_pe]
    avg = avg_ref[...]
    uc = uf - _dot(uf.astype(BF16), avg)
    var = _dot((uc * uc).astype(BF16), avg)
    ug = (uc * lax.rsqrt(var + LN_EPS) * fng_ref[...] + fnb_ref[...]).astype(BF16)
    p_ref[0] = _dot(ug, cbd_ref[...]).astype(BF16)
    qf_ref[0] = _dot(ug, sbd_ref[...]).astype(BF16)


def _rope_tables(length, lt, scale):
    t = jnp.arange(lt, dtype=jnp.int32)
    rows = (t // GRID_W).astype(F32)
    cols = (t % GRID_W).astype(F32)
    half = MLA_ROPE // 2
    inv = ROPE_THETA ** (-jnp.arange(0, half, 2, dtype=F32) / half)
    ar = rows[:, None] * inv[None, :]
    ac = cols[:, None] * inv[None, :]
    ang = jnp.concatenate([ar, ar, ac, ac], -1)
    is_lat = (t < length)[:, None]
    cos = jnp.where(is_lat, jnp.cos(ang), 1.0)
    sin = jnp.where(is_lat, jnp.sin(ang), 0.0)
    qd = MLA_ROPE // 4
    ones = jnp.ones((lt, MLA_NOPE), F32)
    zeros = jnp.zeros((lt, MLA_NOPE), F32)
    tail1 = jnp.ones((lt, HEAD_PAD - MLA_NOPE - MLA_ROPE), F32)
    tail0 = jnp.zeros((lt, HEAD_PAD - MLA_NOPE - MLA_ROPE), F32)
    z8 = jnp.zeros((lt, qd), F32)
    c = jnp.concatenate([ones, cos, tail1], -1)
    sl = jnp.concatenate([zeros, -sin[:, :qd], z8, -sin[:, 2 * qd:3 * qd], z8, tail0], -1)
    sr = jnp.concatenate([zeros, z8, sin[:, qd:2 * qd], z8, sin[:, 3 * qd:], tail0], -1)
    return c * scale, sl * scale, sr * scale


def _head_slots(w, per_head, take_from, take_n):
    k = w.shape[0]
    w3 = w.reshape(k, MLA_HEADS, per_head)[:, :, take_from:take_from + take_n]
    w3 = jnp.pad(w3, ((0, 0), (0, 0), (0, HEAD_PAD - take_n)))
    return w3.reshape(k, MLA_HEADS * HEAD_PAD)


def _front_cd(xl, xc, modtab, w_in, q_norm, w_uq, kv_norm, w_ukv, fn_g, fn_b):
    b, length, d = xl.shape
    lt = length + xc.shape[1]
    nlat = length // ROW_BLK
    o_kv = MLA_Q_RANK
    o_pe = o_kv + MLA_KV_RANK
    o_fn = o_pe + MLA_ROPE
    hw = MLA_HEADS * HEAD_PAD
    w_perm = jnp.concatenate([w_in[:, :o_pe], w_in[:, o_fn:], w_in[:, o_pe:o_fn],
                              jnp.zeros((d, LANES - MLA_ROPE), w_in.dtype)], -1).astype(BF16)
    wuq = _head_slots(w_uq, MLA_NOPE + MLA_ROPE, 0, MLA_NOPE + MLA_ROPE).astype(BF16)
    wuk = _head_slots(w_ukv, MLA_NOPE + MLA_V, 0, MLA_NOPE).astype(BF16)
    wuv = _head_slots(w_ukv, MLA_NOPE + MLA_V, MLA_NOPE, MLA_V).astype(BF16)
    epe = np.zeros((LANES, hw), np.float32)
    for hd in range(MLA_HEADS):
        for i in range(MLA_ROPE):
            epe[i, hd * HEAD_PAD + MLA_NOPE + i] = 1.0
    epe = jnp.asarray(epe, dtype=BF16)
    ones_col = np.zeros((1, hw), np.float32)
    ones_col[0, MLA_V::HEAD_PAD] = 1.0
    ones_col = jnp.asarray(ones_col)
    cm, sm = _cs(1, FN_GD, FN_GD, FN_GD)
    eye = np.eye(FN_GROUPS)
    cbd = jnp.asarray(np.kron(eye, cm), dtype=BF16)
    sbd = jnp.asarray(np.kron(eye, -sm), dtype=BF16)
    avg = jnp.asarray(np.kron(eye, np.full((FN_GD, FN_GD), 1.0 / FN_GD)), dtype=BF16)
    qtab = _rope_tables(length, lt, (MLA_NOPE + MLA_ROPE) ** -0.5 * math.log2(math.e))
    ktab = _rope_tables(length, lt, 1.0)
    row = lambda n: pl.BlockSpec((1, ROW_BLK, n), lambda i, j: (i, j, 0))
    tab = pl.BlockSpec((ROW_BLK, HEAD_PAD), lambda i, j: (j, 0))
    out = lambda n: jax.ShapeDtypeStruct((b, lt, n), BF16)
    return pl.pallas_call(
        functools.partial(_front_cd_kernel, ctx_blk=nlat),
        grid=(b, lt // ROW_BLK),
        in_specs=[pl.BlockSpec((1, ROW_BLK, d), lambda i, j: (i, jnp.minimum(j, nlat - 1), 0)),
                  pl.BlockSpec((1, ROW_BLK, d), lambda i, j: (i, 0, 0)),
                  pl.BlockSpec((1, 1, 6, d), lambda i, j: (i, j // nlat, 0, 0)),
                  _resident(w_perm.shape), _resident((1, MLA_Q_RANK)), _resident((1, MLA_KV_RANK)),
                  _resident(wuq.shape), _resident(wuk.shape), _resident(wuv.shape), _resident(epe.shape),
                  _resident(ones_col.shape),
                  _resident((1, FN_CH)), _resident((1, FN_CH)), _resident(avg.shape), _resident(cbd.shape),
                  _resident(sbd.shape),
                  tab, tab, tab, tab, tab, tab],
        out_specs=[row(hw), row(hw), row(hw), row(FN_CH), row(FN_CH)],
        out_shape=[out(hw), out(hw), out(hw), out(FN_CH), out(FN_CH)],
        compiler_params=_cparams(("parallel", "parallel")),
        name="front_cd",
    )(xl, xc, modtab, w_perm, q_norm.reshape(1, -1), kv_norm.reshape(1, -1), wuq, wuk, wuv, epe, ones_col,
      fn_g.reshape(1, -1), fn_b.reshape(1, -1), avg, cbd, sbd, *qtab, *ktab)


def _mla_kernel(q_ref, k_ref, v_ref, o_ref, *, rows):
    n = q_ref.shape[1] // rows

    def scores(i):
        s = _dot_nt(q_ref[0, i * rows:(i + 1) * rows, :], k_ref[0])
        return s, s.max(axis=-1, keepdims=True)

    def finish(i, s, m):
        acc = _dot(jnp.exp2(s - m).astype(BF16), v_ref[0])
        o_ref[0, i * rows:(i + 1) * rows, :] = (acc / acc[:, MLA_V:MLA_V + 1]).astype(o_ref.dtype)

    pending = scores(0)
    for i in range(n):
        nxt = scores(i + 1) if i + 1 < n else None
        finish(i, *pending)
        pending = nxt


def _mla_attention(q, k, v, length):
    b, lt, hw = q.shape
    heads = hw // HEAD_PAD
    tq = MLA_Q_BLK
    kv = pl.BlockSpec((1, lt, HEAD_PAD), lambda i, h, j: (i, 0, h))
    qs = pl.BlockSpec((1, tq, HEAD_PAD), lambda i, h, j: (i, j, h))
    return pl.pallas_call(
        functools.partial(_mla_kernel, rows=MLA_Q_SUB),
        grid=(b, heads, length // tq),
        in_specs=[qs, kv, kv],
        out_specs=qs,
        out_shape=jax.ShapeDtypeStruct((b, length, hw), BF16),
        compiler_params=_cparams(("parallel", "parallel", "parallel")),
        name="mla_attention",
    )(q, k, v)


def _fn1_kernel(m_ref, zr_ref, zi_ref, o_ref):
    n = zr_ref.shape[1]
    a = _dot(m_ref[...], jnp.concatenate([zr_ref[0], zi_ref[0]], axis=0))
    o_ref[0, 0] = a[:n].astype(o_ref.dtype)
    o_ref[0, 1] = a[n:].astype(o_ref.dtype)


def _fn2_kernel(a_ref, c_ref, s_ref, m_ref, o_ref):
    for i in range(a_ref.shape[2]):
        ar = a_ref[0, 0, i].astype(F32)
        ai = a_ref[0, 1, i].astype(F32)
        c = c_ref[i]
        s = s_ref[i]
        t = jnp.concatenate([ar * c + ai * s, ai * c - ar * s], axis=0).astype(BF16)
        o_ref[0, i] = _dot(m_ref[...], t).astype(o_ref.dtype)


def _fnet(p, qn):
    b, length, c = p.shape
    n1 = 128
    n2 = length // n1
    cm, sm = _cs(1, n1, n1, n1)
    m1 = jnp.asarray(np.block([[cm, sm], [-sm, cm]]), dtype=BF16)
    c2, s2 = _cs(1, n2, n2, n2)
    m2 = jnp.asarray(np.concatenate([c2, s2], 1) / math.sqrt(length * FN_GD), dtype=BF16)
    twc, tws = _twiddle(n1, n2)
    lane_blk = 2048
    zs = pl.BlockSpec((1, n1, lane_blk), lambda i, j: (i, 0, j))
    a = pl.pallas_call(
        _fn1_kernel,
        grid=(b, n2 * c // lane_blk),
        in_specs=[_resident(m1.shape), zs, zs],
        out_specs=pl.BlockSpec((1, 2, n1, lane_blk), lambda i, j: (i, 0, 0, j)),
        out_shape=jax.ShapeDtypeStruct((b, 2, n1, n2 * c), BF16),
        compiler_params=_cparams(("parallel", "parallel")),
        name="fnet_stage1",
    )(m1, p.reshape(b, n1, n2 * c), qn.reshape(b, n1, n2 * c))
    kb = 8
    y = pl.pallas_call(
        _fn2_kernel,
        grid=(b, n1 // kb),
        in_specs=[pl.BlockSpec((1, 2, kb, n2, c), lambda i, j: (i, 0, j, 0, 0)),
                  pl.BlockSpec((kb, n2, 1), lambda i, j: (j, 0, 0)),
                  pl.BlockSpec((kb, n2, 1), lambda i, j: (j, 0, 0)),
                  _resident(m2.shape)],
        out_specs=pl.BlockSpec((1, kb, n2, c), lambda i, j: (i, j, 0, 0)),
        out_shape=jax.ShapeDtypeStruct((b, n1, n2, c), BF16),
        compiler_params=_cparams(("parallel", "parallel")),
        name="fnet_stage2",
    )(a.reshape(b, 2, n1, n2, c), twc, tws, m2)
    return y.transpose(0, 2, 1, 3).reshape(b, length, c)


def kernel(x, c, ctx, c_ctx, mod_w, mod_b, ln_g, ln_b, mlp_w1, mlp_w2,
           ab_w_in, ab_w_out, hy_conv_w, hy_w1, hy_b1, hy_freq, hy_w2, hy_b2, hy_w3, hy_log_decay, hy_skip, na_rpb,
           cd_w_in, cd_w_out, mla_q_norm, mla_w_uq, mla_kv_norm, mla_w_ukv, fn_norm_g, fn_norm_b):
    b, length, d = x.shape
    lc = ctx.shape[1]

    cc = jnp.concatenate([c, c_ctx[None], jnp.zeros((8 - b - 1, d), F32)], 0)
    mods = _mod_vectors(cc, mod_w, mod_b).reshape(DEPTH, 8, 6, d)
    modtab = [jnp.stack([mods[l, :b], jnp.broadcast_to(mods[l, b], (b, 6, d))], axis=1) for l in range(DEPTH)]

    n_hy = 3 * HY_CH
    w_in = ab_w_in[0].astype(BF16)
    uq, ut = _front_ab_lat(x, modtab[0], w_in[:, n_hy:], w_in[:, :n_hy].T)
    uc = _front_ab_ctx(ctx, modtab[0], w_in)
    fargs = (hy_w1[0], hy_b1[0], hy_freq[0], hy_w2[0], hy_b2[0], hy_w3[0], hy_log_decay[0])
    mats = _dft_mats()
    twc, tws = _twiddle2d(FFT_N2)
    spec = _filter_spec_t(_hy_filters_t(length, *fargs, hy_skip[0]), mats[1], mats[2], twc, tws)
    y_hy_t = _hyena_core(ut, hy_conv_w[0], spec, mats, twc, tws)
    x1c, x2c, vc = _hy_prep(uc, hy_conv_w[0], 0, lc // ROW_BLK)
    y_hy_c = _hy_ctx(vc, x1c, x2c, _bidir_taps(_hy_filters(lc, *fargs), hy_skip[0], lc))
    y_na = _natten(uq, uc, _natten_bias(na_rpb[0]))
    y_na_c = _ctx_attn(uc)
    w_out = ab_w_out[0].astype(BF16)
    mlp = (ln_g[0], ln_b[0], mlp_w1[0].astype(BF16), mlp_w2[0].astype(BF16))
    xl = _post(x, modtab[0], 0, y_hy_t, y_na, w_out[:HY_CH], w_out[HY_CH:], *mlp, ya_t=True)
    xc = _post(ctx, modtab[0], 1, y_hy_c, y_na_c, w_out[:HY_CH], w_out[HY_CH:], *mlp)

    q, k, vv, p, qn = _front_cd(xl, xc, modtab[1], cd_w_in[0], mla_q_norm[0], mla_w_uq[0], mla_kv_norm[0],
                                mla_w_ukv[0], fn_norm_g[0], fn_norm_b[0])
    o = _mla_attention(q, k, vv, length)
    y_fn = _fnet(p[:, :length], qn[:, :length])
    w_out = cd_w_out[0]
    n_mla = MLA_HEADS * MLA_V
    wa = jnp.pad(w_out[:n_mla].reshape(MLA_HEADS, MLA_V, d), ((0, 0), (0, HEAD_PAD - MLA_V), (0, 0)))
    wa = wa.reshape(MLA_HEADS * HEAD_PAD, d).astype(BF16)
    return _post(xl, modtab[1], 0, o, y_fn, wa, w_out[n_mla:].astype(BF16),
                 ln_g[1], ln_b[1], mlp_w1[1].astype(BF16), mlp_w2[1].astype(BF16))
```

```python
import functools
import math

import numpy as np
import jax
import jax.numpy as jnp
from jax import lax
from jax.experimental import pallas as pl
from jax.experimental.pallas import tpu as pltpu

F32 = jnp.float32
BF16 = jnp.bfloat16

D_MODEL = 1024
DEPTH = 2
GRID_W = 64
HY_CH = 512
HY_EMB = 33
HY_BANDS = (HY_EMB - 1) // 2
NA_HEADS = 8
NA_HD = 64
NA_WIN_R = 8
NA_WIN_C = 16
MLA_HEADS = 8
MLA_Q_RANK = 384
MLA_KV_RANK = 256
MLA_NOPE = 64
MLA_ROPE = 32
MLA_V = 96
ROPE_THETA = 10000.0
FN_CH = 256
FN_GROUPS = 4
FN_GD = FN_CH // FN_GROUPS
D_FF = 4 * D_MODEL
ALPHA = (2.0 * DEPTH) ** 0.25
LN_EPS = 1e-5

LANES = 128
ROW_BLK = 256
HEAD_PAD = 128
FFT_N2 = 128
NA_ROWS_PER_TRIP = 8
FN_N1 = 128
FN_PAIRS = 16
FN_GROUP = 4
MLP_CHUNK = 1024
MLA_Q_BLK = 1024
MLA_Q_SUB = 256
HY_GROUP = 16
HY_CHAINS = 2
HY_CBLK = HY_GROUP * HY_CHAINS
VMEM_LIMIT = 56 * 1024 * 1024
NEG_BIG = -1e30


def _cparams(sem, vmem=VMEM_LIMIT):
    return pltpu.CompilerParams(dimension_semantics=sem, vmem_limit_bytes=vmem)


def _resident(shape):
    nd = len(shape)
    return pl.BlockSpec(shape, lambda *_: (0,) * nd, pipeline_mode=pl.Buffered(1))


def _norm_rows(x):
    mu = jnp.mean(x, axis=-1, keepdims=True)
    xc = x - mu
    var = jnp.mean(xc * xc, axis=-1, keepdims=True)
    return xc * lax.rsqrt(var + LN_EPS)


def _dot(a, b):
    return jnp.dot(a, b, preferred_element_type=F32)


def _dot_nt(a, b):
    return lax.dot_general(a, b, (((1,), (1,)), ((), ())), preferred_element_type=F32)


def _mod_kernel(c_ref, w_ref, b_ref, o_ref):
    c = c_ref[...]
    s = c * (1.0 / (1.0 + jnp.exp(-c)))
    o_ref[0] = jnp.dot(s, w_ref[0], preferred_element_type=F32,
                       precision=lax.Precision.HIGHEST) + b_ref[0]


def _mod_vectors(cc, mod_w, mod_b):
    depth, d, n = mod_w.shape
    nb = 1024
    return pl.pallas_call(
        _mod_kernel,
        grid=(depth, n // nb),
        in_specs=[pl.BlockSpec((8, d), lambda l, j: (0, 0)),
                  pl.BlockSpec((1, d, nb), lambda l, j: (l, 0, j)),
                  pl.BlockSpec((1, 1, nb), lambda l, j: (l, 0, j))],
        out_specs=pl.BlockSpec((1, 8, nb), lambda l, j: (l, 0, j)),
        out_shape=jax.ShapeDtypeStruct((depth, 8, n), F32),
        compiler_params=_cparams(("parallel", "parallel")),
        name="mod_vectors",
    )(cc, mod_w, mod_b.reshape(depth, 1, n))


def _front_ab_ctx_kernel(x_ref, mod_ref, w_ref, u_ref):
    m = mod_ref[0, 0]
    h = _norm_rows(x_ref[0]) * (1.0 + m[1:2]) + m[0:1]
    u_ref[0] = _dot(h.astype(BF16), w_ref[...]).astype(BF16)


def _front_ab_ctx(xc, modtab, w_in):
    b, lc, d = xc.shape
    n = w_in.shape[1]
    return pl.pallas_call(
        _front_ab_ctx_kernel,
        grid=(b, lc // ROW_BLK),
        in_specs=[pl.BlockSpec((1, ROW_BLK, d), lambda i, j: (i, j, 0)),
                  pl.BlockSpec((1, 1, 6, d), lambda i, j: (i, 1, 0, 0)),
                  _resident((d, n))],
        out_specs=pl.BlockSpec((1, ROW_BLK, n), lambda i, j: (i, j, 0)),
        out_shape=jax.ShapeDtypeStruct((b, lc, n), BF16),
        compiler_params=_cparams(("parallel", "parallel")),
        name="front_ab_ctx",
    )(xc, modtab, w_in)


def _front_ab_lat_kernel(x_ref, mod_ref, wq_ref, wht_ref, u_ref, ut_ref):
    m = mod_ref[0, 0]
    hs = []
    for t in range(x_ref.shape[1] // ROW_BLK):
        rs = slice(t * ROW_BLK, (t + 1) * ROW_BLK)
        hs.append((_norm_rows(x_ref[0, rs, :]) * (1.0 + m[1:2]) + m[0:1]).astype(BF16))
        u_ref[0, rs, :] = _dot(hs[-1], wq_ref[...]).astype(BF16)
    h_all = jnp.concatenate(hs, axis=0)
    for c0 in range(0, wht_ref.shape[0], HY_CH):
        ut = _dot_nt(wht_ref[c0:c0 + HY_CH, :], h_all)
        for s in range(ut_ref.shape[2]):
            ut_ref[0, c0:c0 + HY_CH, s, :] = ut[:, s * FFT_N2:(s + 1) * FFT_N2]


def _front_ab_lat(x, modtab, w_qkv, w_hy_t):
    b, length, d = x.shape
    nq = w_qkv.shape[1]
    nh = w_hy_t.shape[0]
    rows = 8 * FFT_N2
    return pl.pallas_call(
        _front_ab_lat_kernel,
        grid=(b, length // rows),
        in_specs=[pl.BlockSpec((1, rows, d), lambda i, j: (i, j, 0)),
                  pl.BlockSpec((1, 1, 6, d), lambda i, j: (i, 0, 0, 0)),
                  _resident((d, nq)), _resident((nh, d))],
        out_specs=[pl.BlockSpec((1, rows, nq), lambda i, j: (i, j, 0)),
                   pl.BlockSpec((1, nh, 8, FFT_N2), lambda i, j: (i, 0, j, 0))],
        out_shape=[jax.ShapeDtypeStruct((b, length, nq), BF16),
                   jax.ShapeDtypeStruct((b, nh, length // FFT_N2, FFT_N2), F32)],
        compiler_params=_cparams(("parallel", "parallel")),
        name="front_ab_lat",
    )(x, modtab, w_qkv, w_hy_t)


def _hy_prep_kernel(cur_ref, prev_ref, next_ref, w_ref, x1_ref, x2_ref, v_ref, *, nblk):
    j = pl.program_id(1)
    cur = cur_ref[0].astype(F32)
    rows = cur.shape[0]
    has_prev = (j > 0).astype(F32)
    has_next = (j < nblk - 1).astype(F32)
    prev_row = prev_ref[0][7:8].astype(F32) * has_prev
    next_row = next_ref[0][0:1].astype(F32) * has_next
    rid = lax.broadcasted_iota(jnp.int32, (rows, 1), 0)
    up = jnp.where(rid == 0, prev_row, pltpu.roll(cur, 1, axis=0))
    dn = jnp.where(rid == rows - 1, next_row, pltpu.roll(cur, rows - 1, axis=0))
    w = w_ref[...]
    y = up * w[0:1] + cur * w[1:2] + dn * w[2:3]
    c = HY_CH
    x1_ref[0] = y[:, :c].astype(BF16)
    x2_ref[0] = y[:, c:2 * c].astype(BF16)
    v_ref[0] = y[:, 2 * c:].astype(BF16)


def _hy_prep(u, conv_w, blk0, nblk):
    b, lt, _ = u.shape
    n = 3 * HY_CH
    sub = ROW_BLK // 8
    last8 = lt // 8 - 1
    out = jax.ShapeDtypeStruct((b, nblk * ROW_BLK, HY_CH), BF16)
    ospec = pl.BlockSpec((1, ROW_BLK, HY_CH), lambda i, j: (i, j, 0))
    return pl.pallas_call(
        functools.partial(_hy_prep_kernel, nblk=nblk),
        grid=(b, nblk),
        in_specs=[pl.BlockSpec((1, ROW_BLK, n), lambda i, j: (i, blk0 + j, 0)),
                  pl.BlockSpec((1, 8, n), lambda i, j: (i, jnp.maximum((blk0 + j) * sub - 1, 0), 0)),
                  pl.BlockSpec((1, 8, n), lambda i, j: (i, jnp.minimum((blk0 + j + 1) * sub, last8), 0)),
                  _resident((3, n))],
        out_specs=[ospec, ospec, ospec],
        out_shape=[out, out, out],
        compiler_params=_cparams(("parallel", "parallel")),
        name="hy_prep",
    )(u, u, u, conv_w)


def _hy_filt_kernel(z_ref, w1_ref, b1_ref, fr_ref, w2_ref, b2_ref, w3_ref, ld_ref, o_ref):
    hi = lax.Precision.HIGHEST
    z = z_ref[...]
    fr = fr_ref[...]
    hid = jnp.sin(fr * (jnp.dot(z, w1_ref[...], preferred_element_type=F32, precision=hi) + b1_ref[...]))
    hid = jnp.sin(fr * (jnp.dot(hid, w2_ref[...], preferred_element_type=F32, precision=hi) + b2_ref[...]))
    h = jnp.dot(hid, w3_ref[...], preferred_element_type=F32, precision=hi)
    t = z[:, 0:1]
    o_ref[...] = h * jnp.exp(-t * jnp.exp(ld_ref[...]))


def _pad2(a, rows, cols):
    return jnp.pad(a, ((0, rows - a.shape[0]), (0, cols - a.shape[1])))


def _hy_filters(length, w1, b1, freq, w2, b2, w3, log_decay):
    pos = jnp.arange(length, dtype=F32)
    t = pos / max(length - 1, 1)
    w = 2.0 * math.pi * pos / length
    f = jnp.linspace(1e-4, HY_BANDS - 1, HY_BANDS, dtype=F32)
    ang = w[:, None] * f[None, :]
    z = jnp.concatenate([t[:, None], jnp.cos(ang), -jnp.sin(ang)], -1)
    z = _pad2(z, length, LANES)
    n = w3.shape[1]
    rb = min(length, 512)
    vec = lambda a: _pad2(a.reshape(1, -1), 1, LANES)
    return pl.pallas_call(
        _hy_filt_kernel,
        grid=(length // rb,),
        in_specs=[pl.BlockSpec((rb, LANES), lambda i: (i, 0)),
                  _resident((LANES, LANES)), _resident((1, LANES)), _resident((1, LANES)),
                  _resident((LANES, LANES)), _resident((1, LANES)),
                  _resident((LANES, n)), _resident((1, n))],
        out_specs=pl.BlockSpec((rb, n), lambda i: (i, 0)),
        out_shape=jax.ShapeDtypeStruct((length, n), F32),
        compiler_params=_cparams(("parallel",)),
        name="hy_filters",
    )(z, _pad2(w1, LANES, LANES), vec(b1), vec(freq), _pad2(w2, LANES, LANES), vec(b2),
      _pad2(w3, LANES, n), log_decay.reshape(1, n))


def _bidir_taps(h, skip, length):
    h4 = h.reshape(length, 2, 2, HY_CH)
    cols = []
    for o in range(2):
        hf = h4[:, o, 0].at[0].add(skip[o])
        hb = h4[:, o, 1]
        cols.append(jnp.concatenate([hf, jnp.zeros_like(hf[:1]), hb[:0:-1]], 0))
    return jnp.concatenate(cols, -1)


def _left_mm_kernel(m_ref, x_ref, o_ref):
    o_ref[0] = _dot(m_ref[...], x_ref[0]).astype(o_ref.dtype)


def _left_mm(mat, x, out_dtype, lane_blk):
    g, k, n = x.shape
    m = mat.shape[0]
    lane_blk = min(lane_blk, n)
    return pl.pallas_call(
        _left_mm_kernel, grid=(g, n // lane_blk),
        in_specs=[_resident((m, k)), pl.BlockSpec((1, k, lane_blk), lambda i, j: (i, 0, j))],
        out_specs=pl.BlockSpec((1, m, lane_blk), lambda i, j: (i, 0, j)),
        out_shape=jax.ShapeDtypeStruct((g, m, n), out_dtype),
        compiler_params=_cparams(("parallel", "parallel")),
        name="left_mm",
    )(mat, x)


def _cs(num, den, rows, cols):
    ang = 2.0 * np.pi * np.outer(np.arange(rows), np.arange(cols)) * (num / den)
    return np.cos(ang), np.sin(ang)


def _hy_filt_t_kernel(z_ref, msk_ref, w1_ref, b1_ref, fr_ref, w2_ref, b2_ref, w3_ref, ld_ref, sk_ref, o_ref):
    hi = lax.Precision.HIGHEST
    z = z_ref[...]
    fr = fr_ref[...]
    hid = jnp.sin(fr * (jnp.dot(w1_ref[...], z, preferred_element_type=F32, precision=hi) + b1_ref[...]))
    hid = jnp.sin(fr * (jnp.dot(w2_ref[...], hid, preferred_element_type=F32, precision=hi) + b2_ref[...]))
    h = jnp.dot(w3_ref[0], hid, preferred_element_type=F32, precision=hi)
    h = h * jnp.exp(-jnp.exp(ld_ref[0]) * z[0:1, :])
    msk = msk_ref[...]
    h = h * msk[0:1, :] + sk_ref[...] * msk[1:2, :]
    for s in range(o_ref.shape[1]):
        o_ref[:, s, :] = h[:, s * FFT_N2:(s + 1) * FFT_N2]


def _hy_filters_t(length, w1, b1, freq, w2, b2, w3, log_decay, skip):
    n = 2 * length
    tt = jnp.arange(n, dtype=jnp.int32)
    pos = jnp.where(tt < length, tt, n - tt).astype(F32)
    t = pos / max(length - 1, 1)
    w = 2.0 * math.pi * pos / length
    f = jnp.linspace(1e-4, HY_BANDS - 1, HY_BANDS, dtype=F32)
    ang = f[:, None] * w[None, :]
    z = jnp.concatenate([t[None, :], jnp.cos(ang), -jnp.sin(ang)], 0)
    z = jnp.pad(z, ((0, LANES - z.shape[0]), (0, 0)))
    msk = jnp.stack([(tt != length).astype(F32), (tt == 0).astype(F32)])
    msk = jnp.pad(msk, ((0, 6), (0, 0)))
    col = lambda a: a.reshape(-1, 1)
    c2 = 2 * HY_CH
    nf = w3.shape[0]
    w3d = w3.reshape(nf, 2, 2, HY_CH).transpose(2, 1, 3, 0).reshape(2, c2, nf)
    ldd = log_decay.reshape(2, 2, HY_CH).transpose(1, 0, 2).reshape(2, c2, 1)
    rows = 8
    pb = rows * FFT_N2
    half = length // pb
    return pl.pallas_call(
        _hy_filt_t_kernel,
        grid=(n // pb,),
        in_specs=[pl.BlockSpec((LANES, pb), lambda i: (0, i)),
                  pl.BlockSpec((8, pb), lambda i: (0, i)),
                  _resident((nf, LANES)), _resident((nf, 1)), _resident((nf, 1)),
                  _resident((nf, nf)), _resident((nf, 1)),
                  pl.BlockSpec((1, c2, nf), lambda i: (i // half, 0, 0)),
                  pl.BlockSpec((1, c2, 1), lambda i: (i // half, 0, 0)),
                  _resident((c2, 1))],
        out_specs=pl.BlockSpec((c2, rows, FFT_N2), lambda i: (0, i, 0)),
        out_shape=jax.ShapeDtypeStruct((c2, n // FFT_N2, FFT_N2), F32),
        compiler_params=_cparams(("parallel",)),
        name="hy_filters_t",
    )(z, msk, _pad2(w1.T, nf, LANES), col(b1), col(freq), w2.T, col(b2), w3d, ldd, skip.reshape(c2, 1))


def _dft_mats():
    n = FFT_N2
    c, s = _cs(1, n, n, n)
    ch, sh = c[:, :n // 2], s[:, :n // 2]
    f1_pair = np.block([[ch, sh], [-sh, ch]])
    f1_full = np.concatenate([c, -s], 0)
    m2r = np.block([[c, -s], [s, c]])
    m2i = np.block([[c, s], [-s, c]])
    f1_inv = np.block([[ch.T, -sh.T], [sh.T, ch.T]]) / (n * n)
    cast = lambda a: jnp.asarray(a, dtype=BF16)
    return cast(f1_pair), cast(f1_full), cast(m2r), cast(m2i), cast(f1_inv)


def _twiddle2d(n):
    k1 = lax.broadcasted_iota(jnp.int32, (n, n), 0)
    m2 = lax.broadcasted_iota(jnp.int32, (n, n), 1)
    ang = (k1 * m2).astype(F32) * (2.0 * math.pi / (n * n))
    return jnp.cos(ang), jnp.sin(ang)


def _fwd_spectrum(xs, f1, m2r, c, s):
    n = FFT_N2
    a = _dot(f1, jnp.concatenate(xs, axis=1))
    ts = []
    for g in range(len(xs)):
        ar = a[:n, g * n:(g + 1) * n]
        ai = a[n:, g * n:(g + 1) * n]
        ts.append(jnp.concatenate([ar * c + ai * s, ai * c - ar * s], axis=1))
    return _dot(jnp.concatenate(ts, axis=0).astype(BF16), m2r)


def _filter_spec_t_kernel(t_ref, f1_ref, m2r_ref, c_ref, s_ref, o_ref):
    g = t_ref.shape[0]
    xs = [t_ref[i].astype(BF16) for i in range(g)]
    spec = _fwd_spectrum(xs, f1_ref[...], m2r_ref[...], c_ref[...], s_ref[...])
    for i in range(g):
        o_ref[i] = spec[i * FFT_N2:(i + 1) * FFT_N2]


def _filter_spec_t(taps, f1_full, m2r, twc, tws):
    nch, n1, n = taps.shape
    g = HY_GROUP
    return pl.pallas_call(
        _filter_spec_t_kernel,
        grid=(nch // g,),
        in_specs=[pl.BlockSpec((g, n1, n), lambda i: (i, 0, 0)),
                  _resident(f1_full.shape), _resident(m2r.shape), _resident((n, n)), _resident((n, n))],
        out_specs=pl.BlockSpec((g, n, 2 * n), lambda i: (i, 0, 0)),
        out_shape=jax.ShapeDtypeStruct((nch, n, 2 * n), F32),
        compiler_params=_cparams(("parallel",)),
        name="filter_spec_t",
    )(taps, f1_full, m2r, twc, tws)


def _hyena_core_kernel(x1_ref, x2_ref, v_ref, w1_ref, w2_ref, wv_ref, h0_ref, h1_ref,
                       f1_ref, m2r_ref, m2i_ref, f1i_ref, c_ref, s_ref, o_ref):
    n = FFT_N2
    n1 = v_ref.shape[2]
    c = c_ref[...]
    s = s_ref[...]
    lane = lax.broadcasted_iota(jnp.int32, (n1, n), 1)
    row = lax.broadcasted_iota(jnp.int32, (n1, n), 0)
    first_lane, last_lane = lane == 0, lane == n - 1
    seq_start, seq_end = first_lane & (row == 0), last_lane & (row == n1 - 1)

    def short_conv(x, w):
        r = pltpu.roll(x, 1, axis=1)
        up = jnp.where(first_lane, pltpu.roll(r, 1, axis=0), r)
        up = jnp.where(seq_start, 0.0, up)
        l = pltpu.roll(x, n - 1, axis=1)
        dn = jnp.where(last_lane, pltpu.roll(l, n1 - 1, axis=0), l)
        dn = jnp.where(seq_end, 0.0, dn)
        return up * w[0:1] + x * w[1:2] + dn * w[2:3]

    def conv_all(xss, h_ref):
        spec_s = [_fwd_spectrum(xs, f1_ref[...], m2r_ref[...], c, s) for xs in xss]
        y_s = []
        for chain, spec in enumerate(spec_s):
            ys = []
            for g in range(HY_GROUP):
                xr = spec[g * n:(g + 1) * n, :n]
                xi = spec[g * n:(g + 1) * n, n:]
                hh = h_ref[chain * HY_GROUP + g]
                hr, hi = hh[:, :n], hh[:, n:]
                ys.append(jnp.concatenate([xr * hr - xi * hi, xr * hi + xi * hr], axis=1))
            y_s.append(jnp.concatenate(ys, axis=0).astype(BF16))
        bm_s = [_dot(y, m2i_ref[...]) for y in y_s]
        b_s = []
        for bm in bm_s:
            bs = []
            for g in range(HY_GROUP):
                br = bm[g * n:(g + 1) * n, :n]
                bi = bm[g * n:(g + 1) * n, n:]
                bs.append(jnp.concatenate([br * c - bi * s, bi * c + br * s], axis=0))
            b_s.append(jnp.concatenate(bs, axis=1).astype(BF16))
        outs = [_dot(f1i_ref[...], bc) for bc in b_s]
        return [[y[:, g * n:(g + 1) * n] for g in range(HY_GROUP)] for y in outs]

    def pair(ref, w_ref, ch):
        return jnp.concatenate([short_conv(ref[0, ch], w_ref[ch]), short_conv(ref[1, ch], w_ref[ch])], axis=0)

    chans = [[chain * HY_GROUP + g for g in range(HY_GROUP)] for chain in range(HY_CHAINS)]
    y1 = conv_all([[pair(v_ref, wv_ref, ch).astype(BF16) for ch in grp] for grp in chans], h0_ref)
    z = [[(y1[k][g] * pair(x1_ref, w1_ref, ch)).astype(BF16) for g, ch in enumerate(grp)]
         for k, grp in enumerate(chans)]
    y2 = conv_all(z, h1_ref)
    for k, grp in enumerate(chans):
        for g, ch in enumerate(grp):
            y = y2[k][g] * pair(x2_ref, w2_ref, ch)
            o_ref[0, :, ch, :] = y[:n1]
            o_ref[1, :, ch, :] = y[n1:]


def _hyena_core(ut, conv_w, spec, mats, twc, tws):
    f1_pair, _, m2r, m2i, f1_inv = mats
    b, nch, n1, n = ut.shape
    assert b % 2 == 0
    cb = HY_CBLK
    nblk = HY_CH // cb
    wt = jnp.broadcast_to(conv_w.T[:, :, None], (nch, conv_w.shape[0], n))
    xspec = lambda off: pl.BlockSpec((2, cb, n1, n), lambda j, i: (i, off * nblk + j, 0, 0))
    wspec = lambda off: pl.BlockSpec((cb, conv_w.shape[0], n), lambda j, i: (off * nblk + j, 0, 0))
    hspec = lambda off: pl.BlockSpec((cb, n, 2 * n), lambda j, i: (off * nblk + j, 0, 0))
    return pl.pallas_call(
        _hyena_core_kernel,
        grid=(nblk, b // 2),
        in_specs=[xspec(0), xspec(1), xspec(2), wspec(0), wspec(1), wspec(2), hspec(0), hspec(1),
                  _resident(f1_pair.shape), _resident(m2r.shape), _resident(m2i.shape), _resident(f1_inv.shape),
                  _resident((n, n)), _resident((n, n))],
        out_specs=pl.BlockSpec((2, n1, cb, n), lambda j, i: (i, 0, j, 0)),
        out_shape=jax.ShapeDtypeStruct((b, n1, HY_CH, n), F32),
        compiler_params=_cparams(("parallel", "parallel")),
        name="hyena_core",
    )(ut, ut, ut, wt, wt, wt, spec, spec, f1_pair, m2r, m2i, f1_inv, twc, tws)


def _hy_ctx_kernel(v_ref, x1_ref, x2_ref, f_ref, fi_ref, h_ref, o_ref):
    nf = f_ref.shape[0] // 2
    zin = v_ref[0]
    gates = (x1_ref, x2_ref)
    for o in range(2):
        x = _dot(f_ref[...], zin)
        xr, xi = x[:nf], x[nf:]
        hr = h_ref[o, :nf]
        hi = h_ref[o, nf:]
        y = jnp.concatenate([xr * hr - xi * hi, xr * hi + xi * hr], axis=0).astype(BF16)
        zin = (_dot(fi_ref[...], y) * gates[o][0].astype(F32)).astype(BF16)
    o_ref[0] = zin


def _hy_ctx(v, x1, x2, taps):
    b, lc, c = v.shape
    nf = 2 * lc
    cm, sm = _cs(1, nf, nf, nf)
    fwd = jnp.asarray(np.concatenate([cm[:, :lc], -sm[:, :lc]], 0), dtype=BF16)
    fwd_full = jnp.asarray(np.concatenate([cm, -sm], 0), dtype=BF16)
    inv = jnp.asarray(np.concatenate([cm[:lc, :], -sm[:lc, :]], 1) / nf, dtype=BF16)
    spec = _left_mm(fwd_full, taps.astype(BF16).reshape(1, nf, 2 * c), F32, 2 * c)
    spec = spec.reshape(2 * nf, 2, c).transpose(1, 0, 2)
    blk = pl.BlockSpec((1, lc, c), lambda i: (i, 0, 0))
    return pl.pallas_call(
        _hy_ctx_kernel,
        grid=(b,),
        in_specs=[blk, blk, blk, _resident((2 * nf, lc)), _resident((lc, 2 * nf)),
                  _resident((2, 2 * nf, c))],
        out_specs=blk,
        out_shape=jax.ShapeDtypeStruct((b, lc, c), BF16),
        compiler_params=_cparams(("parallel",)),
        name="hy_ctx",
    )(v, x1, x2, fwd, inv, spec)


def _pair_rows(q2):
    lane = lax.broadcasted_iota(jnp.int32, q2.shape, 1)
    zero = jnp.zeros_like(q2)
    return jnp.concatenate([jnp.where(lane < NA_HD, q2, zero), jnp.where(lane >= NA_HD, q2, zero)], axis=0)


def _unpair_rows(o):
    r = o.shape[0] // 2
    lane = lax.broadcasted_iota(jnp.int32, (r, o.shape[1]), 1)
    return jnp.where(lane < NA_HD, o[:r], o[r:])


def _pair_softmax_pv(scores, values):
    m = scores[0].max(axis=-1, keepdims=True)
    for s in scores[1:]:
        m = jnp.maximum(m, s.max(axis=-1, keepdims=True))
    den = None
    acc = None
    for s, v in zip(scores, values):
        p = jnp.exp(s - m)
        d = p.sum(axis=-1, keepdims=True)
        a = _dot(p.astype(BF16), v)
        den = d if den is None else den + d
        acc = a if acc is None else acc + a
    return acc / den


def _natten_kernel(q_ref, k0, k1, k2, k3, v0, v1, v2, v3, kc_ref, vc_ref, bias_ref, o_ref,
                   kwin, vwin, *, rows):
    g = pl.program_id(1)
    rb = 4 * GRID_W
    for i, (kr, vr) in enumerate(((k0, v0), (k1, v1), (k2, v2), (k3, v3))):
        kwin[i * rb:(i + 1) * rb, :] = kr[0]
        vwin[i * rb:(i + 1) * rb, :] = vr[0]
    base = 4 * jnp.clip(2 * g - 1, 0, rows // 4 - 4)
    nwin = NA_WIN_R * GRID_W
    qscale = jnp.asarray(NA_HD ** -0.5, BF16)

    ones_lat = jnp.ones((nwin, LANES), BF16)
    ones_ctx = jnp.ones((kc_ref.shape[1], LANES), BF16)

    def rows_body(it, carry):
        work = []
        for u in range(NA_ROWS_PER_TRIP):
            rr = it * NA_ROWS_PER_TRIP + u
            r = 8 * g + rr
            rs = jnp.clip(r - NA_WIN_R // 2, 0, rows - NA_WIN_R)
            st = pl.multiple_of((rs - base) * GRID_W, GRID_W)
            qo = pl.multiple_of(rr * GRID_W, GRID_W)
            work += [(qo, st, rs - r + NA_WIN_R - 1, p) for p in range(NA_HEADS // 2)]
        scores = []
        for qo, st, d0, p in work:
            ls = slice(p * LANES, (p + 1) * LANES)
            qp = _pair_rows(q_ref[0, pl.ds(qo, GRID_W), ls] * qscale)
            scores.append((_dot_nt(qp, kwin[pl.ds(st, nwin), ls]) + bias_ref[d0, p].astype(F32),
                           _dot_nt(qp, kc_ref[0, :, ls])))
        maxima = [jnp.maximum(a.max(axis=-1, keepdims=True), b.max(axis=-1, keepdims=True)) for a, b in scores]
        for (qo, st, d0, p), (s_lat, s_ctx), m in zip(work, scores, maxima):
            ls = slice(p * LANES, (p + 1) * LANES)
            v_lat = jnp.concatenate([vwin[pl.ds(st, nwin), ls], ones_lat], axis=1)
            v_ctx = jnp.concatenate([vc_ref[0, :, ls], ones_ctx], axis=1)
            acc = _dot(jnp.exp(s_lat - m).astype(BF16), v_lat) + _dot(jnp.exp(s_ctx - m).astype(BF16), v_ctx)
            o = acc[:, :LANES] / acc[:, LANES:LANES + 1]
            o_ref[0, pl.ds(qo, GRID_W), ls] = _unpair_rows(o).astype(o_ref.dtype)
        return carry

    lax.fori_loop(0, 8 // NA_ROWS_PER_TRIP, rows_body, 0)


def _natten_bias(rpb):
    c = np.arange(GRID_W)[:, None]
    kc = np.arange(GRID_W)[None, :]
    cs = np.clip(c - NA_WIN_C // 2, 0, GRID_W - NA_WIN_C)
    valid = (kc >= cs) & (kc < cs + NA_WIN_C)
    dc = np.clip(kc - c + NA_WIN_C - 1, 0, 2 * NA_WIN_C - 2)
    tb = jnp.where(valid[None, None], rpb[:, :, dc], NEG_BIG)
    slabs = []
    for d0 in range(NA_WIN_R):
        s = tb[:, d0:d0 + NA_WIN_R]
        s = s.transpose(0, 2, 1, 3).reshape(NA_HEADS, GRID_W, NA_WIN_R * GRID_W)
        slabs.append(s.reshape(NA_HEADS // 2, 2 * GRID_W, NA_WIN_R * GRID_W))
    return jnp.stack(slabs).astype(BF16)


def _natten(uq, uc, bias):
    b, length, _ = uq.shape
    c = NA_HEADS * NA_HD
    rows = length // GRID_W
    rb = 4 * GRID_W
    nkb = length // rb
    qrows = 8 * GRID_W
    lc = uc.shape[1]

    def kv_spec(col, off):
        return pl.BlockSpec((1, rb, c), lambda i, g: (i, jnp.clip(2 * g - 1, 0, nkb - 4) + off, col))

    return pl.pallas_call(
        functools.partial(_natten_kernel, rows=rows),
        grid=(b, rows // 8),
        in_specs=[pl.BlockSpec((1, qrows, c), lambda i, g: (i, g, 0))]
                 + [kv_spec(1, o) for o in range(4)] + [kv_spec(2, o) for o in range(4)]
                 + [pl.BlockSpec((1, lc, c), lambda i, g: (i, 0, 4)),
                    pl.BlockSpec((1, lc, c), lambda i, g: (i, 0, 5)),
                    _resident(bias.shape)],
        out_specs=pl.BlockSpec((1, qrows, c), lambda i, g: (i, g, 0)),
        out_shape=jax.ShapeDtypeStruct((b, length, c), BF16),
        scratch_shapes=[pltpu.VMEM((4 * rb, c), BF16), pltpu.VMEM((4 * rb, c), BF16)],
        compiler_params=_cparams(("parallel", "parallel")),
        name="natten",
    )(uq, *([uq] * 8), uc, uc, bias)


def _ctx_attn_kernel(q_ref, k_ref, v_ref, o_ref):
    qscale = jnp.asarray(NA_HD ** -0.5, BF16)
    for p in range(NA_HEADS // 2):
        ls = slice(p * LANES, (p + 1) * LANES)
        qp = _pair_rows(q_ref[0, :, ls] * qscale)
        o = _pair_softmax_pv([_dot_nt(qp, k_ref[0, :, ls])], [v_ref[0, :, ls]])
        o_ref[0, :, ls] = _unpair_rows(o).astype(o_ref.dtype)


def _ctx_attn(u):
    b, lc, _ = u.shape
    c = NA_HEADS * NA_HD
    spec = lambda col: pl.BlockSpec((1, lc, c), lambda i: (i, 0, col))
    return pl.pallas_call(
        _ctx_attn_kernel,
        grid=(b,),
        in_specs=[spec(3), spec(4), spec(5)],
        out_specs=pl.BlockSpec((1, lc, c), lambda i: (i, 0, 0)),
        out_shape=jax.ShapeDtypeStruct((b, lc, c), BF16),
        compiler_params=_cparams(("parallel",)),
        name="ctx_attn",
    )(u, u, u)


def _post_kernel(x_ref, mod_ref, ya_ref, yb_ref, wa_ref, wb_ref, lng_ref, lnb_ref, w1_ref, w2_ref, o_ref, *,
                 ya_slabs, yb_slabs):
    m = mod_ref[0, 0]
    lng = lng_ref[...]
    lnb = lnb_ref[...]
    ff = w1_ref.shape[1]
    step = MLP_CHUNK
    sub = ROW_BLK
    nsub = x_ref.shape[1] // sub

    def rows_of(ref, slabs, t):
        if slabs:
            tiles = range(t * sub // LANES, (t + 1) * sub // LANES)
            return jnp.concatenate([ref[0, s].T for s in tiles], axis=0).astype(BF16)
        return ref[0, t * sub:(t + 1) * sub, :]

    def head(t):
        rs = slice(t * sub, (t + 1) * sub)
        y = _dot(rows_of(ya_ref, ya_slabs, t), wa_ref[...]) + _dot(rows_of(yb_ref, yb_slabs, t), wb_ref[...])
        x1 = _norm_rows(ALPHA * x_ref[0, rs, :] + m[2:3] * y) * lng[0:1] + lnb[0:1]
        return x1, (_norm_rows(x1) * (1.0 + m[4:5]) + m[3:4]).astype(BF16)

    def mlp(h):
        acc = None
        for c in range(ff // step):
            a = jnp.maximum(_dot(h, w1_ref[:, c * step:(c + 1) * step]), 0.0)
            d = _dot((a * a).astype(BF16), w2_ref[c * step:(c + 1) * step, :])
            acc = d if acc is None else acc + d
        return acc

    cur = head(0)
    for t in range(nsub):
        nxt = head(t + 1) if t + 1 < nsub else None
        acc = mlp(cur[1])
        o_ref[0, t * sub:(t + 1) * sub, :] = _norm_rows(ALPHA * cur[0] + m[5:6] * acc) * lng[1:2] + lnb[1:2]
        cur = nxt


def _post(x, modtab, mod_row, ya, yb, wa, wb, lng, lnb, w1, w2, ya_slabs=False, yb_slabs=False):
    b, r, d = x.shape
    ka, kb = wa.shape[0], wb.shape[0]
    rows = min(r, 2 * ROW_BLK)
    row = lambda k: pl.BlockSpec((1, rows, k), lambda i, j: (i, j, 0))
    slab = lambda k: pl.BlockSpec((1, rows // LANES, k, LANES), lambda i, j: (i, j, 0, 0))
    return pl.pallas_call(
        functools.partial(_post_kernel, ya_slabs=ya_slabs, yb_slabs=yb_slabs),
        grid=(b, r // rows),
        in_specs=[row(d), pl.BlockSpec((1, 1, 6, d), lambda i, j: (i, mod_row, 0, 0)),
                  slab(ka) if ya_slabs else row(ka), slab(kb) if yb_slabs else row(kb),
                  _resident(wa.shape), _resident(wb.shape), _resident(lng.shape), _resident(lnb.shape),
                  _resident(w1.shape), _resident(w2.shape)],
        out_specs=row(d),
        out_shape=jax.ShapeDtypeStruct((b, r, d), F32),
        compiler_params=_cparams(("parallel", "parallel")),
        name="post_mixer",
    )(x, modtab, ya, yb, wa, wb, lng, lnb, w1, w2)


def _rope(x, cos, sinl, sinr):
    reps = x.shape[1] // LANES
    tile = lambda t: jnp.concatenate([t] * reps, axis=1)
    n = x.shape[1]
    quarter = MLA_ROPE // 4
    return (x * tile(cos) + pltpu.roll(x, n - quarter, axis=1) * tile(sinl)
            + pltpu.roll(x, quarter, axis=1) * tile(sinr))


def _front_cd_kernel(x_ref, xc_ref, mod_ref, w_ref, qn_ref, kvn_ref, wuq_ref, wuk_ref, wuv_ref, epe_ref, one_ref,
                     fng_ref, fnb_ref, avg_ref, cbd_ref, sbd_ref,
                     cq_ref, slq_ref, srq_ref, ck_ref, slk_ref, srk_ref,
                     q_ref, k_ref, v_ref, p_ref, qf_ref, *, ctx_blk):
    m = mod_ref[0, 0]
    is_ctx = (jnp.zeros((ROW_BLK, 1), jnp.int32) + pl.program_id(1)) == ctx_blk
    x = jnp.where(is_ctx, xc_ref[0], x_ref[0])
    h = _norm_rows(x) * (1.0 + m[1:2]) + m[0:1]
    u = _dot(h.astype(BF16), w_ref[...])
    o_kv = MLA_Q_RANK
    o_fn = o_kv + MLA_KV_RANK
    o_pe = o_fn + FN_CH

    def rms(x, g):
        return x * lax.rsqrt(jnp.mean(x * x, axis=-1, keepdims=True) + LN_EPS) * g

    cq = rms(u[:, :o_kv], qn_ref[...]).astype(BF16)
    q = _dot(cq, wuq_ref[...])
    q_ref[0] = _rope(q, cq_ref[...], slq_ref[...], srq_ref[...]).astype(BF16)

    ckv = rms(u[:, o_kv:o_fn], kvn_ref[...]).astype(BF16)
    kpe = _dot(u[:, o_pe:].astype(BF16), epe_ref[...])
    k = _dot(ckv, wuk_ref[...]) + _rope(kpe, ck_ref[...], slk_ref[...], srk_ref[...])
    k_ref[0] = k.astype(BF16)
    v_ref[0] = (_dot(ckv, wuv_ref[...]) + one_ref[...]).astype(BF16)

    uf = u[:, o_fn:o_pe]
    avg = avg_ref[...]
    uc = uf - _dot(uf.astype(BF16), avg)
    var = _dot((uc * uc).astype(BF16), avg)
    ug = (uc * lax.rsqrt(var + LN_EPS) * fng_ref[...] + fnb_ref[...]).astype(BF16)
    p_ref[0] = _dot_nt(cbd_ref[...], ug).astype(BF16)
    qf_ref[0] = _dot_nt(sbd_ref[...], ug).astype(BF16)


def _rope_tables(length, lt, scale):
    t = jnp.arange(lt, dtype=jnp.int32)
    rows = (t // GRID_W).astype(F32)
    cols = (t % GRID_W).astype(F32)
    half = MLA_ROPE // 2
    inv = ROPE_THETA ** (-jnp.arange(0, half, 2, dtype=F32) / half)
    ar = rows[:, None] * inv[None, :]
    ac = cols[:, None] * inv[None, :]
    ang = jnp.concatenate([ar, ar, ac, ac], -1)
    is_lat = (t < length)[:, None]
    cos = jnp.where(is_lat, jnp.cos(ang), 1.0)
    sin = jnp.where(is_lat, jnp.sin(ang), 0.0)
    qd = MLA_ROPE // 4
    ones = jnp.ones((lt, MLA_NOPE), F32)
    zeros = jnp.zeros((lt, MLA_NOPE), F32)
    tail1 = jnp.ones((lt, HEAD_PAD - MLA_NOPE - MLA_ROPE), F32)
    tail0 = jnp.zeros((lt, HEAD_PAD - MLA_NOPE - MLA_ROPE), F32)
    z8 = jnp.zeros((lt, qd), F32)
    c = jnp.concatenate([ones, cos, tail1], -1)
    sl = jnp.concatenate([zeros, -sin[:, :qd], z8, -sin[:, 2 * qd:3 * qd], z8, tail0], -1)
    sr = jnp.concatenate([zeros, z8, sin[:, qd:2 * qd], z8, sin[:, 3 * qd:], tail0], -1)
    return c * scale, sl * scale, sr * scale


def _head_slots(w, per_head, take_from, take_n):
    k = w.shape[0]
    w3 = w.reshape(k, MLA_HEADS, per_head)[:, :, take_from:take_from + take_n]
    w3 = jnp.pad(w3, ((0, 0), (0, 0), (0, HEAD_PAD - take_n)))
    return w3.reshape(k, MLA_HEADS * HEAD_PAD)


def _front_cd(xl, xc, modtab, w_in, q_norm, w_uq, kv_norm, w_ukv, fn_g, fn_b):
    b, length, d = xl.shape
    lt = length + xc.shape[1]
    nlat = length // ROW_BLK
    o_kv = MLA_Q_RANK
    o_pe = o_kv + MLA_KV_RANK
    o_fn = o_pe + MLA_ROPE
    hw = MLA_HEADS * HEAD_PAD
    w_perm = jnp.concatenate([w_in[:, :o_pe], w_in[:, o_fn:], w_in[:, o_pe:o_fn],
                              jnp.zeros((d, LANES - MLA_ROPE), w_in.dtype)], -1).astype(BF16)
    wuq = _head_slots(w_uq, MLA_NOPE + MLA_ROPE, 0, MLA_NOPE + MLA_ROPE).astype(BF16)
    wuk = _head_slots(w_ukv, MLA_NOPE + MLA_V, 0, MLA_NOPE).astype(BF16)
    wuv = _head_slots(w_ukv, MLA_NOPE + MLA_V, MLA_NOPE, MLA_V).astype(BF16)
    epe = np.zeros((LANES, hw), np.float32)
    for hd in range(MLA_HEADS):
        for i in range(MLA_ROPE):
            epe[i, hd * HEAD_PAD + MLA_NOPE + i] = 1.0
    epe = jnp.asarray(epe, dtype=BF16)
    ones_col = np.zeros((1, hw), np.float32)
    ones_col[0, MLA_V::HEAD_PAD] = 1.0
    ones_col = jnp.asarray(ones_col)
    cm, sm = _cs(1, FN_GD, FN_GD, FN_GD)
    eye = np.eye(FN_GROUPS)
    cbd = jnp.asarray(np.kron(eye, cm), dtype=BF16)
    sbd = jnp.asarray(np.kron(eye, -sm), dtype=BF16)
    avg = jnp.asarray(np.kron(eye, np.full((FN_GD, FN_GD), 1.0 / FN_GD)), dtype=BF16)
    qtab = _rope_tables(length, lt, (MLA_NOPE + MLA_ROPE) ** -0.5 * math.log2(math.e))
    ktab = _rope_tables(length, lt, 1.0)
    row = lambda n: pl.BlockSpec((1, ROW_BLK, n), lambda i, j: (i, j, 0))
    tab = pl.BlockSpec((ROW_BLK, HEAD_PAD), lambda i, j: (j, 0))
    out = lambda n: jax.ShapeDtypeStruct((b, lt, n), BF16)
    fn_t = pl.BlockSpec((1, FN_CH, ROW_BLK), lambda i, j: (i, 0, j))
    fn_out = jax.ShapeDtypeStruct((b, FN_CH, lt), BF16)
    return pl.pallas_call(
        functools.partial(_front_cd_kernel, ctx_blk=nlat),
        grid=(b, lt // ROW_BLK),
        in_specs=[pl.BlockSpec((1, ROW_BLK, d), lambda i, j: (i, jnp.minimum(j, nlat - 1), 0)),
                  pl.BlockSpec((1, ROW_BLK, d), lambda i, j: (i, 0, 0)),
                  pl.BlockSpec((1, 1, 6, d), lambda i, j: (i, j // nlat, 0, 0)),
                  _resident(w_perm.shape), _resident((1, MLA_Q_RANK)), _resident((1, MLA_KV_RANK)),
                  _resident(wuq.shape), _resident(wuk.shape), _resident(wuv.shape), _resident(epe.shape),
                  _resident(ones_col.shape),
                  _resident((1, FN_CH)), _resident((1, FN_CH)), _resident(avg.shape), _resident(cbd.shape),
                  _resident(sbd.shape),
                  tab, tab, tab, tab, tab, tab],
        out_specs=[row(hw), row(hw), row(hw), fn_t, fn_t],
        out_shape=[out(hw), out(hw), out(hw), fn_out, fn_out],
        compiler_params=_cparams(("parallel", "parallel")),
        name="front_cd",
    )(xl, xc, modtab, w_perm, q_norm.reshape(1, -1), kv_norm.reshape(1, -1), wuq, wuk, wuv, epe, ones_col,
      fn_g.reshape(1, -1), fn_b.reshape(1, -1), avg, cbd, sbd, *qtab, *ktab)


def _mla_kernel(q_ref, k_ref, v_ref, o_ref, *, rows):
    n = q_ref.shape[1] // rows

    def scores(i):
        s = _dot_nt(q_ref[0, i * rows:(i + 1) * rows, :], k_ref[0])
        return s, s.max(axis=-1, keepdims=True)

    def finish(i, s, m):
        acc = _dot(jnp.exp2(s - m).astype(BF16), v_ref[0])
        o_ref[0, i * rows:(i + 1) * rows, :] = (acc / acc[:, MLA_V:MLA_V + 1]).astype(o_ref.dtype)

    pending = scores(0)
    for i in range(n):
        nxt = scores(i + 1) if i + 1 < n else None
        finish(i, *pending)
        pending = nxt


def _mla_attention(q, k, v, length):
    b, lt, hw = q.shape
    heads = hw // HEAD_PAD
    tq = MLA_Q_BLK
    kv = pl.BlockSpec((1, lt, HEAD_PAD), lambda i, h, j: (i, 0, h))
    qs = pl.BlockSpec((1, tq, HEAD_PAD), lambda i, h, j: (i, j, h))
    return pl.pallas_call(
        functools.partial(_mla_kernel, rows=MLA_Q_SUB),
        grid=(b, heads, length // tq),
        in_specs=[qs, kv, kv],
        out_specs=qs,
        out_shape=jax.ShapeDtypeStruct((b, length, hw), BF16),
        compiler_params=_cparams(("parallel", "parallel", "parallel")),
        name="mla_attention",
    )(q, k, v)


def _fnet_kernel(pa_ref, pb_ref, qa_ref, qb_ref, m1_ref, c_ref, s_ref, m2_ref, o_ref, zr_scr, zi_scr):
    n1 = FN_N1
    n2 = zr_scr.shape[2] // 2
    pairs = zr_scr.shape[0]

    def fill(scr, a_ref, b_ref):
        za = a_ref[0].astype(F32)
        zb = b_ref[0].astype(F32)
        for r in range(n1):
            scr[:, r, :] = jnp.concatenate([za[:, r * n2:(r + 1) * n2], zb[:, r * n2:(r + 1) * n2]], axis=1)

    fill(zr_scr, pa_ref, pb_ref)
    fill(zi_scr, qa_ref, qb_ref)
    c = c_ref[...]
    s = s_ref[...]
    for g0 in range(0, pairs, FN_GROUP):
        x = jnp.concatenate([jnp.concatenate([zr_scr[g0 + g], zi_scr[g0 + g]], axis=0) for g in range(FN_GROUP)],
                            axis=1).astype(BF16)
        a = _dot(m1_ref[...], x)
        ts = []
        for g in range(FN_GROUP):
            ar = a[:n1, g * LANES:(g + 1) * LANES]
            ai = a[n1:, g * LANES:(g + 1) * LANES]
            ts.append(jnp.concatenate([ar * c + ai * s, ai * c - ar * s], axis=1))
        y = _dot(jnp.concatenate(ts, axis=0).astype(BF16), m2_ref[...])
        for g in range(FN_GROUP):
            yt = y[g * n1:(g + 1) * n1].T
            o_ref[0, :, 2 * (g0 + g), :] = yt[:n2]
            o_ref[0, :, 2 * (g0 + g) + 1, :] = yt[n2:]


def _fnet(pt, qt, length):
    b, ch, _ = pt.shape
    n1 = FN_N1
    n2 = length // n1
    assert 2 * n2 == LANES
    half = ch // 2
    cm, sm = _cs(1, n1, n1, n1)
    m1 = jnp.asarray(np.block([[cm, sm], [-sm, cm]]), dtype=BF16)
    c2, s2 = _cs(1, n2, n2, n2)
    z = np.zeros_like(c2)
    m2 = np.block([[c2, z], [z, c2], [s2, z], [z, s2]]) / math.sqrt(length * FN_GD)
    m2 = jnp.asarray(m2, dtype=BF16)
    k1 = lax.broadcasted_iota(jnp.int32, (n1, LANES), 0)
    m = lax.broadcasted_iota(jnp.int32, (n1, LANES), 1) % n2
    ang = (k1 * m).astype(F32) * (2.0 * math.pi / length)
    twc, tws = jnp.cos(ang), jnp.sin(ang)
    pairs = FN_PAIRS
    nblk = half // pairs
    spec = lambda off: pl.BlockSpec((1, pairs, length), lambda j, i: (i, off * nblk + j, 0))
    return pl.pallas_call(
        _fnet_kernel,
        grid=(nblk, b),
        in_specs=[spec(0), spec(1), spec(0), spec(1), _resident(m1.shape), _resident((n1, LANES)),
                  _resident((n1, LANES)), _resident(m2.shape)],
        out_specs=pl.BlockSpec((1, n2, 2 * pairs, LANES), lambda j, i: (i, 0, j, 0)),
        out_shape=jax.ShapeDtypeStruct((b, n2, ch, LANES), F32),
        scratch_shapes=[pltpu.VMEM((pairs, n1, LANES), F32), pltpu.VMEM((pairs, n1, LANES), F32)],
        compiler_params=_cparams(("parallel", "parallel")),
        name="fnet",
    )(pt, pt, qt, qt, m1, twc, tws, m2)


def kernel(x, c, ctx, c_ctx, mod_w, mod_b, ln_g, ln_b, mlp_w1, mlp_w2,
           ab_w_in, ab_w_out, hy_conv_w, hy_w1, hy_b1, hy_freq, hy_w2, hy_b2, hy_w3, hy_log_decay, hy_skip, na_rpb,
           cd_w_in, cd_w_out, mla_q_norm, mla_w_uq, mla_kv_norm, mla_w_ukv, fn_norm_g, fn_norm_b):
    b, length, d = x.shape
    lc = ctx.shape[1]

    cc = jnp.concatenate([c, c_ctx[None], jnp.zeros((8 - b - 1, d), F32)], 0)
    mods = _mod_vectors(cc, mod_w, mod_b).reshape(DEPTH, 8, 6, d)
    modtab = [jnp.stack([mods[l, :b], jnp.broadcast_to(mods[l, b], (b, 6, d))], axis=1) for l in range(DEPTH)]

    n_hy = 3 * HY_CH
    w_in = ab_w_in[0].astype(BF16)
    uq, ut = _front_ab_lat(x, modtab[0], w_in[:, n_hy:], w_in[:, :n_hy].T)
    uc = _front_ab_ctx(ctx, modtab[0], w_in)
    fargs = (hy_w1[0], hy_b1[0], hy_freq[0], hy_w2[0], hy_b2[0], hy_w3[0], hy_log_decay[0])
    mats = _dft_mats()
    twc, tws = _twiddle2d(FFT_N2)
    spec = _filter_spec_t(_hy_filters_t(length, *fargs, hy_skip[0]), mats[1], mats[2], twc, tws)
    y_hy_t = _hyena_core(ut, hy_conv_w[0], spec, mats, twc, tws)
    x1c, x2c, vc = _hy_prep(uc, hy_conv_w[0], 0, lc // ROW_BLK)
    y_hy_c = _hy_ctx(vc, x1c, x2c, _bidir_taps(_hy_filters(lc, *fargs), hy_skip[0], lc))
    y_na = _natten(uq, uc, _natten_bias(na_rpb[0]))
    y_na_c = _ctx_attn(uc)
    w_out = ab_w_out[0].astype(BF16)
    mlp = (ln_g[0], ln_b[0], mlp_w1[0].astype(BF16), mlp_w2[0].astype(BF16))
    xl = _post(x, modtab[0], 0, y_hy_t, y_na, w_out[:HY_CH], w_out[HY_CH:], *mlp, ya_slabs=True)
    xc = _post(ctx, modtab[0], 1, y_hy_c, y_na_c, w_out[:HY_CH], w_out[HY_CH:], *mlp)

    q, k, vv, pt, qt = _front_cd(xl, xc, modtab[1], cd_w_in[0], mla_q_norm[0], mla_w_uq[0], mla_kv_norm[0],
                                 mla_w_ukv[0], fn_norm_g[0], fn_norm_b[0])
    o = _mla_attention(q, k, vv, length)
    y_fn = _fnet(pt, qt, length)
    w_out = cd_w_out[0]
    n_mla = MLA_HEADS * MLA_V
    wa = jnp.pad(w_out[:n_mla].reshape(MLA_HEADS, MLA_V, d), ((0, 0), (0, HEAD_PAD - MLA_V), (0, 0)))
    wa = wa.reshape(MLA_HEADS * HEAD_PAD, d).astype(BF16)
    wb = w_out[n_mla:].reshape(2, FN_CH // 2, d).transpose(1, 0, 2).reshape(FN_CH, d)
    return _post(xl, modtab[1], 0, o, y_fn, wa, wb.astype(BF16),
                 ln_g[1], ln_b[1], mlp_w1[1].astype(BF16), mlp_w2[1].astype(BF16), yb_slabs=True)
```

```python
import functools
import math

import numpy as np
import jax
import jax.numpy as jnp
from jax import lax
from jax.experimental import pallas as pl
from jax.experimental.pallas import tpu as pltpu

F32 = jnp.float32
BF16 = jnp.bfloat16

D_MODEL = 1024
DEPTH = 2
GRID_W = 64
HY_CH = 512
HY_EMB = 33
HY_BANDS = (HY_EMB - 1) // 2
NA_HEADS = 8
NA_HD = 64
NA_WIN_R = 8
NA_WIN_C = 16
MLA_HEADS = 8
MLA_Q_RANK = 384
MLA_KV_RANK = 256
MLA_NOPE = 64
MLA_ROPE = 32
MLA_V = 96
ROPE_THETA = 10000.0
FN_CH = 256
FN_GROUPS = 4
FN_GD = FN_CH // FN_GROUPS
D_FF = 4 * D_MODEL
ALPHA = (2.0 * DEPTH) ** 0.25
LN_EPS = 1e-5

LANES = 128
ROW_BLK = 256
HEAD_PAD = 128
FFT_N2 = 128
NA_ROWS_PER_TRIP = 8
FN_N1 = 128
FN_PAIRS = 16
FN_GROUP = 4
MLP_CHUNK = 1024
MLA_Q_BLK = 1024
MLA_Q_SUB = 256
HY_GROUP = 16
HY_CHAINS = 2
HY_CBLK = HY_GROUP * HY_CHAINS
VMEM_LIMIT = 56 * 1024 * 1024
NEG_BIG = -1e30


def _cparams(sem, vmem=VMEM_LIMIT):
    return pltpu.CompilerParams(dimension_semantics=sem, vmem_limit_bytes=vmem)


def _resident(shape):
    nd = len(shape)
    return pl.BlockSpec(shape, lambda *_: (0,) * nd, pipeline_mode=pl.Buffered(1))


def _norm_rows(x):
    mu = jnp.mean(x, axis=-1, keepdims=True)
    xc = x - mu
    var = jnp.mean(xc * xc, axis=-1, keepdims=True)
    return xc * lax.rsqrt(var + LN_EPS)


def _dot(a, b):
    return jnp.dot(a, b, preferred_element_type=F32)


def _dot_nt(a, b):
    return lax.dot_general(a, b, (((1,), (1,)), ((), ())), preferred_element_type=F32)


def _mod_kernel(c_ref, w_ref, b_ref, o_ref):
    c = c_ref[...]
    s = c * (1.0 / (1.0 + jnp.exp(-c)))
    o_ref[0] = jnp.dot(s, w_ref[0], preferred_element_type=F32,
                       precision=lax.Precision.HIGHEST) + b_ref[0]


def _mod_vectors(cc, mod_w, mod_b):
    depth, d, n = mod_w.shape
    nb = 1024
    return pl.pallas_call(
        _mod_kernel,
        grid=(depth, n // nb),
        in_specs=[pl.BlockSpec((8, d), lambda l, j: (0, 0)),
                  pl.BlockSpec((1, d, nb), lambda l, j: (l, 0, j)),
                  pl.BlockSpec((1, 1, nb), lambda l, j: (l, 0, j))],
        out_specs=pl.BlockSpec((1, 8, nb), lambda l, j: (l, 0, j)),
        out_shape=jax.ShapeDtypeStruct((depth, 8, n), F32),
        compiler_params=_cparams(("parallel", "parallel")),
        name="mod_vectors",
    )(cc, mod_w, mod_b.reshape(depth, 1, n))


def _front_ab_ctx_kernel(x_ref, mod_ref, w_ref, u_ref):
    m = mod_ref[0, 0]
    h = _norm_rows(x_ref[0]) * (1.0 + m[1:2]) + m[0:1]
    u_ref[0] = _dot(h.astype(BF16), w_ref[...]).astype(BF16)


def _front_ab_ctx(xc, modtab, w_in):
    b, lc, d = xc.shape
    n = w_in.shape[1]
    return pl.pallas_call(
        _front_ab_ctx_kernel,
        grid=(b, lc // ROW_BLK),
        in_specs=[pl.BlockSpec((1, ROW_BLK, d), lambda i, j: (i, j, 0)),
                  pl.BlockSpec((1, 1, 6, d), lambda i, j: (i, 1, 0, 0)),
                  _resident((d, n))],
        out_specs=pl.BlockSpec((1, ROW_BLK, n), lambda i, j: (i, j, 0)),
        out_shape=jax.ShapeDtypeStruct((b, lc, n), BF16),
        compiler_params=_cparams(("parallel", "parallel")),
        name="front_ab_ctx",
    )(xc, modtab, w_in)


def _front_ab_lat_kernel(x_ref, mod_ref, wq_ref, wht_ref, u_ref, ut_ref):
    m = mod_ref[0, 0]
    hs = []
    for t in range(x_ref.shape[1] // ROW_BLK):
        rs = slice(t * ROW_BLK, (t + 1) * ROW_BLK)
        hs.append((_norm_rows(x_ref[0, rs, :]) * (1.0 + m[1:2]) + m[0:1]).astype(BF16))
        u_ref[0, rs, :] = _dot(hs[-1], wq_ref[...]).astype(BF16)
    h_all = jnp.concatenate(hs, axis=0)
    for c0 in range(0, wht_ref.shape[0], HY_CH):
        ut = _dot_nt(wht_ref[c0:c0 + HY_CH, :], h_all)
        for s in range(ut_ref.shape[2]):
            ut_ref[0, c0:c0 + HY_CH, s, :] = ut[:, s * FFT_N2:(s + 1) * FFT_N2]


def _front_ab_lat(x, modtab, w_qkv, w_hy_t):
    b, length, d = x.shape
    nq = w_qkv.shape[1]
    nh = w_hy_t.shape[0]
    rows = 8 * FFT_N2
    return pl.pallas_call(
        _front_ab_lat_kernel,
        grid=(b, length // rows),
        in_specs=[pl.BlockSpec((1, rows, d), lambda i, j: (i, j, 0)),
                  pl.BlockSpec((1, 1, 6, d), lambda i, j: (i, 0, 0, 0)),
                  _resident((d, nq)), _resident((nh, d))],
        out_specs=[pl.BlockSpec((1, rows, nq), lambda i, j: (i, j, 0)),
                   pl.BlockSpec((1, nh, 8, FFT_N2), lambda i, j: (i, 0, j, 0))],
        out_shape=[jax.ShapeDtypeStruct((b, length, nq), BF16),
                   jax.ShapeDtypeStruct((b, nh, length // FFT_N2, FFT_N2), F32)],
        compiler_params=_cparams(("parallel", "parallel")),
        name="front_ab_lat",
    )(x, modtab, w_qkv, w_hy_t)


def _hy_prep_kernel(cur_ref, prev_ref, next_ref, w_ref, x1_ref, x2_ref, v_ref, *, nblk):
    j = pl.program_id(1)
    cur = cur_ref[0].astype(F32)
    rows = cur.shape[0]
    has_prev = (j > 0).astype(F32)
    has_next = (j < nblk - 1).astype(F32)
    prev_row = prev_ref[0][7:8].astype(F32) * has_prev
    next_row = next_ref[0][0:1].astype(F32) * has_next
    rid = lax.broadcasted_iota(jnp.int32, (rows, 1), 0)
    up = jnp.where(rid == 0, prev_row, pltpu.roll(cur, 1, axis=0))
    dn = jnp.where(rid == rows - 1, next_row, pltpu.roll(cur, rows - 1, axis=0))
    w = w_ref[...]
    y = up * w[0:1] + cur * w[1:2] + dn * w[2:3]
    c = HY_CH
    x1_ref[0] = y[:, :c].astype(BF16)
    x2_ref[0] = y[:, c:2 * c].astype(BF16)
    v_ref[0] = y[:, 2 * c:].astype(BF16)


def _hy_prep(u, conv_w, blk0, nblk):
    b, lt, _ = u.shape
    n = 3 * HY_CH
    sub = ROW_BLK // 8
    last8 = lt // 8 - 1
    out = jax.ShapeDtypeStruct((b, nblk * ROW_BLK, HY_CH), BF16)
    ospec = pl.BlockSpec((1, ROW_BLK, HY_CH), lambda i, j: (i, j, 0))
    return pl.pallas_call(
        functools.partial(_hy_prep_kernel, nblk=nblk),
        grid=(b, nblk),
        in_specs=[pl.BlockSpec((1, ROW_BLK, n), lambda i, j: (i, blk0 + j, 0)),
                  pl.BlockSpec((1, 8, n), lambda i, j: (i, jnp.maximum((blk0 + j) * sub - 1, 0), 0)),
                  pl.BlockSpec((1, 8, n), lambda i, j: (i, jnp.minimum((blk0 + j + 1) * sub, last8), 0)),
                  _resident((3, n))],
        out_specs=[ospec, ospec, ospec],
        out_shape=[out, out, out],
        compiler_params=_cparams(("parallel", "parallel")),
        name="hy_prep",
    )(u, u, u, conv_w)


def _hy_filt_kernel(z_ref, w1_ref, b1_ref, fr_ref, w2_ref, b2_ref, w3_ref, ld_ref, o_ref):
    hi = lax.Precision.HIGHEST
    z = z_ref[...]
    fr = fr_ref[...]
    hid = jnp.sin(fr * (jnp.dot(z, w1_ref[...], preferred_element_type=F32, precision=hi) + b1_ref[...]))
    hid = jnp.sin(fr * (jnp.dot(hid, w2_ref[...], preferred_element_type=F32, precision=hi) + b2_ref[...]))
    h = jnp.dot(hid, w3_ref[...], preferred_element_type=F32, precision=hi)
    t = z[:, 0:1]
    o_ref[...] = h * jnp.exp(-t * jnp.exp(ld_ref[...]))


def _pad2(a, rows, cols):
    return jnp.pad(a, ((0, rows - a.shape[0]), (0, cols - a.shape[1])))


def _hy_filters(length, w1, b1, freq, w2, b2, w3, log_decay):
    pos = jnp.arange(length, dtype=F32)
    t = pos / max(length - 1, 1)
    w = 2.0 * math.pi * pos / length
    f = jnp.linspace(1e-4, HY_BANDS - 1, HY_BANDS, dtype=F32)
    ang = w[:, None] * f[None, :]
    z = jnp.concatenate([t[:, None], jnp.cos(ang), -jnp.sin(ang)], -1)
    z = _pad2(z, length, LANES)
    n = w3.shape[1]
    rb = min(length, 512)
    vec = lambda a: _pad2(a.reshape(1, -1), 1, LANES)
    return pl.pallas_call(
        _hy_filt_kernel,
        grid=(length // rb,),
        in_specs=[pl.BlockSpec((rb, LANES), lambda i: (i, 0)),
                  _resident((LANES, LANES)), _resident((1, LANES)), _resident((1, LANES)),
                  _resident((LANES, LANES)), _resident((1, LANES)),
                  _resident((LANES, n)), _resident((1, n))],
        out_specs=pl.BlockSpec((rb, n), lambda i: (i, 0)),
        out_shape=jax.ShapeDtypeStruct((length, n), F32),
        compiler_params=_cparams(("parallel",)),
        name="hy_filters",
    )(z, _pad2(w1, LANES, LANES), vec(b1), vec(freq), _pad2(w2, LANES, LANES), vec(b2),
      _pad2(w3, LANES, n), log_decay.reshape(1, n))


def _bidir_taps(h, skip, length):
    h4 = h.reshape(length, 2, 2, HY_CH)
    cols = []
    for o in range(2):
        hf = h4[:, o, 0].at[0].add(skip[o])
        hb = h4[:, o, 1]
        cols.append(jnp.concatenate([hf, jnp.zeros_like(hf[:1]), hb[:0:-1]], 0))
    return jnp.concatenate(cols, -1)


def _left_mm_kernel(m_ref, x_ref, o_ref):
    o_ref[0] = _dot(m_ref[...], x_ref[0]).astype(o_ref.dtype)


def _left_mm(mat, x, out_dtype, lane_blk):
    g, k, n = x.shape
    m = mat.shape[0]
    lane_blk = min(lane_blk, n)
    return pl.pallas_call(
        _left_mm_kernel, grid=(g, n // lane_blk),
        in_specs=[_resident((m, k)), pl.BlockSpec((1, k, lane_blk), lambda i, j: (i, 0, j))],
        out_specs=pl.BlockSpec((1, m, lane_blk), lambda i, j: (i, 0, j)),
        out_shape=jax.ShapeDtypeStruct((g, m, n), out_dtype),
        compiler_params=_cparams(("parallel", "parallel")),
        name="left_mm",
    )(mat, x)


def _cs(num, den, rows, cols):
    ang = 2.0 * np.pi * np.outer(np.arange(rows), np.arange(cols)) * (num / den)
    return np.cos(ang), np.sin(ang)


def _hy_filt_t_kernel(z_ref, msk_ref, w1_ref, b1_ref, fr_ref, w2_ref, b2_ref, w3_ref, ld_ref, sk_ref, o_ref):
    hi = lax.Precision.HIGHEST
    z = z_ref[...]
    fr = fr_ref[...]
    hid = jnp.sin(fr * (jnp.dot(w1_ref[...], z, preferred_element_type=F32, precision=hi) + b1_ref[...]))
    hid = jnp.sin(fr * (jnp.dot(w2_ref[...], hid, preferred_element_type=F32, precision=hi) + b2_ref[...]))
    h = jnp.dot(w3_ref[0], hid, preferred_element_type=F32, precision=hi)
    h = h * jnp.exp(-jnp.exp(ld_ref[0]) * z[0:1, :])
    msk = msk_ref[...]
    h = h * msk[0:1, :] + sk_ref[...] * msk[1:2, :]
    for s in range(o_ref.shape[1]):
        o_ref[:, s, :] = h[:, s * FFT_N2:(s + 1) * FFT_N2]


def _hy_filters_t(length, w1, b1, freq, w2, b2, w3, log_decay, skip):
    n = 2 * length
    tt = jnp.arange(n, dtype=jnp.int32)
    pos = jnp.where(tt < length, tt, n - tt).astype(F32)
    t = pos / max(length - 1, 1)
    w = 2.0 * math.pi * pos / length
    f = jnp.linspace(1e-4, HY_BANDS - 1, HY_BANDS, dtype=F32)
    ang = f[:, None] * w[None, :]
    z = jnp.concatenate([t[None, :], jnp.cos(ang), -jnp.sin(ang)], 0)
    z = jnp.pad(z, ((0, LANES - z.shape[0]), (0, 0)))
    msk = jnp.stack([(tt != length).astype(F32), (tt == 0).astype(F32)])
    msk = jnp.pad(msk, ((0, 6), (0, 0)))
    col = lambda a: a.reshape(-1, 1)
    c2 = 2 * HY_CH
    nf = w3.shape[0]
    w3d = w3.reshape(nf, 2, 2, HY_CH).transpose(2, 1, 3, 0).reshape(2, c2, nf)
    ldd = log_decay.reshape(2, 2, HY_CH).transpose(1, 0, 2).reshape(2, c2, 1)
    rows = 8
    pb = rows * FFT_N2
    half = length // pb
    return pl.pallas_call(
        _hy_filt_t_kernel,
        grid=(n // pb,),
        in_specs=[pl.BlockSpec((LANES, pb), lambda i: (0, i)),
                  pl.BlockSpec((8, pb), lambda i: (0, i)),
                  _resident((nf, LANES)), _resident((nf, 1)), _resident((nf, 1)),
                  _resident((nf, nf)), _resident((nf, 1)),
                  pl.BlockSpec((1, c2, nf), lambda i: (i // half, 0, 0)),
                  pl.BlockSpec((1, c2, 1), lambda i: (i // half, 0, 0)),
                  _resident((c2, 1))],
        out_specs=pl.BlockSpec((c2, rows, FFT_N2), lambda i: (0, i, 0)),
        out_shape=jax.ShapeDtypeStruct((c2, n // FFT_N2, FFT_N2), F32),
        compiler_params=_cparams(("parallel",)),
        name="hy_filters_t",
    )(z, msk, _pad2(w1.T, nf, LANES), col(b1), col(freq), w2.T, col(b2), w3d, ldd, skip.reshape(c2, 1))


def _dft_mats():
    n = FFT_N2
    c, s = _cs(1, n, n, n)
    ch, sh = c[:, :n // 2], s[:, :n // 2]
    f1_pair = np.block([[ch, sh], [-sh, ch]])
    f1_full = np.concatenate([c, -s], 0)
    m2r = np.block([[c, -s], [s, c]])
    m2i = np.block([[c, s], [-s, c]])
    f1_inv = np.block([[ch.T, -sh.T], [sh.T, ch.T]]) / (n * n)
    cast = lambda a: jnp.asarray(a, dtype=BF16)
    return cast(f1_pair), cast(f1_full), cast(m2r), cast(m2i), cast(f1_inv)


def _twiddle2d(n):
    k1 = lax.broadcasted_iota(jnp.int32, (n, n), 0)
    m2 = lax.broadcasted_iota(jnp.int32, (n, n), 1)
    ang = (k1 * m2).astype(F32) * (2.0 * math.pi / (n * n))
    return jnp.cos(ang), jnp.sin(ang)


def _fwd_spectrum(xs, f1, m2r, c, s):
    n = FFT_N2
    a = _dot(f1, jnp.concatenate(xs, axis=1))
    ts = []
    for g in range(len(xs)):
        ar = a[:n, g * n:(g + 1) * n]
        ai = a[n:, g * n:(g + 1) * n]
        ts.append(jnp.concatenate([ar * c + ai * s, ai * c - ar * s], axis=1))
    return _dot(jnp.concatenate(ts, axis=0).astype(BF16), m2r)


def _filter_spec_t_kernel(t_ref, f1_ref, m2r_ref, c_ref, s_ref, o_ref):
    g = t_ref.shape[0]
    xs = [t_ref[i].astype(BF16) for i in range(g)]
    spec = _fwd_spectrum(xs, f1_ref[...], m2r_ref[...], c_ref[...], s_ref[...])
    for i in range(g):
        o_ref[i] = spec[i * FFT_N2:(i + 1) * FFT_N2]


def _filter_spec_t(taps, f1_full, m2r, twc, tws):
    nch, n1, n = taps.shape
    g = HY_GROUP
    return pl.pallas_call(
        _filter_spec_t_kernel,
        grid=(nch // g,),
        in_specs=[pl.BlockSpec((g, n1, n), lambda i: (i, 0, 0)),
                  _resident(f1_full.shape), _resident(m2r.shape), _resident((n, n)), _resident((n, n))],
        out_specs=pl.BlockSpec((g, n, 2 * n), lambda i: (i, 0, 0)),
        out_shape=jax.ShapeDtypeStruct((nch, n, 2 * n), F32),
        compiler_params=_cparams(("parallel",)),
        name="filter_spec_t",
    )(taps, f1_full, m2r, twc, tws)


def _hyena_core_kernel(x1_ref, x2_ref, v_ref, w1_ref, w2_ref, wv_ref, h0_ref, h1_ref,
                       f1_ref, m2r_ref, m2i_ref, f1i_ref, c_ref, s_ref, o_ref):
    n = FFT_N2
    n1 = v_ref.shape[2]
    c = c_ref[...]
    s = s_ref[...]
    lane = lax.broadcasted_iota(jnp.int32, (n1, n), 1)
    row = lax.broadcasted_iota(jnp.int32, (n1, n), 0)
    first_lane, last_lane = lane == 0, lane == n - 1
    seq_start, seq_end = first_lane & (row == 0), last_lane & (row == n1 - 1)

    def short_conv(x, w):
        r = pltpu.roll(x, 1, axis=1)
        up = jnp.where(first_lane, pltpu.roll(r, 1, axis=0), r)
        up = jnp.where(seq_start, 0.0, up)
        l = pltpu.roll(x, n - 1, axis=1)
        dn = jnp.where(last_lane, pltpu.roll(l, n1 - 1, axis=0), l)
        dn = jnp.where(seq_end, 0.0, dn)
        return up * w[0:1] + x * w[1:2] + dn * w[2:3]

    def conv_all(xss, h_ref):
        spec_s = [_fwd_spectrum(xs, f1_ref[...], m2r_ref[...], c, s) for xs in xss]
        y_s = []
        for chain, spec in enumerate(spec_s):
            ys = []
            for g in range(HY_GROUP):
                xr = spec[g * n:(g + 1) * n, :n]
                xi = spec[g * n:(g + 1) * n, n:]
                hh = h_ref[chain * HY_GROUP + g]
                hr, hi = hh[:, :n], hh[:, n:]
                ys.append(jnp.concatenate([xr * hr - xi * hi, xr * hi + xi * hr], axis=1))
            y_s.append(jnp.concatenate(ys, axis=0).astype(BF16))
        bm_s = [_dot(y, m2i_ref[...]) for y in y_s]
        b_s = []
        for bm in bm_s:
            bs = []
            for g in range(HY_GROUP):
                br = bm[g * n:(g + 1) * n, :n]
                bi = bm[g * n:(g + 1) * n, n:]
                bs.append(jnp.concatenate([br * c - bi * s, bi * c + br * s], axis=0))
            b_s.append(jnp.concatenate(bs, axis=1).astype(BF16))
        outs = [_dot(f1i_ref[...], bc) for bc in b_s]
        return [[y[:, g * n:(g + 1) * n] for g in range(HY_GROUP)] for y in outs]

    def pair(ref, w_ref, ch):
        return jnp.concatenate([short_conv(ref[0, ch], w_ref[ch]), short_conv(ref[1, ch], w_ref[ch])], axis=0)

    chans = [[chain * HY_GROUP + g for g in range(HY_GROUP)] for chain in range(HY_CHAINS)]
    y1 = conv_all([[pair(v_ref, wv_ref, ch).astype(BF16) for ch in grp] for grp in chans], h0_ref)
    z = [[(y1[k][g] * pair(x1_ref, w1_ref, ch)).astype(BF16) for g, ch in enumerate(grp)]
         for k, grp in enumerate(chans)]
    y2 = conv_all(z, h1_ref)
    for k, grp in enumerate(chans):
        for g, ch in enumerate(grp):
            y = y2[k][g] * pair(x2_ref, w2_ref, ch)
            o_ref[0, :, ch, :] = y[:n1]
            o_ref[1, :, ch, :] = y[n1:]


def _hyena_core(ut, conv_w, spec, mats, twc, tws):
    f1_pair, _, m2r, m2i, f1_inv = mats
    b, nch, n1, n = ut.shape
    assert b % 2 == 0
    cb = HY_CBLK
    nblk = HY_CH // cb
    wt = jnp.broadcast_to(conv_w.T[:, :, None], (nch, conv_w.shape[0], n))
    xspec = lambda off: pl.BlockSpec((2, cb, n1, n), lambda j, i: (i, off * nblk + j, 0, 0))
    wspec = lambda off: pl.BlockSpec((cb, conv_w.shape[0], n), lambda j, i: (off * nblk + j, 0, 0))
    hspec = lambda off: pl.BlockSpec((cb, n, 2 * n), lambda j, i: (off * nblk + j, 0, 0))
    return pl.pallas_call(
        _hyena_core_kernel,
        grid=(nblk, b // 2),
        in_specs=[xspec(0), xspec(1), xspec(2), wspec(0), wspec(1), wspec(2), hspec(0), hspec(1),
                  _resident(f1_pair.shape), _resident(m2r.shape), _resident(m2i.shape), _resident(f1_inv.shape),
                  _resident((n, n)), _resident((n, n))],
        out_specs=pl.BlockSpec((2, n1, cb, n), lambda j, i: (i, 0, j, 0)),
        out_shape=jax.ShapeDtypeStruct((b, n1, HY_CH, n), F32),
        compiler_params=_cparams(("parallel", "parallel")),
        name="hyena_core",
    )(ut, ut, ut, wt, wt, wt, spec, spec, f1_pair, m2r, m2i, f1_inv, twc, tws)


def _hy_ctx_kernel(v_ref, x1_ref, x2_ref, f_ref, fi_ref, h_ref, o_ref):
    nf = f_ref.shape[0] // 2
    zin = v_ref[0]
    gates = (x1_ref, x2_ref)
    for o in range(2):
        x = _dot(f_ref[...], zin)
        xr, xi = x[:nf], x[nf:]
        hr = h_ref[o, :nf]
        hi = h_ref[o, nf:]
        y = jnp.concatenate([xr * hr - xi * hi, xr * hi + xi * hr], axis=0).astype(BF16)
        zin = (_dot(fi_ref[...], y) * gates[o][0].astype(F32)).astype(BF16)
    o_ref[0] = zin


def _hy_ctx(v, x1, x2, taps):
    b, lc, c = v.shape
    nf = 2 * lc
    cm, sm = _cs(1, nf, nf, nf)
    fwd = jnp.asarray(np.concatenate([cm[:, :lc], -sm[:, :lc]], 0), dtype=BF16)
    fwd_full = jnp.asarray(np.concatenate([cm, -sm], 0), dtype=BF16)
    inv = jnp.asarray(np.concatenate([cm[:lc, :], -sm[:lc, :]], 1) / nf, dtype=BF16)
    spec = _left_mm(fwd_full, taps.astype(BF16).reshape(1, nf, 2 * c), F32, 2 * c)
    spec = spec.reshape(2 * nf, 2, c).transpose(1, 0, 2)
    blk = pl.BlockSpec((1, lc, c), lambda i: (i, 0, 0))
    return pl.pallas_call(
        _hy_ctx_kernel,
        grid=(b,),
        in_specs=[blk, blk, blk, _resident((2 * nf, lc)), _resident((lc, 2 * nf)),
                  _resident((2, 2 * nf, c))],
        out_specs=blk,
        out_shape=jax.ShapeDtypeStruct((b, lc, c), BF16),
        compiler_params=_cparams(("parallel",)),
        name="hy_ctx",
    )(v, x1, x2, fwd, inv, spec)


def _pair_rows(q2):
    lane = lax.broadcasted_iota(jnp.int32, q2.shape, 1)
    zero = jnp.zeros_like(q2)
    return jnp.concatenate([jnp.where(lane < NA_HD, q2, zero), jnp.where(lane >= NA_HD, q2, zero)], axis=0)


def _unpair_rows(o):
    r = o.shape[0] // 2
    lane = lax.broadcasted_iota(jnp.int32, (r, o.shape[1]), 1)
    return jnp.where(lane < NA_HD, o[:r], o[r:])


def _pair_softmax_pv(scores, values):
    m = scores[0].max(axis=-1, keepdims=True)
    for s in scores[1:]:
        m = jnp.maximum(m, s.max(axis=-1, keepdims=True))
    den = None
    acc = None
    for s, v in zip(scores, values):
        p = jnp.exp(s - m)
        d = p.sum(axis=-1, keepdims=True)
        a = _dot(p.astype(BF16), v)
        den = d if den is None else den + d
        acc = a if acc is None else acc + a
    return acc / den


def _natten_kernel(q_ref, k0, k1, k2, k3, v0, v1, v2, v3, kc_ref, vc_ref, bias_ref, o_ref,
                   kwin, vwin, *, rows):
    g = pl.program_id(1)
    rb = 4 * GRID_W
    for i, (kr, vr) in enumerate(((k0, v0), (k1, v1), (k2, v2), (k3, v3))):
        kwin[i * rb:(i + 1) * rb, :] = kr[0]
        vwin[i * rb:(i + 1) * rb, :] = vr[0]
    base = 4 * jnp.clip(2 * g - 1, 0, rows // 4 - 4)
    nwin = NA_WIN_R * GRID_W
    qscale = jnp.asarray(NA_HD ** -0.5, BF16)

    ones_lat = jnp.ones((nwin, LANES), BF16)
    ones_ctx = jnp.ones((kc_ref.shape[1], LANES), BF16)

    def rows_body(it, carry):
        work = []
        for u in range(NA_ROWS_PER_TRIP):
            rr = it * NA_ROWS_PER_TRIP + u
            r = 8 * g + rr
            rs = jnp.clip(r - NA_WIN_R // 2, 0, rows - NA_WIN_R)
            st = pl.multiple_of((rs - base) * GRID_W, GRID_W)
            qo = pl.multiple_of(rr * GRID_W, GRID_W)
            work += [(qo, st, rs - r + NA_WIN_R - 1, p) for p in range(NA_HEADS // 2)]
        scores = []
        for qo, st, d0, p in work:
            ls = slice(p * LANES, (p + 1) * LANES)
            qp = _pair_rows(q_ref[0, pl.ds(qo, GRID_W), ls] * qscale)
            scores.append((_dot_nt(qp, kwin[pl.ds(st, nwin), ls]) + bias_ref[d0, p].astype(F32),
                           _dot_nt(qp, kc_ref[0, :, ls])))
        maxima = [jnp.maximum(a.max(axis=-1, keepdims=True), b.max(axis=-1, keepdims=True)) for a, b in scores]
        for (qo, st, d0, p), (s_lat, s_ctx), m in zip(work, scores, maxima):
            ls = slice(p * LANES, (p + 1) * LANES)
            v_lat = jnp.concatenate([vwin[pl.ds(st, nwin), ls], ones_lat], axis=1)
            v_ctx = jnp.concatenate([vc_ref[0, :, ls], ones_ctx], axis=1)
            acc = _dot(jnp.exp(s_lat - m).astype(BF16), v_lat) + _dot(jnp.exp(s_ctx - m).astype(BF16), v_ctx)
            o = acc[:, :LANES] / acc[:, LANES:LANES + 1]
            o_ref[0, pl.ds(qo, GRID_W), ls] = _unpair_rows(o).astype(o_ref.dtype)
        return carry

    lax.fori_loop(0, 8 // NA_ROWS_PER_TRIP, rows_body, 0)


def _natten_bias(rpb):
    c = np.arange(GRID_W)[:, None]
    kc = np.arange(GRID_W)[None, :]
    cs = np.clip(c - NA_WIN_C // 2, 0, GRID_W - NA_WIN_C)
    valid = (kc >= cs) & (kc < cs + NA_WIN_C)
    dc = np.clip(kc - c + NA_WIN_C - 1, 0, 2 * NA_WIN_C - 2)
    ndc = 2 * NA_WIN_C - 1
    pick = ((dc[None] == np.arange(ndc)[:, None, None]) & valid[None]).astype(np.float32)
    tb = jnp.einsum('hrd,dck->hrck', rpb, jnp.asarray(pick), precision=lax.Precision.HIGHEST)
    tb = tb + jnp.asarray(np.where(valid, 0.0, NEG_BIG).astype(np.float32))
    slabs = []
    for d0 in range(NA_WIN_R):
        s = tb[:, d0:d0 + NA_WIN_R]
        s = s.transpose(0, 2, 1, 3).reshape(NA_HEADS, GRID_W, NA_WIN_R * GRID_W)
        slabs.append(s.reshape(NA_HEADS // 2, 2 * GRID_W, NA_WIN_R * GRID_W))
    return jnp.stack(slabs).astype(BF16)


def _natten(uq, uc, bias):
    b, length, _ = uq.shape
    c = NA_HEADS * NA_HD
    rows = length // GRID_W
    rb = 4 * GRID_W
    nkb = length // rb
    qrows = 8 * GRID_W
    lc = uc.shape[1]

    def kv_spec(col, off):
        return pl.BlockSpec((1, rb, c), lambda i, g: (i, jnp.clip(2 * g - 1, 0, nkb - 4) + off, col))

    return pl.pallas_call(
        functools.partial(_natten_kernel, rows=rows),
        grid=(b, rows // 8),
        in_specs=[pl.BlockSpec((1, qrows, c), lambda i, g: (i, g, 0))]
                 + [kv_spec(1, o) for o in range(4)] + [kv_spec(2, o) for o in range(4)]
                 + [pl.BlockSpec((1, lc, c), lambda i, g: (i, 0, 4)),
                    pl.BlockSpec((1, lc, c), lambda i, g: (i, 0, 5)),
                    _resident(bias.shape)],
        out_specs=pl.BlockSpec((1, qrows, c), lambda i, g: (i, g, 0)),
        out_shape=jax.ShapeDtypeStruct((b, length, c), BF16),
        scratch_shapes=[pltpu.VMEM((4 * rb, c), BF16), pltpu.VMEM((4 * rb, c), BF16)],
        compiler_params=_cparams(("parallel", "parallel")),
        name="natten",
    )(uq, *([uq] * 8), uc, uc, bias)


def _ctx_attn_kernel(q_ref, k_ref, v_ref, o_ref):
    qscale = jnp.asarray(NA_HD ** -0.5, BF16)
    for p in range(NA_HEADS // 2):
        ls = slice(p * LANES, (p + 1) * LANES)
        qp = _pair_rows(q_ref[0, :, ls] * qscale)
        o = _pair_softmax_pv([_dot_nt(qp, k_ref[0, :, ls])], [v_ref[0, :, ls]])
        o_ref[0, :, ls] = _unpair_rows(o).astype(o_ref.dtype)


def _ctx_attn(u):
    b, lc, _ = u.shape
    c = NA_HEADS * NA_HD
    spec = lambda col: pl.BlockSpec((1, lc, c), lambda i: (i, 0, col))
    return pl.pallas_call(
        _ctx_attn_kernel,
        grid=(b,),
        in_specs=[spec(3), spec(4), spec(5)],
        out_specs=pl.BlockSpec((1, lc, c), lambda i: (i, 0, 0)),
        out_shape=jax.ShapeDtypeStruct((b, lc, c), BF16),
        compiler_params=_cparams(("parallel",)),
        name="ctx_attn",
    )(u, u, u)


def _post_kernel(x_ref, mod_ref, ya_ref, yb_ref, wa_ref, wb_ref, lng_ref, lnb_ref, w1_ref, w2_ref, o_ref, *,
                 ya_slabs, yb_slabs):
    m = mod_ref[0, 0]
    lng = lng_ref[...]
    lnb = lnb_ref[...]
    ff = w1_ref.shape[1]
    step = MLP_CHUNK
    sub = ROW_BLK
    nsub = x_ref.shape[1] // sub

    def rows_of(ref, slabs, t):
        if slabs:
            tiles = range(t * sub // LANES, (t + 1) * sub // LANES)
            return jnp.concatenate([ref[0, s].T for s in tiles], axis=0).astype(BF16)
        return ref[0, t * sub:(t + 1) * sub, :]

    def head(t):
        rs = slice(t * sub, (t + 1) * sub)
        y = _dot(rows_of(ya_ref, ya_slabs, t), wa_ref[...]) + _dot(rows_of(yb_ref, yb_slabs, t), wb_ref[...])
        x1 = _norm_rows(ALPHA * x_ref[0, rs, :] + m[2:3] * y) * lng[0:1] + lnb[0:1]
        return x1, (_norm_rows(x1) * (1.0 + m[4:5]) + m[3:4]).astype(BF16)

    def mlp(h):
        acc = None
        for c in range(ff // step):
            a = jnp.maximum(_dot(h, w1_ref[:, c * step:(c + 1) * step]), 0.0)
            d = _dot((a * a).astype(BF16), w2_ref[c * step:(c + 1) * step, :])
            acc = d if acc is None else acc + d
        return acc

    cur = head(0)
    for t in range(nsub):
        nxt = head(t + 1) if t + 1 < nsub else None
        acc = mlp(cur[1])
        o_ref[0, t * sub:(t + 1) * sub, :] = _norm_rows(ALPHA * cur[0] + m[5:6] * acc) * lng[1:2] + lnb[1:2]
        cur = nxt


def _post(x, modtab, mod_row, ya, yb, wa, wb, lng, lnb, w1, w2, ya_slabs=False, yb_slabs=False):
    b, r, d = x.shape
    ka, kb = wa.shape[0], wb.shape[0]
    rows = min(r, 2 * ROW_BLK)
    row = lambda k: pl.BlockSpec((1, rows, k), lambda i, j: (i, j, 0))
    slab = lambda k: pl.BlockSpec((1, rows // LANES, k, LANES), lambda i, j: (i, j, 0, 0))
    return pl.pallas_call(
        functools.partial(_post_kernel, ya_slabs=ya_slabs, yb_slabs=yb_slabs),
        grid=(b, r // rows),
        in_specs=[row(d), pl.BlockSpec((1, 1, 6, d), lambda i, j: (i, mod_row, 0, 0)),
                  slab(ka) if ya_slabs else row(ka), slab(kb) if yb_slabs else row(kb),
                  _resident(wa.shape), _resident(wb.shape), _resident(lng.shape), _resident(lnb.shape),
                  _resident(w1.shape), _resident(w2.shape)],
        out_specs=row(d),
        out_shape=jax.ShapeDtypeStruct((b, r, d), F32),
        compiler_params=_cparams(("parallel", "parallel")),
        name="post_mixer",
    )(x, modtab, ya, yb, wa, wb, lng, lnb, w1, w2)


def _rope(x, cos, sinl, sinr):
    reps = x.shape[1] // LANES
    tile = lambda t: jnp.concatenate([t] * reps, axis=1)
    n = x.shape[1]
    quarter = MLA_ROPE // 4
    return (x * tile(cos) + pltpu.roll(x, n - quarter, axis=1) * tile(sinl)
            + pltpu.roll(x, quarter, axis=1) * tile(sinr))


def _front_cd_kernel(x_ref, xc_ref, mod_ref, w_ref, qn_ref, kvn_ref, wuq_ref, wuk_ref, wuv_ref, epe_ref, one_ref,
                     fng_ref, fnb_ref, avg_ref, cbd_ref, sbd_ref,
                     cos_ref, sl_ref, sr_ref,
                     q_ref, k_ref, v_ref, p_ref, qf_ref, *, ctx_blk):
    m = mod_ref[0, 0]
    is_ctx = (jnp.zeros((ROW_BLK, 1), jnp.int32) + pl.program_id(1)) == ctx_blk
    x = jnp.where(is_ctx, xc_ref[0], x_ref[0])
    h = _norm_rows(x) * (1.0 + m[1:2]) + m[0:1]
    u = _dot(h.astype(BF16), w_ref[...])
    o_kv = MLA_Q_RANK
    o_fn = o_kv + MLA_KV_RANK
    o_pe = o_fn + FN_CH

    def rms(x, g):
        return x * lax.rsqrt(jnp.mean(x * x, axis=-1, keepdims=True) + LN_EPS) * g

    cq = rms(u[:, :o_kv], qn_ref[...]).astype(BF16)
    q = _dot(cq, wuq_ref[...])
    q_ref[0] = _rope(q, cos_ref[...], sl_ref[...], sr_ref[...]).astype(BF16)

    ckv = rms(u[:, o_kv:o_fn], kvn_ref[...]).astype(BF16)
    kpe = _dot(u[:, o_pe:].astype(BF16), epe_ref[...])
    k = _dot(ckv, wuk_ref[...]) + _rope(kpe, cos_ref[...], sl_ref[...], sr_ref[...])
    k_ref[0] = k.astype(BF16)
    v_ref[0] = (_dot(ckv, wuv_ref[...]) + one_ref[...]).astype(BF16)

    uf = u[:, o_fn:o_pe]
    avg = avg_ref[...]
    uc = uf - _dot(uf.astype(BF16), avg)
    var = _dot((uc * uc).astype(BF16), avg)
    ug = (uc * lax.rsqrt(var + LN_EPS) * fng_ref[...] + fnb_ref[...]).astype(BF16)
    p_ref[0] = _dot_nt(cbd_ref[...], ug).astype(BF16)
    qf_ref[0] = _dot_nt(sbd_ref[...], ug).astype(BF16)


def _rope_tables(length, lt):
    t = jnp.arange(lt, dtype=jnp.int32)
    rows = (t // GRID_W).astype(F32)
    cols = (t % GRID_W).astype(F32)
    half = MLA_ROPE // 2
    inv = ROPE_THETA ** (-jnp.arange(0, half, 2, dtype=F32) / half)
    ar = rows[:, None] * inv[None, :]
    ac = cols[:, None] * inv[None, :]
    ang = jnp.concatenate([ar, ar, ac, ac], -1)
    is_lat = (t < length)[:, None]
    cos = jnp.where(is_lat, jnp.cos(ang), 1.0)
    sin = jnp.where(is_lat, jnp.sin(ang), 0.0)
    qd = MLA_ROPE // 4
    ones = jnp.ones((lt, MLA_NOPE), F32)
    zeros = jnp.zeros((lt, MLA_NOPE), F32)
    tail1 = jnp.ones((lt, HEAD_PAD - MLA_NOPE - MLA_ROPE), F32)
    tail0 = jnp.zeros((lt, HEAD_PAD - MLA_NOPE - MLA_ROPE), F32)
    z8 = jnp.zeros((lt, qd), F32)
    c = jnp.concatenate([ones, cos, tail1], -1)
    sl = jnp.concatenate([zeros, -sin[:, :qd], z8, -sin[:, 2 * qd:3 * qd], z8, tail0], -1)
    sr = jnp.concatenate([zeros, z8, sin[:, qd:2 * qd], z8, sin[:, 3 * qd:], tail0], -1)
    return c, sl, sr


def _head_slots(w, per_head, take_from, take_n):
    k = w.shape[0]
    w3 = w.reshape(k, MLA_HEADS, per_head)[:, :, take_from:take_from + take_n]
    w3 = jnp.pad(w3, ((0, 0), (0, 0), (0, HEAD_PAD - take_n)))
    return w3.reshape(k, MLA_HEADS * HEAD_PAD)


def _front_cd(xl, xc, modtab, w_in, q_norm, w_uq, kv_norm, w_ukv, fn_g, fn_b):
    b, length, d = xl.shape
    lt = length + xc.shape[1]
    nlat = length // ROW_BLK
    o_kv = MLA_Q_RANK
    o_pe = o_kv + MLA_KV_RANK
    o_fn = o_pe + MLA_ROPE
    hw = MLA_HEADS * HEAD_PAD
    w_perm = jnp.concatenate([w_in[:, :o_pe], w_in[:, o_fn:], w_in[:, o_pe:o_fn],
                              jnp.zeros((d, LANES - MLA_ROPE), w_in.dtype)], -1).astype(BF16)
    q_scale = (MLA_NOPE + MLA_ROPE) ** -0.5 * math.log2(math.e)
    wuq = _head_slots(w_uq * q_scale, MLA_NOPE + MLA_ROPE, 0, MLA_NOPE + MLA_ROPE).astype(BF16)
    wuk = _head_slots(w_ukv, MLA_NOPE + MLA_V, 0, MLA_NOPE).astype(BF16)
    wuv = _head_slots(w_ukv, MLA_NOPE + MLA_V, MLA_NOPE, MLA_V).astype(BF16)
    epe = np.zeros((LANES, hw), np.float32)
    for hd in range(MLA_HEADS):
        for i in range(MLA_ROPE):
            epe[i, hd * HEAD_PAD + MLA_NOPE + i] = 1.0
    epe = jnp.asarray(epe, dtype=BF16)
    ones_col = np.zeros((1, hw), np.float32)
    ones_col[0, MLA_V::HEAD_PAD] = 1.0
    ones_col = jnp.asarray(ones_col)
    cm, sm = _cs(1, FN_GD, FN_GD, FN_GD)
    eye = np.eye(FN_GROUPS)
    cbd = jnp.asarray(np.kron(eye, cm), dtype=BF16)
    sbd = jnp.asarray(np.kron(eye, -sm), dtype=BF16)
    avg = jnp.asarray(np.kron(eye, np.full((FN_GD, FN_GD), 1.0 / FN_GD)), dtype=BF16)
    rtab = _rope_tables(length, lt)
    row = lambda n: pl.BlockSpec((1, ROW_BLK, n), lambda i, j: (i, j, 0))
    tab = pl.BlockSpec((ROW_BLK, HEAD_PAD), lambda i, j: (j, 0))
    out = lambda n: jax.ShapeDtypeStruct((b, lt, n), BF16)
    fn_t = pl.BlockSpec((1, FN_CH, ROW_BLK), lambda i, j: (i, 0, j))
    fn_out = jax.ShapeDtypeStruct((b, FN_CH, lt), BF16)
    return pl.pallas_call(
        functools.partial(_front_cd_kernel, ctx_blk=nlat),
        grid=(b, lt // ROW_BLK),
        in_specs=[pl.BlockSpec((1, ROW_BLK, d), lambda i, j: (i, jnp.minimum(j, nlat - 1), 0)),
                  pl.BlockSpec((1, ROW_BLK, d), lambda i, j: (i, 0, 0)),
                  pl.BlockSpec((1, 1, 6, d), lambda i, j: (i, j // nlat, 0, 0)),
                  _resident(w_perm.shape), _resident((1, MLA_Q_RANK)), _resident((1, MLA_KV_RANK)),
                  _resident(wuq.shape), _resident(wuk.shape), _resident(wuv.shape), _resident(epe.shape),
                  _resident(ones_col.shape),
                  _resident((1, FN_CH)), _resident((1, FN_CH)), _resident(avg.shape), _resident(cbd.shape),
                  _resident(sbd.shape),
                  tab, tab, tab],
        out_specs=[row(hw), row(hw), row(hw), fn_t, fn_t],
        out_shape=[out(hw), out(hw), out(hw), fn_out, fn_out],
        compiler_params=_cparams(("parallel", "parallel")),
        name="front_cd",
    )(xl, xc, modtab, w_perm, q_norm.reshape(1, -1), kv_norm.reshape(1, -1), wuq, wuk, wuv, epe, ones_col,
      fn_g.reshape(1, -1), fn_b.reshape(1, -1), avg, cbd, sbd, *rtab)


def _mla_kernel(q_ref, k_ref, v_ref, o_ref, *, rows):
    n = q_ref.shape[1] // rows

    def scores(i):
        s = _dot_nt(q_ref[0, i * rows:(i + 1) * rows, :], k_ref[0])
        return s, s.max(axis=-1, keepdims=True)

    def finish(i, s, m):
        acc = _dot(jnp.exp2(s - m).astype(BF16), v_ref[0])
        o_ref[0, i * rows:(i + 1) * rows, :] = (acc / acc[:, MLA_V:MLA_V + 1]).astype(o_ref.dtype)

    pending = scores(0)
    for i in range(n):
        nxt = scores(i + 1) if i + 1 < n else None
        finish(i, *pending)
        pending = nxt


def _mla_attention(q, k, v, length):
    b, lt, hw = q.shape
    heads = hw // HEAD_PAD
    tq = MLA_Q_BLK
    kv = pl.BlockSpec((1, lt, HEAD_PAD), lambda i, h, j: (i, 0, h))
    qs = pl.BlockSpec((1, tq, HEAD_PAD), lambda i, h, j: (i, j, h))
    return pl.pallas_call(
        functools.partial(_mla_kernel, rows=MLA_Q_SUB),
        grid=(b, heads, length // tq),
        in_specs=[qs, kv, kv],
        out_specs=qs,
        out_shape=jax.ShapeDtypeStruct((b, length, hw), BF16),
        compiler_params=_cparams(("parallel", "parallel", "parallel")),
        name="mla_attention",
    )(q, k, v)


def _fnet_kernel(pa_ref, pb_ref, qa_ref, qb_ref, m1_ref, c_ref, s_ref, m2_ref, o_ref, zr_scr, zi_scr):
    n1 = FN_N1
    n2 = zr_scr.shape[2] // 2
    pairs = zr_scr.shape[0]

    def fill(scr, a_ref, b_ref):
        za = a_ref[0].astype(F32)
        zb = b_ref[0].astype(F32)
        for r in range(n1):
            scr[:, r, :] = jnp.concatenate([za[:, r * n2:(r + 1) * n2], zb[:, r * n2:(r + 1) * n2]], axis=1)

    fill(zr_scr, pa_ref, pb_ref)
    fill(zi_scr, qa_ref, qb_ref)
    c = c_ref[...]
    s = s_ref[...]
    for g0 in range(0, pairs, FN_GROUP):
        x = jnp.concatenate([jnp.concatenate([zr_scr[g0 + g], zi_scr[g0 + g]], axis=0) for g in range(FN_GROUP)],
                            axis=1).astype(BF16)
        a = _dot(m1_ref[...], x)
        ts = []
        for g in range(FN_GROUP):
            ar = a[:n1, g * LANES:(g + 1) * LANES]
            ai = a[n1:, g * LANES:(g + 1) * LANES]
            ts.append(jnp.concatenate([ar * c + ai * s, ai * c - ar * s], axis=1))
        y = _dot(jnp.concatenate(ts, axis=0).astype(BF16), m2_ref[...])
        for g in range(FN_GROUP):
            yt = y[g * n1:(g + 1) * n1].T
            o_ref[0, :, 2 * (g0 + g), :] = yt[:n2]
            o_ref[0, :, 2 * (g0 + g) + 1, :] = yt[n2:]


def _fnet(pt, qt, length):
    b, ch, _ = pt.shape
    n1 = FN_N1
    n2 = length // n1
    assert 2 * n2 == LANES
    half = ch // 2
    cm, sm = _cs(1, n1, n1, n1)
    m1 = jnp.asarray(np.block([[cm, sm], [-sm, cm]]), dtype=BF16)
    c2, s2 = _cs(1, n2, n2, n2)
    z = np.zeros_like(c2)
    m2 = np.block([[c2, z], [z, c2], [s2, z], [z, s2]]) / math.sqrt(length * FN_GD)
    m2 = jnp.asarray(m2, dtype=BF16)
    k1 = lax.broadcasted_iota(jnp.int32, (n1, LANES), 0)
    m = lax.broadcasted_iota(jnp.int32, (n1, LANES), 1) % n2
    ang = (k1 * m).astype(F32) * (2.0 * math.pi / length)
    twc, tws = jnp.cos(ang), jnp.sin(ang)
    pairs = FN_PAIRS
    nblk = half // pairs
    spec = lambda off: pl.BlockSpec((1, pairs, length), lambda j, i: (i, off * nblk + j, 0))
    return pl.pallas_call(
        _fnet_kernel,
        grid=(nblk, b),
        in_specs=[spec(0), spec(1), spec(0), spec(1), _resident(m1.shape), _resident((n1, LANES)),
                  _resident((n1, LANES)), _resident(m2.shape)],
        out_specs=pl.BlockSpec((1, n2, 2 * pairs, LANES), lambda j, i: (i, 0, j, 0)),
        out_shape=jax.ShapeDtypeStruct((b, n2, ch, LANES), F32),
        scratch_shapes=[pltpu.VMEM((pairs, n1, LANES), F32), pltpu.VMEM((pairs, n1, LANES), F32)],
        compiler_params=_cparams(("parallel", "parallel")),
        name="fnet",
    )(pt, pt, qt, qt, m1, twc, tws, m2)


def kernel(x, c, ctx, c_ctx, mod_w, mod_b, ln_g, ln_b, mlp_w1, mlp_w2,
           ab_w_in, ab_w_out, hy_conv_w, hy_w1, hy_b1, hy_freq, hy_w2, hy_b2, hy_w3, hy_log_decay, hy_skip, na_rpb,
           cd_w_in, cd_w_out, mla_q_norm, mla_w_uq, mla_kv_norm, mla_w_ukv, fn_norm_g, fn_norm_b):
    b, length, d = x.shape
    lc = ctx.shape[1]

    cc = jnp.concatenate([c, c_ctx[None], jnp.zeros((8 - b - 1, d), F32)], 0)
    mods = _mod_vectors(cc, mod_w, mod_b).reshape(DEPTH, 8, 6, d)
    modtab = [jnp.stack([mods[l, :b], jnp.broadcast_to(mods[l, b], (b, 6, d))], axis=1) for l in range(DEPTH)]

    n_hy = 3 * HY_CH
    w_in = ab_w_in[0].astype(BF16)
    uq, ut = _front_ab_lat(x, modtab[0], w_in[:, n_hy:], w_in[:, :n_hy].T)
    uc = _front_ab_ctx(ctx, modtab[0], w_in)
    fargs = (hy_w1[0], hy_b1[0], hy_freq[0], hy_w2[0], hy_b2[0], hy_w3[0], hy_log_decay[0])
    mats = _dft_mats()
    twc, tws = _twiddle2d(FFT_N2)
    spec = _filter_spec_t(_hy_filters_t(length, *fargs, hy_skip[0]), mats[1], mats[2], twc, tws)
    y_hy_t = _hyena_core(ut, hy_conv_w[0], spec, mats, twc, tws)
    x1c, x2c, vc = _hy_prep(uc, hy_conv_w[0], 0, lc // ROW_BLK)
    y_hy_c = _hy_ctx(vc, x1c, x2c, _bidir_taps(_hy_filters(lc, *fargs), hy_skip[0], lc))
    y_na = _natten(uq, uc, _natten_bias(na_rpb[0]))
    y_na_c = _ctx_attn(uc)
    w_out = ab_w_out[0].astype(BF16)
    mlp = (ln_g[0], ln_b[0], mlp_w1[0].astype(BF16), mlp_w2[0].astype(BF16))
    xl = _post(x, modtab[0], 0, y_hy_t, y_na, w_out[:HY_CH], w_out[HY_CH:], *mlp, ya_slabs=True)
    xc = _post(ctx, modtab[0], 1, y_hy_c, y_na_c, w_out[:HY_CH], w_out[HY_CH:], *mlp)

    q, k, vv, pt, qt = _front_cd(xl, xc, modtab[1], cd_w_in[0], mla_q_norm[0], mla_w_uq[0], mla_kv_norm[0],
                                 mla_w_ukv[0], fn_norm_g[0], fn_norm_b[0])
    o = _mla_attention(q, k, vv, length)
    y_fn = _fnet(pt, qt, length)
    w_out = cd_w_out[0]
    n_mla = MLA_HEADS * MLA_V
    wa = jnp.pad(w_out[:n_mla].reshape(MLA_HEADS, MLA_V, d), ((0, 0), (0, HEAD_PAD - MLA_V), (0, 0)))
    wa = wa.reshape(MLA_HEADS * HEAD_PAD, d).astype(BF16)
    wb = w_out[n_mla:].reshape(2, FN_CH // 2, d).transpose(1, 0, 2).reshape(FN_CH, d)
    return _post(xl, modtab[1], 0, o, y_fn, wa, wb.astype(BF16),
                 ln_g[1], ln_b[1], mlp_w1[1].astype(BF16), mlp_w2[1].astype(BF16), yb_slabs=True)
```

```python
import functools
import math

import numpy as np
import jax
import jax.numpy as jnp
from jax import lax
from jax.experimental import pallas as pl
from jax.experimental.pallas import tpu as pltpu

F32 = jnp.float32
BF16 = jnp.bfloat16

D_MODEL = 1024
DEPTH = 2
GRID_W = 64
HY_CH = 512
HY_EMB = 33
HY_BANDS = (HY_EMB - 1) // 2
NA_HEADS = 8
NA_HD = 64
NA_WIN_R = 8
NA_WIN_C = 16
MLA_HEADS = 8
MLA_Q_RANK = 384
MLA_KV_RANK = 256
MLA_NOPE = 64
MLA_ROPE = 32
MLA_V = 96
ROPE_THETA = 10000.0
FN_CH = 256
FN_GROUPS = 4
FN_GD = FN_CH // FN_GROUPS
D_FF = 4 * D_MODEL
ALPHA = (2.0 * DEPTH) ** 0.25
LN_EPS = 1e-5

LANES = 128
ROW_BLK = 256
HEAD_PAD = 128
FFT_N2 = 128
NA_ROWS_PER_TRIP = 8
FN_N1 = 128
FN_PAIRS = 16
FN_GROUP = 4
MLP_CHUNK = 1024
MLA_Q_BLK = 1024
MLA_Q_SUB = 256
HY_GROUP = 16
HY_CHAINS = 2
HY_CBLK = HY_GROUP * HY_CHAINS
VMEM_LIMIT = 56 * 1024 * 1024
NEG_BIG = -1e30


def _cparams(sem, vmem=VMEM_LIMIT):
    return pltpu.CompilerParams(dimension_semantics=sem, vmem_limit_bytes=vmem)


def _resident(shape):
    nd = len(shape)
    return pl.BlockSpec(shape, lambda *_: (0,) * nd, pipeline_mode=pl.Buffered(1))


def _norm_rows(x):
    mu = jnp.mean(x, axis=-1, keepdims=True)
    xc = x - mu
    var = jnp.mean(xc * xc, axis=-1, keepdims=True)
    return xc * lax.rsqrt(var + LN_EPS)


def _dot(a, b):
    return jnp.dot(a, b, preferred_element_type=F32)


def _mxu_const(a):
    return jnp.asarray(a, dtype=F32).astype(BF16)


def _dot_split(a, b):
    ah = a.astype(BF16)
    bh = b.astype(BF16)
    al = (a - ah.astype(F32)).astype(BF16)
    bl = (b - bh.astype(F32)).astype(BF16)
    return _dot(ah, bh) + _dot(ah, bl) + _dot(al, bh)


def _dot_nt(a, b):
    return lax.dot_general(a, b, (((1,), (1,)), ((), ())), preferred_element_type=F32)


def _mod_kernel(c_ref, w_ref, b_ref, o_ref):
    c = c_ref[...]
    s = c * (1.0 / (1.0 + jnp.exp(-c)))
    o_ref[0] = jnp.dot(s, w_ref[0], preferred_element_type=F32,
                       precision=lax.Precision.HIGHEST) + b_ref[0]


def _mod_vectors(cc, mod_w, mod_b):
    depth, d, n = mod_w.shape
    nb = 1024
    return pl.pallas_call(
        _mod_kernel,
        grid=(depth, n // nb),
        in_specs=[pl.BlockSpec((8, d), lambda l, j: (0, 0)),
                  pl.BlockSpec((1, d, nb), lambda l, j: (l, 0, j)),
                  pl.BlockSpec((1, 1, nb), lambda l, j: (l, 0, j))],
        out_specs=pl.BlockSpec((1, 8, nb), lambda l, j: (l, 0, j)),
        out_shape=jax.ShapeDtypeStruct((depth, 8, n), F32),
        compiler_params=_cparams(("parallel", "parallel")),
        name="mod_vectors",
    )(cc, mod_w, mod_b.reshape(depth, 1, n))


def _front_ab_ctx_kernel(x_ref, mod_ref, w_ref, u_ref):
    m = mod_ref[0, 0]
    h = _norm_rows(x_ref[0]) * (1.0 + m[1:2]) + m[0:1]
    u_ref[0] = _dot(h.astype(BF16), w_ref[...]).astype(BF16)


def _front_ab_ctx(xc, modtab, w_in):
    b, lc, d = xc.shape
    n = w_in.shape[1]
    return pl.pallas_call(
        _front_ab_ctx_kernel,
        grid=(b, lc // ROW_BLK),
        in_specs=[pl.BlockSpec((1, ROW_BLK, d), lambda i, j: (i, j, 0)),
                  pl.BlockSpec((1, 1, 6, d), lambda i, j: (i, 1, 0, 0)),
                  _resident((d, n))],
        out_specs=pl.BlockSpec((1, ROW_BLK, n), lambda i, j: (i, j, 0)),
        out_shape=jax.ShapeDtypeStruct((b, lc, n), BF16),
        compiler_params=_cparams(("parallel", "parallel")),
        name="front_ab_ctx",
    )(xc, modtab, w_in)


def _front_ab_lat_kernel(x_ref, mod_ref, wq_ref, wht_ref, u_ref, ut_ref):
    m = mod_ref[0, 0]
    hs = []
    for t in range(x_ref.shape[1] // ROW_BLK):
        rs = slice(t * ROW_BLK, (t + 1) * ROW_BLK)
        hs.append((_norm_rows(x_ref[0, rs, :]) * (1.0 + m[1:2]) + m[0:1]).astype(BF16))
        u_ref[0, rs, :] = _dot(hs[-1], wq_ref[...]).astype(BF16)
    h_all = jnp.concatenate(hs, axis=0)
    for c0 in range(0, wht_ref.shape[0], HY_CH):
        ut = _dot_nt(wht_ref[c0:c0 + HY_CH, :], h_all)
        for s in range(ut_ref.shape[2]):
            ut_ref[0, c0:c0 + HY_CH, s, :] = ut[:, s * FFT_N2:(s + 1) * FFT_N2]


def _front_ab_lat(x, modtab, w_qkv, w_hy_t):
    b, length, d = x.shape
    nq = w_qkv.shape[1]
    nh = w_hy_t.shape[0]
    rows = 8 * FFT_N2
    return pl.pallas_call(
        _front_ab_lat_kernel,
        grid=(b, length // rows),
        in_specs=[pl.BlockSpec((1, rows, d), lambda i, j: (i, j, 0)),
                  pl.BlockSpec((1, 1, 6, d), lambda i, j: (i, 0, 0, 0)),
                  _resident((d, nq)), _resident((nh, d))],
        out_specs=[pl.BlockSpec((1, rows, nq), lambda i, j: (i, j, 0)),
                   pl.BlockSpec((1, nh, 8, FFT_N2), lambda i, j: (i, 0, j, 0))],
        out_shape=[jax.ShapeDtypeStruct((b, length, nq), BF16),
                   jax.ShapeDtypeStruct((b, nh, length // FFT_N2, FFT_N2), F32)],
        compiler_params=_cparams(("parallel", "parallel")),
        name="front_ab_lat",
    )(x, modtab, w_qkv, w_hy_t)


def _hy_prep_kernel(cur_ref, prev_ref, next_ref, w_ref, x1_ref, x2_ref, v_ref, *, nblk):
    j = pl.program_id(1)
    cur = cur_ref[0].astype(F32)
    rows = cur.shape[0]
    has_prev = (j > 0).astype(F32)
    has_next = (j < nblk - 1).astype(F32)
    prev_row = prev_ref[0][7:8].astype(F32) * has_prev
    next_row = next_ref[0][0:1].astype(F32) * has_next
    rid = lax.broadcasted_iota(jnp.int32, (rows, 1), 0)
    up = jnp.where(rid == 0, prev_row, pltpu.roll(cur, 1, axis=0))
    dn = jnp.where(rid == rows - 1, next_row, pltpu.roll(cur, rows - 1, axis=0))
    w = w_ref[...]
    y = up * w[0:1] + cur * w[1:2] + dn * w[2:3]
    c = HY_CH
    x1_ref[0] = y[:, :c].astype(BF16)
    x2_ref[0] = y[:, c:2 * c].astype(BF16)
    v_ref[0] = y[:, 2 * c:].astype(BF16)


def _hy_prep(u, conv_w, blk0, nblk):
    b, lt, _ = u.shape
    n = 3 * HY_CH
    sub = ROW_BLK // 8
    last8 = lt // 8 - 1
    out = jax.ShapeDtypeStruct((b, nblk * ROW_BLK, HY_CH), BF16)
    ospec = pl.BlockSpec((1, ROW_BLK, HY_CH), lambda i, j: (i, j, 0))
    return pl.pallas_call(
        functools.partial(_hy_prep_kernel, nblk=nblk),
        grid=(b, nblk),
        in_specs=[pl.BlockSpec((1, ROW_BLK, n), lambda i, j: (i, blk0 + j, 0)),
                  pl.BlockSpec((1, 8, n), lambda i, j: (i, jnp.maximum((blk0 + j) * sub - 1, 0), 0)),
                  pl.BlockSpec((1, 8, n), lambda i, j: (i, jnp.minimum((blk0 + j + 1) * sub, last8), 0)),
                  _resident((3, n))],
        out_specs=[ospec, ospec, ospec],
        out_shape=[out, out, out],
        compiler_params=_cparams(("parallel", "parallel")),
        name="hy_prep",
    )(u, u, u, conv_w)


def _hy_filt_kernel(z_ref, w1_ref, b1_ref, fr_ref, w2_ref, b2_ref, w3_ref, ld_ref, o_ref):
    hi = lax.Precision.HIGHEST
    z = z_ref[...]
    fr = fr_ref[...]
    hid = jnp.sin(fr * (jnp.dot(z, w1_ref[...], preferred_element_type=F32, precision=hi) + b1_ref[...]))
    hid = jnp.sin(fr * (jnp.dot(hid, w2_ref[...], preferred_element_type=F32, precision=hi) + b2_ref[...]))
    h = jnp.dot(hid, w3_ref[...], preferred_element_type=F32, precision=hi)
    t = z[:, 0:1]
    o_ref[...] = h * jnp.exp(-t * jnp.exp(ld_ref[...]))


def _pad2(a, rows, cols):
    return jnp.pad(a, ((0, rows - a.shape[0]), (0, cols - a.shape[1])))


def _hy_filters(length, w1, b1, freq, w2, b2, w3, log_decay):
    pos = jnp.arange(length, dtype=F32)
    t = pos / max(length - 1, 1)
    w = 2.0 * math.pi * pos / length
    f = jnp.linspace(1e-4, HY_BANDS - 1, HY_BANDS, dtype=F32)
    ang = w[:, None] * f[None, :]
    z = jnp.concatenate([t[:, None], jnp.cos(ang), -jnp.sin(ang)], -1)
    z = _pad2(z, length, LANES)
    n = w3.shape[1]
    rb = min(length, 512)
    vec = lambda a: _pad2(a.reshape(1, -1), 1, LANES)
    return pl.pallas_call(
        _hy_filt_kernel,
        grid=(length // rb,),
        in_specs=[pl.BlockSpec((rb, LANES), lambda i: (i, 0)),
                  _resident((LANES, LANES)), _resident((1, LANES)), _resident((1, LANES)),
                  _resident((LANES, LANES)), _resident((1, LANES)),
                  _resident((LANES, n)), _resident((1, n))],
        out_specs=pl.BlockSpec((rb, n), lambda i: (i, 0)),
        out_shape=jax.ShapeDtypeStruct((length, n), F32),
        compiler_params=_cparams(("parallel",)),
        name="hy_filters",
    )(z, _pad2(w1, LANES, LANES), vec(b1), vec(freq), _pad2(w2, LANES, LANES), vec(b2),
      _pad2(w3, LANES, n), log_decay.reshape(1, n))


def _bidir_taps(h, skip, length):
    h4 = h.reshape(length, 2, 2, HY_CH)
    cols = []
    for o in range(2):
        hf = h4[:, o, 0].at[0].add(skip[o])
        hb = h4[:, o, 1]
        cols.append(jnp.concatenate([hf, jnp.zeros_like(hf[:1]), hb[:0:-1]], 0))
    return jnp.concatenate(cols, -1)


def _left_mm_kernel(m_ref, x_ref, o_ref):
    o_ref[0] = _dot(m_ref[...], x_ref[0]).astype(o_ref.dtype)


def _left_mm(mat, x, out_dtype, lane_blk):
    g, k, n = x.shape
    m = mat.shape[0]
    lane_blk = min(lane_blk, n)
    return pl.pallas_call(
        _left_mm_kernel, grid=(g, n // lane_blk),
        in_specs=[_resident((m, k)), pl.BlockSpec((1, k, lane_blk), lambda i, j: (i, 0, j))],
        out_specs=pl.BlockSpec((1, m, lane_blk), lambda i, j: (i, 0, j)),
        out_shape=jax.ShapeDtypeStruct((g, m, n), out_dtype),
        compiler_params=_cparams(("parallel", "parallel")),
        name="left_mm",
    )(mat, x)


def _cs(num, den, rows, cols):
    ang = 2.0 * np.pi * np.outer(np.arange(rows), np.arange(cols)) * (num / den)
    return np.cos(ang), np.sin(ang)


def _hy_filt_t_kernel(z_ref, msk_ref, w1_ref, b1_ref, fr_ref, w2_ref, b2_ref, w3_ref, ld_ref, sk_ref, o_ref):
    hi = lax.Precision.HIGHEST
    z = z_ref[...]
    fr = fr_ref[...]
    hid = jnp.sin(fr * (jnp.dot(w1_ref[...], z, preferred_element_type=F32, precision=hi) + b1_ref[...]))
    hid = jnp.sin(fr * (jnp.dot(w2_ref[...], hid, preferred_element_type=F32, precision=hi) + b2_ref[...]))
    h = _dot_split(w3_ref[0], hid)
    h = h * jnp.exp(-jnp.exp(ld_ref[0]) * z[0:1, :])
    msk = msk_ref[...]
    h = h * msk[0:1, :] + sk_ref[...] * msk[1:2, :]
    for s in range(o_ref.shape[1]):
        o_ref[:, s, :] = h[:, s * FFT_N2:(s + 1) * FFT_N2]


def _hy_filters_t(length, w1, b1, freq, w2, b2, w3, log_decay, skip):
    n = 2 * length
    tt = jnp.arange(n, dtype=jnp.int32)
    pos = jnp.where(tt < length, tt, n - tt).astype(F32)
    t = pos / max(length - 1, 1)
    w = 2.0 * math.pi * pos / length
    f = jnp.linspace(1e-4, HY_BANDS - 1, HY_BANDS, dtype=F32)
    ang = f[:, None] * w[None, :]
    z = jnp.concatenate([t[None, :], jnp.cos(ang), -jnp.sin(ang)], 0)
    nz = -(-z.shape[0] // 8) * 8
    z = jnp.pad(z, ((0, nz - z.shape[0]), (0, 0)))
    msk = jnp.stack([(tt != length).astype(F32), (tt == 0).astype(F32)])
    msk = jnp.pad(msk, ((0, 6), (0, 0)))
    col = lambda a: a.reshape(-1, 1)
    c2 = 2 * HY_CH
    nf = w3.shape[0]
    w3d = w3.reshape(nf, 2, 2, HY_CH).transpose(2, 1, 3, 0).reshape(2, c2, nf)
    ldd = log_decay.reshape(2, 2, HY_CH).transpose(1, 0, 2).reshape(2, c2, 1)
    rows = 8
    pb = rows * FFT_N2
    half = length // pb
    return pl.pallas_call(
        _hy_filt_t_kernel,
        grid=(n // pb,),
        in_specs=[pl.BlockSpec((nz, pb), lambda i: (0, i)),
                  pl.BlockSpec((8, pb), lambda i: (0, i)),
                  _resident((nf, nz)), _resident((nf, 1)), _resident((nf, 1)),
                  _resident((nf, nf)), _resident((nf, 1)),
                  pl.BlockSpec((1, c2, nf), lambda i: (i // half, 0, 0)),
                  pl.BlockSpec((1, c2, 1), lambda i: (i // half, 0, 0)),
                  _resident((c2, 1))],
        out_specs=pl.BlockSpec((c2, rows, FFT_N2), lambda i: (0, i, 0)),
        out_shape=jax.ShapeDtypeStruct((c2, n // FFT_N2, FFT_N2), F32),
        compiler_params=_cparams(("parallel",)),
        name="hy_filters_t",
    )(z, msk, _pad2(w1.T, nf, nz), col(b1), col(freq), w2.T, col(b2), w3d, ldd, skip.reshape(c2, 1))


def _dft_mats():
    n = FFT_N2
    c, s = _cs(1, n, n, n)
    ch, sh = c[:, :n // 2], s[:, :n // 2]
    f1_pair = np.block([[ch, sh], [-sh, ch]])
    f1_full = np.concatenate([c, -s], 0)
    m2r = np.block([[c, -s], [s, c]])
    m2i = np.block([[c, s], [-s, c]])
    f1_inv = np.block([[ch.T, -sh.T], [sh.T, ch.T]]) / (n * n)
    return tuple(_mxu_const(a) for a in (f1_pair, f1_full, m2r, m2i, f1_inv))


def _twiddle2d(n):
    k1 = lax.broadcasted_iota(jnp.int32, (n, n), 0)
    m2 = lax.broadcasted_iota(jnp.int32, (n, n), 1)
    ang = (k1 * m2).astype(F32) * (2.0 * math.pi / (n * n))
    return jnp.cos(ang), jnp.sin(ang)


def _fwd_spectrum(xs, f1, m2r, c, s):
    n = FFT_N2
    a = _dot(f1, jnp.concatenate(xs, axis=1))
    ts = []
    for g in range(len(xs)):
        ar = a[:n, g * n:(g + 1) * n]
        ai = a[n:, g * n:(g + 1) * n]
        ts.append(jnp.concatenate([ar * c + ai * s, ai * c - ar * s], axis=1))
    return _dot(jnp.concatenate(ts, axis=0).astype(BF16), m2r)


def _filter_spec_t_kernel(t_ref, f1_ref, m2r_ref, c_ref, s_ref, o_ref):
    g = t_ref.shape[0]
    xs = [t_ref[i].astype(BF16) for i in range(g)]
    spec = _fwd_spectrum(xs, f1_ref[...], m2r_ref[...], c_ref[...], s_ref[...])
    for i in range(g):
        o_ref[i] = spec[i * FFT_N2:(i + 1) * FFT_N2]


def _filter_spec_t(taps, f1_full, m2r, twc, tws):
    nch, n1, n = taps.shape
    g = HY_GROUP
    return pl.pallas_call(
        _filter_spec_t_kernel,
        grid=(nch // g,),
        in_specs=[pl.BlockSpec((g, n1, n), lambda i: (i, 0, 0)),
                  _resident(f1_full.shape), _resident(m2r.shape), _resident((n, n)), _resident((n, n))],
        out_specs=pl.BlockSpec((g, n, 2 * n), lambda i: (i, 0, 0)),
        out_shape=jax.ShapeDtypeStruct((nch, n, 2 * n), F32),
        compiler_params=_cparams(("parallel",)),
        name="filter_spec_t",
    )(taps, f1_full, m2r, twc, tws)


def _hyena_core_kernel(x1_ref, x2_ref, v_ref, w1_ref, w2_ref, wv_ref, h0_ref, h1_ref,
                       f1_ref, m2r_ref, m2i_ref, f1i_ref, c_ref, s_ref, o_ref):
    n = FFT_N2
    n1 = v_ref.shape[2]
    c = c_ref[...]
    s = s_ref[...]
    lane = lax.broadcasted_iota(jnp.int32, (n1, n), 1)
    row = lax.broadcasted_iota(jnp.int32, (n1, n), 0)
    first_lane, last_lane = lane == 0, lane == n - 1
    seq_start, seq_end = first_lane & (row == 0), last_lane & (row == n1 - 1)

    def short_conv(x, w):
        r = pltpu.roll(x, 1, axis=1)
        up = jnp.where(first_lane, pltpu.roll(r, 1, axis=0), r)
        up = jnp.where(seq_start, 0.0, up)
        l = pltpu.roll(x, n - 1, axis=1)
        dn = jnp.where(last_lane, pltpu.roll(l, n1 - 1, axis=0), l)
        dn = jnp.where(seq_end, 0.0, dn)
        return up * w[0:1] + x * w[1:2] + dn * w[2:3]

    def conv_all(xss, h_ref):
        spec_s = [_fwd_spectrum(xs, f1_ref[...], m2r_ref[...], c, s) for xs in xss]
        y_s = []
        for chain, spec in enumerate(spec_s):
            ys = []
            for g in range(HY_GROUP):
                xr = spec[g * n:(g + 1) * n, :n]
                xi = spec[g * n:(g + 1) * n, n:]
                hh = h_ref[chain * HY_GROUP + g]
                hr, hi = hh[:, :n], hh[:, n:]
                ys.append(jnp.concatenate([xr * hr - xi * hi, xr * hi + xi * hr], axis=1))
            y_s.append(jnp.concatenate(ys, axis=0).astype(BF16))
        bm_s = [_dot(y, m2i_ref[...]) for y in y_s]
        b_s = []
        for bm in bm_s:
            bs = []
            for g in range(HY_GROUP):
                br = bm[g * n:(g + 1) * n, :n]
                bi = bm[g * n:(g + 1) * n, n:]
                bs.append(jnp.concatenate([br * c - bi * s, bi * c + br * s], axis=0))
            b_s.append(jnp.concatenate(bs, axis=1).astype(BF16))
        outs = [_dot(f1i_ref[...], bc) for bc in b_s]
        return [[y[:, g * n:(g + 1) * n] for g in range(HY_GROUP)] for y in outs]

    def pair(ref, w_ref, ch):
        return jnp.concatenate([short_conv(ref[0, ch], w_ref[ch]), short_conv(ref[1, ch], w_ref[ch])], axis=0)

    chans = [[chain * HY_GROUP + g for g in range(HY_GROUP)] for chain in range(HY_CHAINS)]
    y1 = conv_all([[pair(v_ref, wv_ref, ch).astype(BF16) for ch in grp] for grp in chans], h0_ref)
    z = [[(y1[k][g] * pair(x1_ref, w1_ref, ch)).astype(BF16) for g, ch in enumerate(grp)]
         for k, grp in enumerate(chans)]
    y2 = conv_all(z, h1_ref)
    for k, grp in enumerate(chans):
        for g, ch in enumerate(grp):
            y = y2[k][g] * pair(x2_ref, w2_ref, ch)
            o_ref[0, :, ch, :] = y[:n1]
            o_ref[1, :, ch, :] = y[n1:]


def _hyena_core(ut, conv_w, spec, mats, twc, tws):
    f1_pair, _, m2r, m2i, f1_inv = mats
    b, nch, n1, n = ut.shape
    assert b % 2 == 0
    cb = HY_CBLK
    nblk = HY_CH // cb
    wt = jnp.broadcast_to(conv_w.T[:, :, None], (nch, conv_w.shape[0], n))
    xspec = lambda off: pl.BlockSpec((2, cb, n1, n), lambda j, i: (i, off * nblk + j, 0, 0))
    wspec = lambda off: pl.BlockSpec((cb, conv_w.shape[0], n), lambda j, i: (off * nblk + j, 0, 0))
    hspec = lambda off: pl.BlockSpec((cb, n, 2 * n), lambda j, i: (off * nblk + j, 0, 0))
    return pl.pallas_call(
        _hyena_core_kernel,
        grid=(nblk, b // 2),
        in_specs=[xspec(0), xspec(1), xspec(2), wspec(0), wspec(1), wspec(2), hspec(0), hspec(1),
                  _resident(f1_pair.shape), _resident(m2r.shape), _resident(m2i.shape), _resident(f1_inv.shape),
                  _resident((n, n)), _resident((n, n))],
        out_specs=pl.BlockSpec((2, n1, cb, n), lambda j, i: (i, 0, j, 0)),
        out_shape=jax.ShapeDtypeStruct((b, n1, HY_CH, n), F32),
        compiler_params=_cparams(("parallel", "parallel")),
        name="hyena_core",
    )(ut, ut, ut, wt, wt, wt, spec, spec, f1_pair, m2r, m2i, f1_inv, twc, tws)


def _hy_ctx_kernel(v_ref, x1_ref, x2_ref, f_ref, fi_ref, h_ref, o_ref):
    nf = f_ref.shape[0] // 2
    zin = v_ref[0]
    gates = (x1_ref, x2_ref)
    for o in range(2):
        x = _dot(f_ref[...], zin)
        xr, xi = x[:nf], x[nf:]
        hr = h_ref[o, :nf]
        hi = h_ref[o, nf:]
        y = jnp.concatenate([xr * hr - xi * hi, xr * hi + xi * hr], axis=0).astype(BF16)
        zin = (_dot(fi_ref[...], y) * gates[o][0].astype(F32)).astype(BF16)
    o_ref[0] = zin


def _hy_ctx(v, x1, x2, taps):
    b, lc, c = v.shape
    nf = 2 * lc
    cm, sm = _cs(1, nf, nf, nf)
    fwd = _mxu_const(np.concatenate([cm[:, :lc], -sm[:, :lc]], 0))
    fwd_full = _mxu_const(np.concatenate([cm, -sm], 0))
    inv = _mxu_const(np.concatenate([cm[:lc, :], -sm[:lc, :]], 1) / nf)
    spec = _left_mm(fwd_full, taps.astype(BF16).reshape(1, nf, 2 * c), F32, 2 * c)
    spec = spec.reshape(2 * nf, 2, c).transpose(1, 0, 2)
    blk = pl.BlockSpec((1, lc, c), lambda i: (i, 0, 0))
    return pl.pallas_call(
        _hy_ctx_kernel,
        grid=(b,),
        in_specs=[blk, blk, blk, _resident((2 * nf, lc)), _resident((lc, 2 * nf)),
                  _resident((2, 2 * nf, c))],
        out_specs=blk,
        out_shape=jax.ShapeDtypeStruct((b, lc, c), BF16),
        compiler_params=_cparams(("parallel",)),
        name="hy_ctx",
    )(v, x1, x2, fwd, inv, spec)


def _pair_rows(q2):
    lane = lax.broadcasted_iota(jnp.int32, q2.shape, 1)
    zero = jnp.zeros_like(q2)
    return jnp.concatenate([jnp.where(lane < NA_HD, q2, zero), jnp.where(lane >= NA_HD, q2, zero)], axis=0)


def _unpair_rows(o):
    r = o.shape[0] // 2
    lane = lax.broadcasted_iota(jnp.int32, (r, o.shape[1]), 1)
    return jnp.where(lane < NA_HD, o[:r], o[r:])


def _pair_softmax_pv(scores, values):
    m = scores[0].max(axis=-1, keepdims=True)
    for s in scores[1:]:
        m = jnp.maximum(m, s.max(axis=-1, keepdims=True))
    den = None
    acc = None
    for s, v in zip(scores, values):
        p = jnp.exp(s - m)
        d = p.sum(axis=-1, keepdims=True)
        a = _dot(p.astype(BF16), v)
        den = d if den is None else den + d
        acc = a if acc is None else acc + a
    return acc / den


def _natten_kernel(q_ref, k0, k1, k2, k3, v0, v1, v2, v3, kc_ref, vc_ref, bias_ref, o_ref,
                   kwin, vwin, *, rows):
    g = pl.program_id(1)
    rb = 4 * GRID_W
    for i, (kr, vr) in enumerate(((k0, v0), (k1, v1), (k2, v2), (k3, v3))):
        kwin[i * rb:(i + 1) * rb, :] = kr[0]
        vwin[i * rb:(i + 1) * rb, :] = vr[0]
    base = 4 * jnp.clip(2 * g - 1, 0, rows // 4 - 4)
    nwin = NA_WIN_R * GRID_W
    qscale = jnp.asarray(NA_HD ** -0.5, BF16)

    ones_lat = jnp.ones((nwin, LANES), BF16)
    ones_ctx = jnp.ones((kc_ref.shape[1], LANES), BF16)

    def rows_body(it, carry):
        work = []
        for u in range(NA_ROWS_PER_TRIP):
            rr = it * NA_ROWS_PER_TRIP + u
            r = 8 * g + rr
            rs = jnp.clip(r - NA_WIN_R // 2, 0, rows - NA_WIN_R)
            st = pl.multiple_of((rs - base) * GRID_W, GRID_W)
            qo = pl.multiple_of(rr * GRID_W, GRID_W)
            work += [(qo, st, rs - r + NA_WIN_R - 1, p) for p in range(NA_HEADS // 2)]
        scores = []
        for qo, st, d0, p in work:
            ls = slice(p * LANES, (p + 1) * LANES)
            qp = _pair_rows(q_ref[0, pl.ds(qo, GRID_W), ls] * qscale)
            scores.append((_dot_nt(qp, kwin[pl.ds(st, nwin), ls]) + bias_ref[d0, p].astype(F32),
                           _dot_nt(qp, kc_ref[0, :, ls])))
        maxima = [jnp.maximum(a.max(axis=-1, keepdims=True), b.max(axis=-1, keepdims=True)) for a, b in scores]
        for (qo, st, d0, p), (s_lat, s_ctx), m in zip(work, scores, maxima):
            ls = slice(p * LANES, (p + 1) * LANES)
            v_lat = jnp.concatenate([vwin[pl.ds(st, nwin), ls], ones_lat], axis=1)
            v_ctx = jnp.concatenate([vc_ref[0, :, ls], ones_ctx], axis=1)
            acc = _dot(jnp.exp(s_lat - m).astype(BF16), v_lat) + _dot(jnp.exp(s_ctx - m).astype(BF16), v_ctx)
            o = acc[:, :LANES] / acc[:, LANES:LANES + 1]
            o_ref[0, pl.ds(qo, GRID_W), ls] = _unpair_rows(o).astype(o_ref.dtype)
        return carry

    lax.fori_loop(0, 8 // NA_ROWS_PER_TRIP, rows_body, 0)


def _natten_bias(rpb):
    c = np.arange(GRID_W)[:, None]
    kc = np.arange(GRID_W)[None, :]
    cs = np.clip(c - NA_WIN_C // 2, 0, GRID_W - NA_WIN_C)
    valid = (kc >= cs) & (kc < cs + NA_WIN_C)
    dc = np.clip(kc - c + NA_WIN_C - 1, 0, 2 * NA_WIN_C - 2)
    ndc = 2 * NA_WIN_C - 1
    pick = ((dc[None] == np.arange(ndc)[:, None, None]) & valid[None]).astype(np.float32)
    tb = jnp.einsum('hrd,dck->hrck', rpb, jnp.asarray(pick), precision=lax.Precision.HIGHEST)
    tb = tb + jnp.asarray(np.where(valid, 0.0, NEG_BIG).astype(np.float32))
    slabs = []
    for d0 in range(NA_WIN_R):
        s = tb[:, d0:d0 + NA_WIN_R]
        s = s.transpose(0, 2, 1, 3).reshape(NA_HEADS, GRID_W, NA_WIN_R * GRID_W)
        slabs.append(s.reshape(NA_HEADS // 2, 2 * GRID_W, NA_WIN_R * GRID_W))
    return jnp.stack(slabs).astype(BF16)


def _natten(uq, uc, bias):
    b, length, _ = uq.shape
    c = NA_HEADS * NA_HD
    rows = length // GRID_W
    rb = 4 * GRID_W
    nkb = length // rb
    qrows = 8 * GRID_W
    lc = uc.shape[1]

    def kv_spec(col, off):
        return pl.BlockSpec((1, rb, c), lambda i, g: (i, jnp.clip(2 * g - 1, 0, nkb - 4) + off, col))

    return pl.pallas_call(
        functools.partial(_natten_kernel, rows=rows),
        grid=(b, rows // 8),
        in_specs=[pl.BlockSpec((1, qrows, c), lambda i, g: (i, g, 0))]
                 + [kv_spec(1, o) for o in range(4)] + [kv_spec(2, o) for o in range(4)]
                 + [pl.BlockSpec((1, lc, c), lambda i, g: (i, 0, 4)),
                    pl.BlockSpec((1, lc, c), lambda i, g: (i, 0, 5)),
                    _resident(bias.shape)],
        out_specs=pl.BlockSpec((1, qrows, c), lambda i, g: (i, g, 0)),
        out_shape=jax.ShapeDtypeStruct((b, length, c), BF16),
        scratch_shapes=[pltpu.VMEM((4 * rb, c), BF16), pltpu.VMEM((4 * rb, c), BF16)],
        compiler_params=_cparams(("parallel", "parallel")),
        name="natten",
    )(uq, *([uq] * 8), uc, uc, bias)


def _ctx_attn_kernel(q_ref, k_ref, v_ref, o_ref):
    qscale = jnp.asarray(NA_HD ** -0.5, BF16)
    for p in range(NA_HEADS // 2):
        ls = slice(p * LANES, (p + 1) * LANES)
        qp = _pair_rows(q_ref[0, :, ls] * qscale)
        o = _pair_softmax_pv([_dot_nt(qp, k_ref[0, :, ls])], [v_ref[0, :, ls]])
        o_ref[0, :, ls] = _unpair_rows(o).astype(o_ref.dtype)


def _ctx_attn(u):
    b, lc, _ = u.shape
    c = NA_HEADS * NA_HD
    spec = lambda col: pl.BlockSpec((1, lc, c), lambda i: (i, 0, col))
    return pl.pallas_call(
        _ctx_attn_kernel,
        grid=(b,),
        in_specs=[spec(3), spec(4), spec(5)],
        out_specs=pl.BlockSpec((1, lc, c), lambda i: (i, 0, 0)),
        out_shape=jax.ShapeDtypeStruct((b, lc, c), BF16),
        compiler_params=_cparams(("parallel",)),
        name="ctx_attn",
    )(u, u, u)


def _post_kernel(x_ref, mod_ref, ya_ref, yb_ref, wa_ref, wb_ref, lng_ref, lnb_ref, w1_ref, w2_ref, o_ref, *,
                 ya_slabs, yb_slabs):
    m = mod_ref[0, 0]
    lng = lng_ref[...]
    lnb = lnb_ref[...]
    ff = w1_ref.shape[1]
    step = MLP_CHUNK
    sub = ROW_BLK
    nsub = x_ref.shape[1] // sub

    def rows_of(ref, slabs, t):
        if slabs:
            tiles = range(t * sub // LANES, (t + 1) * sub // LANES)
            return jnp.concatenate([ref[0, s].T for s in tiles], axis=0).astype(BF16)
        return ref[0, t * sub:(t + 1) * sub, :]

    def head(t):
        rs = slice(t * sub, (t + 1) * sub)
        y = _dot(rows_of(ya_ref, ya_slabs, t), wa_ref[...]) + _dot(rows_of(yb_ref, yb_slabs, t), wb_ref[...])
        x1 = _norm_rows(ALPHA * x_ref[0, rs, :] + m[2:3] * y) * lng[0:1] + lnb[0:1]
        return x1, (_norm_rows(x1) * (1.0 + m[4:5]) + m[3:4]).astype(BF16)

    def mlp(h):
        acc = None
        for c in range(ff // step):
            a = jnp.maximum(_dot(h, w1_ref[:, c * step:(c + 1) * step]), 0.0)
            d = _dot((a * a).astype(BF16), w2_ref[c * step:(c + 1) * step, :])
            acc = d if acc is None else acc + d
        return acc

    cur = head(0)
    for t in range(nsub):
        nxt = head(t + 1) if t + 1 < nsub else None
        acc = mlp(cur[1])
        o_ref[0, t * sub:(t + 1) * sub, :] = _norm_rows(ALPHA * cur[0] + m[5:6] * acc) * lng[1:2] + lnb[1:2]
        cur = nxt


def _post(x, modtab, mod_row, ya, yb, wa, wb, lng, lnb, w1, w2, ya_slabs=False, yb_slabs=False):
    b, r, d = x.shape
    ka, kb = wa.shape[0], wb.shape[0]
    rows = min(r, 2 * ROW_BLK)
    row = lambda k: pl.BlockSpec((1, rows, k), lambda i, j: (i, j, 0))
    slab = lambda k: pl.BlockSpec((1, rows // LANES, k, LANES), lambda i, j: (i, j, 0, 0))
    return pl.pallas_call(
        functools.partial(_post_kernel, ya_slabs=ya_slabs, yb_slabs=yb_slabs),
        grid=(b, r // rows),
        in_specs=[row(d), pl.BlockSpec((1, 1, 6, d), lambda i, j: (i, mod_row, 0, 0)),
                  slab(ka) if ya_slabs else row(ka), slab(kb) if yb_slabs else row(kb),
                  _resident(wa.shape), _resident(wb.shape), _resident(lng.shape), _resident(lnb.shape),
                  _resident(w1.shape), _resident(w2.shape)],
        out_specs=row(d),
        out_shape=jax.ShapeDtypeStruct((b, r, d), F32),
        compiler_params=_cparams(("parallel", "parallel")),
        name="post_mixer",
    )(x, modtab, ya, yb, wa, wb, lng, lnb, w1, w2)


def _rope(x, cos, sinl, sinr):
    reps = x.shape[1] // LANES
    tile = lambda t: jnp.concatenate([t] * reps, axis=1)
    n = x.shape[1]
    quarter = MLA_ROPE // 4
    return (x * tile(cos) + pltpu.roll(x, n - quarter, axis=1) * tile(sinl)
            + pltpu.roll(x, quarter, axis=1) * tile(sinr))


def _front_cd_kernel(x_ref, xc_ref, mod_ref, w_ref, qn_ref, kvn_ref, wuq_ref, wuk_ref, wuv_ref, epe_ref, one_ref,
                     fng_ref, fnb_ref, avg_ref, cbd_ref, sbd_ref,
                     cos_ref, sl_ref, sr_ref,
                     q_ref, k_ref, v_ref, p_ref, qf_ref, *, ctx_blk):
    m = mod_ref[0, 0]
    is_ctx = (jnp.zeros((ROW_BLK, 1), jnp.int32) + pl.program_id(1)) == ctx_blk
    x = jnp.where(is_ctx, xc_ref[0], x_ref[0])
    h = _norm_rows(x) * (1.0 + m[1:2]) + m[0:1]
    u = _dot(h.astype(BF16), w_ref[...])
    o_kv = MLA_Q_RANK
    o_fn = o_kv + MLA_KV_RANK
    o_pe = o_fn + FN_CH

    def rms(x, g):
        return x * lax.rsqrt(jnp.mean(x * x, axis=-1, keepdims=True) + LN_EPS) * g

    cq = rms(u[:, :o_kv], qn_ref[...]).astype(BF16)
    q = _dot(cq, wuq_ref[...])
    q_ref[0] = _rope(q, cos_ref[...], sl_ref[...], sr_ref[...]).astype(BF16)

    ckv = rms(u[:, o_kv:o_fn], kvn_ref[...]).astype(BF16)
    kpe = _dot(u[:, o_pe:].astype(BF16), epe_ref[...])
    k = _dot(ckv, wuk_ref[...]) + _rope(kpe, cos_ref[...], sl_ref[...], sr_ref[...])
    k_ref[0] = k.astype(BF16)
    v_ref[0] = (_dot(ckv, wuv_ref[...]) + one_ref[...]).astype(BF16)

    uf = u[:, o_fn:o_pe]
    avg = avg_ref[...]
    uc = uf - _dot(uf.astype(BF16), avg)
    var = _dot((uc * uc).astype(BF16), avg)
    ug = (uc * lax.rsqrt(var + LN_EPS) * fng_ref[...] + fnb_ref[...]).astype(BF16)
    p_ref[0] = _dot_nt(cbd_ref[...], ug).astype(BF16)
    qf_ref[0] = _dot_nt(sbd_ref[...], ug).astype(BF16)


def _rope_tables(length, lt):
    t = jnp.arange(lt, dtype=jnp.int32)
    rows = (t // GRID_W).astype(F32)
    cols = (t % GRID_W).astype(F32)
    half = MLA_ROPE // 2
    inv = ROPE_THETA ** (-jnp.arange(0, half, 2, dtype=F32) / half)
    ar = rows[:, None] * inv[None, :]
    ac = cols[:, None] * inv[None, :]
    ang = jnp.concatenate([ar, ar, ac, ac], -1)
    is_lat = (t < length)[:, None]
    cos = jnp.where(is_lat, jnp.cos(ang), 1.0)
    sin = jnp.where(is_lat, jnp.sin(ang), 0.0)
    qd = MLA_ROPE // 4
    ones = jnp.ones((lt, MLA_NOPE), F32)
    zeros = jnp.zeros((lt, MLA_NOPE), F32)
    tail1 = jnp.ones((lt, HEAD_PAD - MLA_NOPE - MLA_ROPE), F32)
    tail0 = jnp.zeros((lt, HEAD_PAD - MLA_NOPE - MLA_ROPE), F32)
    z8 = jnp.zeros((lt, qd), F32)
    c = jnp.concatenate([ones, cos, tail1], -1)
    sl = jnp.concatenate([zeros, -sin[:, :qd], z8, -sin[:, 2 * qd:3 * qd], z8, tail0], -1)
    sr = jnp.concatenate([zeros, z8, sin[:, qd:2 * qd], z8, sin[:, 3 * qd:], tail0], -1)
    return c, sl, sr


def _head_slots(w, per_head, take_from, take_n):
    k = w.shape[0]
    w3 = w.reshape(k, MLA_HEADS, per_head)[:, :, take_from:take_from + take_n]
    w3 = jnp.pad(w3, ((0, 0), (0, 0), (0, HEAD_PAD - take_n)))
    return w3.reshape(k, MLA_HEADS * HEAD_PAD)


def _front_cd(xl, xc, modtab, w_in, q_norm, w_uq, kv_norm, w_ukv, fn_g, fn_b):
    b, length, d = xl.shape
    lt = length + xc.shape[1]
    nlat = length // ROW_BLK
    o_kv = MLA_Q_RANK
    o_pe = o_kv + MLA_KV_RANK
    o_fn = o_pe + MLA_ROPE
    hw = MLA_HEADS * HEAD_PAD
    w_perm = jnp.concatenate([w_in[:, :o_pe], w_in[:, o_fn:], w_in[:, o_pe:o_fn],
                              jnp.zeros((d, LANES - MLA_ROPE), w_in.dtype)], -1).astype(BF16)
    q_scale = (MLA_NOPE + MLA_ROPE) ** -0.5 * math.log2(math.e)
    wuq = _head_slots(w_uq * q_scale, MLA_NOPE + MLA_ROPE, 0, MLA_NOPE + MLA_ROPE).astype(BF16)
    wuk = _head_slots(w_ukv, MLA_NOPE + MLA_V, 0, MLA_NOPE).astype(BF16)
    wuv = _head_slots(w_ukv, MLA_NOPE + MLA_V, MLA_NOPE, MLA_V).astype(BF16)
    epe = np.zeros((LANES, hw), np.float32)
    for hd in range(MLA_HEADS):
        for i in range(MLA_ROPE):
            epe[i, hd * HEAD_PAD + MLA_NOPE + i] = 1.0
    epe = _mxu_const(epe)
    ones_col = np.zeros((1, hw), np.float32)
    ones_col[0, MLA_V::HEAD_PAD] = 1.0
    ones_col = jnp.asarray(ones_col)
    cm, sm = _cs(1, FN_GD, FN_GD, FN_GD)
    eye = np.eye(FN_GROUPS)
    cbd = _mxu_const(np.kron(eye, cm))
    sbd = _mxu_const(np.kron(eye, -sm))
    avg = _mxu_const(np.kron(eye, np.full((FN_GD, FN_GD), 1.0 / FN_GD)))
    rtab = _rope_tables(length, lt)
    row = lambda n: pl.BlockSpec((1, ROW_BLK, n), lambda i, j: (i, j, 0))
    tab = pl.BlockSpec((ROW_BLK, HEAD_PAD), lambda i, j: (j, 0))
    out = lambda n: jax.ShapeDtypeStruct((b, lt, n), BF16)
    fn_t = pl.BlockSpec((1, FN_CH, ROW_BLK), lambda i, j: (i, 0, j))
    fn_out = jax.ShapeDtypeStruct((b, FN_CH, lt), BF16)
    return pl.pallas_call(
        functools.partial(_front_cd_kernel, ctx_blk=nlat),
        grid=(b, lt // ROW_BLK),
        in_specs=[pl.BlockSpec((1, ROW_BLK, d), lambda i, j: (i, jnp.minimum(j, nlat - 1), 0)),
                  pl.BlockSpec((1, ROW_BLK, d), lambda i, j: (i, 0, 0)),
                  pl.BlockSpec((1, 1, 6, d), lambda i, j: (i, j // nlat, 0, 0)),
                  _resident(w_perm.shape), _resident((1, MLA_Q_RANK)), _resident((1, MLA_KV_RANK)),
                  _resident(wuq.shape), _resident(wuk.shape), _resident(wuv.shape), _resident(epe.shape),
                  _resident(ones_col.shape),
                  _resident((1, FN_CH)), _resident((1, FN_CH)), _resident(avg.shape), _resident(cbd.shape),
                  _resident(sbd.shape),
                  tab, tab, tab],
        out_specs=[row(hw), row(hw), row(hw), fn_t, fn_t],
        out_shape=[out(hw), out(hw), out(hw), fn_out, fn_out],
        compiler_params=_cparams(("parallel", "parallel")),
        name="front_cd",
    )(xl, xc, modtab, w_perm, q_norm.reshape(1, -1), kv_norm.reshape(1, -1), wuq, wuk, wuv, epe, ones_col,
      fn_g.reshape(1, -1), fn_b.reshape(1, -1), avg, cbd, sbd, *rtab)


def _mla_kernel(q_ref, k_ref, v_ref, o_ref, *, rows):
    n = q_ref.shape[1] // rows

    def scores(i):
        s = _dot_nt(q_ref[0, i * rows:(i + 1) * rows, :], k_ref[0])
        return s, s.max(axis=-1, keepdims=True)

    def finish(i, s, m):
        acc = _dot(jnp.exp2(s - m).astype(BF16), v_ref[0])
        o_ref[0, i * rows:(i + 1) * rows, :] = (acc / acc[:, MLA_V:MLA_V + 1]).astype(o_ref.dtype)

    pending = scores(0)
    for i in range(n):
        nxt = scores(i + 1) if i + 1 < n else None
        finish(i, *pending)
        pending = nxt


def _mla_attention(q, k, v, length):
    b, lt, hw = q.shape
    heads = hw // HEAD_PAD
    tq = MLA_Q_BLK
    kv = pl.BlockSpec((1, lt, HEAD_PAD), lambda i, h, j: (i, 0, h))
    qs = pl.BlockSpec((1, tq, HEAD_PAD), lambda i, h, j: (i, j, h))
    return pl.pallas_call(
        functools.partial(_mla_kernel, rows=MLA_Q_SUB),
        grid=(b, heads, length // tq),
        in_specs=[qs, kv, kv],
        out_specs=qs,
        out_shape=jax.ShapeDtypeStruct((b, length, hw), BF16),
        compiler_params=_cparams(("parallel", "parallel", "parallel")),
        name="mla_attention",
    )(q, k, v)


def _fnet_kernel(pa_ref, pb_ref, qa_ref, qb_ref, m1_ref, c_ref, s_ref, m2_ref, o_ref, zr_scr, zi_scr):
    n1 = FN_N1
    n2 = zr_scr.shape[2] // 2
    pairs = zr_scr.shape[0]

    def fill(scr, a_ref, b_ref):
        za = a_ref[0].astype(F32)
        zb = b_ref[0].astype(F32)
        for r in range(n1):
            scr[:, r, :] = jnp.concatenate([za[:, r * n2:(r + 1) * n2], zb[:, r * n2:(r + 1) * n2]], axis=1)

    fill(zr_scr, pa_ref, pb_ref)
    fill(zi_scr, qa_ref, qb_ref)
    c = c_ref[...]
    s = s_ref[...]
    for g0 in range(0, pairs, FN_GROUP):
        x = jnp.concatenate([jnp.concatenate([zr_scr[g0 + g], zi_scr[g0 + g]], axis=0) for g in range(FN_GROUP)],
                            axis=1).astype(BF16)
        a = _dot(m1_ref[...], x)
        ts = []
        for g in range(FN_GROUP):
            ar = a[:n1, g * LANES:(g + 1) * LANES]
            ai = a[n1:, g * LANES:(g + 1) * LANES]
            ts.append(jnp.concatenate([ar * c + ai * s, ai * c - ar * s], axis=1))
        y = _dot(jnp.concatenate(ts, axis=0).astype(BF16), m2_ref[...])
        for g in range(FN_GROUP):
            yt = y[g * n1:(g + 1) * n1].T
            o_ref[0, :, 2 * (g0 + g), :] = yt[:n2]
            o_ref[0, :, 2 * (g0 + g) + 1, :] = yt[n2:]


def _fnet(pt, qt, length):
    b, ch, _ = pt.shape
    n1 = FN_N1
    n2 = length // n1
    assert 2 * n2 == LANES
    half = ch // 2
    cm, sm = _cs(1, n1, n1, n1)
    m1 = _mxu_const(np.block([[cm, sm], [-sm, cm]]))
    c2, s2 = _cs(1, n2, n2, n2)
    z = np.zeros_like(c2)
    m2 = np.block([[c2, z], [z, c2], [s2, z], [z, s2]]) / math.sqrt(length * FN_GD)
    m2 = _mxu_const(m2)
    k1 = lax.broadcasted_iota(jnp.int32, (n1, LANES), 0)
    m = lax.broadcasted_iota(jnp.int32, (n1, LANES), 1) % n2
    ang = (k1 * m).astype(F32) * (2.0 * math.pi / length)
    twc, tws = jnp.cos(ang), jnp.sin(ang)
    pairs = FN_PAIRS
    nblk = half // pairs
    spec = lambda off: pl.BlockSpec((1, pairs, length), lambda j, i: (i, off * nblk + j, 0))
    return pl.pallas_call(
        _fnet_kernel,
        grid=(nblk, b),
        in_specs=[spec(0), spec(1), spec(0), spec(1), _resident(m1.shape), _resident((n1, LANES)),
                  _resident((n1, LANES)), _resident(m2.shape)],
        out_specs=pl.BlockSpec((1, n2, 2 * pairs, LANES), lambda j, i: (i, 0, j, 0)),
        out_shape=jax.ShapeDtypeStruct((b, n2, ch, LANES), F32),
        scratch_shapes=[pltpu.VMEM((pairs, n1, LANES), F32), pltpu.VMEM((pairs, n1, LANES), F32)],
        compiler_params=_cparams(("parallel", "parallel")),
        name="fnet",
    )(pt, pt, qt, qt, m1, twc, tws, m2)


def kernel(x, c, ctx, c_ctx, mod_w, mod_b, ln_g, ln_b, mlp_w1, mlp_w2,
           ab_w_in, ab_w_out, hy_conv_w, hy_w1, hy_b1, hy_freq, hy_w2, hy_b2, hy_w3, hy_log_decay, hy_skip, na_rpb,
           cd_w_in, cd_w_out, mla_q_norm, mla_w_uq, mla_kv_norm, mla_w_ukv, fn_norm_g, fn_norm_b):
    b, length, d = x.shape
    lc = ctx.shape[1]

    cc = jnp.concatenate([c, c_ctx[None], jnp.zeros((8 - b - 1, d), F32)], 0)
    mods = _mod_vectors(cc, mod_w, mod_b).reshape(DEPTH, 8, 6, d)
    modtab = [jnp.stack([mods[l, :b], jnp.broadcast_to(mods[l, b], (b, 6, d))], axis=1) for l in range(DEPTH)]

    n_hy = 3 * HY_CH
    w_in = ab_w_in[0].astype(BF16)
    uq, ut = _front_ab_lat(x, modtab[0], w_in[:, n_hy:], w_in[:, :n_hy].T)
    uc = _front_ab_ctx(ctx, modtab[0], w_in)
    fargs = (hy_w1[0], hy_b1[0], hy_freq[0], hy_w2[0], hy_b2[0], hy_w3[0], hy_log_decay[0])
    mats = _dft_mats()
    twc, tws = _twiddle2d(FFT_N2)
    spec = _filter_spec_t(_hy_filters_t(length, *fargs, hy_skip[0]), mats[1], mats[2], twc, tws)
    y_hy_t = _hyena_core(ut, hy_conv_w[0], spec, mats, twc, tws)
    x1c, x2c, vc = _hy_prep(uc, hy_conv_w[0], 0, lc // ROW_BLK)
    y_hy_c = _hy_ctx(vc, x1c, x2c, _bidir_taps(_hy_filters(lc, *fargs), hy_skip[0], lc))
    y_na = _natten(uq, uc, _natten_bias(na_rpb[0]))
    y_na_c = _ctx_attn(uc)
    w_out = ab_w_out[0].astype(BF16)
    mlp = (ln_g[0], ln_b[0], mlp_w1[0].astype(BF16), mlp_w2[0].astype(BF16))
    xl = _post(x, modtab[0], 0, y_hy_t, y_na, w_out[:HY_CH], w_out[HY_CH:], *mlp, ya_slabs=True)
    xc = _post(ctx, modtab[0], 1, y_hy_c, y_na_c, w_out[:HY_CH], w_out[HY_CH:], *mlp)

    q, k, vv, pt, qt = _front_cd(xl, xc, modtab[1], cd_w_in[0], mla_q_norm[0], mla_w_uq[0], mla_kv_norm[0],
                                 mla_w_ukv[0], fn_norm_g[0], fn_norm_b[0])
    o = _mla_attention(q, k, vv, length)
    y_fn = _fnet(pt, qt, length)
    w_out = cd_w_out[0]
    n_mla = MLA_HEADS * MLA_V
    wa = jnp.pad(w_out[:n_mla].reshape(MLA_HEADS, MLA_V, d), ((0, 0), (0, HEAD_PAD - MLA_V), (0, 0)))
    wa = wa.reshape(MLA_HEADS * HEAD_PAD, d).astype(BF16)
    wb = w_out[n_mla:].reshape(2, FN_CH // 2, d).transpose(1, 0, 2).reshape(FN_CH, d)
    return _post(xl, modtab[1], 0, o, y_fn, wa, wb.astype(BF16),
                 ln_g[1], ln_b[1], mlp_w1[1].astype(BF16), mlp_w2[1].astype(BF16), yb_slabs=True)
```

```python
import functools
import math

import numpy as np
import jax
import jax.numpy as jnp
from jax import lax
from jax.experimental import pallas as pl
from jax.experimental.pallas import tpu as pltpu

F32 = jnp.float32
BF16 = jnp.bfloat16

D_MODEL = 1024
DEPTH = 2
GRID_W = 64
HY_CH = 512
HY_EMB = 33
HY_BANDS = (HY_EMB - 1) // 2
NA_HEADS = 8
NA_HD = 64
NA_WIN_R = 8
NA_WIN_C = 16
MLA_HEADS = 8
MLA_Q_RANK = 384
MLA_KV_RANK = 256
MLA_NOPE = 64
MLA_ROPE = 32
MLA_V = 96
ROPE_THETA = 10000.0
FN_CH = 256
FN_GROUPS = 4
FN_GD = FN_CH // FN_GROUPS
D_FF = 4 * D_MODEL
ALPHA = (2.0 * DEPTH) ** 0.25
LN_EPS = 1e-5

LANES = 128
ROW_BLK = 256
HEAD_PAD = 128
FFT_N2 = 128
NA_ROWS_PER_TRIP = 8
FN_N1 = 128
FN_PAIRS = 16
FN_GROUP = 4
MLP_CHUNK = 1024
MLA_Q_BLK = 1024
MLA_Q_SUB = 256
HY_GROUP = 16
HY_CHAINS = 1
HY_CBLK = HY_GROUP * HY_CHAINS
VMEM_LIMIT = 56 * 1024 * 1024
NEG_BIG = -1e30


def _cparams(sem, vmem=VMEM_LIMIT):
    return pltpu.CompilerParams(dimension_semantics=sem, vmem_limit_bytes=vmem)


def _resident(shape):
    nd = len(shape)
    return pl.BlockSpec(shape, lambda *_: (0,) * nd, pipeline_mode=pl.Buffered(1))


def _norm_rows(x):
    mu = jnp.mean(x, axis=-1, keepdims=True)
    xc = x - mu
    var = jnp.mean(xc * xc, axis=-1, keepdims=True)
    return xc * lax.rsqrt(var + LN_EPS)


def _dot(a, b):
    return jnp.dot(a, b, preferred_element_type=F32)


def _mxu_const(a):
    return jnp.asarray(a, dtype=F32).astype(BF16)


def _dot_split(a, b):
    ah = a.astype(BF16)
    bh = b.astype(BF16)
    al = (a - ah.astype(F32)).astype(BF16)
    bl = (b - bh.astype(F32)).astype(BF16)
    return _dot(ah, bh) + _dot(ah, bl) + _dot(al, bh)


def _dot_nt(a, b):
    return lax.dot_general(a, b, (((1,), (1,)), ((), ())), preferred_element_type=F32)


def _mod_kernel(c_ref, w_ref, b_ref, o_ref):
    c = c_ref[...]
    s = c * (1.0 / (1.0 + jnp.exp(-c)))
    o_ref[0] = jnp.dot(s, w_ref[0], preferred_element_type=F32,
                       precision=lax.Precision.HIGHEST) + b_ref[0]


def _mod_vectors(cc, mod_w, mod_b):
    depth, d, n = mod_w.shape
    nb = 1024
    return pl.pallas_call(
        _mod_kernel,
        grid=(depth, n // nb),
        in_specs=[pl.BlockSpec((8, d), lambda l, j: (0, 0)),
                  pl.BlockSpec((1, d, nb), lambda l, j: (l, 0, j)),
                  pl.BlockSpec((1, 1, nb), lambda l, j: (l, 0, j))],
        out_specs=pl.BlockSpec((1, 8, nb), lambda l, j: (l, 0, j)),
        out_shape=jax.ShapeDtypeStruct((depth, 8, n), F32),
        compiler_params=_cparams(("parallel", "parallel")),
        name="mod_vectors",
    )(cc, mod_w, mod_b.reshape(depth, 1, n))


def _front_ab_ctx_kernel(x_ref, mod_ref, w_ref, u_ref):
    m = mod_ref[0, 0]
    h = _norm_rows(x_ref[0]) * (1.0 + m[1:2]) + m[0:1]
    u_ref[0] = _dot(h.astype(BF16), w_ref[...]).astype(BF16)


def _front_ab_ctx(xc, modtab, w_in):
    b, lc, d = xc.shape
    n = w_in.shape[1]
    return pl.pallas_call(
        _front_ab_ctx_kernel,
        grid=(b, lc // ROW_BLK),
        in_specs=[pl.BlockSpec((1, ROW_BLK, d), lambda i, j: (i, j, 0)),
                  pl.BlockSpec((1, 1, 6, d), lambda i, j: (i, 1, 0, 0)),
                  _resident((d, n))],
        out_specs=pl.BlockSpec((1, ROW_BLK, n), lambda i, j: (i, j, 0)),
        out_shape=jax.ShapeDtypeStruct((b, lc, n), BF16),
        compiler_params=_cparams(("parallel", "parallel")),
        name="front_ab_ctx",
    )(xc, modtab, w_in)


def _front_ab_lat_kernel(x_ref, mod_ref, wq_ref, wht_ref, u_ref, ut_ref):
    m = mod_ref[0, 0]
    hs = []
    for t in range(x_ref.shape[1] // ROW_BLK):
        rs = slice(t * ROW_BLK, (t + 1) * ROW_BLK)
        hs.append((_norm_rows(x_ref[0, rs, :]) * (1.0 + m[1:2]) + m[0:1]).astype(BF16))
        u_ref[0, rs, :] = _dot(hs[-1], wq_ref[...]).astype(BF16)
    h_all = jnp.concatenate(hs, axis=0)
    for c0 in range(0, wht_ref.shape[0], HY_CH):
        ut = _dot_nt(wht_ref[c0:c0 + HY_CH, :], h_all)
        for s in range(ut_ref.shape[2]):
            ut_ref[0, c0:c0 + HY_CH, s, :] = ut[:, s * FFT_N2:(s + 1) * FFT_N2]


def _front_ab_lat(x, modtab, w_qkv, w_hy_t):
    b, length, d = x.shape
    nq = w_qkv.shape[1]
    nh = w_hy_t.shape[0]
    rows = 8 * FFT_N2
    return pl.pallas_call(
        _front_ab_lat_kernel,
        grid=(b, length // rows),
        in_specs=[pl.BlockSpec((1, rows, d), lambda i, j: (i, j, 0)),
                  pl.BlockSpec((1, 1, 6, d), lambda i, j: (i, 0, 0, 0)),
                  _resident((d, nq)), _resident((nh, d))],
        out_specs=[pl.BlockSpec((1, rows, nq), lambda i, j: (i, j, 0)),
                   pl.BlockSpec((1, nh, 8, FFT_N2), lambda i, j: (i, 0, j, 0))],
        out_shape=[jax.ShapeDtypeStruct((b, length, nq), BF16),
                   jax.ShapeDtypeStruct((b, nh, length // FFT_N2, FFT_N2), F32)],
        compiler_params=_cparams(("parallel", "parallel")),
        name="front_ab_lat",
    )(x, modtab, w_qkv, w_hy_t)


def _hy_prep_kernel(cur_ref, prev_ref, next_ref, w_ref, x1_ref, x2_ref, v_ref, *, nblk):
    j = pl.program_id(1)
    cur = cur_ref[0].astype(F32)
    rows = cur.shape[0]
    has_prev = (j > 0).astype(F32)
    has_next = (j < nblk - 1).astype(F32)
    prev_row = prev_ref[0][7:8].astype(F32) * has_prev
    next_row = next_ref[0][0:1].astype(F32) * has_next
    rid = lax.broadcasted_iota(jnp.int32, (rows, 1), 0)
    up = jnp.where(rid == 0, prev_row, pltpu.roll(cur, 1, axis=0))
    dn = jnp.where(rid == rows - 1, next_row, pltpu.roll(cur, rows - 1, axis=0))
    w = w_ref[...]
    y = up * w[0:1] + cur * w[1:2] + dn * w[2:3]
    c = HY_CH
    x1_ref[0] = y[:, :c].astype(BF16)
    x2_ref[0] = y[:, c:2 * c].astype(BF16)
    v_ref[0] = y[:, 2 * c:].astype(BF16)


def _hy_prep(u, conv_w, blk0, nblk):
    b, lt, _ = u.shape
    n = 3 * HY_CH
    sub = ROW_BLK // 8
    last8 = lt // 8 - 1
    out = jax.ShapeDtypeStruct((b, nblk * ROW_BLK, HY_CH), BF16)
    ospec = pl.BlockSpec((1, ROW_BLK, HY_CH), lambda i, j: (i, j, 0))
    return pl.pallas_call(
        functools.partial(_hy_prep_kernel, nblk=nblk),
        grid=(b, nblk),
        in_specs=[pl.BlockSpec((1, ROW_BLK, n), lambda i, j: (i, blk0 + j, 0)),
                  pl.BlockSpec((1, 8, n), lambda i, j: (i, jnp.maximum((blk0 + j) * sub - 1, 0), 0)),
                  pl.BlockSpec((1, 8, n), lambda i, j: (i, jnp.minimum((blk0 + j + 1) * sub, last8), 0)),
                  _resident((3, n))],
        out_specs=[ospec, ospec, ospec],
        out_shape=[out, out, out],
        compiler_params=_cparams(("parallel", "parallel")),
        name="hy_prep",
    )(u, u, u, conv_w)


def _hy_filt_kernel(z_ref, w1_ref, b1_ref, fr_ref, w2_ref, b2_ref, w3_ref, ld_ref, o_ref):
    hi = lax.Precision.HIGHEST
    z = z_ref[...]
    fr = fr_ref[...]
    hid = jnp.sin(fr * (jnp.dot(z, w1_ref[...], preferred_element_type=F32, precision=hi) + b1_ref[...]))
    hid = jnp.sin(fr * (jnp.dot(hid, w2_ref[...], preferred_element_type=F32, precision=hi) + b2_ref[...]))
    h = jnp.dot(hid, w3_ref[...], preferred_element_type=F32, precision=hi)
    t = z[:, 0:1]
    o_ref[...] = h * jnp.exp(-t * jnp.exp(ld_ref[...]))


def _pad2(a, rows, cols):
    return jnp.pad(a, ((0, rows - a.shape[0]), (0, cols - a.shape[1])))


def _hy_filters(length, w1, b1, freq, w2, b2, w3, log_decay):
    pos = jnp.arange(length, dtype=F32)
    t = pos / max(length - 1, 1)
    w = 2.0 * math.pi * pos / length
    f = jnp.linspace(1e-4, HY_BANDS - 1, HY_BANDS, dtype=F32)
    ang = w[:, None] * f[None, :]
    z = jnp.concatenate([t[:, None], jnp.cos(ang), -jnp.sin(ang)], -1)
    z = _pad2(z, length, LANES)
    n = w3.shape[1]
    rb = min(length, 512)
    vec = lambda a: _pad2(a.reshape(1, -1), 1, LANES)
    return pl.pallas_call(
        _hy_filt_kernel,
        grid=(length // rb,),
        in_specs=[pl.BlockSpec((rb, LANES), lambda i: (i, 0)),
                  _resident((LANES, LANES)), _resident((1, LANES)), _resident((1, LANES)),
                  _resident((LANES, LANES)), _resident((1, LANES)),
                  _resident((LANES, n)), _resident((1, n))],
        out_specs=pl.BlockSpec((rb, n), lambda i: (i, 0)),
        out_shape=jax.ShapeDtypeStruct((length, n), F32),
        compiler_params=_cparams(("parallel",)),
        name="hy_filters",
    )(z, _pad2(w1, LANES, LANES), vec(b1), vec(freq), _pad2(w2, LANES, LANES), vec(b2),
      _pad2(w3, LANES, n), log_decay.reshape(1, n))


def _bidir_taps(h, skip, length):
    h4 = h.reshape(length, 2, 2, HY_CH)
    cols = []
    for o in range(2):
        hf = h4[:, o, 0].at[0].add(skip[o])
        hb = h4[:, o, 1]
        cols.append(jnp.concatenate([hf, jnp.zeros_like(hf[:1]), hb[:0:-1]], 0))
    return jnp.concatenate(cols, -1)


def _left_mm_kernel(m_ref, x_ref, o_ref):
    o_ref[0] = _dot(m_ref[...], x_ref[0]).astype(o_ref.dtype)


def _left_mm(mat, x, out_dtype, lane_blk):
    g, k, n = x.shape
    m = mat.shape[0]
    lane_blk = min(lane_blk, n)
    return pl.pallas_call(
        _left_mm_kernel, grid=(g, n // lane_blk),
        in_specs=[_resident((m, k)), pl.BlockSpec((1, k, lane_blk), lambda i, j: (i, 0, j))],
        out_specs=pl.BlockSpec((1, m, lane_blk), lambda i, j: (i, 0, j)),
        out_shape=jax.ShapeDtypeStruct((g, m, n), out_dtype),
        compiler_params=_cparams(("parallel", "parallel")),
        name="left_mm",
    )(mat, x)


def _cs(num, den, rows, cols):
    ang = 2.0 * np.pi * np.outer(np.arange(rows), np.arange(cols)) * (num / den)
    return np.cos(ang), np.sin(ang)


def _hy_filt_t_kernel(z_ref, msk_ref, w1_ref, b1_ref, fr_ref, w2_ref, b2_ref, w3_ref, ld_ref, sk_ref, o_ref):
    hi = lax.Precision.HIGHEST
    z = z_ref[...]
    fr = fr_ref[...]
    hid = jnp.sin(fr * (jnp.dot(w1_ref[...], z, preferred_element_type=F32, precision=hi) + b1_ref[...]))
    hid = jnp.sin(fr * (jnp.dot(w2_ref[...], hid, preferred_element_type=F32, precision=hi) + b2_ref[...]))
    h = _dot_split(w3_ref[0], hid)
    h = h * jnp.exp(-jnp.exp(ld_ref[0]) * z[0:1, :])
    msk = msk_ref[...]
    h = h * msk[0:1, :] + sk_ref[...] * msk[1:2, :]
    for s in range(o_ref.shape[1]):
        o_ref[:, s, :] = h[:, s * FFT_N2:(s + 1) * FFT_N2]


def _hy_filters_t(length, w1, b1, freq, w2, b2, w3, log_decay, skip):
    n = 2 * length
    tt = jnp.arange(n, dtype=jnp.int32)
    pos = jnp.where(tt < length, tt, n - tt).astype(F32)
    t = pos / max(length - 1, 1)
    w = 2.0 * math.pi * pos / length
    f = jnp.linspace(1e-4, HY_BANDS - 1, HY_BANDS, dtype=F32)
    ang = f[:, None] * w[None, :]
    z = jnp.concatenate([t[None, :], jnp.cos(ang), -jnp.sin(ang)], 0)
    nz = -(-z.shape[0] // 8) * 8
    z = jnp.pad(z, ((0, nz - z.shape[0]), (0, 0)))
    msk = jnp.stack([(tt != length).astype(F32), (tt == 0).astype(F32)])
    msk = jnp.pad(msk, ((0, 6), (0, 0)))
    col = lambda a: a.reshape(-1, 1)
    c2 = 2 * HY_CH
    nf = w3.shape[0]
    w3d = w3.reshape(nf, 2, 2, HY_CH).transpose(2, 1, 3, 0).reshape(2, c2, nf)
    ldd = log_decay.reshape(2, 2, HY_CH).transpose(1, 0, 2).reshape(2, c2, 1)
    rows = 8
    pb = rows * FFT_N2
    half = length // pb
    return pl.pallas_call(
        _hy_filt_t_kernel,
        grid=(n // pb,),
        in_specs=[pl.BlockSpec((nz, pb), lambda i: (0, i)),
                  pl.BlockSpec((8, pb), lambda i: (0, i)),
                  _resident((nf, nz)), _resident((nf, 1)), _resident((nf, 1)),
                  _resident((nf, nf)), _resident((nf, 1)),
                  pl.BlockSpec((1, c2, nf), lambda i: (i // half, 0, 0)),
                  pl.BlockSpec((1, c2, 1), lambda i: (i // half, 0, 0)),
                  _resident((c2, 1))],
        out_specs=pl.BlockSpec((c2, rows, FFT_N2), lambda i: (0, i, 0)),
        out_shape=jax.ShapeDtypeStruct((c2, n // FFT_N2, FFT_N2), F32),
        compiler_params=_cparams(("parallel",)),
        name="hy_filters_t",
    )(z, msk, _pad2(w1.T, nf, nz), col(b1), col(freq), w2.T, col(b2), w3d, ldd, skip.reshape(c2, 1))


def _dft_mats():
    n = FFT_N2
    c, s = _cs(1, n, n, n)
    ch, sh = c[:, :n // 2], s[:, :n // 2]
    f1_pair = np.block([[ch, sh], [-sh, ch]])
    f1_full = np.concatenate([c, -s], 0)
    m2r = np.block([[c, -s], [s, c]])
    m2i = np.block([[c, s], [-s, c]])
    f1_inv = np.block([[ch.T, -sh.T], [sh.T, ch.T]]) / (n * n)
    return tuple(_mxu_const(a) for a in (f1_pair, f1_full, m2r, m2i, f1_inv))


def _twiddle2d(n):
    k1 = lax.broadcasted_iota(jnp.int32, (n, n), 0)
    m2 = lax.broadcasted_iota(jnp.int32, (n, n), 1)
    ang = (k1 * m2).astype(F32) * (2.0 * math.pi / (n * n))
    return jnp.cos(ang), jnp.sin(ang)


def _fwd_spectrum(xs, f1, m2r, c, s):
    n = FFT_N2
    a = _dot(f1, jnp.concatenate(xs, axis=1))
    ts = []
    for g in range(len(xs)):
        ar = a[:n, g * n:(g + 1) * n]
        ai = a[n:, g * n:(g + 1) * n]
        ts.append(jnp.concatenate([ar * c + ai * s, ai * c - ar * s], axis=1))
    return _dot(jnp.concatenate(ts, axis=0).astype(BF16), m2r)


def _filter_spec_t_kernel(t_ref, f1_ref, m2r_ref, c_ref, s_ref, o_ref):
    g = t_ref.shape[0]
    xs = [t_ref[i].astype(BF16) for i in range(g)]
    spec = _fwd_spectrum(xs, f1_ref[...], m2r_ref[...], c_ref[...], s_ref[...])
    for i in range(g):
        o_ref[i] = spec[i * FFT_N2:(i + 1) * FFT_N2]


def _filter_spec_t(taps, f1_full, m2r, twc, tws):
    nch, n1, n = taps.shape
    g = HY_GROUP
    return pl.pallas_call(
        _filter_spec_t_kernel,
        grid=(nch // g,),
        in_specs=[pl.BlockSpec((g, n1, n), lambda i: (i, 0, 0)),
                  _resident(f1_full.shape), _resident(m2r.shape), _resident((n, n)), _resident((n, n))],
        out_specs=pl.BlockSpec((g, n, 2 * n), lambda i: (i, 0, 0)),
        out_shape=jax.ShapeDtypeStruct((nch, n, 2 * n), F32),
        compiler_params=_cparams(("parallel",)),
        name="filter_spec_t",
    )(taps, f1_full, m2r, twc, tws)


def _hyena_core_kernel(x1_ref, x2_ref, v_ref, w1_ref, w2_ref, wv_ref, h0_ref, h1_ref,
                       f1_ref, m2r_ref, m2i_ref, f1i_ref, c_ref, s_ref, o_ref):
    n = FFT_N2
    n1 = v_ref.shape[2]
    c = c_ref[...]
    s = s_ref[...]
    lane = lax.broadcasted_iota(jnp.int32, (n1, n), 1)
    row = lax.broadcasted_iota(jnp.int32, (n1, n), 0)
    first_lane, last_lane = lane == 0, lane == n - 1
    seq_start, seq_end = first_lane & (row == 0), last_lane & (row == n1 - 1)

    def short_conv(x, w):
        r = pltpu.roll(x, 1, axis=1)
        up = jnp.where(first_lane, pltpu.roll(r, 1, axis=0), r)
        up = jnp.where(seq_start, 0.0, up)
        l = pltpu.roll(x, n - 1, axis=1)
        dn = jnp.where(last_lane, pltpu.roll(l, n1 - 1, axis=0), l)
        dn = jnp.where(seq_end, 0.0, dn)
        return up * w[0:1] + x * w[1:2] + dn * w[2:3]

    def conv_all(xss, h_ref):
        spec_s = [_fwd_spectrum(xs, f1_ref[...], m2r_ref[...], c, s) for xs in xss]
        y_s = []
        for chain, spec in enumerate(spec_s):
            ys = []
            for g in range(HY_GROUP):
                xr = spec[g * n:(g + 1) * n, :n]
                xi = spec[g * n:(g + 1) * n, n:]
                hh = h_ref[chain * HY_GROUP + g]
                hr, hi = hh[:, :n], hh[:, n:]
                ys.append(jnp.concatenate([xr * hr - xi * hi, xr * hi + xi * hr], axis=1))
            y_s.append(jnp.concatenate(ys, axis=0).astype(BF16))
        bm_s = [_dot(y, m2i_ref[...]) for y in y_s]
        b_s = []
        for bm in bm_s:
            bs = []
            for g in range(HY_GROUP):
                br = bm[g * n:(g + 1) * n, :n]
                bi = bm[g * n:(g + 1) * n, n:]
                bs.append(jnp.concatenate([br * c - bi * s, bi * c + br * s], axis=0))
            b_s.append(jnp.concatenate(bs, axis=1).astype(BF16))
        outs = [_dot(f1i_ref[...], bc) for bc in b_s]
        return [[y[:, g * n:(g + 1) * n] for g in range(HY_GROUP)] for y in outs]

    def pair(ref, w_ref, ch):
        return jnp.concatenate([short_conv(ref[0, ch], w_ref[ch]), short_conv(ref[1, ch], w_ref[ch])], axis=0)

    chans = [[chain * HY_GROUP + g for g in range(HY_GROUP)] for chain in range(HY_CHAINS)]
    y1 = conv_all([[pair(v_ref, wv_ref, ch).astype(BF16) for ch in grp] for grp in chans], h0_ref)
    z = [[(y1[k][g] * pair(x1_ref, w1_ref, ch)).astype(BF16) for g, ch in enumerate(grp)]
         for k, grp in enumerate(chans)]
    y2 = conv_all(z, h1_ref)
    for k, grp in enumerate(chans):
        for g, ch in enumerate(grp):
            y = y2[k][g] * pair(x2_ref, w2_ref, ch)
            o_ref[0, :, ch, :] = y[:n1]
            o_ref[1, :, ch, :] = y[n1:]


def _hyena_core(ut, conv_w, spec, mats, twc, tws):
    f1_pair, _, m2r, m2i, f1_inv = mats
    b, nch, n1, n = ut.shape
    assert b % 2 == 0
    cb = HY_CBLK
    nblk = HY_CH // cb
    wt = jnp.broadcast_to(conv_w.T[:, :, None], (nch, conv_w.shape[0], n))
    xspec = lambda off: pl.BlockSpec((2, cb, n1, n), lambda j, i: (i, off * nblk + j, 0, 0))
    wspec = lambda off: pl.BlockSpec((cb, conv_w.shape[0], n), lambda j, i: (off * nblk + j, 0, 0))
    hspec = lambda off: pl.BlockSpec((cb, n, 2 * n), lambda j, i: (off * nblk + j, 0, 0))
    return pl.pallas_call(
        _hyena_core_kernel,
        grid=(nblk, b // 2),
        in_specs=[xspec(0), xspec(1), xspec(2), wspec(0), wspec(1), wspec(2), hspec(0), hspec(1),
                  _resident(f1_pair.shape), _resident(m2r.shape), _resident(m2i.shape), _resident(f1_inv.shape),
                  _resident((n, n)), _resident((n, n))],
        out_specs=pl.BlockSpec((2, n1, cb, n), lambda j, i: (i, 0, j, 0)),
        out_shape=jax.ShapeDtypeStruct((b, n1, HY_CH, n), F32),
        compiler_params=_cparams(("parallel", "parallel")),
        name="hyena_core",
    )(ut, ut, ut, wt, wt, wt, spec, spec, f1_pair, m2r, m2i, f1_inv, twc, tws)


def _hy_ctx_kernel(v_ref, x1_ref, x2_ref, f_ref, fi_ref, h_ref, o_ref):
    nf = f_ref.shape[0] // 2
    zin = v_ref[0]
    gates = (x1_ref, x2_ref)
    for o in range(2):
        x = _dot(f_ref[...], zin)
        xr, xi = x[:nf], x[nf:]
        hr = h_ref[o, :nf]
        hi = h_ref[o, nf:]
        y = jnp.concatenate([xr * hr - xi * hi, xr * hi + xi * hr], axis=0).astype(BF16)
        zin = (_dot(fi_ref[...], y) * gates[o][0].astype(F32)).astype(BF16)
    o_ref[0] = zin


def _hy_ctx(v, x1, x2, taps):
    b, lc, c = v.shape
    nf = 2 * lc
    cm, sm = _cs(1, nf, nf, nf)
    fwd = _mxu_const(np.concatenate([cm[:, :lc], -sm[:, :lc]], 0))
    fwd_full = _mxu_const(np.concatenate([cm, -sm], 0))
    inv = _mxu_const(np.concatenate([cm[:lc, :], -sm[:lc, :]], 1) / nf)
    spec = _left_mm(fwd_full, taps.astype(BF16).reshape(1, nf, 2 * c), F32, 2 * c)
    spec = spec.reshape(2 * nf, 2, c).transpose(1, 0, 2)
    blk = pl.BlockSpec((1, lc, c), lambda i: (i, 0, 0))
    return pl.pallas_call(
        _hy_ctx_kernel,
        grid=(b,),
        in_specs=[blk, blk, blk, _resident((2 * nf, lc)), _resident((lc, 2 * nf)),
                  _resident((2, 2 * nf, c))],
        out_specs=blk,
        out_shape=jax.ShapeDtypeStruct((b, lc, c), BF16),
        compiler_params=_cparams(("parallel",)),
        name="hy_ctx",
    )(v, x1, x2, fwd, inv, spec)


def _pair_rows(q2):
    lane = lax.broadcasted_iota(jnp.int32, q2.shape, 1)
    zero = jnp.zeros_like(q2)
    return jnp.concatenate([jnp.where(lane < NA_HD, q2, zero), jnp.where(lane >= NA_HD, q2, zero)], axis=0)


def _unpair_rows(o):
    r = o.shape[0] // 2
    lane = lax.broadcasted_iota(jnp.int32, (r, o.shape[1]), 1)
    return jnp.where(lane < NA_HD, o[:r], o[r:])


def _pair_softmax_pv(scores, values):
    m = scores[0].max(axis=-1, keepdims=True)
    for s in scores[1:]:
        m = jnp.maximum(m, s.max(axis=-1, keepdims=True))
    den = None
    acc = None
    for s, v in zip(scores, values):
        p = jnp.exp(s - m)
        d = p.sum(axis=-1, keepdims=True)
        a = _dot(p.astype(BF16), v)
        den = d if den is None else den + d
        acc = a if acc is None else acc + a
    return acc / den


def _natten_kernel(q_ref, k0, k1, k2, k3, v0, v1, v2, v3, kc_ref, vc_ref, bias_ref, o_ref,
                   kwin, vwin, *, rows):
    g = pl.program_id(1)
    rb = 4 * GRID_W
    for i, (kr, vr) in enumerate(((k0, v0), (k1, v1), (k2, v2), (k3, v3))):
        kwin[i * rb:(i + 1) * rb, :] = kr[0]
        vwin[i * rb:(i + 1) * rb, :] = vr[0]
    base = 4 * jnp.clip(2 * g - 1, 0, rows // 4 - 4)
    nwin = NA_WIN_R * GRID_W
    qscale = jnp.asarray(NA_HD ** -0.5, BF16)

    ones_lat = jnp.ones((nwin, LANES), BF16)
    ones_ctx = jnp.ones((kc_ref.shape[1], LANES), BF16)

    def rows_body(it, carry):
        work = []
        for u in range(NA_ROWS_PER_TRIP):
            rr = it * NA_ROWS_PER_TRIP + u
            r = 8 * g + rr
            rs = jnp.clip(r - NA_WIN_R // 2, 0, rows - NA_WIN_R)
            st = pl.multiple_of((rs - base) * GRID_W, GRID_W)
            qo = pl.multiple_of(rr * GRID_W, GRID_W)
            work += [(qo, st, rs - r + NA_WIN_R - 1, p) for p in range(NA_HEADS // 2)]
        scores = []
        for qo, st, d0, p in work:
            ls = slice(p * LANES, (p + 1) * LANES)
            qp = _pair_rows(q_ref[0, pl.ds(qo, GRID_W), ls] * qscale)
            scores.append((_dot_nt(qp, kwin[pl.ds(st, nwin), ls]) + bias_ref[d0, p].astype(F32),
                           _dot_nt(qp, kc_ref[0, :, ls])))
        maxima = [jnp.maximum(a.max(axis=-1, keepdims=True), b.max(axis=-1, keepdims=True)) for a, b in scores]
        for (qo, st, d0, p), (s_lat, s_ctx), m in zip(work, scores, maxima):
            ls = slice(p * LANES, (p + 1) * LANES)
            v_lat = jnp.concatenate([vwin[pl.ds(st, nwin), ls], ones_lat], axis=1)
            v_ctx = jnp.concatenate([vc_ref[0, :, ls], ones_ctx], axis=1)
            acc = _dot(jnp.exp(s_lat - m).astype(BF16), v_lat) + _dot(jnp.exp(s_ctx - m).astype(BF16), v_ctx)
            o = acc[:, :LANES] / acc[:, LANES:LANES + 1]
            o_ref[0, pl.ds(qo, GRID_W), ls] = _unpair_rows(o).astype(o_ref.dtype)
        return carry

    lax.fori_loop(0, 8 // NA_ROWS_PER_TRIP, rows_body, 0)


def _natten_bias(rpb):
    c = np.arange(GRID_W)[:, None]
    kc = np.arange(GRID_W)[None, :]
    cs = np.clip(c - NA_WIN_C // 2, 0, GRID_W - NA_WIN_C)
    valid = (kc >= cs) & (kc < cs + NA_WIN_C)
    dc = np.clip(kc - c + NA_WIN_C - 1, 0, 2 * NA_WIN_C - 2)
    ndc = 2 * NA_WIN_C - 1
    pick = ((dc[None] == np.arange(ndc)[:, None, None]) & valid[None]).astype(np.float32)
    tb = jnp.einsum('hrd,dck->hrck', rpb, jnp.asarray(pick), precision=lax.Precision.HIGHEST)
    tb = tb + jnp.asarray(np.where(valid, 0.0, NEG_BIG).astype(np.float32))
    slabs = []
    for d0 in range(NA_WIN_R):
        s = tb[:, d0:d0 + NA_WIN_R]
        s = s.transpose(0, 2, 1, 3).reshape(NA_HEADS, GRID_W, NA_WIN_R * GRID_W)
        slabs.append(s.reshape(NA_HEADS // 2, 2 * GRID_W, NA_WIN_R * GRID_W))
    return jnp.stack(slabs).astype(BF16)


def _natten(uq, uc, bias):
    b, length, _ = uq.shape
    c = NA_HEADS * NA_HD
    rows = length // GRID_W
    rb = 4 * GRID_W
    nkb = length // rb
    qrows = 8 * GRID_W
    lc = uc.shape[1]

    def kv_spec(col, off):
        return pl.BlockSpec((1, rb, c), lambda i, g: (i, jnp.clip(2 * g - 1, 0, nkb - 4) + off, col))

    return pl.pallas_call(
        functools.partial(_natten_kernel, rows=rows),
        grid=(b, rows // 8),
        in_specs=[pl.BlockSpec((1, qrows, c), lambda i, g: (i, g, 0))]
                 + [kv_spec(1, o) for o in range(4)] + [kv_spec(2, o) for o in range(4)]
                 + [pl.BlockSpec((1, lc, c), lambda i, g: (i, 0, 4)),
                    pl.BlockSpec((1, lc, c), lambda i, g: (i, 0, 5)),
                    _resident(bias.shape)],
        out_specs=pl.BlockSpec((1, qrows, c), lambda i, g: (i, g, 0)),
        out_shape=jax.ShapeDtypeStruct((b, length, c), BF16),
        scratch_shapes=[pltpu.VMEM((4 * rb, c), BF16), pltpu.VMEM((4 * rb, c), BF16)],
        compiler_params=_cparams(("parallel", "parallel")),
        name="natten",
    )(uq, *([uq] * 8), uc, uc, bias)


def _ctx_attn_kernel(q_ref, k_ref, v_ref, o_ref):
    qscale = jnp.asarray(NA_HD ** -0.5, BF16)
    for p in range(NA_HEADS // 2):
        ls = slice(p * LANES, (p + 1) * LANES)
        qp = _pair_rows(q_ref[0, :, ls] * qscale)
        o = _pair_softmax_pv([_dot_nt(qp, k_ref[0, :, ls])], [v_ref[0, :, ls]])
        o_ref[0, :, ls] = _unpair_rows(o).astype(o_ref.dtype)


def _ctx_attn(u):
    b, lc, _ = u.shape
    c = NA_HEADS * NA_HD
    spec = lambda col: pl.BlockSpec((1, lc, c), lambda i: (i, 0, col))
    return pl.pallas_call(
        _ctx_attn_kernel,
        grid=(b,),
        in_specs=[spec(3), spec(4), spec(5)],
        out_specs=pl.BlockSpec((1, lc, c), lambda i: (i, 0, 0)),
        out_shape=jax.ShapeDtypeStruct((b, lc, c), BF16),
        compiler_params=_cparams(("parallel",)),
        name="ctx_attn",
    )(u, u, u)


def _post_kernel(x_ref, mod_ref, ya_ref, yb_ref, wa_ref, wb_ref, lng_ref, lnb_ref, w1_ref, w2_ref, o_ref, *,
                 ya_slabs, yb_slabs):
    m = mod_ref[0, 0]
    lng = lng_ref[...]
    lnb = lnb_ref[...]
    ff = w1_ref.shape[1]
    step = MLP_CHUNK
    sub = ROW_BLK
    nsub = x_ref.shape[1] // sub

    def rows_of(ref, slabs, t):
        if slabs:
            tiles = range(t * sub // LANES, (t + 1) * sub // LANES)
            return jnp.concatenate([ref[0, s].T for s in tiles], axis=0).astype(BF16)
        return ref[0, t * sub:(t + 1) * sub, :]

    def head(t):
        rs = slice(t * sub, (t + 1) * sub)
        y = _dot(rows_of(ya_ref, ya_slabs, t), wa_ref[...]) + _dot(rows_of(yb_ref, yb_slabs, t), wb_ref[...])
        x1 = _norm_rows(ALPHA * x_ref[0, rs, :] + m[2:3] * y) * lng[0:1] + lnb[0:1]
        return x1, (_norm_rows(x1) * (1.0 + m[4:5]) + m[3:4]).astype(BF16)

    def mlp(h):
        acc = None
        for c in range(ff // step):
            a = jnp.maximum(_dot(h, w1_ref[:, c * step:(c + 1) * step]), 0.0)
            d = _dot((a * a).astype(BF16), w2_ref[c * step:(c + 1) * step, :])
            acc = d if acc is None else acc + d
        return acc

    cur = head(0)
    for t in range(nsub):
        nxt = head(t + 1) if t + 1 < nsub else None
        acc = mlp(cur[1])
        o_ref[0, t * sub:(t + 1) * sub, :] = _norm_rows(ALPHA * cur[0] + m[5:6] * acc) * lng[1:2] + lnb[1:2]
        cur = nxt


def _post(x, modtab, mod_row, ya, yb, wa, wb, lng, lnb, w1, w2, ya_slabs=False, yb_slabs=False):
    b, r, d = x.shape
    ka, kb = wa.shape[0], wb.shape[0]
    rows = min(r, 4 * ROW_BLK)
    row = lambda k: pl.BlockSpec((1, rows, k), lambda i, j: (i, j, 0))
    slab = lambda k: pl.BlockSpec((1, rows // LANES, k, LANES), lambda i, j: (i, j, 0, 0))
    return pl.pallas_call(
        functools.partial(_post_kernel, ya_slabs=ya_slabs, yb_slabs=yb_slabs),
        grid=(b, r // rows),
        in_specs=[row(d), pl.BlockSpec((1, 1, 6, d), lambda i, j: (i, mod_row, 0, 0)),
                  slab(ka) if ya_slabs else row(ka), slab(kb) if yb_slabs else row(kb),
                  _resident(wa.shape), _resident(wb.shape), _resident(lng.shape), _resident(lnb.shape),
                  _resident(w1.shape), _resident(w2.shape)],
        out_specs=row(d),
        out_shape=jax.ShapeDtypeStruct((b, r, d), F32),
        compiler_params=_cparams(("parallel", "parallel")),
        name="post_mixer",
    )(x, modtab, ya, yb, wa, wb, lng, lnb, w1, w2)


def _rope(x, cos, sinl, sinr):
    reps = x.shape[1] // LANES
    tile = lambda t: jnp.concatenate([t] * reps, axis=1)
    n = x.shape[1]
    quarter = MLA_ROPE // 4
    return (x * tile(cos) + pltpu.roll(x, n - quarter, axis=1) * tile(sinl)
            + pltpu.roll(x, quarter, axis=1) * tile(sinr))


def _front_cd_kernel(x_ref, xc_ref, mod_ref, w_ref, qn_ref, kvn_ref, wuq_ref, wuk_ref, wuv_ref, epe_ref, one_ref,
                     fng_ref, fnb_ref, avg_ref, cbd_ref, sbd_ref,
                     cos_ref, sl_ref, sr_ref,
                     q_ref, k_ref, v_ref, p_ref, qf_ref, *, ctx_blk):
    m = mod_ref[0, 0]
    is_ctx = (jnp.zeros((ROW_BLK, 1), jnp.int32) + pl.program_id(1)) == ctx_blk
    x = jnp.where(is_ctx, xc_ref[0], x_ref[0])
    h = _norm_rows(x) * (1.0 + m[1:2]) + m[0:1]
    u = _dot(h.astype(BF16), w_ref[...])
    o_kv = MLA_Q_RANK
    o_fn = o_kv + MLA_KV_RANK
    o_pe = o_fn + FN_CH

    def rms(x, g):
        return x * lax.rsqrt(jnp.mean(x * x, axis=-1, keepdims=True) + LN_EPS) * g

    cq = rms(u[:, :o_kv], qn_ref[...]).astype(BF16)
    q = _dot(cq, wuq_ref[...])
    q_ref[0] = _rope(q, cos_ref[...], sl_ref[...], sr_ref[...]).astype(BF16)

    ckv = rms(u[:, o_kv:o_fn], kvn_ref[...]).astype(BF16)
    kpe = _dot(u[:, o_pe:].astype(BF16), epe_ref[...])
    k = _dot(ckv, wuk_ref[...]) + _rope(kpe, cos_ref[...], sl_ref[...], sr_ref[...])
    k_ref[0] = k.astype(BF16)
    v_ref[0] = (_dot(ckv, wuv_ref[...]) + one_ref[...]).astype(BF16)

    uf = u[:, o_fn:o_pe]
    avg = avg_ref[...]
    uc = uf - _dot(uf.astype(BF16), avg)
    var = _dot((uc * uc).astype(BF16), avg)
    ug = (uc * lax.rsqrt(var + LN_EPS) * fng_ref[...] + fnb_ref[...]).astype(BF16)
    p_ref[0] = _dot_nt(cbd_ref[...], ug).astype(BF16)
    qf_ref[0] = _dot_nt(sbd_ref[...], ug).astype(BF16)


def _rope_tables(length, lt):
    t = jnp.arange(lt, dtype=jnp.int32)
    rows = (t // GRID_W).astype(F32)
    cols = (t % GRID_W).astype(F32)
    half = MLA_ROPE // 2
    inv = ROPE_THETA ** (-jnp.arange(0, half, 2, dtype=F32) / half)
    ar = rows[:, None] * inv[None, :]
    ac = cols[:, None] * inv[None, :]
    ang = jnp.concatenate([ar, ar, ac, ac], -1)
    is_lat = (t < length)[:, None]
    cos = jnp.where(is_lat, jnp.cos(ang), 1.0)
    sin = jnp.where(is_lat, jnp.sin(ang), 0.0)
    qd = MLA_ROPE // 4
    ones = jnp.ones((lt, MLA_NOPE), F32)
    zeros = jnp.zeros((lt, MLA_NOPE), F32)
    tail1 = jnp.ones((lt, HEAD_PAD - MLA_NOPE - MLA_ROPE), F32)
    tail0 = jnp.zeros((lt, HEAD_PAD - MLA_NOPE - MLA_ROPE), F32)
    z8 = jnp.zeros((lt, qd), F32)
    c = jnp.concatenate([ones, cos, tail1], -1)
    sl = jnp.concatenate([zeros, -sin[:, :qd], z8, -sin[:, 2 * qd:3 * qd], z8, tail0], -1)
    sr = jnp.concatenate([zeros, z8, sin[:, qd:2 * qd], z8, sin[:, 3 * qd:], tail0], -1)
    return c, sl, sr


def _head_slots(w, per_head, take_from, take_n):
    k = w.shape[0]
    w3 = w.reshape(k, MLA_HEADS, per_head)[:, :, take_from:take_from + take_n]
    w3 = jnp.pad(w3, ((0, 0), (0, 0), (0, HEAD_PAD - take_n)))
    return w3.reshape(k, MLA_HEADS * HEAD_PAD)


def _front_cd(xl, xc, modtab, w_in, q_norm, w_uq, kv_norm, w_ukv, fn_g, fn_b):
    b, length, d = xl.shape
    lt = length + xc.shape[1]
    nlat = length // ROW_BLK
    o_kv = MLA_Q_RANK
    o_pe = o_kv + MLA_KV_RANK
    o_fn = o_pe + MLA_ROPE
    hw = MLA_HEADS * HEAD_PAD
    w_perm = jnp.concatenate([w_in[:, :o_pe], w_in[:, o_fn:], w_in[:, o_pe:o_fn],
                              jnp.zeros((d, LANES - MLA_ROPE), w_in.dtype)], -1).astype(BF16)
    q_scale = (MLA_NOPE + MLA_ROPE) ** -0.5 * math.log2(math.e)
    wuq = _head_slots(w_uq * q_scale, MLA_NOPE + MLA_ROPE, 0, MLA_NOPE + MLA_ROPE).astype(BF16)
    wuk = _head_slots(w_ukv, MLA_NOPE + MLA_V, 0, MLA_NOPE).astype(BF16)
    wuv = _head_slots(w_ukv, MLA_NOPE + MLA_V, MLA_NOPE, MLA_V).astype(BF16)
    epe = np.zeros((LANES, hw), np.float32)
    for hd in range(MLA_HEADS):
        for i in range(MLA_ROPE):
            epe[i, hd * HEAD_PAD + MLA_NOPE + i] = 1.0
    epe = _mxu_const(epe)
    ones_col = np.zeros((1, hw), np.float32)
    ones_col[0, MLA_V::HEAD_PAD] = 1.0
    ones_col = jnp.asarray(ones_col)
    cm, sm = _cs(1, FN_GD, FN_GD, FN_GD)
    eye = np.eye(FN_GROUPS)
    cbd = _mxu_const(np.kron(eye, cm))
    sbd = _mxu_const(np.kron(eye, -sm))
    avg = _mxu_const(np.kron(eye, np.full((FN_GD, FN_GD), 1.0 / FN_GD)))
    rtab = _rope_tables(length, lt)
    row = lambda n: pl.BlockSpec((1, ROW_BLK, n), lambda i, j: (i, j, 0))
    tab = pl.BlockSpec((ROW_BLK, HEAD_PAD), lambda i, j: (j, 0))
    out = lambda n: jax.ShapeDtypeStruct((b, lt, n), BF16)
    fn_t = pl.BlockSpec((1, FN_CH, ROW_BLK), lambda i, j: (i, 0, j))
    fn_out = jax.ShapeDtypeStruct((b, FN_CH, lt), BF16)
    return pl.pallas_call(
        functools.partial(_front_cd_kernel, ctx_blk=nlat),
        grid=(b, lt // ROW_BLK),
        in_specs=[pl.BlockSpec((1, ROW_BLK, d), lambda i, j: (i, jnp.minimum(j, nlat - 1), 0)),
                  pl.BlockSpec((1, ROW_BLK, d), lambda i, j: (i, 0, 0)),
                  pl.BlockSpec((1, 1, 6, d), lambda i, j: (i, j // nlat, 0, 0)),
                  _resident(w_perm.shape), _resident((1, MLA_Q_RANK)), _resident((1, MLA_KV_RANK)),
                  _resident(wuq.shape), _resident(wuk.shape), _resident(wuv.shape), _resident(epe.shape),
                  _resident(ones_col.shape),
                  _resident((1, FN_CH)), _resident((1, FN_CH)), _resident(avg.shape), _resident(cbd.shape),
                  _resident(sbd.shape),
                  tab, tab, tab],
        out_specs=[row(hw), row(hw), row(hw), fn_t, fn_t],
        out_shape=[out(hw), out(hw), out(hw), fn_out, fn_out],
        compiler_params=_cparams(("parallel", "parallel")),
        name="front_cd",
    )(xl, xc, modtab, w_perm, q_norm.reshape(1, -1), kv_norm.reshape(1, -1), wuq, wuk, wuv, epe, ones_col,
      fn_g.reshape(1, -1), fn_b.reshape(1, -1), avg, cbd, sbd, *rtab)


def _mla_kernel(q_ref, k_ref, v_ref, o_ref, *, rows):
    n = q_ref.shape[1] // rows

    def scores(i):
        s = _dot_nt(q_ref[0, i * rows:(i + 1) * rows, :], k_ref[0])
        return s, s.max(axis=-1, keepdims=True)

    def finish(i, s, m):
        acc = _dot(jnp.exp2(s - m).astype(BF16), v_ref[0])
        o_ref[0, i * rows:(i + 1) * rows, :] = (acc / acc[:, MLA_V:MLA_V + 1]).astype(o_ref.dtype)

    pending = scores(0)
    for i in range(n):
        nxt = scores(i + 1) if i + 1 < n else None
        finish(i, *pending)
        pending = nxt


def _mla_attention(q, k, v, length):
    b, lt, hw = q.shape
    heads = hw // HEAD_PAD
    tq = MLA_Q_BLK
    kv = pl.BlockSpec((1, lt, HEAD_PAD), lambda i, h, j: (i, 0, h))
    qs = pl.BlockSpec((1, tq, HEAD_PAD), lambda i, h, j: (i, j, h))
    return pl.pallas_call(
        functools.partial(_mla_kernel, rows=MLA_Q_SUB),
        grid=(b, heads, length // tq),
        in_specs=[qs, kv, kv],
        out_specs=qs,
        out_shape=jax.ShapeDtypeStruct((b, length, hw), BF16),
        compiler_params=_cparams(("parallel", "parallel", "parallel")),
        name="mla_attention",
    )(q, k, v)


def _fnet_kernel(pa_ref, pb_ref, qa_ref, qb_ref, m1_ref, c_ref, s_ref, m2_ref, o_ref, zr_scr, zi_scr):
    n1 = FN_N1
    n2 = zr_scr.shape[2] // 2
    pairs = zr_scr.shape[0]

    def fill(scr, a_ref, b_ref):
        za = a_ref[0].astype(F32)
        zb = b_ref[0].astype(F32)
        for r in range(n1):
            scr[:, r, :] = jnp.concatenate([za[:, r * n2:(r + 1) * n2], zb[:, r * n2:(r + 1) * n2]], axis=1)

    fill(zr_scr, pa_ref, pb_ref)
    fill(zi_scr, qa_ref, qb_ref)
    c = c_ref[...]
    s = s_ref[...]
    for g0 in range(0, pairs, FN_GROUP):
        x = jnp.concatenate([jnp.concatenate([zr_scr[g0 + g], zi_scr[g0 + g]], axis=0) for g in range(FN_GROUP)],
                            axis=1).astype(BF16)
        a = _dot(m1_ref[...], x)
        ts = []
        for g in range(FN_GROUP):
            ar = a[:n1, g * LANES:(g + 1) * LANES]
            ai = a[n1:, g * LANES:(g + 1) * LANES]
            ts.append(jnp.concatenate([ar * c + ai * s, ai * c - ar * s], axis=1))
        y = _dot(jnp.concatenate(ts, axis=0).astype(BF16), m2_ref[...])
        for g in range(FN_GROUP):
            yt = y[g * n1:(g + 1) * n1].T
            o_ref[0, :, 2 * (g0 + g), :] = yt[:n2]
            o_ref[0, :, 2 * (g0 + g) + 1, :] = yt[n2:]


def _fnet(pt, qt, length):
    b, ch, _ = pt.shape
    n1 = FN_N1
    n2 = length // n1
    assert 2 * n2 == LANES
    half = ch // 2
    cm, sm = _cs(1, n1, n1, n1)
    m1 = _mxu_const(np.block([[cm, sm], [-sm, cm]]))
    c2, s2 = _cs(1, n2, n2, n2)
    z = np.zeros_like(c2)
    m2 = np.block([[c2, z], [z, c2], [s2, z], [z, s2]]) / math.sqrt(length * FN_GD)
    m2 = _mxu_const(m2)
    k1 = lax.broadcasted_iota(jnp.int32, (n1, LANES), 0)
    m = lax.broadcasted_iota(jnp.int32, (n1, LANES), 1) % n2
    ang = (k1 * m).astype(F32) * (2.0 * math.pi / length)
    twc, tws = jnp.cos(ang), jnp.sin(ang)
    pairs = FN_PAIRS
    nblk = half // pairs
    spec = lambda off: pl.BlockSpec((1, pairs, length), lambda j, i: (i, off * nblk + j, 0))
    return pl.pallas_call(
        _fnet_kernel,
        grid=(nblk, b),
        in_specs=[spec(0), spec(1), spec(0), spec(1), _resident(m1.shape), _resident((n1, LANES)),
                  _resident((n1, LANES)), _resident(m2.shape)],
        out_specs=pl.BlockSpec((1, n2, 2 * pairs, LANES), lambda j, i: (i, 0, j, 0)),
        out_shape=jax.ShapeDtypeStruct((b, n2, ch, LANES), F32),
        scratch_shapes=[pltpu.VMEM((pairs, n1, LANES), F32), pltpu.VMEM((pairs, n1, LANES), F32)],
        compiler_params=_cparams(("parallel", "parallel")),
        name="fnet",
    )(pt, pt, qt, qt, m1, twc, tws, m2)


def kernel(x, c, ctx, c_ctx, mod_w, mod_b, ln_g, ln_b, mlp_w1, mlp_w2,
           ab_w_in, ab_w_out, hy_conv_w, hy_w1, hy_b1, hy_freq, hy_w2, hy_b2, hy_w3, hy_log_decay, hy_skip, na_rpb,
           cd_w_in, cd_w_out, mla_q_norm, mla_w_uq, mla_kv_norm, mla_w_ukv, fn_norm_g, fn_norm_b):
    b, length, d = x.shape
    lc = ctx.shape[1]

    cc = jnp.concatenate([c, c_ctx[None], jnp.zeros((8 - b - 1, d), F32)], 0)
    mods = _mod_vectors(cc, mod_w, mod_b).reshape(DEPTH, 8, 6, d)
    modtab = [jnp.stack([mods[l, :b], jnp.broadcast_to(mods[l, b], (b, 6, d))], axis=1) for l in range(DEPTH)]

    n_hy = 3 * HY_CH
    w_in = ab_w_in[0].astype(BF16)
    uq, ut = _front_ab_lat(x, modtab[0], w_in[:, n_hy:], w_in[:, :n_hy].T)
    uc = _front_ab_ctx(ctx, modtab[0], w_in)
    fargs = (hy_w1[0], hy_b1[0], hy_freq[0], hy_w2[0], hy_b2[0], hy_w3[0], hy_log_decay[0])
    mats = _dft_mats()
    twc, tws = _twiddle2d(FFT_N2)
    spec = _filter_spec_t(_hy_filters_t(length, *fargs, hy_skip[0]), mats[1], mats[2], twc, tws)
    y_hy_t = _hyena_core(ut, hy_conv_w[0], spec, mats, twc, tws)
    x1c, x2c, vc = _hy_prep(uc, hy_conv_w[0], 0, lc // ROW_BLK)
    y_hy_c = _hy_ctx(vc, x1c, x2c, _bidir_taps(_hy_filters(lc, *fargs), hy_skip[0], lc))
    y_na = _natten(uq, uc, _natten_bias(na_rpb[0]))
    y_na_c = _ctx_attn(uc)
    w_out = ab_w_out[0].astype(BF16)
    mlp = (ln_g[0], ln_b[0], mlp_w1[0].astype(BF16), mlp_w2[0].astype(BF16))
    xl = _post(x, modtab[0], 0, y_hy_t, y_na, w_out[:HY_CH], w_out[HY_CH:], *mlp, ya_slabs=True)
    xc = _post(ctx, modtab[0], 1, y_hy_c, y_na_c, w_out[:HY_CH], w_out[HY_CH:], *mlp)

    q, k, vv, pt, qt = _front_cd(xl, xc, modtab[1], cd_w_in[0], mla_q_norm[0], mla_w_uq[0], mla_kv_norm[0],
                                 mla_w_ukv[0], fn_norm_g[0], fn_norm_b[0])
    o = _mla_attention(q, k, vv, length)
    y_fn = _fnet(pt, qt, length)
    w_out = cd_w_out[0]
    n_mla = MLA_HEADS * MLA_V
    wa = jnp.pad(w_out[:n_mla].reshape(MLA_HEADS, MLA_V, d), ((0, 0), (0, HEAD_PAD - MLA_V), (0, 0)))
    wa = wa.reshape(MLA_HEADS * HEAD_PAD, d).astype(BF16)
    wb = w_out[n_mla:].reshape(2, FN_CH // 2, d).transpose(1, 0, 2).reshape(FN_CH, d)
    return _post(xl, modtab[1], 0, o, y_fn, wa, wb.astype(BF16),
                 ln_g[1], ln_b[1], mlp_w1[1].astype(BF16), mlp_w2[1].astype(BF16), yb_slabs=True)
```

```python
import functools
import math

import numpy as np
import jax
import jax.numpy as jnp
from jax import lax
from jax.experimental import pallas as pl
from jax.experimental.pallas import tpu as pltpu

F32 = jnp.float32
BF16 = jnp.bfloat16

D_MODEL = 1024
DEPTH = 2
GRID_W = 64
HY_CH = 512
HY_EMB = 33
HY_BANDS = (HY_EMB - 1) // 2
NA_HEADS = 8
NA_HD = 64
NA_WIN_R = 8
NA_WIN_C = 16
MLA_HEADS = 8
MLA_Q_RANK = 384
MLA_KV_RANK = 256
MLA_NOPE = 64
MLA_ROPE = 32
MLA_V = 96
ROPE_THETA = 10000.0
FN_CH = 256
FN_GROUPS = 4
FN_GD = FN_CH // FN_GROUPS
D_FF = 4 * D_MODEL
ALPHA = (2.0 * DEPTH) ** 0.25
LN_EPS = 1e-5

LANES = 128
ROW_BLK = 256
HEAD_PAD = 128
FFT_N2 = 128
NA_ROWS_PER_TRIP = 8
FN_N1 = 128
FN_PAIRS = 16
FN_GROUP = 4
MLP_CHUNK = 1024
MLA_Q_BLK = 1024
MLA_Q_SUB = 256
HY_GROUP = 16
HY_CHAINS = 1
HY_CBLK = HY_GROUP * HY_CHAINS
VMEM_LIMIT = 56 * 1024 * 1024
NEG_BIG = -1e30


def _cparams(sem, vmem=VMEM_LIMIT):
    return pltpu.CompilerParams(dimension_semantics=sem, vmem_limit_bytes=vmem)


def _resident(shape):
    nd = len(shape)
    return pl.BlockSpec(shape, lambda *_: (0,) * nd, pipeline_mode=pl.Buffered(1))


def _norm_rows(x):
    mu = jnp.mean(x, axis=-1, keepdims=True)
    xc = x - mu
    var = jnp.mean(xc * xc, axis=-1, keepdims=True)
    return xc * lax.rsqrt(var + LN_EPS)


def _dot(a, b):
    return jnp.dot(a, b, preferred_element_type=F32)


def _mxu_const(a):
    return jnp.asarray(a, dtype=F32).astype(BF16)


def _dot_split(a, b):
    ah = a.astype(BF16)
    bh = b.astype(BF16)
    al = (a - ah.astype(F32)).astype(BF16)
    bl = (b - bh.astype(F32)).astype(BF16)
    return _dot(ah, bh) + _dot(ah, bl) + _dot(al, bh)


def _dot_nt(a, b):
    return lax.dot_general(a, b, (((1,), (1,)), ((), ())), preferred_element_type=F32)


def _mod_kernel(c_ref, w_ref, b_ref, o_ref):
    c = c_ref[...]
    s = c * (1.0 / (1.0 + jnp.exp(-c)))
    o_ref[0] = jnp.dot(s, w_ref[0], preferred_element_type=F32,
                       precision=lax.Precision.HIGHEST) + b_ref[0]


def _mod_vectors(cc, mod_w, mod_b):
    depth, d, n = mod_w.shape
    nb = 1024
    return pl.pallas_call(
        _mod_kernel,
        grid=(depth, n // nb),
        in_specs=[pl.BlockSpec((8, d), lambda l, j: (0, 0)),
                  pl.BlockSpec((1, d, nb), lambda l, j: (l, 0, j)),
                  pl.BlockSpec((1, 1, nb), lambda l, j: (l, 0, j))],
        out_specs=pl.BlockSpec((1, 8, nb), lambda l, j: (l, 0, j)),
        out_shape=jax.ShapeDtypeStruct((depth, 8, n), F32),
        compiler_params=_cparams(("parallel", "parallel")),
        name="mod_vectors",
    )(cc, mod_w, mod_b.reshape(depth, 1, n))


def _front_ab_ctx_kernel(x_ref, mod_ref, w_ref, u_ref):
    m = mod_ref[0, 0]
    h = _norm_rows(x_ref[0]) * (1.0 + m[1:2]) + m[0:1]
    u_ref[0] = _dot(h.astype(BF16), w_ref[...]).astype(BF16)


def _front_ab_ctx(xc, modtab, w_in):
    b, lc, d = xc.shape
    n = w_in.shape[1]
    return pl.pallas_call(
        _front_ab_ctx_kernel,
        grid=(b, lc // ROW_BLK),
        in_specs=[pl.BlockSpec((1, ROW_BLK, d), lambda i, j: (i, j, 0)),
                  pl.BlockSpec((1, 1, 6, d), lambda i, j: (i, 1, 0, 0)),
                  _resident((d, n))],
        out_specs=pl.BlockSpec((1, ROW_BLK, n), lambda i, j: (i, j, 0)),
        out_shape=jax.ShapeDtypeStruct((b, lc, n), BF16),
        compiler_params=_cparams(("parallel", "parallel")),
        name="front_ab_ctx",
    )(xc, modtab, w_in)


def _front_ab_lat_kernel(x_ref, mod_ref, wq_ref, wht_ref, u_ref, ut_ref):
    m = mod_ref[0, 0]
    hs = []
    for t in range(x_ref.shape[1] // ROW_BLK):
        rs = slice(t * ROW_BLK, (t + 1) * ROW_BLK)
        hs.append((_norm_rows(x_ref[0, rs, :]) * (1.0 + m[1:2]) + m[0:1]).astype(BF16))
        u_ref[0, rs, :] = _dot(hs[-1], wq_ref[...]).astype(BF16)
    h_all = jnp.concatenate(hs, axis=0)
    for c0 in range(0, wht_ref.shape[0], HY_CH):
        ut = _dot_nt(wht_ref[c0:c0 + HY_CH, :], h_all)
        for s in range(ut_ref.shape[2]):
            ut_ref[0, c0:c0 + HY_CH, s, :] = ut[:, s * FFT_N2:(s + 1) * FFT_N2]


def _front_ab_lat(x, modtab, w_qkv, w_hy_t):
    b, length, d = x.shape
    nq = w_qkv.shape[1]
    nh = w_hy_t.shape[0]
    rows = 8 * FFT_N2
    return pl.pallas_call(
        _front_ab_lat_kernel,
        grid=(b, length // rows),
        in_specs=[pl.BlockSpec((1, rows, d), lambda i, j: (i, j, 0)),
                  pl.BlockSpec((1, 1, 6, d), lambda i, j: (i, 0, 0, 0)),
                  _resident((d, nq)), _resident((nh, d))],
        out_specs=[pl.BlockSpec((1, rows, nq), lambda i, j: (i, j, 0)),
                   pl.BlockSpec((1, nh, 8, FFT_N2), lambda i, j: (i, 0, j, 0))],
        out_shape=[jax.ShapeDtypeStruct((b, length, nq), BF16),
                   jax.ShapeDtypeStruct((b, nh, length // FFT_N2, FFT_N2), F32)],
        compiler_params=_cparams(("parallel", "parallel")),
        name="front_ab_lat",
    )(x, modtab, w_qkv, w_hy_t)


def _hy_prep_kernel(cur_ref, prev_ref, next_ref, w_ref, x1_ref, x2_ref, v_ref, *, nblk):
    j = pl.program_id(1)
    cur = cur_ref[0].astype(F32)
    rows = cur.shape[0]
    has_prev = (j > 0).astype(F32)
    has_next = (j < nblk - 1).astype(F32)
    prev_row = prev_ref[0][7:8].astype(F32) * has_prev
    next_row = next_ref[0][0:1].astype(F32) * has_next
    rid = lax.broadcasted_iota(jnp.int32, (rows, 1), 0)
    up = jnp.where(rid == 0, prev_row, pltpu.roll(cur, 1, axis=0))
    dn = jnp.where(rid == rows - 1, next_row, pltpu.roll(cur, rows - 1, axis=0))
    w = w_ref[...]
    y = up * w[0:1] + cur * w[1:2] + dn * w[2:3]
    c = HY_CH
    x1_ref[0] = y[:, :c].astype(BF16)
    x2_ref[0] = y[:, c:2 * c].astype(BF16)
    v_ref[0] = y[:, 2 * c:].astype(BF16)


def _hy_prep(u, conv_w, blk0, nblk):
    b, lt, _ = u.shape
    n = 3 * HY_CH
    sub = ROW_BLK // 8
    last8 = lt // 8 - 1
    out = jax.ShapeDtypeStruct((b, nblk * ROW_BLK, HY_CH), BF16)
    ospec = pl.BlockSpec((1, ROW_BLK, HY_CH), lambda i, j: (i, j, 0))
    return pl.pallas_call(
        functools.partial(_hy_prep_kernel, nblk=nblk),
        grid=(b, nblk),
        in_specs=[pl.BlockSpec((1, ROW_BLK, n), lambda i, j: (i, blk0 + j, 0)),
                  pl.BlockSpec((1, 8, n), lambda i, j: (i, jnp.maximum((blk0 + j) * sub - 1, 0), 0)),
                  pl.BlockSpec((1, 8, n), lambda i, j: (i, jnp.minimum((blk0 + j + 1) * sub, last8), 0)),
                  _resident((3, n))],
        out_specs=[ospec, ospec, ospec],
        out_shape=[out, out, out],
        compiler_params=_cparams(("parallel", "parallel")),
        name="hy_prep",
    )(u, u, u, conv_w)


def _hy_filt_kernel(z_ref, w1_ref, b1_ref, fr_ref, w2_ref, b2_ref, w3_ref, ld_ref, o_ref):
    hi = lax.Precision.HIGHEST
    z = z_ref[...]
    fr = fr_ref[...]
    hid = jnp.sin(fr * (jnp.dot(z, w1_ref[...], preferred_element_type=F32, precision=hi) + b1_ref[...]))
    hid = jnp.sin(fr * (jnp.dot(hid, w2_ref[...], preferred_element_type=F32, precision=hi) + b2_ref[...]))
    h = jnp.dot(hid, w3_ref[...], preferred_element_type=F32, precision=hi)
    t = z[:, 0:1]
    o_ref[...] = h * jnp.exp(-t * jnp.exp(ld_ref[...]))


def _pad2(a, rows, cols):
    return jnp.pad(a, ((0, rows - a.shape[0]), (0, cols - a.shape[1])))


def _hy_filters(length, w1, b1, freq, w2, b2, w3, log_decay):
    pos = jnp.arange(length, dtype=F32)
    t = pos / max(length - 1, 1)
    w = 2.0 * math.pi * pos / length
    f = jnp.linspace(1e-4, HY_BANDS - 1, HY_BANDS, dtype=F32)
    ang = w[:, None] * f[None, :]
    z = jnp.concatenate([t[:, None], jnp.cos(ang), -jnp.sin(ang)], -1)
    z = _pad2(z, length, LANES)
    n = w3.shape[1]
    rb = min(length, 512)
    vec = lambda a: _pad2(a.reshape(1, -1), 1, LANES)
    return pl.pallas_call(
        _hy_filt_kernel,
        grid=(length // rb,),
        in_specs=[pl.BlockSpec((rb, LANES), lambda i: (i, 0)),
                  _resident((LANES, LANES)), _resident((1, LANES)), _resident((1, LANES)),
                  _resident((LANES, LANES)), _resident((1, LANES)),
                  _resident((LANES, n)), _resident((1, n))],
        out_specs=pl.BlockSpec((rb, n), lambda i: (i, 0)),
        out_shape=jax.ShapeDtypeStruct((length, n), F32),
        compiler_params=_cparams(("parallel",)),
        name="hy_filters",
    )(z, _pad2(w1, LANES, LANES), vec(b1), vec(freq), _pad2(w2, LANES, LANES), vec(b2),
      _pad2(w3, LANES, n), log_decay.reshape(1, n))


def _bidir_taps(h, skip, length):
    h4 = h.reshape(length, 2, 2, HY_CH)
    cols = []
    for o in range(2):
        hf = h4[:, o, 0].at[0].add(skip[o])
        hb = h4[:, o, 1]
        cols.append(jnp.concatenate([hf, jnp.zeros_like(hf[:1]), hb[:0:-1]], 0))
    return jnp.concatenate(cols, -1)


def _left_mm_kernel(m_ref, x_ref, o_ref):
    o_ref[0] = _dot(m_ref[...], x_ref[0]).astype(o_ref.dtype)


def _left_mm(mat, x, out_dtype, lane_blk):
    g, k, n = x.shape
    m = mat.shape[0]
    lane_blk = min(lane_blk, n)
    return pl.pallas_call(
        _left_mm_kernel, grid=(g, n // lane_blk),
        in_specs=[_resident((m, k)), pl.BlockSpec((1, k, lane_blk), lambda i, j: (i, 0, j))],
        out_specs=pl.BlockSpec((1, m, lane_blk), lambda i, j: (i, 0, j)),
        out_shape=jax.ShapeDtypeStruct((g, m, n), out_dtype),
        compiler_params=_cparams(("parallel", "parallel")),
        name="left_mm",
    )(mat, x)


def _cs(num, den, rows, cols):
    ang = 2.0 * np.pi * np.outer(np.arange(rows), np.arange(cols)) * (num / den)
    return np.cos(ang), np.sin(ang)


def _hy_filt_t_kernel(f_ref, w1_ref, b1_ref, fr_ref, w2_ref, b2_ref, w3_ref, ld_ref, sk_ref, o_ref, *, length):
    hi = lax.Precision.HIGHEST
    pb = o_ref.shape[1] * FFT_N2
    tt = pl.program_id(0) * pb + lax.broadcasted_iota(jnp.int32, (1, pb), 1)
    pos = jnp.where(tt < length, tt, 2 * length - tt).astype(F32)
    t = pos * (1.0 / max(length - 1, 1))
    ang = f_ref[...] * ((2.0 * math.pi * pos) * (1.0 / length))
    first = lax.broadcasted_iota(jnp.int32, (8, pb), 0) == 0
    z = jnp.concatenate([jnp.cos(ang), -jnp.sin(ang), jnp.where(first, t, 0.0)], axis=0)
    fr = fr_ref[...]
    hid = jnp.sin(fr * (jnp.dot(w1_ref[...], z, preferred_element_type=F32, precision=hi) + b1_ref[...]))
    hid = jnp.sin(fr * (jnp.dot(w2_ref[...], hid, preferred_element_type=F32, precision=hi) + b2_ref[...]))
    h = _dot_split(w3_ref[0], hid)
    h = h * jnp.exp(-jnp.exp(ld_ref[0]) * t)
    h = jnp.where(tt == length, 0.0, h) + sk_ref[...] * (tt == 0).astype(F32)
    for s in range(o_ref.shape[1]):
        o_ref[:, s, :] = h[:, s * FFT_N2:(s + 1) * FFT_N2]


def _hy_filters_t(length, w1, b1, freq, w2, b2, w3, log_decay, skip):
    n = 2 * length
    f = jnp.linspace(1e-4, HY_BANDS - 1, HY_BANDS, dtype=F32).reshape(HY_BANDS, 1)
    nz = 2 * HY_BANDS + 8
    w1p = jnp.concatenate([w1[1:], w1[:1], jnp.zeros((7, w1.shape[1]), w1.dtype)], 0)
    col = lambda a: a.reshape(-1, 1)
    c2 = 2 * HY_CH
    nf = w3.shape[0]
    w3d = w3.reshape(nf, 2, 2, HY_CH).transpose(2, 1, 3, 0).reshape(2, c2, nf)
    ldd = log_decay.reshape(2, 2, HY_CH).transpose(1, 0, 2).reshape(2, c2, 1)
    rows = 8
    pb = rows * FFT_N2
    half = length // pb
    return pl.pallas_call(
        functools.partial(_hy_filt_t_kernel, length=length),
        grid=(n // pb,),
        in_specs=[_resident((HY_BANDS, 1)),
                  _resident((nf, nz)), _resident((nf, 1)), _resident((nf, 1)),
                  _resident((nf, nf)), _resident((nf, 1)),
                  pl.BlockSpec((1, c2, nf), lambda i: (i // half, 0, 0)),
                  pl.BlockSpec((1, c2, 1), lambda i: (i // half, 0, 0)),
                  _resident((c2, 1))],
        out_specs=pl.BlockSpec((c2, rows, FFT_N2), lambda i: (0, i, 0)),
        out_shape=jax.ShapeDtypeStruct((c2, n // FFT_N2, FFT_N2), F32),
        compiler_params=_cparams(("parallel",)),
        name="hy_filters_t",
    )(f, w1p.T, col(b1), col(freq), w2.T, col(b2), w3d, ldd, skip.reshape(c2, 1))


def _dft_mats():
    n = FFT_N2
    c, s = _cs(1, n, n, n)
    ch, sh = c[:, :n // 2], s[:, :n // 2]
    f1_pair = np.block([[ch, sh], [-sh, ch]])
    f1_full = np.concatenate([c, -s], 0)
    m2r = np.block([[c, -s], [s, c]])
    m2i = np.block([[c, s], [-s, c]])
    f1_inv = np.block([[ch.T, -sh.T], [sh.T, ch.T]]) / (n * n)
    return tuple(_mxu_const(a) for a in (f1_pair, f1_full, m2r, m2i, f1_inv))


def _twiddle2d(n):
    k1 = lax.broadcasted_iota(jnp.int32, (n, n), 0)
    m2 = lax.broadcasted_iota(jnp.int32, (n, n), 1)
    ang = (k1 * m2).astype(F32) * (2.0 * math.pi / (n * n))
    return jnp.cos(ang), jnp.sin(ang)


def _fwd_spectrum(xs, f1, m2r, c, s):
    n = FFT_N2
    a = _dot(f1, jnp.concatenate(xs, axis=1))
    ts = []
    for g in range(len(xs)):
        ar = a[:n, g * n:(g + 1) * n]
        ai = a[n:, g * n:(g + 1) * n]
        ts.append(jnp.concatenate([ar * c + ai * s, ai * c - ar * s], axis=1))
    return _dot(jnp.concatenate(ts, axis=0).astype(BF16), m2r)


def _filter_spec_t_kernel(t_ref, f1_ref, m2r_ref, c_ref, s_ref, o_ref):
    g = t_ref.shape[0]
    xs = [t_ref[i].astype(BF16) for i in range(g)]
    spec = _fwd_spectrum(xs, f1_ref[...], m2r_ref[...], c_ref[...], s_ref[...])
    for i in range(g):
        o_ref[i] = spec[i * FFT_N2:(i + 1) * FFT_N2]


def _filter_spec_t(taps, f1_full, m2r, twc, tws):
    nch, n1, n = taps.shape
    g = HY_GROUP
    return pl.pallas_call(
        _filter_spec_t_kernel,
        grid=(nch // g,),
        in_specs=[pl.BlockSpec((g, n1, n), lambda i: (i, 0, 0)),
                  _resident(f1_full.shape), _resident(m2r.shape), _resident((n, n)), _resident((n, n))],
        out_specs=pl.BlockSpec((g, n, 2 * n), lambda i: (i, 0, 0)),
        out_shape=jax.ShapeDtypeStruct((nch, n, 2 * n), F32),
        compiler_params=_cparams(("parallel",)),
        name="filter_spec_t",
    )(taps, f1_full, m2r, twc, tws)


def _hyena_core_kernel(x1_ref, x2_ref, v_ref, w1_ref, w2_ref, wv_ref, h0_ref, h1_ref,
                       f1_ref, m2r_ref, m2i_ref, f1i_ref, c_ref, s_ref, o_ref):
    n = FFT_N2
    n1 = v_ref.shape[2]
    c = c_ref[...]
    s = s_ref[...]
    lane = lax.broadcasted_iota(jnp.int32, (n1, n), 1)
    row = lax.broadcasted_iota(jnp.int32, (n1, n), 0)
    first_lane, last_lane = lane == 0, lane == n - 1
    seq_start, seq_end = first_lane & (row == 0), last_lane & (row == n1 - 1)

    def short_conv(x, w):
        r = pltpu.roll(x, 1, axis=1)
        up = jnp.where(first_lane, pltpu.roll(r, 1, axis=0), r)
        up = jnp.where(seq_start, 0.0, up)
        l = pltpu.roll(x, n - 1, axis=1)
        dn = jnp.where(last_lane, pltpu.roll(l, n1 - 1, axis=0), l)
        dn = jnp.where(seq_end, 0.0, dn)
        return up * w[0:1] + x * w[1:2] + dn * w[2:3]

    def conv_all(xss, h_ref):
        spec_s = [_fwd_spectrum(xs, f1_ref[...], m2r_ref[...], c, s) for xs in xss]
        y_s = []
        for chain, spec in enumerate(spec_s):
            ys = []
            for g in range(HY_GROUP):
                xr = spec[g * n:(g + 1) * n, :n]
                xi = spec[g * n:(g + 1) * n, n:]
                hh = h_ref[chain * HY_GROUP + g]
                hr, hi = hh[:, :n], hh[:, n:]
                ys.append(jnp.concatenate([xr * hr - xi * hi, xr * hi + xi * hr], axis=1))
            y_s.append(jnp.concatenate(ys, axis=0).astype(BF16))
        bm_s = [_dot(y, m2i_ref[...]) for y in y_s]
        b_s = []
        for bm in bm_s:
            bs = []
            for g in range(HY_GROUP):
                br = bm[g * n:(g + 1) * n, :n]
                bi = bm[g * n:(g + 1) * n, n:]
                bs.append(jnp.concatenate([br * c - bi * s, bi * c + br * s], axis=0))
            b_s.append(jnp.concatenate(bs, axis=1).astype(BF16))
        outs = [_dot(f1i_ref[...], bc) for bc in b_s]
        return [[y[:, g * n:(g + 1) * n] for g in range(HY_GROUP)] for y in outs]

    def pair(ref, w_ref, ch):
        return jnp.concatenate([short_conv(ref[0, ch], w_ref[ch]), short_conv(ref[1, ch], w_ref[ch])], axis=0)

    chans = [[chain * HY_GROUP + g for g in range(HY_GROUP)] for chain in range(HY_CHAINS)]
    y1 = conv_all([[pair(v_ref, wv_ref, ch).astype(BF16) for ch in grp] for grp in chans], h0_ref)
    z = [[(y1[k][g] * pair(x1_ref, w1_ref, ch)).astype(BF16) for g, ch in enumerate(grp)]
         for k, grp in enumerate(chans)]
    y2 = conv_all(z, h1_ref)
    for k, grp in enumerate(chans):
        for g, ch in enumerate(grp):
            y = y2[k][g] * pair(x2_ref, w2_ref, ch)
            o_ref[0, :, ch, :] = y[:n1]
            o_ref[1, :, ch, :] = y[n1:]


def _hyena_core(ut, conv_w, spec, mats, twc, tws):
    f1_pair, _, m2r, m2i, f1_inv = mats
    b, nch, n1, n = ut.shape
    assert b % 2 == 0
    cb = HY_CBLK
    nblk = HY_CH // cb
    wt = jnp.broadcast_to(conv_w.T[:, :, None], (nch, conv_w.shape[0], n))
    xspec = lambda off: pl.BlockSpec((2, cb, n1, n), lambda j, i: (i, off * nblk + j, 0, 0))
    wspec = lambda off: pl.BlockSpec((cb, conv_w.shape[0], n), lambda j, i: (off * nblk + j, 0, 0))
    hspec = lambda off: pl.BlockSpec((cb, n, 2 * n), lambda j, i: (off * nblk + j, 0, 0))
    return pl.pallas_call(
        _hyena_core_kernel,
        grid=(nblk, b // 2),
        in_specs=[xspec(0), xspec(1), xspec(2), wspec(0), wspec(1), wspec(2), hspec(0), hspec(1),
                  _resident(f1_pair.shape), _resident(m2r.shape), _resident(m2i.shape), _resident(f1_inv.shape),
                  _resident((n, n)), _resident((n, n))],
        out_specs=pl.BlockSpec((2, n1, cb, n), lambda j, i: (i, 0, j, 0)),
        out_shape=jax.ShapeDtypeStruct((b, n1, HY_CH, n), F32),
        compiler_params=_cparams(("parallel", "parallel")),
        name="hyena_core",
    )(ut, ut, ut, wt, wt, wt, spec, spec, f1_pair, m2r, m2i, f1_inv, twc, tws)


def _hy_ctx_kernel(v_ref, x1_ref, x2_ref, f_ref, fi_ref, h_ref, o_ref):
    nf = f_ref.shape[0] // 2
    zin = v_ref[0]
    gates = (x1_ref, x2_ref)
    for o in range(2):
        x = _dot(f_ref[...], zin)
        xr, xi = x[:nf], x[nf:]
        hr = h_ref[o, :nf]
        hi = h_ref[o, nf:]
        y = jnp.concatenate([xr * hr - xi * hi, xr * hi + xi * hr], axis=0).astype(BF16)
        zin = (_dot(fi_ref[...], y) * gates[o][0].astype(F32)).astype(BF16)
    o_ref[0] = zin


def _hy_ctx(v, x1, x2, taps):
    b, lc, c = v.shape
    nf = 2 * lc
    cm, sm = _cs(1, nf, nf, nf)
    fwd = _mxu_const(np.concatenate([cm[:, :lc], -sm[:, :lc]], 0))
    fwd_full = _mxu_const(np.concatenate([cm, -sm], 0))
    inv = _mxu_const(np.concatenate([cm[:lc, :], -sm[:lc, :]], 1) / nf)
    spec = _left_mm(fwd_full, taps.astype(BF16).reshape(1, nf, 2 * c), F32, 2 * c)
    spec = spec.reshape(2 * nf, 2, c).transpose(1, 0, 2)
    blk = pl.BlockSpec((1, lc, c), lambda i: (i, 0, 0))
    return pl.pallas_call(
        _hy_ctx_kernel,
        grid=(b,),
        in_specs=[blk, blk, blk, _resident((2 * nf, lc)), _resident((lc, 2 * nf)),
                  _resident((2, 2 * nf, c))],
        out_specs=blk,
        out_shape=jax.ShapeDtypeStruct((b, lc, c), BF16),
        compiler_params=_cparams(("parallel",)),
        name="hy_ctx",
    )(v, x1, x2, fwd, inv, spec)


def _pair_rows(q2):
    lane = lax.broadcasted_iota(jnp.int32, q2.shape, 1)
    zero = jnp.zeros_like(q2)
    return jnp.concatenate([jnp.where(lane < NA_HD, q2, zero), jnp.where(lane >= NA_HD, q2, zero)], axis=0)


def _unpair_rows(o):
    r = o.shape[0] // 2
    lane = lax.broadcasted_iota(jnp.int32, (r, o.shape[1]), 1)
    return jnp.where(lane < NA_HD, o[:r], o[r:])


def _pair_softmax_pv(scores, values):
    m = scores[0].max(axis=-1, keepdims=True)
    for s in scores[1:]:
        m = jnp.maximum(m, s.max(axis=-1, keepdims=True))
    den = None
    acc = None
    for s, v in zip(scores, values):
        p = jnp.exp(s - m)
        d = p.sum(axis=-1, keepdims=True)
        a = _dot(p.astype(BF16), v)
        den = d if den is None else den + d
        acc = a if acc is None else acc + a
    return acc / den


def _natten_kernel(q_ref, k0, k1, k2, k3, v0, v1, v2, v3, kc_ref, vc_ref, bias_ref, o_ref,
                   kwin, vwin, *, rows):
    g = pl.program_id(1)
    rb = 4 * GRID_W
    for i, (kr, vr) in enumerate(((k0, v0), (k1, v1), (k2, v2), (k3, v3))):
        kwin[i * rb:(i + 1) * rb, :] = kr[0]
        vwin[i * rb:(i + 1) * rb, :] = vr[0]
    base = 4 * jnp.clip(2 * g - 1, 0, rows // 4 - 4)
    nwin = NA_WIN_R * GRID_W
    qscale = jnp.asarray(NA_HD ** -0.5, BF16)

    ones_lat = jnp.ones((nwin, LANES), BF16)
    ones_ctx = jnp.ones((kc_ref.shape[1], LANES), BF16)

    def rows_body(it, carry):
        work = []
        for u in range(NA_ROWS_PER_TRIP):
            rr = it * NA_ROWS_PER_TRIP + u
            r = 8 * g + rr
            rs = jnp.clip(r - NA_WIN_R // 2, 0, rows - NA_WIN_R)
            st = pl.multiple_of((rs - base) * GRID_W, GRID_W)
            qo = pl.multiple_of(rr * GRID_W, GRID_W)
            work += [(qo, st, rs - r + NA_WIN_R - 1, p) for p in range(NA_HEADS // 2)]
        scores = []
        for qo, st, d0, p in work:
            ls = slice(p * LANES, (p + 1) * LANES)
            qp = _pair_rows(q_ref[0, pl.ds(qo, GRID_W), ls] * qscale)
            scores.append((_dot_nt(qp, kwin[pl.ds(st, nwin), ls]) + bias_ref[d0, p].astype(F32),
                           _dot_nt(qp, kc_ref[0, :, ls])))
        maxima = [jnp.maximum(a.max(axis=-1, keepdims=True), b.max(axis=-1, keepdims=True)) for a, b in scores]
        for (qo, st, d0, p), (s_lat, s_ctx), m in zip(work, scores, maxima):
            ls = slice(p * LANES, (p + 1) * LANES)
            v_lat = jnp.concatenate([vwin[pl.ds(st, nwin), ls], ones_lat], axis=1)
            v_ctx = jnp.concatenate([vc_ref[0, :, ls], ones_ctx], axis=1)
            acc = _dot(jnp.exp(s_lat - m).astype(BF16), v_lat) + _dot(jnp.exp(s_ctx - m).astype(BF16), v_ctx)
            o = acc[:, :LANES] / acc[:, LANES:LANES + 1]
            o_ref[0, pl.ds(qo, GRID_W), ls] = _unpair_rows(o).astype(o_ref.dtype)
        return carry

    lax.fori_loop(0, 8 // NA_ROWS_PER_TRIP, rows_body, 0)


def _natten_bias(rpb):
    c = np.arange(GRID_W)[:, None]
    kc = np.arange(GRID_W)[None, :]
    cs = np.clip(c - NA_WIN_C // 2, 0, GRID_W - NA_WIN_C)
    valid = (kc >= cs) & (kc < cs + NA_WIN_C)
    dc = np.clip(kc - c + NA_WIN_C - 1, 0, 2 * NA_WIN_C - 2)
    ndc = 2 * NA_WIN_C - 1
    pick = ((dc[None] == np.arange(ndc)[:, None, None]) & valid[None]).astype(np.float32)
    tb = jnp.einsum('hrd,dck->hrck', rpb, jnp.asarray(pick), precision=lax.Precision.HIGHEST)
    tb = tb + jnp.asarray(np.where(valid, 0.0, NEG_BIG).astype(np.float32))
    slabs = []
    for d0 in range(NA_WIN_R):
        s = tb[:, d0:d0 + NA_WIN_R]
        s = s.transpose(0, 2, 1, 3).reshape(NA_HEADS, GRID_W, NA_WIN_R * GRID_W)
        slabs.append(s.reshape(NA_HEADS // 2, 2 * GRID_W, NA_WIN_R * GRID_W))
    return jnp.stack(slabs).astype(BF16)


def _natten(uq, uc, bias):
    b, length, _ = uq.shape
    c = NA_HEADS * NA_HD
    rows = length // GRID_W
    rb = 4 * GRID_W
    nkb = length // rb
    qrows = 8 * GRID_W
    lc = uc.shape[1]

    def kv_spec(col, off):
        return pl.BlockSpec((1, rb, c), lambda i, g: (i, jnp.clip(2 * g - 1, 0, nkb - 4) + off, col))

    return pl.pallas_call(
        functools.partial(_natten_kernel, rows=rows),
        grid=(b, rows // 8),
        in_specs=[pl.BlockSpec((1, qrows, c), lambda i, g: (i, g, 0))]
                 + [kv_spec(1, o) for o in range(4)] + [kv_spec(2, o) for o in range(4)]
                 + [pl.BlockSpec((1, lc, c), lambda i, g: (i, 0, 4)),
                    pl.BlockSpec((1, lc, c), lambda i, g: (i, 0, 5)),
                    _resident(bias.shape)],
        out_specs=pl.BlockSpec((1, qrows, c), lambda i, g: (i, g, 0)),
        out_shape=jax.ShapeDtypeStruct((b, length, c), BF16),
        scratch_shapes=[pltpu.VMEM((4 * rb, c), BF16), pltpu.VMEM((4 * rb, c), BF16)],
        compiler_params=_cparams(("parallel", "parallel")),
        name="natten",
    )(uq, *([uq] * 8), uc, uc, bias)


def _ctx_attn_kernel(q_ref, k_ref, v_ref, o_ref):
    qscale = jnp.asarray(NA_HD ** -0.5, BF16)
    for p in range(NA_HEADS // 2):
        ls = slice(p * LANES, (p + 1) * LANES)
        qp = _pair_rows(q_ref[0, :, ls] * qscale)
        o = _pair_softmax_pv([_dot_nt(qp, k_ref[0, :, ls])], [v_ref[0, :, ls]])
        o_ref[0, :, ls] = _unpair_rows(o).astype(o_ref.dtype)


def _ctx_attn(u):
    b, lc, _ = u.shape
    c = NA_HEADS * NA_HD
    spec = lambda col: pl.BlockSpec((1, lc, c), lambda i: (i, 0, col))
    return pl.pallas_call(
        _ctx_attn_kernel,
        grid=(b,),
        in_specs=[spec(3), spec(4), spec(5)],
        out_specs=pl.BlockSpec((1, lc, c), lambda i: (i, 0, 0)),
        out_shape=jax.ShapeDtypeStruct((b, lc, c), BF16),
        compiler_params=_cparams(("parallel",)),
        name="ctx_attn",
    )(u, u, u)


def _post_kernel(x_ref, mod_ref, ya_ref, yb_ref, wa_ref, wb_ref, lng_ref, lnb_ref, w1_ref, w2_ref, o_ref, *,
                 ya_slabs, yb_slabs):
    m = mod_ref[0, 0]
    lng = lng_ref[...]
    lnb = lnb_ref[...]
    ff = w1_ref.shape[1]
    step = MLP_CHUNK
    sub = ROW_BLK
    nsub = x_ref.shape[1] // sub

    def rows_of(ref, slabs, t):
        if slabs:
            tiles = range(t * sub // LANES, (t + 1) * sub // LANES)
            return jnp.concatenate([ref[0, s].T for s in tiles], axis=0).astype(BF16)
        return ref[0, t * sub:(t + 1) * sub, :]

    def head(t):
        rs = slice(t * sub, (t + 1) * sub)
        y = _dot(rows_of(ya_ref, ya_slabs, t), wa_ref[...]) + _dot(rows_of(yb_ref, yb_slabs, t), wb_ref[...])
        x1 = _norm_rows(ALPHA * x_ref[0, rs, :] + m[2:3] * y) * lng[0:1] + lnb[0:1]
        return x1, (_norm_rows(x1) * (1.0 + m[4:5]) + m[3:4]).astype(BF16)

    def mlp(h):
        acc = None
        for c in range(ff // step):
            a = jnp.maximum(_dot(h, w1_ref[:, c * step:(c + 1) * step]), 0.0)
            d = _dot((a * a).astype(BF16), w2_ref[c * step:(c + 1) * step, :])
            acc = d if acc is None else acc + d
        return acc

    cur = head(0)
    for t in range(nsub):
        nxt = head(t + 1) if t + 1 < nsub else None
        acc = mlp(cur[1])
        o_ref[0, t * sub:(t + 1) * sub, :] = _norm_rows(ALPHA * cur[0] + m[5:6] * acc) * lng[1:2] + lnb[1:2]
        cur = nxt


def _post(x, modtab, mod_row, ya, yb, wa, wb, lng, lnb, w1, w2, ya_slabs=False, yb_slabs=False):
    b, r, d = x.shape
    ka, kb = wa.shape[0], wb.shape[0]
    rows = min(r, 4 * ROW_BLK)
    row = lambda k: pl.BlockSpec((1, rows, k), lambda i, j: (i, j, 0))
    slab = lambda k: pl.BlockSpec((1, rows // LANES, k, LANES), lambda i, j: (i, j, 0, 0))
    return pl.pallas_call(
        functools.partial(_post_kernel, ya_slabs=ya_slabs, yb_slabs=yb_slabs),
        grid=(b, r // rows),
        in_specs=[row(d), pl.BlockSpec((1, 1, 6, d), lambda i, j: (i, mod_row, 0, 0)),
                  slab(ka) if ya_slabs else row(ka), slab(kb) if yb_slabs else row(kb),
                  _resident(wa.shape), _resident(wb.shape), _resident(lng.shape), _resident(lnb.shape),
                  _resident(w1.shape), _resident(w2.shape)],
        out_specs=row(d),
        out_shape=jax.ShapeDtypeStruct((b, r, d), F32),
        compiler_params=_cparams(("parallel", "parallel")),
        name="post_mixer",
    )(x, modtab, ya, yb, wa, wb, lng, lnb, w1, w2)


def _rope(x, cos, sinl, sinr):
    reps = x.shape[1] // LANES
    tile = lambda t: jnp.concatenate([t] * reps, axis=1)
    n = x.shape[1]
    quarter = MLA_ROPE // 4
    return (x * tile(cos) + pltpu.roll(x, n - quarter, axis=1) * tile(sinl)
            + pltpu.roll(x, quarter, axis=1) * tile(sinr))


def _front_cd_kernel(x_ref, xc_ref, mod_ref, w_ref, qn_ref, kvn_ref, wuq_ref, wuk_ref, wuv_ref, epe_ref, one_ref,
                     fng_ref, fnb_ref, avg_ref, cbd_ref, sbd_ref,
                     cos_ref, sl_ref, sr_ref,
                     q_ref, k_ref, v_ref, p_ref, qf_ref, *, ctx_blk):
    m = mod_ref[0, 0]
    is_ctx = (jnp.zeros((ROW_BLK, 1), jnp.int32) + pl.program_id(1)) == ctx_blk
    x = jnp.where(is_ctx, xc_ref[0], x_ref[0])
    h = _norm_rows(x) * (1.0 + m[1:2]) + m[0:1]
    u = _dot(h.astype(BF16), w_ref[...])
    o_kv = MLA_Q_RANK
    o_fn = o_kv + MLA_KV_RANK
    o_pe = o_fn + FN_CH

    def rms(x, g):
        return x * lax.rsqrt(jnp.mean(x * x, axis=-1, keepdims=True) + LN_EPS) * g

    cq = rms(u[:, :o_kv], qn_ref[...]).astype(BF16)
    q = _dot(cq, wuq_ref[...])
    q_ref[0] = _rope(q, cos_ref[...], sl_ref[...], sr_ref[...]).astype(BF16)

    ckv = rms(u[:, o_kv:o_fn], kvn_ref[...]).astype(BF16)
    kpe = _dot(u[:, o_pe:].astype(BF16), epe_ref[...])
    k = _dot(ckv, wuk_ref[...]) + _rope(kpe, cos_ref[...], sl_ref[...], sr_ref[...])
    k_ref[0] = k.astype(BF16)
    v_ref[0] = (_dot(ckv, wuv_ref[...]) + one_ref[...]).astype(BF16)

    uf = u[:, o_fn:o_pe]
    avg = avg_ref[...]
    uc = uf - _dot(uf.astype(BF16), avg)
    var = _dot((uc * uc).astype(BF16), avg)
    ug = (uc * lax.rsqrt(var + LN_EPS) * fng_ref[...] + fnb_ref[...]).astype(BF16)
    p_ref[0] = _dot_nt(cbd_ref[...], ug).astype(BF16)
    qf_ref[0] = _dot_nt(sbd_ref[...], ug).astype(BF16)


def _rope_tables(length, lt):
    t = jnp.arange(lt, dtype=jnp.int32)
    rows = (t // GRID_W).astype(F32)
    cols = (t % GRID_W).astype(F32)
    half = MLA_ROPE // 2
    inv = ROPE_THETA ** (-jnp.arange(0, half, 2, dtype=F32) / half)
    ar = rows[:, None] * inv[None, :]
    ac = cols[:, None] * inv[None, :]
    ang = jnp.concatenate([ar, ar, ac, ac], -1)
    is_lat = (t < length)[:, None]
    cos = jnp.where(is_lat, jnp.cos(ang), 1.0)
    sin = jnp.where(is_lat, jnp.sin(ang), 0.0)
    qd = MLA_ROPE // 4
    ones = jnp.ones((lt, MLA_NOPE), F32)
    zeros = jnp.zeros((lt, MLA_NOPE), F32)
    tail1 = jnp.ones((lt, HEAD_PAD - MLA_NOPE - MLA_ROPE), F32)
    tail0 = jnp.zeros((lt, HEAD_PAD - MLA_NOPE - MLA_ROPE), F32)
    z8 = jnp.zeros((lt, qd), F32)
    c = jnp.concatenate([ones, cos, tail1], -1)
    sl = jnp.concatenate([zeros, -sin[:, :qd], z8, -sin[:, 2 * qd:3 * qd], z8, tail0], -1)
    sr = jnp.concatenate([zeros, z8, sin[:, qd:2 * qd], z8, sin[:, 3 * qd:], tail0], -1)
    return c, sl, sr


def _head_slots(w, per_head, take_from, take_n):
    k = w.shape[0]
    w3 = w.reshape(k, MLA_HEADS, per_head)[:, :, take_from:take_from + take_n]
    w3 = jnp.pad(w3, ((0, 0), (0, 0), (0, HEAD_PAD - take_n)))
    return w3.reshape(k, MLA_HEADS * HEAD_PAD)


def _front_cd(xl, xc, modtab, w_in, q_norm, w_uq, kv_norm, w_ukv, fn_g, fn_b):
    b, length, d = xl.shape
    lt = length + xc.shape[1]
    nlat = length // ROW_BLK
    o_kv = MLA_Q_RANK
    o_pe = o_kv + MLA_KV_RANK
    o_fn = o_pe + MLA_ROPE
    hw = MLA_HEADS * HEAD_PAD
    w_perm = jnp.concatenate([w_in[:, :o_pe], w_in[:, o_fn:], w_in[:, o_pe:o_fn],
                              jnp.zeros((d, LANES - MLA_ROPE), w_in.dtype)], -1).astype(BF16)
    q_scale = (MLA_NOPE + MLA_ROPE) ** -0.5 * math.log2(math.e)
    wuq = _head_slots(w_uq * q_scale, MLA_NOPE + MLA_ROPE, 0, MLA_NOPE + MLA_ROPE).astype(BF16)
    wuk = _head_slots(w_ukv, MLA_NOPE + MLA_V, 0, MLA_NOPE).astype(BF16)
    wuv = _head_slots(w_ukv, MLA_NOPE + MLA_V, MLA_NOPE, MLA_V).astype(BF16)
    epe = np.zeros((LANES, hw), np.float32)
    for hd in range(MLA_HEADS):
        for i in range(MLA_ROPE):
            epe[i, hd * HEAD_PAD + MLA_NOPE + i] = 1.0
    epe = _mxu_const(epe)
    ones_col = np.zeros((1, hw), np.float32)
    ones_col[0, MLA_V::HEAD_PAD] = 1.0
    ones_col = jnp.asarray(ones_col)
    cm, sm = _cs(1, FN_GD, FN_GD, FN_GD)
    eye = np.eye(FN_GROUPS)
    cbd = _mxu_const(np.kron(eye, cm))
    sbd = _mxu_const(np.kron(eye, -sm))
    avg = _mxu_const(np.kron(eye, np.full((FN_GD, FN_GD), 1.0 / FN_GD)))
    rtab = _rope_tables(length, lt)
    row = lambda n: pl.BlockSpec((1, ROW_BLK, n), lambda i, j: (i, j, 0))
    tab = pl.BlockSpec((ROW_BLK, HEAD_PAD), lambda i, j: (j, 0))
    out = lambda n: jax.ShapeDtypeStruct((b, lt, n), BF16)
    fn_t = pl.BlockSpec((1, FN_CH, ROW_BLK), lambda i, j: (i, 0, j))
    fn_out = jax.ShapeDtypeStruct((b, FN_CH, lt), BF16)
    return pl.pallas_call(
        functools.partial(_front_cd_kernel, ctx_blk=nlat),
        grid=(b, lt // ROW_BLK),
        in_specs=[pl.BlockSpec((1, ROW_BLK, d), lambda i, j: (i, jnp.minimum(j, nlat - 1), 0)),
                  pl.BlockSpec((1, ROW_BLK, d), lambda i, j: (i, 0, 0)),
                  pl.BlockSpec((1, 1, 6, d), lambda i, j: (i, j // nlat, 0, 0)),
                  _resident(w_perm.shape), _resident((1, MLA_Q_RANK)), _resident((1, MLA_KV_RANK)),
                  _resident(wuq.shape), _resident(wuk.shape), _resident(wuv.shape), _resident(epe.shape),
                  _resident(ones_col.shape),
                  _resident((1, FN_CH)), _resident((1, FN_CH)), _resident(avg.shape), _resident(cbd.shape),
                  _resident(sbd.shape),
                  tab, tab, tab],
        out_specs=[row(hw), row(hw), row(hw), fn_t, fn_t],
        out_shape=[out(hw), out(hw), out(hw), fn_out, fn_out],
        compiler_params=_cparams(("parallel", "parallel")),
        name="front_cd",
    )(xl, xc, modtab, w_perm, q_norm.reshape(1, -1), kv_norm.reshape(1, -1), wuq, wuk, wuv, epe, ones_col,
      fn_g.reshape(1, -1), fn_b.reshape(1, -1), avg, cbd, sbd, *rtab)


def _mla_kernel(q_ref, k_ref, v_ref, o_ref, *, rows):
    n = q_ref.shape[1] // rows

    def scores(i):
        s = _dot_nt(q_ref[0, i * rows:(i + 1) * rows, :], k_ref[0])
        return s, s.max(axis=-1, keepdims=True)

    def finish(i, s, m):
        acc = _dot(jnp.exp2(s - m).astype(BF16), v_ref[0])
        o_ref[0, i * rows:(i + 1) * rows, :] = (acc / acc[:, MLA_V:MLA_V + 1]).astype(o_ref.dtype)

    pending = scores(0)
    for i in range(n):
        nxt = scores(i + 1) if i + 1 < n else None
        finish(i, *pending)
        pending = nxt


def _mla_attention(q, k, v, length):
    b, lt, hw = q.shape
    heads = hw // HEAD_PAD
    tq = MLA_Q_BLK
    kv = pl.BlockSpec((1, lt, HEAD_PAD), lambda i, h, j: (i, 0, h))
    qs = pl.BlockSpec((1, tq, HEAD_PAD), lambda i, h, j: (i, j, h))
    return pl.pallas_call(
        functools.partial(_mla_kernel, rows=MLA_Q_SUB),
        grid=(b, heads, length // tq),
        in_specs=[qs, kv, kv],
        out_specs=qs,
        out_shape=jax.ShapeDtypeStruct((b, length, hw), BF16),
        compiler_params=_cparams(("parallel", "parallel", "parallel")),
        name="mla_attention",
    )(q, k, v)


def _fnet_kernel(pa_ref, pb_ref, qa_ref, qb_ref, m1_ref, c_ref, s_ref, m2_ref, o_ref, zr_scr, zi_scr):
    n1 = FN_N1
    n2 = zr_scr.shape[2] // 2
    pairs = zr_scr.shape[0]

    def fill(scr, a_ref, b_ref):
        za = a_ref[0].astype(F32)
        zb = b_ref[0].astype(F32)
        for r in range(n1):
            scr[:, r, :] = jnp.concatenate([za[:, r * n2:(r + 1) * n2], zb[:, r * n2:(r + 1) * n2]], axis=1)

    fill(zr_scr, pa_ref, pb_ref)
    fill(zi_scr, qa_ref, qb_ref)
    c = c_ref[...]
    s = s_ref[...]
    for g0 in range(0, pairs, FN_GROUP):
        x = jnp.concatenate([jnp.concatenate([zr_scr[g0 + g], zi_scr[g0 + g]], axis=0) for g in range(FN_GROUP)],
                            axis=1).astype(BF16)
        a = _dot(m1_ref[...], x)
        ts = []
        for g in range(FN_GROUP):
            ar = a[:n1, g * LANES:(g + 1) * LANES]
            ai = a[n1:, g * LANES:(g + 1) * LANES]
            ts.append(jnp.concatenate([ar * c + ai * s, ai * c - ar * s], axis=1))
        y = _dot(jnp.concatenate(ts, axis=0).astype(BF16), m2_ref[...])
        for g in range(FN_GROUP):
            yt = y[g * n1:(g + 1) * n1].T
            o_ref[0, :, 2 * (g0 + g), :] = yt[:n2]
            o_ref[0, :, 2 * (g0 + g) + 1, :] = yt[n2:]


def _fnet(pt, qt, length):
    b, ch, _ = pt.shape
    n1 = FN_N1
    n2 = length // n1
    assert 2 * n2 == LANES
    half = ch // 2
    cm, sm = _cs(1, n1, n1, n1)
    m1 = _mxu_const(np.block([[cm, sm], [-sm, cm]]))
    c2, s2 = _cs(1, n2, n2, n2)
    z = np.zeros_like(c2)
    m2 = np.block([[c2, z], [z, c2], [s2, z], [z, s2]]) / math.sqrt(length * FN_GD)
    m2 = _mxu_const(m2)
    k1 = lax.broadcasted_iota(jnp.int32, (n1, LANES), 0)
    m = lax.broadcasted_iota(jnp.int32, (n1, LANES), 1) % n2
    ang = (k1 * m).astype(F32) * (2.0 * math.pi / length)
    twc, tws = jnp.cos(ang), jnp.sin(ang)
    pairs = FN_PAIRS
    nblk = half // pairs
    spec = lambda off: pl.BlockSpec((1, pairs, length), lambda j, i: (i, off * nblk + j, 0))
    return pl.pallas_call(
        _fnet_kernel,
        grid=(nblk, b),
        in_specs=[spec(0), spec(1), spec(0), spec(1), _resident(m1.shape), _resident((n1, LANES)),
                  _resident((n1, LANES)), _resident(m2.shape)],
        out_specs=pl.BlockSpec((1, n2, 2 * pairs, LANES), lambda j, i: (i, 0, j, 0)),
        out_shape=jax.ShapeDtypeStruct((b, n2, ch, LANES), F32),
        scratch_shapes=[pltpu.VMEM((pairs, n1, LANES), F32), pltpu.VMEM((pairs, n1, LANES), F32)],
        compiler_params=_cparams(("parallel", "parallel")),
        name="fnet",
    )(pt, pt, qt, qt, m1, twc, tws, m2)


def kernel(x, c, ctx, c_ctx, mod_w, mod_b, ln_g, ln_b, mlp_w1, mlp_w2,
           ab_w_in, ab_w_out, hy_conv_w, hy_w1, hy_b1, hy_freq, hy_w2, hy_b2, hy_w3, hy_log_decay, hy_skip, na_rpb,
           cd_w_in, cd_w_out, mla_q_norm, mla_w_uq, mla_kv_norm, mla_w_ukv, fn_norm_g, fn_norm_b):
    b, length, d = x.shape
    lc = ctx.shape[1]

    cc = jnp.concatenate([c, c_ctx[None], jnp.zeros((8 - b - 1, d), F32)], 0)
    mods = _mod_vectors(cc, mod_w, mod_b).reshape(DEPTH, 8, 6, d)
    modtab = [jnp.stack([mods[l, :b], jnp.broadcast_to(mods[l, b], (b, 6, d))], axis=1) for l in range(DEPTH)]

    n_hy = 3 * HY_CH
    w_in = ab_w_in[0].astype(BF16)
    uq, ut = _front_ab_lat(x, modtab[0], w_in[:, n_hy:], w_in[:, :n_hy].T)
    uc = _front_ab_ctx(ctx, modtab[0], w_in)
    fargs = (hy_w1[0], hy_b1[0], hy_freq[0], hy_w2[0], hy_b2[0], hy_w3[0], hy_log_decay[0])
    mats = _dft_mats()
    twc, tws = _twiddle2d(FFT_N2)
    spec = _filter_spec_t(_hy_filters_t(length, *fargs, hy_skip[0]), mats[1], mats[2], twc, tws)
    y_hy_t = _hyena_core(ut, hy_conv_w[0], spec, mats, twc, tws)
    x1c, x2c, vc = _hy_prep(uc, hy_conv_w[0], 0, lc // ROW_BLK)
    y_hy_c = _hy_ctx(vc, x1c, x2c, _bidir_taps(_hy_filters(lc, *fargs), hy_skip[0], lc))
    y_na = _natten(uq, uc, _natten_bias(na_rpb[0]))
    y_na_c = _ctx_attn(uc)
    w_out = ab_w_out[0].astype(BF16)
    mlp = (ln_g[0], ln_b[0], mlp_w1[0].astype(BF16), mlp_w2[0].astype(BF16))
    xl = _post(x, modtab[0], 0, y_hy_t, y_na, w_out[:HY_CH], w_out[HY_CH:], *mlp, ya_slabs=True)
    xc = _post(ctx, modtab[0], 1, y_hy_c, y_na_c, w_out[:HY_CH], w_out[HY_CH:], *mlp)

    q, k, vv, pt, qt = _front_cd(xl, xc, modtab[1], cd_w_in[0], mla_q_norm[0], mla_w_uq[0], mla_kv_norm[0],
                                 mla_w_ukv[0], fn_norm_g[0], fn_norm_b[0])
    o = _mla_attention(q, k, vv, length)
    y_fn = _fnet(pt, qt, length)
    w_out = cd_w_out[0]
    n_mla = MLA_HEADS * MLA_V
    wa = jnp.pad(w_out[:n_mla].reshape(MLA_HEADS, MLA_V, d), ((0, 0), (0, HEAD_PAD - MLA_V), (0, 0)))
    wa = wa.reshape(MLA_HEADS * HEAD_PAD, d).astype(BF16)
    wb = w_out[n_mla:].reshape(2, FN_CH // 2, d).transpose(1, 0, 2).reshape(FN_CH, d)
    return _post(xl, modtab[1], 0, o, y_fn, wa, wb.astype(BF16),
                 ln_g[1], ln_b[1], mlp_w1[1].astype(BF16), mlp_w2[1].astype(BF16), yb_slabs=True)
```

```python
import functools
import math

import numpy as np
import jax
import jax.numpy as jnp
from jax import lax
from jax.experimental import pallas as pl
from jax.experimental.pallas import tpu as pltpu

F32 = jnp.float32
BF16 = jnp.bfloat16

D_MODEL = 1024
DEPTH = 2
GRID_W = 64
HY_CH = 512
HY_EMB = 33
HY_BANDS = (HY_EMB - 1) // 2
NA_HEADS = 8
NA_HD = 64
NA_WIN_R = 8
NA_WIN_C = 16
MLA_HEADS = 8
MLA_Q_RANK = 384
MLA_KV_RANK = 256
MLA_NOPE = 64
MLA_ROPE = 32
MLA_V = 96
ROPE_THETA = 10000.0
FN_CH = 256
FN_GROUPS = 4
FN_GD = FN_CH // FN_GROUPS
D_FF = 4 * D_MODEL
ALPHA = (2.0 * DEPTH) ** 0.25
LN_EPS = 1e-5

LANES = 128
ROW_BLK = 256
HEAD_PAD = 128
FFT_N2 = 128
NA_ROWS_PER_TRIP = 8
FN_N1 = 128
FN_PAIRS = 16
FN_GROUP = 4
MLP_CHUNK = 1024
MLA_Q_BLK = 1024
MLA_Q_SUB = 256
HY_GROUP = 16
HY_CHAINS = 1
HY_CBLK = HY_GROUP * HY_CHAINS
VMEM_LIMIT = 56 * 1024 * 1024
NEG_BIG = -1e30


def _cparams(sem, vmem=VMEM_LIMIT):
    return pltpu.CompilerParams(dimension_semantics=sem, vmem_limit_bytes=vmem)


def _resident(shape):
    nd = len(shape)
    return pl.BlockSpec(shape, lambda *_: (0,) * nd, pipeline_mode=pl.Buffered(1))


def _norm_rows(x):
    mu = jnp.mean(x, axis=-1, keepdims=True)
    xc = x - mu
    var = jnp.mean(xc * xc, axis=-1, keepdims=True)
    return xc * lax.rsqrt(var + LN_EPS)


def _dot(a, b):
    return jnp.dot(a, b, preferred_element_type=F32)


def _mxu_const(a):
    return jnp.asarray(a, dtype=F32).astype(BF16)


def _dot_split(a, b):
    ah = a.astype(BF16)
    bh = b.astype(BF16)
    al = (a - ah.astype(F32)).astype(BF16)
    bl = (b - bh.astype(F32)).astype(BF16)
    return _dot(ah, bh) + _dot(ah, bl) + _dot(al, bh)


def _dot_nt(a, b):
    return lax.dot_general(a, b, (((1,), (1,)), ((), ())), preferred_element_type=F32)


def _mod_kernel(c_ref, w_ref, b_ref, o_ref):
    c = c_ref[...]
    s = c * (1.0 / (1.0 + jnp.exp(-c)))
    o_ref[0] = jnp.dot(s, w_ref[0], preferred_element_type=F32,
                       precision=lax.Precision.HIGHEST) + b_ref[0]


def _mod_vectors(cc, mod_w, mod_b):
    depth, d, n = mod_w.shape
    nb = 1024
    return pl.pallas_call(
        _mod_kernel,
        grid=(depth, n // nb),
        in_specs=[pl.BlockSpec((8, d), lambda l, j: (0, 0)),
                  pl.BlockSpec((1, d, nb), lambda l, j: (l, 0, j)),
                  pl.BlockSpec((1, 1, nb), lambda l, j: (l, 0, j))],
        out_specs=pl.BlockSpec((1, 8, nb), lambda l, j: (l, 0, j)),
        out_shape=jax.ShapeDtypeStruct((depth, 8, n), F32),
        compiler_params=_cparams(("parallel", "parallel")),
        name="mod_vectors",
    )(cc, mod_w, mod_b.reshape(depth, 1, n))


def _front_ab_ctx_kernel(x_ref, mod_ref, w_ref, u_ref):
    m = mod_ref[0, 0]
    h = _norm_rows(x_ref[0]) * (1.0 + m[1:2]) + m[0:1]
    u_ref[0] = _dot(h.astype(BF16), w_ref[...]).astype(BF16)


def _front_ab_ctx(xc, modtab, w_in):
    b, lc, d = xc.shape
    n = w_in.shape[1]
    return pl.pallas_call(
        _front_ab_ctx_kernel,
        grid=(b, lc // ROW_BLK),
        in_specs=[pl.BlockSpec((1, ROW_BLK, d), lambda i, j: (i, j, 0)),
                  pl.BlockSpec((1, 1, 6, d), lambda i, j: (i, 1, 0, 0)),
                  _resident((d, n))],
        out_specs=pl.BlockSpec((1, ROW_BLK, n), lambda i, j: (i, j, 0)),
        out_shape=jax.ShapeDtypeStruct((b, lc, n), BF16),
        compiler_params=_cparams(("parallel", "parallel")),
        name="front_ab_ctx",
    )(xc, modtab, w_in)


def _front_ab_lat_kernel(x_ref, mod_ref, wq_ref, wht_ref, u_ref, ut_ref):
    m = mod_ref[0, 0]
    hs = []
    for t in range(x_ref.shape[1] // ROW_BLK):
        rs = slice(t * ROW_BLK, (t + 1) * ROW_BLK)
        hs.append((_norm_rows(x_ref[0, rs, :]) * (1.0 + m[1:2]) + m[0:1]).astype(BF16))
        u_ref[0, rs, :] = _dot(hs[-1], wq_ref[...]).astype(BF16)
    h_all = jnp.concatenate(hs, axis=0)
    for c0 in range(0, wht_ref.shape[0], HY_CH):
        ut = _dot_nt(wht_ref[c0:c0 + HY_CH, :], h_all)
        for s in range(ut_ref.shape[2]):
            ut_ref[0, c0:c0 + HY_CH, s, :] = ut[:, s * FFT_N2:(s + 1) * FFT_N2]


def _front_ab_lat(x, modtab, w_qkv, w_hy_t):
    b, length, d = x.shape
    nq = w_qkv.shape[1]
    nh = w_hy_t.shape[0]
    rows = 8 * FFT_N2
    return pl.pallas_call(
        _front_ab_lat_kernel,
        grid=(b, length // rows),
        in_specs=[pl.BlockSpec((1, rows, d), lambda i, j: (i, j, 0)),
                  pl.BlockSpec((1, 1, 6, d), lambda i, j: (i, 0, 0, 0)),
                  _resident((d, nq)), _resident((nh, d))],
        out_specs=[pl.BlockSpec((1, rows, nq), lambda i, j: (i, j, 0)),
                   pl.BlockSpec((1, nh, 8, FFT_N2), lambda i, j: (i, 0, j, 0))],
        out_shape=[jax.ShapeDtypeStruct((b, length, nq), BF16),
                   jax.ShapeDtypeStruct((b, nh, length // FFT_N2, FFT_N2), F32)],
        compiler_params=_cparams(("parallel", "parallel")),
        name="front_ab_lat",
    )(x, modtab, w_qkv, w_hy_t)


def _hy_prep_kernel(cur_ref, prev_ref, next_ref, w_ref, x1_ref, x2_ref, v_ref, *, nblk):
    j = pl.program_id(1)
    cur = cur_ref[0].astype(F32)
    rows = cur.shape[0]
    has_prev = (j > 0).astype(F32)
    has_next = (j < nblk - 1).astype(F32)
    prev_row = prev_ref[0][7:8].astype(F32) * has_prev
    next_row = next_ref[0][0:1].astype(F32) * has_next
    rid = lax.broadcasted_iota(jnp.int32, (rows, 1), 0)
    up = jnp.where(rid == 0, prev_row, pltpu.roll(cur, 1, axis=0))
    dn = jnp.where(rid == rows - 1, next_row, pltpu.roll(cur, rows - 1, axis=0))
    w = w_ref[...]
    y = up * w[0:1] + cur * w[1:2] + dn * w[2:3]
    c = HY_CH
    x1_ref[0] = y[:, :c].astype(BF16)
    x2_ref[0] = y[:, c:2 * c].astype(BF16)
    v_ref[0] = y[:, 2 * c:].astype(BF16)


def _hy_prep(u, conv_w, blk0, nblk):
    b, lt, _ = u.shape
    n = 3 * HY_CH
    sub = ROW_BLK // 8
    last8 = lt // 8 - 1
    out = jax.ShapeDtypeStruct((b, nblk * ROW_BLK, HY_CH), BF16)
    ospec = pl.BlockSpec((1, ROW_BLK, HY_CH), lambda i, j: (i, j, 0))
    return pl.pallas_call(
        functools.partial(_hy_prep_kernel, nblk=nblk),
        grid=(b, nblk),
        in_specs=[pl.BlockSpec((1, ROW_BLK, n), lambda i, j: (i, blk0 + j, 0)),
                  pl.BlockSpec((1, 8, n), lambda i, j: (i, jnp.maximum((blk0 + j) * sub - 1, 0), 0)),
                  pl.BlockSpec((1, 8, n), lambda i, j: (i, jnp.minimum((blk0 + j + 1) * sub, last8), 0)),
                  _resident((3, n))],
        out_specs=[ospec, ospec, ospec],
        out_shape=[out, out, out],
        compiler_params=_cparams(("parallel", "parallel")),
        name="hy_prep",
    )(u, u, u, conv_w)


def _hy_filt_kernel(z_ref, w1_ref, b1_ref, fr_ref, w2_ref, b2_ref, w3_ref, ld_ref, o_ref):
    hi = lax.Precision.HIGHEST
    z = z_ref[...]
    fr = fr_ref[...]
    hid = jnp.sin(fr * (jnp.dot(z, w1_ref[...], preferred_element_type=F32, precision=hi) + b1_ref[...]))
    hid = jnp.sin(fr * (jnp.dot(hid, w2_ref[...], preferred_element_type=F32, precision=hi) + b2_ref[...]))
    h = jnp.dot(hid, w3_ref[...], preferred_element_type=F32, precision=hi)
    t = z[:, 0:1]
    o_ref[...] = h * jnp.exp(-t * jnp.exp(ld_ref[...]))


def _pad2(a, rows, cols):
    return jnp.pad(a, ((0, rows - a.shape[0]), (0, cols - a.shape[1])))


def _hy_filters(length, w1, b1, freq, w2, b2, w3, log_decay):
    pos = jnp.arange(length, dtype=F32)
    t = pos / max(length - 1, 1)
    w = 2.0 * math.pi * pos / length
    f = jnp.linspace(1e-4, HY_BANDS - 1, HY_BANDS, dtype=F32)
    ang = w[:, None] * f[None, :]
    z = jnp.concatenate([t[:, None], jnp.cos(ang), -jnp.sin(ang)], -1)
    z = _pad2(z, length, LANES)
    n = w3.shape[1]
    rb = min(length, 512)
    vec = lambda a: _pad2(a.reshape(1, -1), 1, LANES)
    return pl.pallas_call(
        _hy_filt_kernel,
        grid=(length // rb,),
        in_specs=[pl.BlockSpec((rb, LANES), lambda i: (i, 0)),
                  _resident((LANES, LANES)), _resident((1, LANES)), _resident((1, LANES)),
                  _resident((LANES, LANES)), _resident((1, LANES)),
                  _resident((LANES, n)), _resident((1, n))],
        out_specs=pl.BlockSpec((rb, n), lambda i: (i, 0)),
        out_shape=jax.ShapeDtypeStruct((length, n), F32),
        compiler_params=_cparams(("parallel",)),
        name="hy_filters",
    )(z, _pad2(w1, LANES, LANES), vec(b1), vec(freq), _pad2(w2, LANES, LANES), vec(b2),
      _pad2(w3, LANES, n), log_decay.reshape(1, n))


def _bidir_taps(h, skip, length):
    h4 = h.reshape(length, 2, 2, HY_CH)
    cols = []
    for o in range(2):
        hf = h4[:, o, 0].at[0].add(skip[o])
        hb = h4[:, o, 1]
        cols.append(jnp.concatenate([hf, jnp.zeros_like(hf[:1]), hb[:0:-1]], 0))
    return jnp.concatenate(cols, -1)


def _left_mm_kernel(m_ref, x_ref, o_ref):
    o_ref[0] = _dot(m_ref[...], x_ref[0]).astype(o_ref.dtype)


def _left_mm(mat, x, out_dtype, lane_blk):
    g, k, n = x.shape
    m = mat.shape[0]
    lane_blk = min(lane_blk, n)
    return pl.pallas_call(
        _left_mm_kernel, grid=(g, n // lane_blk),
        in_specs=[_resident((m, k)), pl.BlockSpec((1, k, lane_blk), lambda i, j: (i, 0, j))],
        out_specs=pl.BlockSpec((1, m, lane_blk), lambda i, j: (i, 0, j)),
        out_shape=jax.ShapeDtypeStruct((g, m, n), out_dtype),
        compiler_params=_cparams(("parallel", "parallel")),
        name="left_mm",
    )(mat, x)


def _cs(num, den, rows, cols):
    ang = 2.0 * np.pi * np.outer(np.arange(rows), np.arange(cols)) * (num / den)
    return np.cos(ang), np.sin(ang)


def _hy_filt_t_kernel(z_ref, msk_ref, w1_ref, b1_ref, fr_ref, w2_ref, b2_ref, w3_ref, ld_ref, sk_ref, o_ref):
    hi = lax.Precision.HIGHEST
    z = z_ref[...]
    fr = fr_ref[...]
    hid = jnp.sin(fr * (jnp.dot(w1_ref[...], z, preferred_element_type=F32, precision=hi) + b1_ref[...]))
    hid = jnp.sin(fr * (jnp.dot(w2_ref[...], hid, preferred_element_type=F32, precision=hi) + b2_ref[...]))
    h = _dot_split(w3_ref[0], hid)
    h = h * jnp.exp(-jnp.exp(ld_ref[0]) * z[0:1, :])
    msk = msk_ref[...]
    h = h * msk[0:1, :] + sk_ref[...] * msk[1:2, :]
    for s in range(o_ref.shape[1]):
        o_ref[:, s, :] = h[:, s * FFT_N2:(s + 1) * FFT_N2]


def _hy_filters_t(length, w1, b1, freq, w2, b2, w3, log_decay, skip):
    n = 2 * length
    tt = jnp.arange(n, dtype=jnp.int32)
    pos = jnp.where(tt < length, tt, n - tt).astype(F32)
    t = pos / max(length - 1, 1)
    w = 2.0 * math.pi * pos / length
    f = jnp.linspace(1e-4, HY_BANDS - 1, HY_BANDS, dtype=F32)
    ang = f[:, None] * w[None, :]
    z = jnp.concatenate([t[None, :], jnp.cos(ang), -jnp.sin(ang)], 0)
    nz = -(-z.shape[0] // 8) * 8
    z = jnp.pad(z, ((0, nz - z.shape[0]), (0, 0)))
    msk = jnp.stack([(tt != length).astype(F32), (tt == 0).astype(F32)])
    msk = jnp.pad(msk, ((0, 6), (0, 0)))
    col = lambda a: a.reshape(-1, 1)
    c2 = 2 * HY_CH
    nf = w3.shape[0]
    w3d = w3.reshape(nf, 2, 2, HY_CH).transpose(2, 1, 3, 0).reshape(2, c2, nf)
    ldd = log_decay.reshape(2, 2, HY_CH).transpose(1, 0, 2).reshape(2, c2, 1)
    rows = 8
    pb = rows * FFT_N2
    half = length // pb
    return pl.pallas_call(
        _hy_filt_t_kernel,
        grid=(n // pb,),
        in_specs=[pl.BlockSpec((nz, pb), lambda i: (0, i)),
                  pl.BlockSpec((8, pb), lambda i: (0, i)),
                  _resident((nf, nz)), _resident((nf, 1)), _resident((nf, 1)),
                  _resident((nf, nf)), _resident((nf, 1)),
                  pl.BlockSpec((1, c2, nf), lambda i: (i // half, 0, 0)),
                  pl.BlockSpec((1, c2, 1), lambda i: (i // half, 0, 0)),
                  _resident((c2, 1))],
        out_specs=pl.BlockSpec((c2, rows, FFT_N2), lambda i: (0, i, 0)),
        out_shape=jax.ShapeDtypeStruct((c2, n // FFT_N2, FFT_N2), F32),
        compiler_params=_cparams(("parallel",)),
        name="hy_filters_t",
    )(z, msk, _pad2(w1.T, nf, nz), col(b1), col(freq), w2.T, col(b2), w3d, ldd, skip.reshape(c2, 1))


def _dft_mats():
    n = FFT_N2
    c, s = _cs(1, n, n, n)
    ch, sh = c[:, :n // 2], s[:, :n // 2]
    f1_pair = np.block([[ch, sh], [-sh, ch]])
    f1_full = np.concatenate([c, -s], 0)
    m2r = np.block([[c, -s], [s, c]])
    m2i = np.block([[c, s], [-s, c]])
    f1_inv = np.block([[ch.T, -sh.T], [sh.T, ch.T]]) / (n * n)
    return tuple(_mxu_const(a) for a in (f1_pair, f1_full, m2r, m2i, f1_inv))


def _twiddle2d(n):
    k1 = lax.broadcasted_iota(jnp.int32, (n, n), 0)
    m2 = lax.broadcasted_iota(jnp.int32, (n, n), 1)
    ang = (k1 * m2).astype(F32) * (2.0 * math.pi / (n * n))
    return jnp.cos(ang), jnp.sin(ang)


def _fwd_spectrum(xs, f1, m2r, c, s):
    n = FFT_N2
    a = _dot(f1, jnp.concatenate(xs, axis=1))
    ts = []
    for g in range(len(xs)):
        ar = a[:n, g * n:(g + 1) * n]
        ai = a[n:, g * n:(g + 1) * n]
        ts.append(jnp.concatenate([ar * c + ai * s, ai * c - ar * s], axis=1))
    return _dot(jnp.concatenate(ts, axis=0).astype(BF16), m2r)


def _filter_spec_t_kernel(t_ref, f1_ref, m2r_ref, c_ref, s_ref, o_ref):
    g = t_ref.shape[0]
    xs = [t_ref[i].astype(BF16) for i in range(g)]
    spec = _fwd_spectrum(xs, f1_ref[...], m2r_ref[...], c_ref[...], s_ref[...])
    for i in range(g):
        o_ref[i] = spec[i * FFT_N2:(i + 1) * FFT_N2]


def _filter_spec_t(taps, f1_full, m2r, twc, tws):
    nch, n1, n = taps.shape
    g = HY_GROUP
    return pl.pallas_call(
        _filter_spec_t_kernel,
        grid=(nch // g,),
        in_specs=[pl.BlockSpec((g, n1, n), lambda i: (i, 0, 0)),
                  _resident(f1_full.shape), _resident(m2r.shape), _resident((n, n)), _resident((n, n))],
        out_specs=pl.BlockSpec((g, n, 2 * n), lambda i: (i, 0, 0)),
        out_shape=jax.ShapeDtypeStruct((nch, n, 2 * n), F32),
        compiler_params=_cparams(("parallel",)),
        name="filter_spec_t",
    )(taps, f1_full, m2r, twc, tws)


def _hyena_core_kernel(x1_ref, x2_ref, v_ref, w1_ref, w2_ref, wv_ref, h0_ref, h1_ref,
                       f1_ref, m2r_ref, m2i_ref, f1i_ref, c_ref, s_ref, o_ref):
    n = FFT_N2
    n1 = v_ref.shape[2]
    c = c_ref[...]
    s = s_ref[...]
    lane = lax.broadcasted_iota(jnp.int32, (n1, n), 1)
    row = lax.broadcasted_iota(jnp.int32, (n1, n), 0)
    first_lane, last_lane = lane == 0, lane == n - 1
    seq_start, seq_end = first_lane & (row == 0), last_lane & (row == n1 - 1)

    def short_conv(x, w):
        r = pltpu.roll(x, 1, axis=1)
        up = jnp.where(first_lane, pltpu.roll(r, 1, axis=0), r)
        up = jnp.where(seq_start, 0.0, up)
        l = pltpu.roll(x, n - 1, axis=1)
        dn = jnp.where(last_lane, pltpu.roll(l, n1 - 1, axis=0), l)
        dn = jnp.where(seq_end, 0.0, dn)
        return up * w[0:1] + x * w[1:2] + dn * w[2:3]

    def conv_all(xss, h_ref):
        spec_s = [_fwd_spectrum(xs, f1_ref[...], m2r_ref[...], c, s) for xs in xss]
        y_s = []
        for chain, spec in enumerate(spec_s):
            ys = []
            for g in range(HY_GROUP):
                xr = spec[g * n:(g + 1) * n, :n]
                xi = spec[g * n:(g + 1) * n, n:]
                hh = h_ref[chain * HY_GROUP + g]
                hr, hi = hh[:, :n], hh[:, n:]
                ys.append(jnp.concatenate([xr * hr - xi * hi, xr * hi + xi * hr], axis=1))
            y_s.append(jnp.concatenate(ys, axis=0).astype(BF16))
        bm_s = [_dot(y, m2i_ref[...]) for y in y_s]
        b_s = []
        for bm in bm_s:
            bs = []
            for g in range(HY_GROUP):
                br = bm[g * n:(g + 1) * n, :n]
                bi = bm[g * n:(g + 1) * n, n:]
                bs.append(jnp.concatenate([br * c - bi * s, bi * c + br * s], axis=0))
            b_s.append(jnp.concatenate(bs, axis=1).astype(BF16))
        outs = [_dot(f1i_ref[...], bc) for bc in b_s]
        return [[y[:, g * n:(g + 1) * n] for g in range(HY_GROUP)] for y in outs]

    def pair(ref, w_ref, ch):
        return jnp.concatenate([short_conv(ref[0, ch], w_ref[ch]), short_conv(ref[1, ch], w_ref[ch])], axis=0)

    chans = [[chain * HY_GROUP + g for g in range(HY_GROUP)] for chain in range(HY_CHAINS)]
    y1 = conv_all([[pair(v_ref, wv_ref, ch).astype(BF16) for ch in grp] for grp in chans], h0_ref)
    z = [[(y1[k][g] * pair(x1_ref, w1_ref, ch)).astype(BF16) for g, ch in enumerate(grp)]
         for k, grp in enumerate(chans)]
    y2 = conv_all(z, h1_ref)
    for k, grp in enumerate(chans):
        for g, ch in enumerate(grp):
            y = y2[k][g] * pair(x2_ref, w2_ref, ch)
            o_ref[0, :, ch, :] = y[:n1]
            o_ref[1, :, ch, :] = y[n1:]


def _hyena_core(ut, conv_w, spec, mats, twc, tws):
    f1_pair, _, m2r, m2i, f1_inv = mats
    b, nch, n1, n = ut.shape
    assert b % 2 == 0
    cb = HY_CBLK
    nblk = HY_CH // cb
    wt = jnp.broadcast_to(conv_w.T[:, :, None], (nch, conv_w.shape[0], n))
    xspec = lambda off: pl.BlockSpec((2, cb, n1, n), lambda j, i: (i, off * nblk + j, 0, 0))
    wspec = lambda off: pl.BlockSpec((cb, conv_w.shape[0], n), lambda j, i: (off * nblk + j, 0, 0))
    hspec = lambda off: pl.BlockSpec((cb, n, 2 * n), lambda j, i: (off * nblk + j, 0, 0))
    return pl.pallas_call(
        _hyena_core_kernel,
        grid=(nblk, b // 2),
        in_specs=[xspec(0), xspec(1), xspec(2), wspec(0), wspec(1), wspec(2), hspec(0), hspec(1),
                  _resident(f1_pair.shape), _resident(m2r.shape), _resident(m2i.shape), _resident(f1_inv.shape),
                  _resident((n, n)), _resident((n, n))],
        out_specs=pl.BlockSpec((2, n1, cb, n), lambda j, i: (i, 0, j, 0)),
        out_shape=jax.ShapeDtypeStruct((b, n1, HY_CH, n), F32),
        compiler_params=_cparams(("parallel", "parallel")),
        name="hyena_core",
    )(ut, ut, ut, wt, wt, wt, spec, spec, f1_pair, m2r, m2i, f1_inv, twc, tws)


def _hy_ctx_kernel(v_ref, x1_ref, x2_ref, f_ref, fi_ref, h_ref, o_ref):
    nf = f_ref.shape[0] // 2
    zin = v_ref[0]
    gates = (x1_ref, x2_ref)
    for o in range(2):
        x = _dot(f_ref[...], zin)
        xr, xi = x[:nf], x[nf:]
        hr = h_ref[o, :nf]
        hi = h_ref[o, nf:]
        y = jnp.concatenate([xr * hr - xi * hi, xr * hi + xi * hr], axis=0).astype(BF16)
        zin = (_dot(fi_ref[...], y) * gates[o][0].astype(F32)).astype(BF16)
    o_ref[0] = zin


def _hy_ctx(v, x1, x2, taps):
    b, lc, c = v.shape
    nf = 2 * lc
    cm, sm = _cs(1, nf, nf, nf)
    fwd = _mxu_const(np.concatenate([cm[:, :lc], -sm[:, :lc]], 0))
    fwd_full = _mxu_const(np.concatenate([cm, -sm], 0))
    inv = _mxu_const(np.concatenate([cm[:lc, :], -sm[:lc, :]], 1) / nf)
    spec = _left_mm(fwd_full, taps.astype(BF16).reshape(1, nf, 2 * c), F32, 2 * c)
    spec = spec.reshape(2 * nf, 2, c).transpose(1, 0, 2)
    blk = pl.BlockSpec((1, lc, c), lambda i: (i, 0, 0))
    return pl.pallas_call(
        _hy_ctx_kernel,
        grid=(b,),
        in_specs=[blk, blk, blk, _resident((2 * nf, lc)), _resident((lc, 2 * nf)),
                  _resident((2, 2 * nf, c))],
        out_specs=blk,
        out_shape=jax.ShapeDtypeStruct((b, lc, c), BF16),
        compiler_params=_cparams(("parallel",)),
        name="hy_ctx",
    )(v, x1, x2, fwd, inv, spec)


def _pair_rows(q2):
    lane = lax.broadcasted_iota(jnp.int32, q2.shape, 1)
    zero = jnp.zeros_like(q2)
    return jnp.concatenate([jnp.where(lane < NA_HD, q2, zero), jnp.where(lane >= NA_HD, q2, zero)], axis=0)


def _unpair_rows(o):
    r = o.shape[0] // 2
    lane = lax.broadcasted_iota(jnp.int32, (r, o.shape[1]), 1)
    return jnp.where(lane < NA_HD, o[:r], o[r:])


def _pair_softmax_pv(scores, values):
    m = scores[0].max(axis=-1, keepdims=True)
    for s in scores[1:]:
        m = jnp.maximum(m, s.max(axis=-1, keepdims=True))
    den = None
    acc = None
    for s, v in zip(scores, values):
        p = jnp.exp(s - m)
        d = p.sum(axis=-1, keepdims=True)
        a = _dot(p.astype(BF16), v)
        den = d if den is None else den + d
        acc = a if acc is None else acc + a
    return acc / den


def _natten_kernel(q_ref, k0, k1, k2, k3, v0, v1, v2, v3, kc_ref, vc_ref, bias_ref, o_ref,
                   kwin, vwin, *, rows):
    g = pl.program_id(1)
    rb = 4 * GRID_W
    for i, (kr, vr) in enumerate(((k0, v0), (k1, v1), (k2, v2), (k3, v3))):
        kwin[i * rb:(i + 1) * rb, :] = kr[0]
        vwin[i * rb:(i + 1) * rb, :] = vr[0]
    base = 4 * jnp.clip(2 * g - 1, 0, rows // 4 - 4)
    nwin = NA_WIN_R * GRID_W
    qscale = jnp.asarray(NA_HD ** -0.5, BF16)

    ones_lat = jnp.ones((nwin, LANES), BF16)
    ones_ctx = jnp.ones((kc_ref.shape[1], LANES), BF16)

    def rows_body(it, carry):
        work = []
        for u in range(NA_ROWS_PER_TRIP):
            rr = it * NA_ROWS_PER_TRIP + u
            r = 8 * g + rr
            rs = jnp.clip(r - NA_WIN_R // 2, 0, rows - NA_WIN_R)
            st = pl.multiple_of((rs - base) * GRID_W, GRID_W)
            qo = pl.multiple_of(rr * GRID_W, GRID_W)
            work += [(qo, st, rs - r + NA_WIN_R - 1, p) for p in range(NA_HEADS // 2)]
        scores = []
        for qo, st, d0, p in work:
            ls = slice(p * LANES, (p + 1) * LANES)
            qp = _pair_rows(q_ref[0, pl.ds(qo, GRID_W), ls] * qscale)
            scores.append((_dot_nt(qp, kwin[pl.ds(st, nwin), ls]) + bias_ref[d0, p].astype(F32),
                           _dot_nt(qp, kc_ref[0, :, ls])))
        maxima = [jnp.maximum(a.max(axis=-1, keepdims=True), b.max(axis=-1, keepdims=True)) for a, b in scores]
        for (qo, st, d0, p), (s_lat, s_ctx), m in zip(work, scores, maxima):
            ls = slice(p * LANES, (p + 1) * LANES)
            v_lat = jnp.concatenate([vwin[pl.ds(st, nwin), ls], ones_lat], axis=1)
            v_ctx = jnp.concatenate([vc_ref[0, :, ls], ones_ctx], axis=1)
            acc = _dot(jnp.exp(s_lat - m).astype(BF16), v_lat) + _dot(jnp.exp(s_ctx - m).astype(BF16), v_ctx)
            o = acc[:, :LANES] / acc[:, LANES:LANES + 1]
            o_ref[0, pl.ds(qo, GRID_W), ls] = _unpair_rows(o).astype(o_ref.dtype)
        return carry

    lax.fori_loop(0, 8 // NA_ROWS_PER_TRIP, rows_body, 0)


def _natten_bias(rpb):
    c = np.arange(GRID_W)[:, None]
    kc = np.arange(GRID_W)[None, :]
    cs = np.clip(c - NA_WIN_C // 2, 0, GRID_W - NA_WIN_C)
    valid = (kc >= cs) & (kc < cs + NA_WIN_C)
    dc = np.clip(kc - c + NA_WIN_C - 1, 0, 2 * NA_WIN_C - 2)
    ndc = 2 * NA_WIN_C - 1
    pick = ((dc[None] == np.arange(ndc)[:, None, None]) & valid[None]).astype(np.float32)
    tb = jnp.einsum('hrd,dck->hrck', rpb, jnp.asarray(pick), precision=lax.Precision.HIGHEST)
    tb = tb + jnp.asarray(np.where(valid, 0.0, NEG_BIG).astype(np.float32))
    slabs = []
    for d0 in range(NA_WIN_R):
        s = tb[:, d0:d0 + NA_WIN_R]
        s = s.transpose(0, 2, 1, 3).reshape(NA_HEADS, GRID_W, NA_WIN_R * GRID_W)
        slabs.append(s.reshape(NA_HEADS // 2, 2 * GRID_W, NA_WIN_R * GRID_W))
    return jnp.stack(slabs).astype(BF16)


def _natten(uq, uc, bias):
    b, length, _ = uq.shape
    c = NA_HEADS * NA_HD
    rows = length // GRID_W
    rb = 4 * GRID_W
    nkb = length // rb
    qrows = 8 * GRID_W
    lc = uc.shape[1]

    def kv_spec(col, off):
        return pl.BlockSpec((1, rb, c), lambda i, g: (i, jnp.clip(2 * g - 1, 0, nkb - 4) + off, col))

    return pl.pallas_call(
        functools.partial(_natten_kernel, rows=rows),
        grid=(b, rows // 8),
        in_specs=[pl.BlockSpec((1, qrows, c), lambda i, g: (i, g, 0))]
                 + [kv_spec(1, o) for o in range(4)] + [kv_spec(2, o) for o in range(4)]
                 + [pl.BlockSpec((1, lc, c), lambda i, g: (i, 0, 4)),
                    pl.BlockSpec((1, lc, c), lambda i, g: (i, 0, 5)),
                    _resident(bias.shape)],
        out_specs=pl.BlockSpec((1, qrows, c), lambda i, g: (i, g, 0)),
        out_shape=jax.ShapeDtypeStruct((b, length, c), BF16),
        scratch_shapes=[pltpu.VMEM((4 * rb, c), BF16), pltpu.VMEM((4 * rb, c), BF16)],
        compiler_params=_cparams(("parallel", "parallel")),
        name="natten",
    )(uq, *([uq] * 8), uc, uc, bias)


def _ctx_attn_kernel(q_ref, k_ref, v_ref, o_ref):
    qscale = jnp.asarray(NA_HD ** -0.5, BF16)
    for p in range(NA_HEADS // 2):
        ls = slice(p * LANES, (p + 1) * LANES)
        qp = _pair_rows(q_ref[0, :, ls] * qscale)
        o = _pair_softmax_pv([_dot_nt(qp, k_ref[0, :, ls])], [v_ref[0, :, ls]])
        o_ref[0, :, ls] = _unpair_rows(o).astype(o_ref.dtype)


def _ctx_attn(u):
    b, lc, _ = u.shape
    c = NA_HEADS * NA_HD
    spec = lambda col: pl.BlockSpec((1, lc, c), lambda i: (i, 0, col))
    return pl.pallas_call(
        _ctx_attn_kernel,
        grid=(b,),
        in_specs=[spec(3), spec(4), spec(5)],
        out_specs=pl.BlockSpec((1, lc, c), lambda i: (i, 0, 0)),
        out_shape=jax.ShapeDtypeStruct((b, lc, c), BF16),
        compiler_params=_cparams(("parallel",)),
        name="ctx_attn",
    )(u, u, u)


def _post_kernel(x_ref, mod_ref, ya_ref, yb_ref, wa_ref, wb_ref, lng_ref, lnb_ref, w1_ref, w2_ref, o_ref, *,
                 ya_slabs, yb_slabs):
    m = mod_ref[0, 0]
    lng = lng_ref[...]
    lnb = lnb_ref[...]
    ff = w1_ref.shape[1]
    step = MLP_CHUNK
    sub = ROW_BLK
    nsub = x_ref.shape[1] // sub

    def rows_of(ref, slabs, t):
        if slabs:
            tiles = range(t * sub // LANES, (t + 1) * sub // LANES)
            return jnp.concatenate([ref[0, s].T for s in tiles], axis=0).astype(BF16)
        return ref[0, t * sub:(t + 1) * sub, :]

    def head(t):
        rs = slice(t * sub, (t + 1) * sub)
        y = _dot(rows_of(ya_ref, ya_slabs, t), wa_ref[...]) + _dot(rows_of(yb_ref, yb_slabs, t), wb_ref[...])
        x1 = _norm_rows(ALPHA * x_ref[0, rs, :] + m[2:3] * y) * lng[0:1] + lnb[0:1]
        return x1, (_norm_rows(x1) * (1.0 + m[4:5]) + m[3:4]).astype(BF16)

    def mlp(h):
        acc = None
        for c in range(ff // step):
            a = jnp.maximum(_dot(h, w1_ref[:, c * step:(c + 1) * step]), 0.0)
            d = _dot((a * a).astype(BF16), w2_ref[c * step:(c + 1) * step, :])
            acc = d if acc is None else acc + d
        return acc

    cur = head(0)
    for t in range(nsub):
        nxt = head(t + 1) if t + 1 < nsub else None
        acc = mlp(cur[1])
        o_ref[0, t * sub:(t + 1) * sub, :] = _norm_rows(ALPHA * cur[0] + m[5:6] * acc) * lng[1:2] + lnb[1:2]
        cur = nxt


def _post(x, modtab, mod_row, ya, yb, wa, wb, lng, lnb, w1, w2, ya_slabs=False, yb_slabs=False):
    b, r, d = x.shape
    ka, kb = wa.shape[0], wb.shape[0]
    rows = min(r, 4 * ROW_BLK)
    row = lambda k: pl.BlockSpec((1, rows, k), lambda i, j: (i, j, 0))
    slab = lambda k: pl.BlockSpec((1, rows // LANES, k, LANES), lambda i, j: (i, j, 0, 0))
    return pl.pallas_call(
        functools.partial(_post_kernel, ya_slabs=ya_slabs, yb_slabs=yb_slabs),
        grid=(b, r // rows),
        in_specs=[row(d), pl.BlockSpec((1, 1, 6, d), lambda i, j: (i, mod_row, 0, 0)),
                  slab(ka) if ya_slabs else row(ka), slab(kb) if yb_slabs else row(kb),
                  _resident(wa.shape), _resident(wb.shape), _resident(lng.shape), _resident(lnb.shape),
                  _resident(w1.shape), _resident(w2.shape)],
        out_specs=row(d),
        out_shape=jax.ShapeDtypeStruct((b, r, d), F32),
        compiler_params=_cparams(("parallel", "parallel")),
        name="post_mixer",
    )(x, modtab, ya, yb, wa, wb, lng, lnb, w1, w2)


def _rope(x, cos, sinl, sinr):
    reps = x.shape[1] // LANES
    tile = lambda t: jnp.concatenate([t] * reps, axis=1)
    n = x.shape[1]
    quarter = MLA_ROPE // 4
    return (x * tile(cos) + pltpu.roll(x, n - quarter, axis=1) * tile(sinl)
            + pltpu.roll(x, quarter, axis=1) * tile(sinr))


def _front_cd_kernel(x_ref, xc_ref, mod_ref, w_ref, qn_ref, kvn_ref, wuq_ref, wuk_ref, wuv_ref, epe_ref, one_ref,
                     fng_ref, fnb_ref, avg_ref, cbd_ref, sbd_ref,
                     cos_ref, sl_ref, sr_ref,
                     q_ref, k_ref, v_ref, p_ref, qf_ref, *, ctx_blk):
    m = mod_ref[0, 0]
    is_ctx = (jnp.zeros((ROW_BLK, 1), jnp.int32) + pl.program_id(1)) == ctx_blk
    x = jnp.where(is_ctx, xc_ref[0], x_ref[0])
    h = _norm_rows(x) * (1.0 + m[1:2]) + m[0:1]
    u = _dot(h.astype(BF16), w_ref[...])
    o_kv = MLA_Q_RANK
    o_fn = o_kv + MLA_KV_RANK
    o_pe = o_fn + FN_CH

    def rms(x, g):
        return x * lax.rsqrt(jnp.mean(x * x, axis=-1, keepdims=True) + LN_EPS) * g

    cq = rms(u[:, :o_kv], qn_ref[...]).astype(BF16)
    q = _dot(cq, wuq_ref[...])
    q_ref[0] = _rope(q, cos_ref[...], sl_ref[...], sr_ref[...]).astype(BF16)

    ckv = rms(u[:, o_kv:o_fn], kvn_ref[...]).astype(BF16)
    kpe = _dot(u[:, o_pe:].astype(BF16), epe_ref[...])
    k = _dot(ckv, wuk_ref[...]) + _rope(kpe, cos_ref[...], sl_ref[...], sr_ref[...])
    k_ref[0] = k.astype(BF16)
    v_ref[0] = (_dot(ckv, wuv_ref[...]) + one_ref[...]).astype(BF16)

    uf = u[:, o_fn:o_pe]
    avg = avg_ref[...]
    uc = uf - _dot(uf.astype(BF16), avg)
    var = _dot((uc * uc).astype(BF16), avg)
    ug = (uc * lax.rsqrt(var + LN_EPS) * fng_ref[...] + fnb_ref[...]).astype(BF16)
    p_ref[0] = _dot_nt(cbd_ref[...], ug).astype(BF16)
    qf_ref[0] = _dot_nt(sbd_ref[...], ug).astype(BF16)


def _rope_tables(length, lt):
    t = jnp.arange(lt, dtype=jnp.int32)
    rows = (t // GRID_W).astype(F32)
    cols = (t % GRID_W).astype(F32)
    half = MLA_ROPE // 2
    inv = ROPE_THETA ** (-jnp.arange(0, half, 2, dtype=F32) / half)
    ar = rows[:, None] * inv[None, :]
    ac = cols[:, None] * inv[None, :]
    ang = jnp.concatenate([ar, ar, ac, ac], -1)
    is_lat = (t < length)[:, None]
    cos = jnp.where(is_lat, jnp.cos(ang), 1.0)
    sin = jnp.where(is_lat, jnp.sin(ang), 0.0)
    qd = MLA_ROPE // 4
    i = np.arange(MLA_ROPE)
    first = (i // qd) % 2 == 0
    place = np.zeros((3, MLA_ROPE, HEAD_PAD), np.float32)
    place[0, i, MLA_NOPE + i] = 1.0
    place[1, i, MLA_NOPE + i] = np.where(first, -1.0, 0.0)
    place[2, i, MLA_NOPE + i] = np.where(first, 0.0, 1.0)
    put = lambda a, k: jnp.dot(a, jnp.asarray(place[k]), precision=lax.Precision.HIGHEST)
    rest = jnp.asarray(1.0 - place[0].sum(0))
    return put(cos, 0) + rest, put(sin, 1), put(sin, 2)


def _head_slots(w, per_head, take_from, take_n):
    k = w.shape[0]
    w3 = w.reshape(k, MLA_HEADS, per_head)[:, :, take_from:take_from + take_n]
    w3 = jnp.pad(w3, ((0, 0), (0, 0), (0, HEAD_PAD - take_n)))
    return w3.reshape(k, MLA_HEADS * HEAD_PAD)


def _front_cd(xl, xc, modtab, w_in, q_norm, w_uq, kv_norm, w_ukv, fn_g, fn_b):
    b, length, d = xl.shape
    lt = length + xc.shape[1]
    nlat = length // ROW_BLK
    o_kv = MLA_Q_RANK
    o_pe = o_kv + MLA_KV_RANK
    o_fn = o_pe + MLA_ROPE
    hw = MLA_HEADS * HEAD_PAD
    w_perm = jnp.concatenate([w_in[:, :o_pe], w_in[:, o_fn:], w_in[:, o_pe:o_fn],
                              jnp.zeros((d, LANES - MLA_ROPE), w_in.dtype)], -1).astype(BF16)
    q_scale = (MLA_NOPE + MLA_ROPE) ** -0.5 * math.log2(math.e)
    wuq = _head_slots(w_uq * q_scale, MLA_NOPE + MLA_ROPE, 0, MLA_NOPE + MLA_ROPE).astype(BF16)
    wuk = _head_slots(w_ukv, MLA_NOPE + MLA_V, 0, MLA_NOPE).astype(BF16)
    wuv = _head_slots(w_ukv, MLA_NOPE + MLA_V, MLA_NOPE, MLA_V).astype(BF16)
    epe = np.zeros((LANES, hw), np.float32)
    for hd in range(MLA_HEADS):
        for i in range(MLA_ROPE):
            epe[i, hd * HEAD_PAD + MLA_NOPE + i] = 1.0
    epe = _mxu_const(epe)
    ones_col = np.zeros((1, hw), np.float32)
    ones_col[0, MLA_V::HEAD_PAD] = 1.0
    ones_col = jnp.asarray(ones_col)
    cm, sm = _cs(1, FN_GD, FN_GD, FN_GD)
    eye = np.eye(FN_GROUPS)
    cbd = _mxu_const(np.kron(eye, cm))
    sbd = _mxu_const(np.kron(eye, -sm))
    avg = _mxu_const(np.kron(eye, np.full((FN_GD, FN_GD), 1.0 / FN_GD)))
    rtab = _rope_tables(length, lt)
    row = lambda n: pl.BlockSpec((1, ROW_BLK, n), lambda i, j: (i, j, 0))
    tab = pl.BlockSpec((ROW_BLK, HEAD_PAD), lambda i, j: (j, 0))
    out = lambda n: jax.ShapeDtypeStruct((b, lt, n), BF16)
    fn_t = pl.BlockSpec((1, FN_CH, ROW_BLK), lambda i, j: (i, 0, j))
    fn_out = jax.ShapeDtypeStruct((b, FN_CH, lt), BF16)
    return pl.pallas_call(
        functools.partial(_front_cd_kernel, ctx_blk=nlat),
        grid=(b, lt // ROW_BLK),
        in_specs=[pl.BlockSpec((1, ROW_BLK, d), lambda i, j: (i, jnp.minimum(j, nlat - 1), 0)),
                  pl.BlockSpec((1, ROW_BLK, d), lambda i, j: (i, 0, 0)),
                  pl.BlockSpec((1, 1, 6, d), lambda i, j: (i, j // nlat, 0, 0)),
                  _resident(w_perm.shape), _resident((1, MLA_Q_RANK)), _resident((1, MLA_KV_RANK)),
                  _resident(wuq.shape), _resident(wuk.shape), _resident(wuv.shape), _resident(epe.shape),
                  _resident(ones_col.shape),
                  _resident((1, FN_CH)), _resident((1, FN_CH)), _resident(avg.shape), _resident(cbd.shape),
                  _resident(sbd.shape),
                  tab, tab, tab],
        out_specs=[row(hw), row(hw), row(hw), fn_t, fn_t],
        out_shape=[out(hw), out(hw), out(hw), fn_out, fn_out],
        compiler_params=_cparams(("parallel", "parallel")),
        name="front_cd",
    )(xl, xc, modtab, w_perm, q_norm.reshape(1, -1), kv_norm.reshape(1, -1), wuq, wuk, wuv, epe, ones_col,
      fn_g.reshape(1, -1), fn_b.reshape(1, -1), avg, cbd, sbd, *rtab)


def _mla_kernel(q_ref, k_ref, v_ref, o_ref, *, rows):
    n = q_ref.shape[1] // rows

    def scores(i):
        s = _dot_nt(q_ref[0, i * rows:(i + 1) * rows, :], k_ref[0])
        return s, s.max(axis=-1, keepdims=True)

    def finish(i, s, m):
        acc = _dot(jnp.exp2(s - m).astype(BF16), v_ref[0])
        o_ref[0, i * rows:(i + 1) * rows, :] = (acc / acc[:, MLA_V:MLA_V + 1]).astype(o_ref.dtype)

    pending = scores(0)
    for i in range(n):
        nxt = scores(i + 1) if i + 1 < n else None
        finish(i, *pending)
        pending = nxt


def _mla_attention(q, k, v, length):
    b, lt, hw = q.shape
    heads = hw // HEAD_PAD
    tq = MLA_Q_BLK
    kv = pl.BlockSpec((1, lt, HEAD_PAD), lambda i, h, j: (i, 0, h))
    qs = pl.BlockSpec((1, tq, HEAD_PAD), lambda i, h, j: (i, j, h))
    return pl.pallas_call(
        functools.partial(_mla_kernel, rows=MLA_Q_SUB),
        grid=(b, heads, length // tq),
        in_specs=[qs, kv, kv],
        out_specs=qs,
        out_shape=jax.ShapeDtypeStruct((b, length, hw), BF16),
        compiler_params=_cparams(("parallel", "parallel", "parallel")),
        name="mla_attention",
    )(q, k, v)


def _fnet_kernel(pa_ref, pb_ref, qa_ref, qb_ref, m1_ref, c_ref, s_ref, m2_ref, o_ref, zr_scr, zi_scr):
    n1 = FN_N1
    n2 = zr_scr.shape[2] // 2
    pairs = zr_scr.shape[0]

    def fill(scr, a_ref, b_ref):
        za = a_ref[0].astype(F32)
        zb = b_ref[0].astype(F32)
        for r in range(n1):
            scr[:, r, :] = jnp.concatenate([za[:, r * n2:(r + 1) * n2], zb[:, r * n2:(r + 1) * n2]], axis=1)

    fill(zr_scr, pa_ref, pb_ref)
    fill(zi_scr, qa_ref, qb_ref)
    c = c_ref[...]
    s = s_ref[...]
    for g0 in range(0, pairs, FN_GROUP):
        x = jnp.concatenate([jnp.concatenate([zr_scr[g0 + g], zi_scr[g0 + g]], axis=0) for g in range(FN_GROUP)],
                            axis=1).astype(BF16)
        a = _dot(m1_ref[...], x)
        ts = []
        for g in range(FN_GROUP):
            ar = a[:n1, g * LANES:(g + 1) * LANES]
            ai = a[n1:, g * LANES:(g + 1) * LANES]
            ts.append(jnp.concatenate([ar * c + ai * s, ai * c - ar * s], axis=1))
        y = _dot(jnp.concatenate(ts, axis=0).astype(BF16), m2_ref[...])
        for g in range(FN_GROUP):
            yt = y[g * n1:(g + 1) * n1].T
            o_ref[0, :, 2 * (g0 + g), :] = yt[:n2]
            o_ref[0, :, 2 * (g0 + g) + 1, :] = yt[n2:]


def _fnet(pt, qt, length):
    b, ch, _ = pt.shape
    n1 = FN_N1
    n2 = length // n1
    assert 2 * n2 == LANES
    half = ch // 2
    cm, sm = _cs(1, n1, n1, n1)
    m1 = _mxu_const(np.block([[cm, sm], [-sm, cm]]))
    c2, s2 = _cs(1, n2, n2, n2)
    z = np.zeros_like(c2)
    m2 = np.block([[c2, z], [z, c2], [s2, z], [z, s2]]) / math.sqrt(length * FN_GD)
    m2 = _mxu_const(m2)
    k1 = lax.broadcasted_iota(jnp.int32, (n1, LANES), 0)
    m = lax.broadcasted_iota(jnp.int32, (n1, LANES), 1) % n2
    ang = (k1 * m).astype(F32) * (2.0 * math.pi / length)
    twc, tws = jnp.cos(ang), jnp.sin(ang)
    pairs = FN_PAIRS
    nblk = half // pairs
    spec = lambda off: pl.BlockSpec((1, pairs, length), lambda j, i: (i, off * nblk + j, 0))
    return pl.pallas_call(
        _fnet_kernel,
        grid=(nblk, b),
        in_specs=[spec(0), spec(1), spec(0), spec(1), _resident(m1.shape), _resident((n1, LANES)),
                  _resident((n1, LANES)), _resident(m2.shape)],
        out_specs=pl.BlockSpec((1, n2, 2 * pairs, LANES), lambda j, i: (i, 0, j, 0)),
        out_shape=jax.ShapeDtypeStruct((b, n2, ch, LANES), F32),
        scratch_shapes=[pltpu.VMEM((pairs, n1, LANES), F32), pltpu.VMEM((pairs, n1, LANES), F32)],
        compiler_params=_cparams(("parallel", "parallel")),
        name="fnet",
    )(pt, pt, qt, qt, m1, twc, tws, m2)


def kernel(x, c, ctx, c_ctx, mod_w, mod_b, ln_g, ln_b, mlp_w1, mlp_w2,
           ab_w_in, ab_w_out, hy_conv_w, hy_w1, hy_b1, hy_freq, hy_w2, hy_b2, hy_w3, hy_log_decay, hy_skip, na_rpb,
           cd_w_in, cd_w_out, mla_q_norm, mla_w_uq, mla_kv_norm, mla_w_ukv, fn_norm_g, fn_norm_b):
    b, length, d = x.shape
    lc = ctx.shape[1]

    cc = jnp.concatenate([c, c_ctx[None], jnp.zeros((8 - b - 1, d), F32)], 0)
    mods = _mod_vectors(cc, mod_w, mod_b).reshape(DEPTH, 8, 6, d)
    modtab = [jnp.stack([mods[l, :b], jnp.broadcast_to(mods[l, b], (b, 6, d))], axis=1) for l in range(DEPTH)]

    n_hy = 3 * HY_CH
    w_in = ab_w_in[0].astype(BF16)
    uq, ut = _front_ab_lat(x, modtab[0], w_in[:, n_hy:], w_in[:, :n_hy].T)
    uc = _front_ab_ctx(ctx, modtab[0], w_in)
    fargs = (hy_w1[0], hy_b1[0], hy_freq[0], hy_w2[0], hy_b2[0], hy_w3[0], hy_log_decay[0])
    mats = _dft_mats()
    twc, tws = _twiddle2d(FFT_N2)
    spec = _filter_spec_t(_hy_filters_t(length, *fargs, hy_skip[0]), mats[1], mats[2], twc, tws)
    y_hy_t = _hyena_core(ut, hy_conv_w[0], spec, mats, twc, tws)
    x1c, x2c, vc = _hy_prep(uc, hy_conv_w[0], 0, lc // ROW_BLK)
    y_hy_c = _hy_ctx(vc, x1c, x2c, _bidir_taps(_hy_filters(lc, *fargs), hy_skip[0], lc))
    y_na = _natten(uq, uc, _natten_bias(na_rpb[0]))
    y_na_c = _ctx_attn(uc)
    w_out = ab_w_out[0].astype(BF16)
    mlp = (ln_g[0], ln_b[0], mlp_w1[0].astype(BF16), mlp_w2[0].astype(BF16))
    xl = _post(x, modtab[0], 0, y_hy_t, y_na, w_out[:HY_CH], w_out[HY_CH:], *mlp, ya_slabs=True)
    xc = _post(ctx, modtab[0], 1, y_hy_c, y_na_c, w_out[:HY_CH], w_out[HY_CH:], *mlp)

    q, k, vv, pt, qt = _front_cd(xl, xc, modtab[1], cd_w_in[0], mla_q_norm[0], mla_w_uq[0], mla_kv_norm[0],
                                 mla_w_ukv[0], fn_norm_g[0], fn_norm_b[0])
    o = _mla_attention(q, k, vv, length)
    y_fn = _fnet(pt, qt, length)
    w_out = cd_w_out[0]
    n_mla = MLA_HEADS * MLA_V
    wa = jnp.pad(w_out[:n_mla].reshape(MLA_HEADS, MLA_V, d), ((0, 0), (0, HEAD_PAD - MLA_V), (0, 0)))
    wa = wa.reshape(MLA_HEADS * HEAD_PAD, d).astype(BF16)
    wb = w_out[n_mla:].reshape(2, FN_CH // 2, d).transpose(1, 0, 2).reshape(FN_CH, d)
    return _post(xl, modtab[1], 0, o, y_fn, wa, wb.astype(BF16),
                 ln_g[1], ln_b[1], mlp_w1[1].astype(BF16), mlp_w2[1].astype(BF16), yb_slabs=True)
```
